```python
import jax
import jax.numpy as jnp
from jax import lax
import numpy as np

D_MODEL = 1024
BATCH = 8
SEQ = 2048
DEPTH = 4
DEC_BATCH = 32
DEC_SEQ = 4
PAST_LEN = 16384
PAGE_SIZE = 128

N_MIXERS = 3
N_A = (DEPTH + 2) // 3
N_B = (DEPTH + 1) // 3
N_C = DEPTH // 3

CHUNK_A = 128
SGU_DIM = D_MODEL
A_GROUPS = 8
A_GDIM = SGU_DIM // A_GROUPS
B_HEADS = 8
B_DK = 128
B_DV = D_MODEL // B_HEADS
B_CHUNK = 64
C_HEADS = 8
C_NOPE = 128
C_ROPE = 64
C_V = 128
C_QLORA = 512
C_KVLORA = 256
ROPE_THETA = 10000.0
Q_BLOCK = 128
D_FF = 4 * D_MODEL
ALPHA = (2.0 * DEPTH) ** 0.25
BETA = (8.0 * DEPTH) ** -0.25
EPS = 1e-6
F32 = jnp.float32

kernel_name = 'hybrid_gmlp_hgrn2_mla_deepnorm_adaln_step'


def layer_norm(x, g, b):
    xf = x.astype(F32)
    mu = jnp.mean(xf, -1, keepdims=True)
    var = jnp.mean(jnp.square(xf - mu), -1, keepdims=True)
    return ((xf - mu) * lax.rsqrt(var + EPS) * g.astype(F32) + b.astype(F32)).astype(x.dtype)


def rms_norm(x, g=None):
    xf = x.astype(F32)
    y = xf * lax.rsqrt(jnp.mean(xf * xf, -1, keepdims=True) + EPS)
    if g is not None:
        y = y * g.astype(F32)
    return y.astype(x.dtype)


def rope(x, pos):
    half = x.shape[-1] // 2
    inv = ROPE_THETA ** (-jnp.arange(half, dtype=F32) / half)
    ang = pos.astype(F32)[:, None] * inv
    ang = ang.reshape((1, ang.shape[0]) + (1,) * (x.ndim - 3) + (half,))
    cos, sin = jnp.cos(ang), jnp.sin(ang)
    x1 = x[..., :half].astype(F32)
    x2 = x[..., half:].astype(F32)
    return jnp.concatenate([x1 * cos - x2 * sin, x2 * cos + x1 * sin], -1).astype(x.dtype)


def chunk_mlp_mixer(h, w_in, ln_g, ln_b, w_s, b_s, w_out):
    B, L, _ = h.shape
    z = jax.nn.gelu(h @ w_in)
    u, v = jnp.split(z, 2, axis=-1)
    v = layer_norm(v, ln_g, ln_b)
    C = min(CHUNK_A, L)
    causal = jnp.tril(jnp.ones((C, C), dtype=bool))
    ws = jnp.where(causal[None], w_s[:, :C, :C], 0)
    vc = v.reshape(B, L // C, C, A_GROUPS, A_GDIM)
    mixed = jnp.einsum('gts,bnsgd->bntgd', ws, vc) + b_s[:, :C].T[None, None, :, :, None]
    out = u * mixed.reshape(B, L, SGU_DIM)
    return out @ w_out, v


def gated_linear_recurrence(q, k, v, log_f, state0):
    B, L, H, _ = q.shape
    C = min(B_CHUNK, L)
    n = L // C

    def to_chunks(t):
        return t.astype(F32).reshape(B, n, C, H, t.shape[-1]).transpose(1, 0, 3, 2, 4)

    qc, kc, vc, gc = to_chunks(q), to_chunks(k), to_chunks(v), to_chunks(log_f)
    causal = jnp.tril(jnp.ones((C, C), dtype=bool))

    def step(S, inp):
        qi, ki, vi, gi = inp
        b = jnp.cumsum(gi, axis=2)
        o_inter = jnp.einsum('bhtd,bhdv->bhtv', qi * jnp.exp(b), S)
        diff = b[:, :, :, None, :] - b[:, :, None, :, :]
        decay = jnp.exp(jnp.where(causal[:, :, None], diff, -jnp.inf))
        A = jnp.einsum('bhtd,bhtsd,bhsd->bhts', qi, decay, ki)
        o_intra = jnp.einsum('bhts,bhsv->bhtv', A, vi)
        b_last = b[:, :, -1:, :]
        S_new = jnp.exp(b_last[:, :, 0, :, None]) * S + jnp.einsum(
            'bhsd,bhsv->bhdv', ki * jnp.exp(b_last - b), vi)
        return S_new, o_inter + o_intra

    S, o = lax.scan(step, state0.astype(F32), (qc, kc, vc, gc))
    o = o.transpose(1, 0, 3, 2, 4).reshape(B, L, H, -1)
    return o.astype(q.dtype), S.astype(state0.dtype)


def hgrn2_mixer(h, w_in, lb, state0, w_out):
    B, L, _ = h.shape
    q, fz, i_in, g = jnp.split(h @ w_in, 4, axis=-1)
    q = jax.nn.silu(q).reshape(B, L, B_HEADS, B_DK)
    fz = fz.astype(F32).reshape(B, L, B_HEADS, B_DK)
    lbh = lb.astype(F32).reshape(B_HEADS, B_DK)
    log_f = jnp.logaddexp(jnp.log(lbh), jnp.log1p(-lbh) + jax.nn.log_sigmoid(fz))
    k = (1.0 - lbh) * jax.nn.sigmoid(-fz)
    v = i_in.reshape(B, L, B_HEADS, B_DV)
    if state0 is None:
        state0 = jnp.zeros((B, B_HEADS, B_DK, B_DV), dtype=h.dtype)
    o, S = gated_linear_recurrence(q, k, v, log_f, state0)
    o = rms_norm(o).reshape(B, L, B_HEADS * B_DV) * jax.nn.silu(g)
    return o @ w_out, S


def latent_attention(q_lat, q_rope, k_lat, k_rope, q_pos, k_pos):
    B, L, H, _ = q_lat.shape
    blk = min(Q_BLOCK, L)
    n = L // blk
    scale = (C_NOPE + C_ROPE) ** -0.5

    def one_block(args):
        ql, qr, qp = args
        s = (jnp.einsum('bqhc,bkc->bhqk', ql, k_lat, preferred_element_type=F32)
             + jnp.einsum('bqhr,bkr->bhqk', qr, k_rope, preferred_element_type=F32)) * scale
        s = jnp.where(k_pos[None, None, None, :] <= qp[None, None, :, None], s, -jnp.inf)
        p = jax.nn.softmax(s, axis=-1)
        return jnp.einsum('bhqk,bkc->bqhc', p.astype(k_lat.dtype), k_lat)

    qb = q_lat.reshape(B, n, blk, H, -1).transpose(1, 0, 2, 3, 4)
    rb = q_rope.reshape(B, n, blk, H, -1).transpose(1, 0, 2, 3, 4)
    out = lax.map(one_block, (qb, rb, q_pos.reshape(n, blk)))
    return out.transpose(1, 0, 2, 3, 4).reshape(B, L, H, -1)


def mla_mixer(h, q_pos, past, w_in, g_q, g_kv, w_uq, w_uk, w_uv, w_out):
    B, L, _ = h.shape
    a = h @ w_in
    cq, ckv, kr = jnp.split(a, [C_QLORA, C_QLORA + C_KVLORA], axis=-1)
    cq = rms_norm(cq, g_q)
    ckv = rms_norm(ckv, g_kv)
    kr = rope(kr, q_pos)
    q = jnp.einsum('blc,chd->blhd', cq, w_uq)
    q_nope = q[..., :C_NOPE]
    q_rope = rope(q[..., C_NOPE:], q_pos)
    q_lat = jnp.einsum('blhd,chd->blhc', q_nope, w_uk)
    if past is None:
        k_lat, k_rope, k_pos = ckv, kr, q_pos
    else:
        p_lat, p_rope = past
        k_lat = jnp.concatenate([p_lat.astype(ckv.dtype), ckv], axis=1)
        k_rope = jnp.concatenate([p_rope.astype(kr.dtype), kr], axis=1)
        k_pos = jnp.concatenate([jnp.arange(p_lat.shape[1], dtype=jnp.int32), q_pos])
    o_lat = latent_attention(q_lat, q_rope, k_lat, k_rope, q_pos, k_pos)
    o = jnp.einsum('blhc,chv->blhv', o_lat, w_uv).reshape(B, L, C_HEADS * C_V)
    return o @ w_out, ckv, kr


def squared_relu_mlp(h, w1, w2):
    return jnp.square(jax.nn.relu(h @ w1)) @ w2


def gather_pages(pool, page_table):
    g = pool[page_table]
    return g.reshape(g.shape[0], g.shape[1] * g.shape[2], g.shape[3])


def run_trunk(x, c, q_pos, hgrn_state0, mla_cache, prm):
    lb_all = jnp.cumsum(jax.nn.softmax(prm['b_lb'].astype(F32), axis=0), axis=0)
    lb_all = lb_all - lb_all[:1]
    chunk_v, hgrn_states, lat_rows, rope_rows = [], [], [], []
    sc = jax.nn.silu(c)
    for i in range(DEPTH):
        kind, j = i % N_MIXERS, i // N_MIXERS
        mod = sc @ prm['w_ada'][i] + prm['b_ada'][i]
        sh1, sc1, g1, sh2, sc2, g2 = jnp.split(mod[:, None, :], 6, axis=-1)
        h = x * (1 + sc1) + sh1
        if kind == 0:
            out, v_rows = chunk_mlp_mixer(h, prm['a_w_in'][j], prm['a_ln_g'][j], prm['a_ln_b'][j],
                                          prm['a_w_s'][j], prm['a_b_s'][j], prm['a_w_out'][j])
            chunk_v.append(v_rows)
        elif kind == 1:
            s0 = None if hgrn_state0 is None else hgrn_state0[j]
            out, S = hgrn2_mixer(h, prm['b_w_in'][j], lb_all[i], s0, prm['b_w_out'][j])
            hgrn_states.append(S)
        else:
            if mla_cache is None:
                past = None
            else:
                pool_lat, pool_rope, pt = mla_cache
                past = (gather_pages(pool_lat[j], pt), gather_pages(pool_rope[j], pt))
            out, lat, kr = mla_mixer(h, q_pos, past, prm['c_w_in'][j], prm['c_g_q'][j],
                                     prm['c_g_kv'][j], prm['c_w_uq'][j], prm['c_w_uk'][j],
                                     prm['c_w_uv'][j], prm['c_w_out'][j])
            lat_rows.append(lat)
            rope_rows.append(kr)
        x = layer_norm(ALPHA * x + g1 * out, prm['ln1_g'][i], prm['ln1_b'][i])
        h = x * (1 + sc2) + sh2
        x = layer_norm(ALPHA * x + g2 * squared_relu_mlp(h, prm['ffn_w1'][i], prm['ffn_w2'][i]),
                       prm['ln2_g'][i], prm['ln2_b'][i])
    return x, jnp.stack(chunk_v), jnp.stack(hgrn_states), jnp.stack(lat_rows), jnp.stack(rope_rows)


def setup_inputs(seed: int = 0) -> dict:
    key = jax.random.key(seed)
    ks = iter(jax.random.split(key, 40))

    def nrm(shape, scale):
        return jax.random.normal(next(ks), shape, F32) * scale

    n_pages = PAST_LEN // PAGE_SIZE
    n_pool = (DEC_BATCH * n_pages * 5) // 4
    page_table = jax.random.permutation(next(ks), n_pool)[: DEC_BATCH * n_pages]
    page_table = page_table.reshape(DEC_BATCH, n_pages).astype(jnp.int32)
    d = D_MODEL
    return {
        'x_prompt': nrm((BATCH, SEQ, d), 1.0),
        'x_sample': nrm((DEC_BATCH, DEC_SEQ, d), 1.0),
        'cache_kv_latent': nrm((N_C, n_pool, PAGE_SIZE, C_KVLORA), 1.0),
        'cache_k_rope': nrm((N_C, n_pool, PAGE_SIZE, C_ROPE), 1.0),
        'state_hgrn': nrm((N_B, DEC_BATCH, B_HEADS, B_DK, B_DV), 0.5),
        'page_table': page_table,
        'c_prompt': nrm((BATCH, d), 1.0),
        'c_sample': nrm((DEC_BATCH, d), 1.0),
        'w_ada': nrm((DEPTH, d, 6 * d), 0.5 * d ** -0.5),
        'b_ada': nrm((DEPTH, 6 * d), 0.01),
        'ln1_g': 1.0 + nrm((DEPTH, d), 0.05),
        'ln1_b': nrm((DEPTH, d), 0.01),
        'ln2_g': 1.0 + nrm((DEPTH, d), 0.05),
        'ln2_b': nrm((DEPTH, d), 0.01),
        'ffn_w1': nrm((DEPTH, d, D_FF), d ** -0.5),
        'ffn_w2': nrm((DEPTH, D_FF, d), BETA * D_FF ** -0.5),
        'a_w_in': nrm((N_A, d, 2 * SGU_DIM), d ** -0.5),
        'a_ln_g': 1.0 + nrm((N_A, SGU_DIM), 0.05),
        'a_ln_b': nrm((N_A, SGU_DIM), 0.01),
        'a_w_s': nrm((N_A, A_GROUPS, CHUNK_A, CHUNK_A), CHUNK_A ** -0.5),
        'a_b_s': 1.0 + nrm((N_A, A_GROUPS, CHUNK_A), 0.1),
        'a_w_out': nrm((N_A, SGU_DIM, d), BETA * SGU_DIM ** -0.5),
        'b_w_in': nrm((N_B, d, 4 * d), d ** -0.5),
        'b_lb': 1.0 + nrm((DEPTH, B_HEADS * B_DK), 0.1),
        'b_w_out': nrm((N_B, B_HEADS * B_DV, d), BETA * (B_HEADS * B_DV) ** -0.5),
        'c_w_in': nrm((N_C, d, C_QLORA + C_KVLORA + C_ROPE), d ** -0.5),
        'c_g_q': 1.0 + nrm((N_C, C_QLORA), 0.05),
        'c_g_kv': 1.0 + nrm((N_C, C_KVLORA), 0.05),
        'c_w_uq': nrm((N_C, C_QLORA, C_HEADS, C_NOPE + C_ROPE), C_QLORA ** -0.5),
        'c_w_uk': nrm((N_C, C_KVLORA, C_HEADS, C_NOPE), C_KVLORA ** -0.5),
        'c_w_uv': nrm((N_C, C_KVLORA, C_HEADS, C_V), C_KVLORA ** -0.5),
        'c_w_out': nrm((N_C, C_HEADS * C_V, d), BETA * (C_HEADS * C_V) ** -0.5),
    }


def reference(x_prompt, x_sample, cache_kv_latent, cache_k_rope, state_hgrn, page_table,
              c_prompt, c_sample, w_ada, b_ada, ln1_g, ln1_b, ln2_g, ln2_b, ffn_w1, ffn_w2,
              a_w_in, a_ln_g, a_ln_b, a_w_s, a_b_s, a_w_out, b_w_in, b_lb, b_w_out,
              c_w_in, c_g_q, c_g_kv, c_w_uq, c_w_uk, c_w_uv, c_w_out):
    prm = dict(w_ada=w_ada, b_ada=b_ada, ln1_g=ln1_g, ln1_b=ln1_b, ln2_g=ln2_g, ln2_b=ln2_b,
               ffn_w1=ffn_w1, ffn_w2=ffn_w2, a_w_in=a_w_in, a_ln_g=a_ln_g, a_ln_b=a_ln_b,
               a_w_s=a_w_s, a_b_s=a_b_s, a_w_out=a_w_out, b_w_in=b_w_in, b_lb=b_lb,
               b_w_out=b_w_out, c_w_in=c_w_in, c_g_q=c_g_q, c_g_kv=c_g_kv, c_w_uq=c_w_uq,
               c_w_uk=c_w_uk, c_w_uv=c_w_uv, c_w_out=c_w_out)
    past_len = page_table.shape[1] * cache_kv_latent.shape[2]
    pos_prompt = jnp.arange(x_prompt.shape[1], dtype=jnp.int32)
    pos_sample = past_len + jnp.arange(x_sample.shape[1], dtype=jnp.int32)
    y_prompt, _, hs_p, lat_p, rope_p = run_trunk(x_prompt, c_prompt, pos_prompt, None, None, prm)
    y_sample, v_s, hs_s, lat_s, rope_s = run_trunk(
        x_sample, c_sample, pos_sample, state_hgrn, (cache_kv_latent, cache_k_rope, page_table), prm)
    return (y_prompt, y_sample, hs_p, hs_s, lat_p, rope_p, lat_s, rope_s, v_s)
```

```python
import functools
import math

import jax
import jax.numpy as jnp
from jax import lax
from jax.experimental import pallas as pl
from jax.experimental.pallas import tpu as pltpu

F32 = jnp.float32
BF16 = jnp.bfloat16

D_MODEL = 1024
DEPTH = 4
N_MIXERS = 3
CHUNK_A = 128
A_GROUPS = 8
A_GDIM = D_MODEL // A_GROUPS
B_HEADS = 8
B_DK = 128
B_DV = D_MODEL // B_HEADS
C_HEADS = 8
C_NOPE = 128
C_ROPE = 64
C_V = 128
C_QLORA = 512
C_KVLORA = 256
ROPE_THETA = 10000.0
D_FF = 4 * D_MODEL
ALPHA = (2.0 * DEPTH) ** 0.25
EPS = 1e-6

LANES = 128
SUBLANES = 8
C_QK = C_KVLORA + LANES
VMEM_LIMIT = 56 * 1024 * 1024


def _cparams(*sem):
    return pltpu.CompilerParams(dimension_semantics=sem, vmem_limit_bytes=VMEM_LIMIT)


def _dot(a, b):
    return jnp.dot(a, b, preferred_element_type=F32)


def _dot_nt(a, b):
    return lax.dot_general(a, b, (((1,), (1,)), ((), ())), preferred_element_type=F32)


def _dot_tn(a, b):
    return lax.dot_general(a, b, (((0,), (0,)), ((), ())), preferred_element_type=F32)


def _layer_norm(y, g, b):
    mu = jnp.mean(y, axis=-1, keepdims=True)
    yc = y - mu
    var = jnp.mean(yc * yc, axis=-1, keepdims=True)
    return yc * lax.rsqrt(var + EPS) * g + b


def _rms(y):
    return y * lax.rsqrt(jnp.mean(y * y, axis=-1, keepdims=True) + EPS)


def _silu(x):
    return x * jax.nn.sigmoid(x)


def _gelu_tanh(x):
    return 0.5 * x * (1.0 + jnp.tanh(math.sqrt(2.0 / math.pi) * (x + 0.044715 * (x * x * x))))


def _modulate(x, sh_ref, sc_ref):
    return x * (1.0 + sc_ref[...]) + sh_ref[...]


def _residual_ln(x, gate_ref, out, lg_ref, lb_ref):
    return _layer_norm(ALPHA * x + gate_ref[...] * out, lg_ref[...], lb_ref[...])


def _mod_spec(mod, layer, which, tiles_per_seq):
    rows = mod.shape[3]
    return pl.BlockSpec((None, None, None, rows, D_MODEL),
                        lambda i: (layer, which, i // tiles_per_seq, 0, 0))


def _vec_spec(layer, width):
    return pl.BlockSpec((None, 1, width), lambda i: (layer, 0, 0))


def _full_spec(arr, layer=None):
    if layer is None:
        nd = arr.ndim
        return pl.BlockSpec(arr.shape, lambda i: (0,) * nd)
    nd = arr.ndim - 1
    return pl.BlockSpec((None,) + arr.shape[1:], lambda i: (layer,) + (0,) * nd)


def _row_spec(tm, width):
    return pl.BlockSpec((tm, width), lambda i: (i, 0))


def _mod_body(c_ref, w_ref, b_ref, o_ref):
    sc = _silu(c_ref[...]).astype(BF16)
    o_ref[...] = _dot(sc, w_ref[...].astype(BF16)) + b_ref[...]


def _modulation(c_all, w_ada, b_ada):
    n = c_all.shape[0]
    tn = 1536
    width = w_ada.shape[2]
    return pl.pallas_call(
        _mod_body,
        out_shape=jax.ShapeDtypeStruct((DEPTH, n, width), F32),
        grid=(DEPTH, width // tn),
        in_specs=[pl.BlockSpec((n, D_MODEL), lambda l, j: (0, 0)),
                  pl.BlockSpec((None, D_MODEL, tn), lambda l, j: (l, 0, j)),
                  pl.BlockSpec((None, 1, tn), lambda l, j: (l, 0, j))],
        out_specs=pl.BlockSpec((None, n, tn), lambda l, j: (l, 0, j)),
        compiler_params=_cparams("arbitrary", "arbitrary"),
        name="adaln_modulation",
    )(c_all, w_ada, b_ada.reshape(DEPTH, 1, width))


def _ffn_body(x_ref, sh_ref, sc_ref, g_ref, w1_ref, w2_ref, lg_ref, lb_ref, o_ref, acc_ref, *, fc):
    x = x_ref[...]
    h = _modulate(x, sh_ref, sc_ref).astype(BF16)
    for c in range(D_FF // fc):
        a = _dot(h, w1_ref[:, c * fc:(c + 1) * fc])
        a = jnp.square(jnp.maximum(a, 0.0)).astype(BF16)
        d = _dot(a, w2_ref[c * fc:(c + 1) * fc, :])
        if c == 0:
            acc_ref[...] = d
        else:
            acc_ref[...] += d
    o_ref[...] = _residual_ln(x, g_ref, acc_ref[...], lg_ref, lb_ref)


def _ffn_layer(x, mod, tps, layer, w1, w2, ln_g, ln_b, tm):
    T = x.shape[0]
    return pl.pallas_call(
        functools.partial(_ffn_body, fc=1024),
        out_shape=jax.ShapeDtypeStruct((T, D_MODEL), F32),
        grid=(T // tm,),
        in_specs=[_row_spec(tm, D_MODEL),
                  _mod_spec(mod, layer, 3, tps), _mod_spec(mod, layer, 4, tps), _mod_spec(mod, layer, 5, tps),
                  _full_spec(w1, layer), _full_spec(w2, layer),
                  _vec_spec(layer, D_MODEL), _vec_spec(layer, D_MODEL)],
        out_specs=_row_spec(tm, D_MODEL),
        scratch_shapes=[pltpu.VMEM((tm, D_MODEL), F32)],
        compiler_params=_cparams("arbitrary"),
        name="ffn_sublayer",
    )(x, mod, mod, mod, w1, w2, ln_g, ln_b)


def _sgu_body(x_ref, sh_ref, sc_ref, g_ref, win_ref, lng_ref, lnb_ref, ws_ref, bias_ref, wout_ref,
              lg_ref, lb_ref, o_ref, *rest, tm, emit_v):
    if emit_v:
        v_ref, gated_ref = rest
    else:
        (gated_ref,) = rest
    x = x_ref[...]
    h = _modulate(x, sh_ref, sc_ref).astype(BF16)
    u = _gelu_tanh(_dot(h, win_ref[:, :D_MODEL]))
    v = _gelu_tanh(_dot(h, win_ref[:, D_MODEL:]))
    v = _layer_norm(v, lng_ref[...], lnb_ref[...])
    if emit_v:
        v_ref[...] = v
    vb = v.astype(BF16)
    for n in range(tm // CHUNK_A):
        r = slice(n * CHUNK_A, (n + 1) * CHUNK_A)
        cols = [_dot(ws_ref[g], vb[r, g * A_GDIM:(g + 1) * A_GDIM]) for g in range(A_GROUPS)]
        mixed = jnp.concatenate(cols, axis=1) + bias_ref[...]
        gated_ref[r, :] = (u[r, :] * mixed).astype(BF16)
    out = _dot(gated_ref[...], wout_ref[...])
    o_ref[...] = _residual_ln(x, g_ref, out, lg_ref, lb_ref)


def _sgu_layer(x, mod, tps, layer, j, w_in, ln_g, ln_b, ws, bias, w_out, ln1_g, ln1_b, tm, emit_v):
    T = x.shape[0]
    out_shape = [jax.ShapeDtypeStruct((T, D_MODEL), F32)]
    out_specs = [_row_spec(tm, D_MODEL)]
    if emit_v:
        out_shape.append(jax.ShapeDtypeStruct((T, D_MODEL), F32))
        out_specs.append(_row_spec(tm, D_MODEL))
    res = pl.pallas_call(
        functools.partial(_sgu_body, tm=tm, emit_v=emit_v),
        out_shape=out_shape,
        grid=(T // tm,),
        in_specs=[_row_spec(tm, D_MODEL),
                  _mod_spec(mod, layer, 0, tps), _mod_spec(mod, layer, 1, tps), _mod_spec(mod, layer, 2, tps),
                  _full_spec(w_in, j), _vec_spec(j, D_MODEL), _vec_spec(j, D_MODEL),
                  _full_spec(ws, j), _full_spec(bias, j), _full_spec(w_out, j),
                  _vec_spec(layer, D_MODEL), _vec_spec(layer, D_MODEL)],
        out_specs=out_specs,
        scratch_shapes=[pltpu.VMEM((tm, D_MODEL), BF16)],
        compiler_params=_cparams("arbitrary"),
        name="sgu_sublayer",
    )(x, mod, mod, mod, w_in, ln_g, ln_b, ws, bias, w_out, ln1_g, ln1_b)
    return (res[0], res[1]) if emit_v else (res[0], None)


def _hgrn_proj_body(x_ref, sh_ref, sc_ref, win_ref, lb_ref, q_ref, k_ref, lf_ref, v_ref, gs_ref):
    h = _modulate(x_ref[...], sh_ref, sc_ref).astype(BF16)
    d = D_MODEL
    q_ref[...] = _silu(_dot(h, win_ref[:, 0:d]))
    fz = _dot(h, win_ref[:, d:2 * d])
    lb = lb_ref[...]
    t = jnp.log1p(jnp.exp(-jnp.abs(fz)))
    a = jnp.log(lb)
    b = jnp.log1p(-lb) + (jnp.minimum(fz, 0.0) - t)
    lf_ref[...] = jnp.maximum(a, b) + jnp.log1p(jnp.exp(-jnp.abs(a - b)))
    k_ref[...] = (1.0 - lb) * jax.nn.sigmoid(-fz)
    v_ref[...] = _dot(h, win_ref[:, 2 * d:3 * d])
    gs_ref[...] = _silu(_dot(h, win_ref[:, 3 * d:4 * d]))


def _hgrn_proj(x, mod, tps, layer, j, w_in, lb, tm):
    T = x.shape[0]
    shp = jax.ShapeDtypeStruct((T, D_MODEL), F32)
    return pl.pallas_call(
        _hgrn_proj_body,
        out_shape=[shp] * 5,
        grid=(T // tm,),
        in_specs=[_row_spec(tm, D_MODEL), _mod_spec(mod, layer, 0, tps), _mod_spec(mod, layer, 1, tps),
                  _full_spec(w_in, j), _full_spec(lb)],
        out_specs=[_row_spec(tm, D_MODEL)] * 5,
        compiler_params=_cparams("arbitrary"),
        name="hgrn_proj",
    )(x, mod, mod, w_in, lb)


def _hgrn_rec_body(*refs, C, nchunk, has_s0):
    if has_s0:
        q_ref, k_ref, g_ref, v_ref, s0_ref, o_ref, sout_ref, st_ref = refs
    else:
        q_ref, k_ref, g_ref, v_ref, o_ref, sout_ref, st_ref = refs
    t = pl.program_id(2)

    @pl.when(t == 0)
    def _():
        if has_s0:
            st_ref[...] = s0_ref[...].T
        else:
            st_ref[...] = jnp.zeros_like(st_ref)

    row = lax.broadcasted_iota(jnp.int32, (C, B_DK), 0)
    row_a = lax.broadcasted_iota(jnp.int32, (C, C), 0)
    col_a = lax.broadcasted_iota(jnp.int32, (C, C), 1)
    tri = jnp.where(row_a >= col_a, 1.0, 0.0).astype(BF16)
    levels = [m for m in (8, 16, 32, 64, 128) if 2 * m <= C]

    def chunk(c, carry):
        r0 = pl.multiple_of(c * C, C)
        qc = q_ref[pl.ds(r0, C), :]
        kc = k_ref[pl.ds(r0, C), :]
        gc = g_ref[pl.ds(r0, C), :]
        vb = v_ref[pl.ds(r0, C), :].astype(BF16)
        g_hi = gc.astype(BF16)
        r1 = gc - g_hi.astype(F32)
        g_mid = r1.astype(BF16)
        g_lo = (r1 - g_mid.astype(F32)).astype(BF16)
        b3 = _dot(tri, jnp.concatenate([g_hi, g_mid, g_lo], axis=1))
        b = b3[:, 0:B_DK] + b3[:, B_DK:2 * B_DK] + b3[:, 2 * B_DK:3 * B_DK]
        b_last = b[C - 1:C, :]
        st = st_ref[...]
        o = _dot_nt((qc * jnp.exp(b)).astype(BF16), st.astype(BF16))
        a_mat = jnp.zeros((C, C), F32)
        for r in range(SUBLANES):
            k_r = kc if r == 0 else pltpu.roll(kc, r, 0)
            b_r = b if r == 0 else pltpu.roll(b, r, 0)
            valid = (row & (SUBLANES - 1)) >= r
            e = jnp.exp(jnp.where(valid, b - b_r, 0.0))
            term = jnp.where(valid, qc * k_r * e, 0.0)
            d_r = jnp.sum(term, axis=1, keepdims=True)
            a_mat = a_mat + jnp.where(col_a == row_a - r, d_r, 0.0)
        for m in levels:
            bref = jnp.concatenate(
                [jnp.broadcast_to(b[i * 2 * m + m - 1:i * 2 * m + m, :], (2 * m, B_DK)) for i in range(C // (2 * m))],
                axis=0)
            upper = (row & (2 * m - 1)) >= m
            q_up = jnp.where(upper, qc * jnp.exp(jnp.where(upper, b - bref, 0.0)), 0.0)
            k_lo = jnp.where(upper, 0.0, kc * jnp.exp(jnp.where(upper, 0.0, bref - b)))
            a_m = _dot_nt(q_up.astype(BF16), k_lo.astype(BF16))
            shift = int(math.log2(2 * m))
            a_mat = a_mat + jnp.where((row_a >> shift) == (col_a >> shift), a_m, 0.0)
        o = o + _dot(a_mat.astype(BF16), vb)
        o_ref[pl.ds(r0, C), :] = o
        k_dec = (kc * jnp.exp(b_last - b)).astype(BF16)
        st_ref[...] = st * jnp.exp(b_last) + _dot_tn(vb, k_dec)
        return carry

    lax.fori_loop(0, nchunk, chunk, 0)

    @pl.when(t == pl.num_programs(2) - 1)
    def _():
        sout_ref[...] = st_ref[...].T


def _hgrn_rec(q, k, lf, v, s0, n_seq, seq_len, tm, C):
    T = q.shape[0]
    nt = seq_len // tm
    blk = pl.BlockSpec((tm, B_DK), lambda b, h, t: (b * nt + t, h))
    st_spec = pl.BlockSpec((None, None, B_DK, B_DV), lambda b, h, t: (b, h, 0, 0))
    has_s0 = s0 is not None
    in_specs = [blk] * 4 + ([st_spec] if has_s0 else [])
    args = (q, k, lf, v) + ((s0,) if has_s0 else ())
    return pl.pallas_call(
        functools.partial(_hgrn_rec_body, C=C, nchunk=tm // C, has_s0=has_s0),
        out_shape=[jax.ShapeDtypeStruct((T, D_MODEL), F32),
                   jax.ShapeDtypeStruct((n_seq, B_HEADS, B_DK, B_DV), F32)],
        grid=(n_seq, B_HEADS, nt),
        in_specs=in_specs,
        out_specs=[blk, st_spec],
        scratch_shapes=[pltpu.VMEM((B_DV, B_DK), F32)],
        compiler_params=_cparams("arbitrary", "arbitrary", "arbitrary"),
        name="hgrn_recurrence",
    )(*args)


def _hgrn_out_body(x_ref, g_ref, o_ref, gs_ref, wout_ref, lg_ref, lb_ref, y_ref):
    o = o_ref[...]
    parts = [_rms(o[:, h * B_DV:(h + 1) * B_DV]) for h in range(B_HEADS)]
    y = (jnp.concatenate(parts, axis=1) * gs_ref[...]).astype(BF16)
    y_ref[...] = _residual_ln(x_ref[...], g_ref, _dot(y, wout_ref[...]), lg_ref, lb_ref)


def _hgrn_out(x, mod, tps, layer, j, o, gs, w_out, ln_g, ln_b, tm):
    T = x.shape[0]
    return pl.pallas_call(
        _hgrn_out_body,
        out_shape=jax.ShapeDtypeStruct((T, D_MODEL), F32),
        grid=(T // tm,),
        in_specs=[_row_spec(tm, D_MODEL), _mod_spec(mod, layer, 2, tps),
                  _row_spec(tm, D_MODEL), _row_spec(tm, D_MODEL),
                  _full_spec(w_out, j), _vec_spec(layer, D_MODEL), _vec_spec(layer, D_MODEL)],
        out_specs=_row_spec(tm, D_MODEL),
        compiler_params=_cparams("arbitrary"),
        name="hgrn_out",
    )(x, mod, o, gs, w_out, ln_g, ln_b)


def _rope_lanes(x, cc_ref, ss_ref, period_first_half):
    n = x.shape[1]
    half = C_ROPE // 2
    rot = jnp.where(period_first_half, pltpu.roll(x, n - half, 1), pltpu.roll(x, half, 1))
    return x * cc_ref[...] + rot * ss_ref[...]


def _mla_proj_body(x_ref, sh_ref, sc_ref, win_ref, gq_ref, gkv_ref, wn_ref, wr_ref, wuk_ref,
                   ccq_ref, ssq_ref, cck_ref, ssk_ref, q_ref, kcat_ref, lat_ref, kr_ref):
    h = _modulate(x_ref[...], sh_ref, sc_ref).astype(BF16)
    a = _dot(h, win_ref[...])
    cq = (_rms(a[:, :C_QLORA]) * gq_ref[...]).astype(BF16)
    ckv = _rms(a[:, C_QLORA:C_QLORA + C_KVLORA]) * gkv_ref[...]
    kr_slab = a[:, C_QLORA + C_KVLORA:]
    lane_k = lax.broadcasted_iota(jnp.int32, kr_slab.shape, 1)
    kr_slab = _rope_lanes(kr_slab, cck_ref, ssk_ref, (lane_k & (C_ROPE - 1)) < C_ROPE // 2)
    lat_ref[...] = ckv
    kr_ref[...] = kr_slab[:, :C_ROPE]
    kcat_ref[...] = jnp.concatenate([ckv, kr_slab], axis=1).astype(BF16)
    qn = _dot(cq, wn_ref[...]).astype(BF16)
    qr = _dot(cq, wr_ref[...])
    lane_q = lax.broadcasted_iota(jnp.int32, qr.shape, 1)
    qr = _rope_lanes(qr, ccq_ref, ssq_ref, (lane_q & (C_ROPE - 1)) < C_ROPE // 2).astype(BF16)
    zeros = jnp.zeros((qr.shape[0], LANES - C_ROPE), BF16)
    for hd in range(C_HEADS):
        ql = _dot(qn[:, hd * C_NOPE:(hd + 1) * C_NOPE], wuk_ref[hd]).astype(BF16)
        q_ref[hd] = jnp.concatenate([ql, qr[:, hd * C_ROPE:(hd + 1) * C_ROPE], zeros], axis=1)


def _mla_proj(x, mod, tps, tab_tiles, layer, j, w_in, g_q, g_kv, wn, wr, wuk, ccq, ssq, cck, ssk, tm):
    T = x.shape[0]
    nt = T // tm
    tab = lambda w: pl.BlockSpec((tm, w), lambda i: (i % tab_tiles, 0))
    return pl.pallas_call(
        _mla_proj_body,
        out_shape=[jax.ShapeDtypeStruct((nt, C_HEADS, tm, C_QK), BF16),
                   jax.ShapeDtypeStruct((T, C_QK), BF16),
                   jax.ShapeDtypeStruct((T, C_KVLORA), F32),
                   jax.ShapeDtypeStruct((T, C_ROPE), F32)],
        grid=(nt,),
        in_specs=[_row_spec(tm, D_MODEL), _mod_spec(mod, layer, 0, tps), _mod_spec(mod, layer, 1, tps),
                  _full_spec(w_in, j), _vec_spec(j, C_QLORA), _vec_spec(j, C_KVLORA),
                  _full_spec(wn, j), _full_spec(wr, j), _full_spec(wuk, j),
                  tab(C_HEADS * C_ROPE), tab(C_HEADS * C_ROPE), tab(LANES), tab(LANES)],
        out_specs=[pl.BlockSpec((None, C_HEADS, tm, C_QK), lambda i: (i, 0, 0, 0)),
                   _row_spec(tm, C_QK), _row_spec(tm, C_KVLORA), _row_spec(tm, C_ROPE)],
        compiler_params=_cparams("arbitrary"),
        name="mla_proj",
    )(x, mod, mod, w_in, g_q, g_kv, wn, wr, wuk, ccq, ssq, cck, ssk)


def _softmax_update(s, m_ref, l_ref, acc_ref, values):
    m_prev = m_ref[...]
    m_new = jnp.maximum(m_prev, jnp.max(s, axis=-1, keepdims=True))
    alpha = jnp.exp(m_prev - m_new)
    p = jnp.exp(s - m_new)
    l_ref[...] = alpha * l_ref[...] + jnp.sum(p, axis=-1, keepdims=True)
    acc_ref[...] = alpha * acc_ref[...] + _dot(p.astype(BF16), values)
    m_ref[...] = m_new


def _softmax_init(m_ref, l_ref, acc_ref):
    m_ref[...] = jnp.full_like(m_ref, -jnp.inf)
    l_ref[...] = jnp.zeros_like(l_ref)
    acc_ref[...] = jnp.zeros_like(acc_ref)


def _attn_body(qi_ref, kj_ref, q_ref, k_ref, o_ref, m_ref, l_ref, acc_ref, *, tq, scale):
    p_id = pl.program_id(1)
    qi = qi_ref[p_id]
    kj = kj_ref[p_id]

    @pl.when(kj == 0)
    def _():
        _softmax_init(m_ref, l_ref, acc_ref)

    q = q_ref[...].reshape(C_HEADS * tq, C_QK)
    k = k_ref[...]
    s = _dot_nt(q, k) * scale
    tok = lax.broadcasted_iota(jnp.int32, s.shape, 0) & (tq - 1)
    key = lax.broadcasted_iota(jnp.int32, s.shape, 1)
    s = jnp.where(key + kj * tq <= tok + qi * tq, s, -jnp.inf)
    _softmax_update(s, m_ref, l_ref, acc_ref, k[:, :C_KVLORA])

    @pl.when(kj == qi)
    def _():
        o_ref[...] = (acc_ref[...] / l_ref[...]).reshape(C_HEADS, tq, C_KVLORA).astype(BF16)


def _attn_prompt(q, kcat, n_seq, seq_len, tq):
    nq = seq_len // tq
    pairs = [(i, j) for i in range(nq) for j in range(i + 1)]
    qi = jnp.asarray([p[0] for p in pairs], jnp.int32)
    kj = jnp.asarray([p[1] for p in pairs], jnp.int32)
    scale = (C_NOPE + C_ROPE) ** -0.5
    grid_spec = pltpu.PrefetchScalarGridSpec(
        num_scalar_prefetch=2,
        grid=(n_seq, len(pairs)),
        in_specs=[pl.BlockSpec((None, C_HEADS, tq, C_QK), lambda b, p, qi, kj: (b * nq + qi[p], 0, 0, 0)),
                  pl.BlockSpec((tq, C_QK), lambda b, p, qi, kj: (b * nq + kj[p], 0))],
        out_specs=pl.BlockSpec((None, C_HEADS, tq, C_KVLORA), lambda b, p, qi, kj: (b * nq + qi[p], 0, 0, 0)),
        scratch_shapes=[pltpu.VMEM((C_HEADS * tq, 1), F32), pltpu.VMEM((C_HEADS * tq, 1), F32),
                        pltpu.VMEM((C_HEADS * tq, C_KVLORA), F32)])
    return pl.pallas_call(
        functools.partial(_attn_body, tq=tq, scale=scale),
        out_shape=jax.ShapeDtypeStruct((n_seq * nq, C_HEADS, tq, C_KVLORA), BF16),
        grid_spec=grid_spec,
        compiler_params=_cparams("arbitrary", "arbitrary"),
        name="mla_attention_prompt",
    )(qi, kj, q, kcat)


def _attn_paged_body(pt_ref, q_ref, nlat_ref, nrope_ref, *rest, pages, seq_new, scale):
    lat_refs = rest[:pages]
    rope_refs = rest[pages:2 * pages]
    o_ref, m_ref, l_ref, acc_ref = rest[2 * pages:]
    step = pl.program_id(1)

    @pl.when(step == 0)
    def _():
        _softmax_init(m_ref, l_ref, acc_ref)

    q = q_ref[...]
    ql = q[:, :C_KVLORA]
    qr = q[:, C_KVLORA:C_KVLORA + C_ROPE]
    lat = jnp.concatenate([r[...] for r in lat_refs], axis=0).astype(BF16)
    rp = jnp.concatenate([r[...] for r in rope_refs], axis=0).astype(BF16)
    s = (_dot_nt(ql, lat) + _dot_nt(qr, rp)) * scale
    _softmax_update(s, m_ref, l_ref, acc_ref, lat)

    @pl.when(step == pl.num_programs(1) - 1)
    def _():
        nlat = nlat_ref[...].astype(BF16)
        s2 = (_dot_nt(ql, nlat) + _dot_nt(qr, nrope_ref[...].astype(BF16))) * scale
        tok = lax.broadcasted_iota(jnp.int32, s2.shape, 0) & (seq_new - 1)
        key = lax.broadcasted_iota(jnp.int32, s2.shape, 1)
        s2 = jnp.where(key <= tok, s2, -jnp.inf)
        _softmax_update(s2, m_ref, l_ref, acc_ref, nlat)
        o_ref[...] = (acc_ref[...] / l_ref[...]).astype(BF16)


def _attn_paged(q, new_lat, new_rope, pool_lat, pool_rope, page_table, j, seq_new):
    n_seq, n_pages = page_table.shape
    page = pool_lat.shape[2]
    pages = 16
    rows = q.shape[1]
    scale = (C_NOPE + C_ROPE) ** -0.5

    def pool_spec(width, i):
        return pl.BlockSpec((None, None, page, width), lambda b, s, pt: (j, pt[b, s * pages + i], 0, 0))

    grid_spec = pltpu.PrefetchScalarGridSpec(
        num_scalar_prefetch=1,
        grid=(n_seq, n_pages // pages),
        in_specs=[pl.BlockSpec((None, rows, C_QK), lambda b, s, pt: (b, 0, 0)),
                  pl.BlockSpec((None,) + new_lat.shape[1:], lambda b, s, pt: (b, 0, 0)),
                  pl.BlockSpec((None,) + new_rope.shape[1:], lambda b, s, pt: (b, 0, 0))]
                 + [pool_spec(C_KVLORA, i) for i in range(pages)]
                 + [pool_spec(C_ROPE, i) for i in range(pages)],
        out_specs=pl.BlockSpec((None, rows, C_KVLORA), lambda b, s, pt: (b, 0, 0)),
        scratch_shapes=[pltpu.VMEM((rows, 1), F32), pltpu.VMEM((rows, 1), F32),
                        pltpu.VMEM((rows, C_KVLORA), F32)])
    return pl.pallas_call(
        functools.partial(_attn_paged_body, pages=pages, seq_new=seq_new, scale=scale),
        out_shape=jax.ShapeDtypeStruct((n_seq, rows, C_KVLORA), BF16),
        grid_spec=grid_spec,
        compiler_params=_cparams("arbitrary", "arbitrary"),
        name="mla_attention_paged",
    )(page_table, q, new_lat, new_rope, *([pool_lat] * pages), *([pool_rope] * pages))


def _mla_out_body(x_ref, g_ref, o_ref, wuv_ref, wout_ref, lg_ref, lb_ref, y_ref):
    parts = [_dot(o_ref[hd], wuv_ref[hd]) for hd in range(C_HEADS)]
    o = jnp.concatenate(parts, axis=1).astype(BF16)
    y_ref[...] = _residual_ln(x_ref[...], g_ref, _dot(o, wout_ref[...]), lg_ref, lb_ref)


def _mla_out(x, mod, tps, layer, j, o_lat, wuv, w_out, ln_g, ln_b, tm):
    T = x.shape[0]
    return pl.pallas_call(
        _mla_out_body,
        out_shape=jax.ShapeDtypeStruct((T, D_MODEL), F32),
        grid=(T // tm,),
        in_specs=[_row_spec(tm, D_MODEL), _mod_spec(mod, layer, 2, tps),
                  pl.BlockSpec((None, C_HEADS, tm, C_KVLORA), lambda i: (i, 0, 0, 0)),
                  _full_spec(wuv, j), _full_spec(w_out, j),
                  _vec_spec(layer, D_MODEL), _vec_spec(layer, D_MODEL)],
        out_specs=_row_spec(tm, D_MODEL),
        compiler_params=_cparams("arbitrary"),
        name="mla_out",
    )(x, mod, o_lat, wuv, w_out, ln_g, ln_b)


def _rope_tables(pos, reps, width):
    half = C_ROPE // 2
    inv = ROPE_THETA ** (-jnp.arange(half, dtype=F32) / half)
    ang = pos.astype(F32)[:, None] * inv
    cos, sin = jnp.cos(ang), jnp.sin(ang)
    cc = jnp.tile(jnp.concatenate([cos, cos], axis=1), (1, reps))
    ss = jnp.tile(jnp.concatenate([-sin, sin], axis=1), (1, reps))
    pad = width - cc.shape[1]
    return jnp.pad(cc, ((0, 0), (0, pad))), jnp.pad(ss, ((0, 0), (0, pad)))


def _prepare_params(p):
    vec = lambda a: a.reshape(a.shape[0], 1, a.shape[1])
    w_uq = p['c_w_uq']
    n_c = w_uq.shape[0]
    c_w_in = jnp.pad(p['c_w_in'], ((0, 0), (0, 0), (0, LANES - C_ROPE)))
    lb_all = jnp.cumsum(jax.nn.softmax(p['b_lb'].astype(F32), axis=0), axis=0)
    lb_all = lb_all - lb_all[:1]
    return dict(
        ln1_g=vec(p['ln1_g']), ln1_b=vec(p['ln1_b']), ln2_g=vec(p['ln2_g']), ln2_b=vec(p['ln2_b']),
        ffn_w1=p['ffn_w1'].astype(BF16), ffn_w2=p['ffn_w2'].astype(BF16),
        a_w_in=p['a_w_in'].astype(BF16), a_ln_g=vec(p['a_ln_g']), a_ln_b=vec(p['a_ln_b']),
        a_w_out=p['a_w_out'].astype(BF16),
        b_w_in=p['b_w_in'].astype(BF16), b_w_out=p['b_w_out'].astype(BF16), lb_all=lb_all,
        c_w_in=c_w_in.astype(BF16), c_g_q=vec(p['c_g_q']), c_g_kv=vec(p['c_g_kv']),
        c_wn=w_uq[..., :C_NOPE].reshape(n_c, C_QLORA, C_HEADS * C_NOPE).astype(BF16),
        c_wr=w_uq[..., C_NOPE:].reshape(n_c, C_QLORA, C_HEADS * C_ROPE).astype(BF16),
        c_wuk=jnp.transpose(p['c_w_uk'], (0, 2, 3, 1)).astype(BF16),
        c_wuv=jnp.transpose(p['c_w_uv'], (0, 2, 1, 3)).astype(BF16),
        c_w_out=p['c_w_out'].astype(BF16),
    )


def _sgu_mixing(w_s, b_s, chunk):
    reps = CHUNK_A // chunk
    causal = jnp.tril(jnp.ones((chunk, chunk), dtype=bool))
    ws = jnp.where(causal, w_s[:, :, :chunk, :chunk], 0)
    eye = jnp.eye(reps, dtype=w_s.dtype)
    ws = jnp.einsum('ab,jgts->jgatbs', eye, ws).reshape(w_s.shape[0], A_GROUPS, CHUNK_A, CHUNK_A)
    bias = jnp.tile(jnp.transpose(b_s[:, :, :chunk], (0, 2, 1)), (1, reps, 1))
    bias = jnp.repeat(bias, A_GDIM, axis=2)
    return ws.astype(BF16), bias


def _run_trunk(x, mod, n_seq, seq_len, q_pos, hgrn_state0, mla_cache, prm, raw, tm):
    T = x.shape[0]
    per_seq_mod = mod.shape[3] == 1
    tps = (seq_len // tm) if per_seq_mod else 1
    sgu_chunk = min(CHUNK_A, seq_len)
    ws, bias = _sgu_mixing(raw['a_w_s'], raw['a_b_s'], sgu_chunk)
    chunk_v, hgrn_states, lat_rows, rope_rows = [], [], [], []
    for i in range(DEPTH):
        kind, j = i % N_MIXERS, i // N_MIXERS
        if kind == 0:
            x, v_rows = _sgu_layer(x, mod, tps, i, j, prm['a_w_in'], prm['a_ln_g'], prm['a_ln_b'], ws, bias,
                                   prm['a_w_out'], prm['ln1_g'], prm['ln1_b'], tm, emit_v=mla_cache is not None)
            chunk_v.append(v_rows)
        elif kind == 1:
            lb = prm['lb_all'][i].reshape(1, D_MODEL)
            q, k, lf, v, gs = _hgrn_proj(x, mod, tps, i, j, prm['b_w_in'], lb, tm)
            if seq_len % 64 == 0:
                C, lpad = 64, seq_len
                rec_tm = min(seq_len, 512)
                rec_in = (q, k, lf, v)
            else:
                C = lpad = rec_tm = SUBLANES
                padseq = lambda a: jnp.pad(a.reshape(n_seq, seq_len, D_MODEL),
                                           ((0, 0), (0, lpad - seq_len), (0, 0))).reshape(n_seq * lpad, D_MODEL)
                rec_in = tuple(padseq(a) for a in (q, k, lf, v))
            s0 = None if hgrn_state0 is None else hgrn_state0[j]
            o, S = _hgrn_rec(*rec_in, s0, n_seq, lpad, rec_tm, C)
            if lpad != seq_len:
                o = o.reshape(n_seq, lpad, D_MODEL)[:, :seq_len].reshape(T, D_MODEL)
            hgrn_states.append(S)
            x = _hgrn_out(x, mod, tps, i, j, o, gs, prm['b_w_out'], prm['ln1_g'], prm['ln1_b'], tm)
        else:
            tq = min(tm, 256)
            tps_q = (seq_len // tq) if per_seq_mod else 1
            pos_rows = q_pos if tq <= seq_len else jnp.tile(q_pos, tq // seq_len)
            ccq, ssq = _rope_tables(pos_rows, C_HEADS, C_HEADS * C_ROPE)
            cck, ssk = _rope_tables(pos_rows, 1, LANES)
            qcat, kcat, lat, kr = _mla_proj(x, mod, tps_q, pos_rows.shape[0] // tq, i, j, prm['c_w_in'],
                                            prm['c_g_q'], prm['c_g_kv'], prm['c_wn'], prm['c_wr'], prm['c_wuk'],
                                            ccq, ssq, cck, ssk, tq)
            if mla_cache is None:
                o_lat = _attn_prompt(qcat, kcat, n_seq, seq_len, tq)
            else:
                pool_lat, pool_rope, pt = mla_cache
                qs = qcat.reshape(C_HEADS, n_seq, seq_len, C_QK).transpose(1, 0, 2, 3)
                qs = qs.reshape(n_seq, C_HEADS * seq_len, C_QK)
                padk = lambda a: jnp.pad(a.reshape(n_seq, seq_len, a.shape[1]), ((0, 0), (0, 16 - seq_len), (0, 0)))
                o_s = _attn_paged(qs, padk(lat), padk(kr), pool_lat, pool_rope, pt, j, seq_len)
                o_lat = o_s.reshape(n_seq, C_HEADS, seq_len, C_KVLORA).transpose(1, 0, 2, 3)
                o_lat = o_lat.reshape(1, C_HEADS, T, C_KVLORA)
            x = _mla_out(x, mod, tps_q, i, j, o_lat, prm['c_wuv'], prm['c_w_out'], prm['ln1_g'], prm['ln1_b'], tq)
            lat_rows.append(lat.reshape(n_seq, seq_len, C_KVLORA))
            rope_rows.append(kr.reshape(n_seq, seq_len, C_ROPE))
        x = _ffn_layer(x, mod, tps, i, prm['ffn_w1'], prm['ffn_w2'], prm['ln2_g'], prm['ln2_b'], tm)
    stack = lambda xs: jnp.stack(xs) if xs and xs[0] is not None else None
    return x, stack(chunk_v), jnp.stack(hgrn_states), jnp.stack(lat_rows), jnp.stack(rope_rows)


def kernel(x_prompt, x_sample, cache_kv_latent, cache_k_rope, state_hgrn, page_table, c_prompt, c_sample,
           w_ada, b_ada, ln1_g, ln1_b, ln2_g, ln2_b, ffn_w1, ffn_w2, a_w_in, a_ln_g, a_ln_b, a_w_s, a_b_s,
           a_w_out, b_w_in, b_lb, b_w_out, c_w_in, c_g_q, c_g_kv, c_w_uq, c_w_uk, c_w_uv, c_w_out):
    raw = dict(ln1_g=ln1_g, ln1_b=ln1_b, ln2_g=ln2_g, ln2_b=ln2_b, ffn_w1=ffn_w1, ffn_w2=ffn_w2,
               a_w_in=a_w_in, a_ln_g=a_ln_g, a_ln_b=a_ln_b, a_w_s=a_w_s, a_b_s=a_b_s, a_w_out=a_w_out,
               b_w_in=b_w_in, b_lb=b_lb, b_w_out=b_w_out, c_w_in=c_w_in, c_g_q=c_g_q, c_g_kv=c_g_kv,
               c_w_uq=c_w_uq, c_w_uk=c_w_uk, c_w_uv=c_w_uv, c_w_out=c_w_out)
    prm = _prepare_params(raw)
    nb, seq, d = x_prompt.shape
    ns, sseq, _ = x_sample.shape
    past_len = page_table.shape[1] * cache_kv_latent.shape[2]
    pos_prompt = jnp.arange(seq, dtype=jnp.int32)
    pos_sample = past_len + jnp.arange(sseq, dtype=jnp.int32)

    mod = _modulation(jnp.concatenate([c_prompt, c_sample], axis=0), w_ada, b_ada)
    mod = mod.reshape(DEPTH, nb + ns, 6, d).transpose(0, 2, 1, 3)
    mod_p = mod[:, :, :nb].reshape(DEPTH, 6, nb, 1, d)
    mod_s = jnp.repeat(mod[:, :, nb:], sseq, axis=2).reshape(DEPTH, 6, 1, ns * sseq, d)

    tm_p = 512
    y_p, _, hs_p, lat_p, rope_p = _run_trunk(x_prompt.reshape(nb * seq, d), mod_p, nb, seq, pos_prompt,
                                             None, None, prm, raw, tm_p)
    y_s, v_s, hs_s, lat_s, rope_s = _run_trunk(x_sample.reshape(ns * sseq, d), mod_s, ns, sseq, pos_sample,
                                               state_hgrn, (cache_kv_latent, cache_k_rope, page_table),
                                               prm, raw, ns * sseq)
    return (y_p.reshape(nb, seq, d), y_s.reshape(ns, sseq, d), hs_p, hs_s, lat_p, rope_p, lat_s, rope_s,
            v_s.reshape(v_s.shape[0], ns, sseq, d))
```

```python
import functools
import math

import jax
import jax.numpy as jnp
from jax import lax
from jax.experimental import pallas as pl
from jax.experimental.pallas import tpu as pltpu

F32 = jnp.float32
BF16 = jnp.bfloat16

D_MODEL = 1024
DEPTH = 4
N_MIXERS = 3
CHUNK_A = 128
A_GROUPS = 8
A_GDIM = D_MODEL // A_GROUPS
B_HEADS = 8
B_DK = 128
B_DV = D_MODEL // B_HEADS
C_HEADS = 8
C_NOPE = 128
C_ROPE = 64
C_V = 128
C_QLORA = 512
C_KVLORA = 256
ROPE_THETA = 10000.0
D_FF = 4 * D_MODEL
ALPHA = (2.0 * DEPTH) ** 0.25
EPS = 1e-6

LANES = 128
SUBLANES = 8
C_QK = C_KVLORA + LANES
VMEM_LIMIT = 56 * 1024 * 1024


def _cparams(*sem):
    return pltpu.CompilerParams(dimension_semantics=sem, vmem_limit_bytes=VMEM_LIMIT)


def _dot(a, b):
    return jnp.dot(a, b, preferred_element_type=F32)


def _dot_nt(a, b):
    return lax.dot_general(a, b, (((1,), (1,)), ((), ())), preferred_element_type=F32)


def _dot_tn(a, b):
    return lax.dot_general(a, b, (((0,), (0,)), ((), ())), preferred_element_type=F32)


def _layer_norm(y, g, b):
    mu = jnp.mean(y, axis=-1, keepdims=True)
    yc = y - mu
    var = jnp.mean(yc * yc, axis=-1, keepdims=True)
    return yc * lax.rsqrt(var + EPS) * g + b


def _rms(y):
    return y * lax.rsqrt(jnp.mean(y * y, axis=-1, keepdims=True) + EPS)


def _silu(x):
    return x * jax.nn.sigmoid(x)


def _gelu_tanh(x):
    return 0.5 * x * (1.0 + jnp.tanh(math.sqrt(2.0 / math.pi) * (x + 0.044715 * (x * x * x))))


def _modulate(x, sh_ref, sc_ref):
    return x * (1.0 + sc_ref[...]) + sh_ref[...]


def _residual_ln(x, gate_ref, out, lg_ref, lb_ref):
    return _layer_norm(ALPHA * x + gate_ref[...] * out, lg_ref[...], lb_ref[...])


def _mod_spec(mod, layer, which, tiles_per_seq):
    rows = mod.shape[3]
    return pl.BlockSpec((None, None, None, rows, D_MODEL),
                        lambda i: (layer, which, i // tiles_per_seq, 0, 0))


def _vec_spec(layer, width):
    return pl.BlockSpec((None, 1, width), lambda i: (layer, 0, 0))


def _full_spec(arr, layer=None):
    if layer is None:
        nd = arr.ndim
        return pl.BlockSpec(arr.shape, lambda i: (0,) * nd)
    nd = arr.ndim - 1
    return pl.BlockSpec((None,) + arr.shape[1:], lambda i: (layer,) + (0,) * nd)


def _row_spec(tm, width):
    return pl.BlockSpec((tm, width), lambda i: (i, 0))


def _mod_body(c_ref, w_ref, b_ref, o_ref):
    sc = _silu(c_ref[...]).astype(BF16)
    o_ref[...] = _dot(sc, w_ref[...].astype(BF16)) + b_ref[...]


def _modulation(c_all, w_ada, b_ada):
    n = c_all.shape[0]
    tn = 1536
    width = w_ada.shape[2]
    return pl.pallas_call(
        _mod_body,
        out_shape=jax.ShapeDtypeStruct((DEPTH, n, width), F32),
        grid=(DEPTH, width // tn),
        in_specs=[pl.BlockSpec((n, D_MODEL), lambda l, j: (0, 0)),
                  pl.BlockSpec((None, D_MODEL, tn), lambda l, j: (l, 0, j)),
                  pl.BlockSpec((None, 1, tn), lambda l, j: (l, 0, j))],
        out_specs=pl.BlockSpec((None, n, tn), lambda l, j: (l, 0, j)),
        compiler_params=_cparams("arbitrary", "arbitrary"),
        name="adaln_modulation",
    )(c_all, w_ada, b_ada.reshape(DEPTH, 1, width))


def _ffn_body(x_ref, sh_ref, sc_ref, g_ref, w1_ref, w2_ref, lg_ref, lb_ref, o_ref, acc_ref, *, fc):
    x = x_ref[...]
    h = _modulate(x, sh_ref, sc_ref).astype(BF16)
    for c in range(D_FF // fc):
        a = _dot(h, w1_ref[:, c * fc:(c + 1) * fc])
        a = jnp.square(jnp.maximum(a, 0.0)).astype(BF16)
        d = _dot(a, w2_ref[c * fc:(c + 1) * fc, :])
        if c == 0:
            acc_ref[...] = d
        else:
            acc_ref[...] += d
    o_ref[...] = _residual_ln(x, g_ref, acc_ref[...], lg_ref, lb_ref)


def _ffn_layer(x, mod, tps, layer, w1, w2, ln_g, ln_b, tm):
    T = x.shape[0]
    return pl.pallas_call(
        functools.partial(_ffn_body, fc=1024),
        out_shape=jax.ShapeDtypeStruct((T, D_MODEL), F32),
        grid=(T // tm,),
        in_specs=[_row_spec(tm, D_MODEL),
                  _mod_spec(mod, layer, 3, tps), _mod_spec(mod, layer, 4, tps), _mod_spec(mod, layer, 5, tps),
                  _full_spec(w1, layer), _full_spec(w2, layer),
                  _vec_spec(layer, D_MODEL), _vec_spec(layer, D_MODEL)],
        out_specs=_row_spec(tm, D_MODEL),
        scratch_shapes=[pltpu.VMEM((tm, D_MODEL), F32)],
        compiler_params=_cparams("arbitrary"),
        name="ffn_sublayer",
    )(x, mod, mod, mod, w1, w2, ln_g, ln_b)


def _sgu_body(x_ref, sh_ref, sc_ref, g_ref, win_ref, lng_ref, lnb_ref, ws_ref, bias_ref, wout_ref,
              lg_ref, lb_ref, o_ref, *rest, tm, emit_v):
    if emit_v:
        v_ref, gated_ref = rest
    else:
        (gated_ref,) = rest
    x = x_ref[...]
    h = _modulate(x, sh_ref, sc_ref).astype(BF16)
    u = _gelu_tanh(_dot(h, win_ref[:, :D_MODEL]))
    v = _gelu_tanh(_dot(h, win_ref[:, D_MODEL:]))
    v = _layer_norm(v, lng_ref[...], lnb_ref[...])
    if emit_v:
        v_ref[...] = v
    vb = v.astype(BF16)
    for n in range(tm // CHUNK_A):
        r = slice(n * CHUNK_A, (n + 1) * CHUNK_A)
        cols = [_dot(ws_ref[g], vb[r, g * A_GDIM:(g + 1) * A_GDIM]) for g in range(A_GROUPS)]
        mixed = jnp.concatenate(cols, axis=1) + bias_ref[...]
        gated_ref[r, :] = (u[r, :] * mixed).astype(BF16)
    out = _dot(gated_ref[...], wout_ref[...])
    o_ref[...] = _residual_ln(x, g_ref, out, lg_ref, lb_ref)


def _sgu_layer(x, mod, tps, layer, j, w_in, ln_g, ln_b, ws, bias, w_out, ln1_g, ln1_b, tm, emit_v):
    T = x.shape[0]
    out_shape = [jax.ShapeDtypeStruct((T, D_MODEL), F32)]
    out_specs = [_row_spec(tm, D_MODEL)]
    if emit_v:
        out_shape.append(jax.ShapeDtypeStruct((T, D_MODEL), F32))
        out_specs.append(_row_spec(tm, D_MODEL))
    res = pl.pallas_call(
        functools.partial(_sgu_body, tm=tm, emit_v=emit_v),
        out_shape=out_shape,
        grid=(T // tm,),
        in_specs=[_row_spec(tm, D_MODEL),
                  _mod_spec(mod, layer, 0, tps), _mod_spec(mod, layer, 1, tps), _mod_spec(mod, layer, 2, tps),
                  _full_spec(w_in, j), _vec_spec(j, D_MODEL), _vec_spec(j, D_MODEL),
                  _full_spec(ws, j), _full_spec(bias, j), _full_spec(w_out, j),
                  _vec_spec(layer, D_MODEL), _vec_spec(layer, D_MODEL)],
        out_specs=out_specs,
        scratch_shapes=[pltpu.VMEM((tm, D_MODEL), BF16)],
        compiler_params=_cparams("arbitrary"),
        name="sgu_sublayer",
    )(x, mod, mod, mod, w_in, ln_g, ln_b, ws, bias, w_out, ln1_g, ln1_b)
    return (res[0], res[1]) if emit_v else (res[0], None)


def _hgrn_proj_body(x_ref, sh_ref, sc_ref, win_ref, lb_ref, q_ref, k_ref, lf_ref, v_ref, gs_ref):
    h = _modulate(x_ref[...], sh_ref, sc_ref).astype(BF16)
    d = D_MODEL
    q_ref[...] = _silu(_dot(h, win_ref[:, 0:d]))
    fz = _dot(h, win_ref[:, d:2 * d])
    lb = lb_ref[...]
    t = jnp.log1p(jnp.exp(-jnp.abs(fz)))
    a = jnp.log(lb)
    b = jnp.log1p(-lb) + (jnp.minimum(fz, 0.0) - t)
    lf_ref[...] = (jnp.maximum(a, b) + jnp.log1p(jnp.exp(-jnp.abs(a - b)))) * math.log2(math.e)
    k_ref[...] = (1.0 - lb) * jax.nn.sigmoid(-fz)
    v_ref[...] = _dot(h, win_ref[:, 2 * d:3 * d])
    gs_ref[...] = _silu(_dot(h, win_ref[:, 3 * d:4 * d]))


def _hgrn_proj(x, mod, tps, layer, j, w_in, lb, tm):
    T = x.shape[0]
    shp = jax.ShapeDtypeStruct((T, D_MODEL), F32)
    return pl.pallas_call(
        _hgrn_proj_body,
        out_shape=[shp] * 5,
        grid=(T // tm,),
        in_specs=[_row_spec(tm, D_MODEL), _mod_spec(mod, layer, 0, tps), _mod_spec(mod, layer, 1, tps),
                  _full_spec(w_in, j), _full_spec(lb)],
        out_specs=[_row_spec(tm, D_MODEL)] * 5,
        compiler_params=_cparams("arbitrary"),
        name="hgrn_proj",
    )(x, mod, mod, w_in, lb)


def _hgrn_rec_body(*refs, C, nchunk, has_s0):
    if has_s0:
        q_ref, k_ref, g_ref, v_ref, s0_ref, o_ref, sout_ref, st_ref = refs
    else:
        q_ref, k_ref, g_ref, v_ref, o_ref, sout_ref, st_ref = refs
    t = pl.program_id(2)

    @pl.when(t == 0)
    def _():
        if has_s0:
            st_ref[...] = s0_ref[...].T
        else:
            st_ref[...] = jnp.zeros_like(st_ref)

    row = lax.broadcasted_iota(jnp.int32, (C, B_DK), 0)
    row_a = lax.broadcasted_iota(jnp.int32, (C, C), 0)
    col_a = lax.broadcasted_iota(jnp.int32, (C, C), 1)
    tri = jnp.where(row_a >= col_a, 1.0, 0.0).astype(BF16)
    band = [(col_a == row_a - r) & ((row_a & (SUBLANES - 1)) >= r) for r in range(SUBLANES)]
    levels = []
    for m in (8, 16, 32, 64, 128):
        if 2 * m <= C:
            shift = int(math.log2(2 * m))
            pair = (((row_a >> shift) == (col_a >> shift)) & ((row_a & (2 * m - 1)) >= m)
                    & ((col_a & (2 * m - 1)) < m))
            levels.append((m, (row & (2 * m - 1)) >= m, pair))

    st = st_ref[...]
    for c in range(nchunk):
        rows = slice(c * C, (c + 1) * C)
        qc = q_ref[rows, :]
        kc = k_ref[rows, :]
        gc = g_ref[rows, :]
        vb = v_ref[rows, :].astype(BF16)
        g_hi = gc.astype(BF16)
        r1 = gc - g_hi.astype(F32)
        g_mid = r1.astype(BF16)
        g_lo = (r1 - g_mid.astype(F32)).astype(BF16)
        b3 = _dot(tri, jnp.concatenate([g_hi, g_mid, g_lo], axis=1))
        b = b3[:, 0:B_DK] + b3[:, B_DK:2 * B_DK] + b3[:, 2 * B_DK:3 * B_DK]
        b_last = b[C - 1:C, :]
        o = _dot_nt((qc * jnp.exp2(b)).astype(BF16), st.astype(BF16))
        a_mat = jnp.zeros((C, C), F32)
        for m, upper, pair in levels:
            bref = jnp.concatenate(
                [jnp.broadcast_to(b[i * 2 * m + m - 1:i * 2 * m + m, :], (2 * m, B_DK)) for i in range(C // (2 * m))],
                axis=0)
            q_up = jnp.where(upper, qc * jnp.exp2(b - bref), 0.0)
            k_lo = jnp.where(upper, 0.0, kc * jnp.exp2(bref - b))
            a_mat = jnp.where(pair, _dot_nt(q_up.astype(BF16), k_lo.astype(BF16)), a_mat)
        for r in range(SUBLANES):
            k_r = kc if r == 0 else pltpu.roll(kc, r, 0)
            b_r = b if r == 0 else pltpu.roll(b, r, 0)
            d_r = jnp.sum(qc * k_r * jnp.exp2(b - b_r), axis=1, keepdims=True)
            a_mat = jnp.where(band[r], d_r, a_mat)
        o_ref[rows, :] = o + _dot(a_mat.astype(BF16), vb)
        k_dec = (kc * jnp.exp2(b_last - b)).astype(BF16)
        st = st * jnp.exp2(b_last) + _dot_tn(vb, k_dec)
    st_ref[...] = st

    @pl.when(t == pl.num_programs(2) - 1)
    def _():
        sout_ref[...] = st_ref[...].T


def _hgrn_rec(q, k, lf, v, s0, n_seq, seq_len, tm, C):
    T = q.shape[0]
    nt = seq_len // tm
    blk = pl.BlockSpec((tm, B_DK), lambda b, h, t: (b * nt + t, h))
    st_spec = pl.BlockSpec((None, None, B_DK, B_DV), lambda b, h, t: (b, h, 0, 0))
    has_s0 = s0 is not None
    in_specs = [blk] * 4 + ([st_spec] if has_s0 else [])
    args = (q, k, lf, v) + ((s0,) if has_s0 else ())
    return pl.pallas_call(
        functools.partial(_hgrn_rec_body, C=C, nchunk=tm // C, has_s0=has_s0),
        out_shape=[jax.ShapeDtypeStruct((T, D_MODEL), F32),
                   jax.ShapeDtypeStruct((n_seq, B_HEADS, B_DK, B_DV), F32)],
        grid=(n_seq, B_HEADS, nt),
        in_specs=in_specs,
        out_specs=[blk, st_spec],
        scratch_shapes=[pltpu.VMEM((B_DV, B_DK), F32)],
        compiler_params=_cparams("arbitrary", "arbitrary", "arbitrary"),
        name="hgrn_recurrence",
    )(*args)


def _hgrn_out_body(x_ref, g_ref, o_ref, gs_ref, wout_ref, lg_ref, lb_ref, y_ref):
    o = o_ref[...]
    parts = [_rms(o[:, h * B_DV:(h + 1) * B_DV]) for h in range(B_HEADS)]
    y = (jnp.concatenate(parts, axis=1) * gs_ref[...]).astype(BF16)
    y_ref[...] = _residual_ln(x_ref[...], g_ref, _dot(y, wout_ref[...]), lg_ref, lb_ref)


def _hgrn_out(x, mod, tps, layer, j, o, gs, w_out, ln_g, ln_b, tm):
    T = x.shape[0]
    return pl.pallas_call(
        _hgrn_out_body,
        out_shape=jax.ShapeDtypeStruct((T, D_MODEL), F32),
        grid=(T // tm,),
        in_specs=[_row_spec(tm, D_MODEL), _mod_spec(mod, layer, 2, tps),
                  _row_spec(tm, D_MODEL), _row_spec(tm, D_MODEL),
                  _full_spec(w_out, j), _vec_spec(layer, D_MODEL), _vec_spec(layer, D_MODEL)],
        out_specs=_row_spec(tm, D_MODEL),
        compiler_params=_cparams("arbitrary"),
        name="hgrn_out",
    )(x, mod, o, gs, w_out, ln_g, ln_b)


def _rope_lanes(x, cc_ref, ss_ref, period_first_half):
    n = x.shape[1]
    half = C_ROPE // 2
    rot = jnp.where(period_first_half, pltpu.roll(x, n - half, 1), pltpu.roll(x, half, 1))
    return x * cc_ref[...] + rot * ss_ref[...]


def _mla_proj_body(x_ref, sh_ref, sc_ref, win_ref, gq_ref, gkv_ref, wn_ref, wr_ref, wuk_ref,
                   ccq_ref, ssq_ref, cck_ref, ssk_ref, q_ref, kcat_ref, klt_ref, lat_ref, kr_ref):
    h = _modulate(x_ref[...], sh_ref, sc_ref).astype(BF16)
    a = _dot(h, win_ref[...])
    cq = (_rms(a[:, :C_QLORA]) * gq_ref[...]).astype(BF16)
    ckv = _rms(a[:, C_QLORA:C_QLORA + C_KVLORA]) * gkv_ref[...]
    kr_slab = a[:, C_QLORA + C_KVLORA:]
    lane_k = lax.broadcasted_iota(jnp.int32, kr_slab.shape, 1)
    kr_slab = _rope_lanes(kr_slab, cck_ref, ssk_ref, (lane_k & (C_ROPE - 1)) < C_ROPE // 2)
    lat_ref[...] = ckv
    kr_ref[...] = kr_slab[:, :C_ROPE]
    kcat_ref[...] = jnp.concatenate([ckv, kr_slab], axis=1).astype(BF16)
    klt_ref[...] = ckv.T.astype(BF16)
    qn = _dot(cq, wn_ref[...]).astype(BF16)
    qr = _dot(cq, wr_ref[...])
    lane_q = lax.broadcasted_iota(jnp.int32, qr.shape, 1)
    qr = _rope_lanes(qr, ccq_ref, ssq_ref, (lane_q & (C_ROPE - 1)) < C_ROPE // 2).astype(BF16)
    zeros = jnp.zeros((qr.shape[0], LANES - C_ROPE), BF16)
    for hd in range(C_HEADS):
        ql = _dot(qn[:, hd * C_NOPE:(hd + 1) * C_NOPE], wuk_ref[hd]).astype(BF16)
        q_ref[hd] = jnp.concatenate([ql, qr[:, hd * C_ROPE:(hd + 1) * C_ROPE], zeros], axis=1)


def _mla_proj(x, mod, tps, tab_tiles, layer, j, w_in, g_q, g_kv, wn, wr, wuk, ccq, ssq, cck, ssk, tm):
    T = x.shape[0]
    nt = T // tm
    tab = lambda w: pl.BlockSpec((tm, w), lambda i: (i % tab_tiles, 0))
    return pl.pallas_call(
        _mla_proj_body,
        out_shape=[jax.ShapeDtypeStruct((nt, C_HEADS, tm, C_QK), BF16),
                   jax.ShapeDtypeStruct((T, C_QK), BF16),
                   jax.ShapeDtypeStruct((C_KVLORA, T), BF16),
                   jax.ShapeDtypeStruct((T, C_KVLORA), F32),
                   jax.ShapeDtypeStruct((T, C_ROPE), F32)],
        grid=(nt,),
        in_specs=[_row_spec(tm, D_MODEL), _mod_spec(mod, layer, 0, tps), _mod_spec(mod, layer, 1, tps),
                  _full_spec(w_in, j), _vec_spec(j, C_QLORA), _vec_spec(j, C_KVLORA),
                  _full_spec(wn, j), _full_spec(wr, j), _full_spec(wuk, j),
                  tab(C_HEADS * C_ROPE), tab(C_HEADS * C_ROPE), tab(LANES), tab(LANES)],
        out_specs=[pl.BlockSpec((None, C_HEADS, tm, C_QK), lambda i: (i, 0, 0, 0)),
                   _row_spec(tm, C_QK), pl.BlockSpec((C_KVLORA, tm), lambda i: (0, i)),
                   _row_spec(tm, C_KVLORA), _row_spec(tm, C_ROPE)],
        compiler_params=_cparams("arbitrary"),
        name="mla_proj",
    )(x, mod, mod, w_in, g_q, g_kv, wn, wr, wuk, ccq, ssq, cck, ssk)


def _softmax_update(s, m_ref, l_ref, acc_ref, values):
    m_prev = m_ref[...]
    m_new = jnp.maximum(m_prev, jnp.max(s, axis=-1, keepdims=True))
    alpha = jnp.exp(m_prev - m_new)
    p = jnp.exp(s - m_new)
    l_ref[...] = alpha * l_ref[...] + jnp.sum(p, axis=-1, keepdims=True)
    acc_ref[...] = alpha * acc_ref[...] + _dot(p.astype(BF16), values)
    m_ref[...] = m_new


def _softmax_init(m_ref, l_ref, acc_ref):
    m_ref[...] = jnp.full_like(m_ref, -jnp.inf)
    l_ref[...] = jnp.zeros_like(l_ref)
    acc_ref[...] = jnp.zeros_like(acc_ref)


def _attn_body(qi_ref, kj_ref, last_ref, q_ref, k_ref, kt_ref, o_ref, m_ref, l_ref, acc_ref, *, tq, tk, scale2):
    p_id = pl.program_id(1)
    qi = qi_ref[p_id]
    kj = kj_ref[p_id]

    @pl.when(kj == 0)
    def _():
        _softmax_init(m_ref, l_ref, acc_ref)

    def step(masked):
        k = k_ref[...]
        kt = kt_ref[...]
        if masked:
            key = lax.broadcasted_iota(jnp.int32, (tk, tq), 0) + kj * tk
            tok = lax.broadcasted_iota(jnp.int32, (tk, tq), 1) + qi * tq
            keep = key <= tok
        for hd in range(C_HEADS):
            t = _dot_nt(k, q_ref[hd]) * scale2
            if masked:
                t = jnp.where(keep, t, -jnp.inf)
            m_prev = m_ref[hd]
            m_new = jnp.maximum(m_prev, jnp.max(t, axis=0, keepdims=True))
            alpha = jnp.exp2(m_prev - m_new)
            p = jnp.exp2(t - m_new)
            l_ref[hd] = alpha * l_ref[hd] + jnp.sum(p, axis=0, keepdims=True)
            acc_ref[hd] = alpha * acc_ref[hd] + _dot(kt, p.astype(BF16))
            m_ref[hd] = m_new

    fully_visible = (kj + 1) * tk - 1 <= qi * tq
    pl.when(fully_visible)(lambda: step(False))
    pl.when(jnp.logical_not(fully_visible))(lambda: step(True))

    @pl.when(last_ref[p_id] == 1)
    def _():
        for hd in range(C_HEADS):
            o_ref[hd] = (acc_ref[hd] / l_ref[hd]).astype(BF16)


def _attn_prompt(q, kcat, klat_t, n_seq, seq_len, tq, tk):
    nq, nk = seq_len // tq, seq_len // tk
    pairs = [(i, j) for i in range(nq) for j in range((i * tq + tq - 1) // tk + 1)]
    qi = jnp.asarray([p[0] for p in pairs], jnp.int32)
    kj = jnp.asarray([p[1] for p in pairs], jnp.int32)
    last = jnp.asarray([int(n + 1 == len(pairs) or pairs[n + 1][0] != p[0]) for n, p in enumerate(pairs)], jnp.int32)
    scale2 = (C_NOPE + C_ROPE) ** -0.5 * math.log2(math.e)
    grid_spec = pltpu.PrefetchScalarGridSpec(
        num_scalar_prefetch=3,
        grid=(n_seq, len(pairs)),
        in_specs=[pl.BlockSpec((None, C_HEADS, tq, C_QK), lambda b, p, qi, kj, last: (b * nq + qi[p], 0, 0, 0)),
                  pl.BlockSpec((tk, C_QK), lambda b, p, qi, kj, last: (b * nk + kj[p], 0)),
                  pl.BlockSpec((C_KVLORA, tk), lambda b, p, qi, kj, last: (0, b * nk + kj[p]))],
        out_specs=pl.BlockSpec((None, C_HEADS, C_KVLORA, tq), lambda b, p, qi, kj, last: (b * nq + qi[p], 0, 0, 0)),
        scratch_shapes=[pltpu.VMEM((C_HEADS, 1, tq), F32), pltpu.VMEM((C_HEADS, 1, tq), F32),
                        pltpu.VMEM((C_HEADS, C_KVLORA, tq), F32)])
    return pl.pallas_call(
        functools.partial(_attn_body, tq=tq, tk=tk, scale2=scale2),
        out_shape=jax.ShapeDtypeStruct((n_seq * nq, C_HEADS, C_KVLORA, tq), BF16),
        grid_spec=grid_spec,
        compiler_params=_cparams("arbitrary", "arbitrary"),
        name="mla_attention_prompt",
    )(qi, kj, last, q, kcat, klat_t)


def _attn_paged_body(pt_ref, q_ref, nlat_ref, nrope_ref, *rest, pages, seq_new, scale):
    lat_refs = rest[:pages]
    rope_refs = rest[pages:2 * pages]
    o_ref, m_ref, l_ref, acc_ref = rest[2 * pages:]
    step = pl.program_id(1)

    @pl.when(step == 0)
    def _():
        _softmax_init(m_ref, l_ref, acc_ref)

    q = q_ref[...]
    ql = q[:, :C_KVLORA]
    qr = q[:, C_KVLORA:C_KVLORA + C_ROPE]
    lat = jnp.concatenate([r[...] for r in lat_refs], axis=0).astype(BF16)
    rp_t = jnp.concatenate([r[...] for r in rope_refs], axis=1).astype(BF16)
    s = (_dot_nt(ql, lat) + _dot(qr, rp_t)) * scale
    _softmax_update(s, m_ref, l_ref, acc_ref, lat)

    @pl.when(step == pl.num_programs(1) - 1)
    def _():
        nlat = nlat_ref[...].astype(BF16)
        s2 = (_dot_nt(ql, nlat) + _dot_nt(qr, nrope_ref[...].astype(BF16))) * scale
        tok = lax.broadcasted_iota(jnp.int32, s2.shape, 0) & (seq_new - 1)
        key = lax.broadcasted_iota(jnp.int32, s2.shape, 1)
        s2 = jnp.where(key <= tok, s2, -jnp.inf)
        _softmax_update(s2, m_ref, l_ref, acc_ref, nlat)
        o_ref[...] = (acc_ref[...] / l_ref[...]).astype(BF16)


def _attn_paged(q, new_lat, new_rope, pool_lat, pool_rope_t, page_table, j, seq_new):
    n_seq, n_pages = page_table.shape
    page = pool_lat.shape[2]
    pages = 16
    rows = q.shape[1]
    scale = (C_NOPE + C_ROPE) ** -0.5

    def pool_spec(shape, i):
        return pl.BlockSpec((None, None) + shape, lambda b, s, pt: (j, pt[b, s * pages + i], 0, 0))

    grid_spec = pltpu.PrefetchScalarGridSpec(
        num_scalar_prefetch=1,
        grid=(n_seq, n_pages // pages),
        in_specs=[pl.BlockSpec((None, rows, C_QK), lambda b, s, pt: (b, 0, 0)),
                  pl.BlockSpec((None,) + new_lat.shape[1:], lambda b, s, pt: (b, 0, 0)),
                  pl.BlockSpec((None,) + new_rope.shape[1:], lambda b, s, pt: (b, 0, 0))]
                 + [pool_spec((page, C_KVLORA), i) for i in range(pages)]
                 + [pool_spec((C_ROPE, page), i) for i in range(pages)],
        out_specs=pl.BlockSpec((None, rows, C_KVLORA), lambda b, s, pt: (b, 0, 0)),
        scratch_shapes=[pltpu.VMEM((rows, 1), F32), pltpu.VMEM((rows, 1), F32),
                        pltpu.VMEM((rows, C_KVLORA), F32)])
    return pl.pallas_call(
        functools.partial(_attn_paged_body, pages=pages, seq_new=seq_new, scale=scale),
        out_shape=jax.ShapeDtypeStruct((n_seq, rows, C_KVLORA), BF16),
        grid_spec=grid_spec,
        compiler_params=_cparams("arbitrary", "arbitrary"),
        name="mla_attention_paged",
    )(page_table, q, new_lat, new_rope, *([pool_lat] * pages), *([pool_rope_t] * pages))


def _mla_out_body(x_ref, g_ref, o_ref, wuv_ref, wout_ref, lg_ref, lb_ref, y_ref):
    parts = [_dot_tn(o_ref[hd], wuv_ref[hd]) for hd in range(C_HEADS)]
    o = jnp.concatenate(parts, axis=1).astype(BF16)
    y_ref[...] = _residual_ln(x_ref[...], g_ref, _dot(o, wout_ref[...]), lg_ref, lb_ref)


def _mla_out(x, mod, tps, layer, j, o_lat, wuv, w_out, ln_g, ln_b, tm):
    T = x.shape[0]
    return pl.pallas_call(
        _mla_out_body,
        out_shape=jax.ShapeDtypeStruct((T, D_MODEL), F32),
        grid=(T // tm,),
        in_specs=[_row_spec(tm, D_MODEL), _mod_spec(mod, layer, 2, tps),
                  pl.BlockSpec((None, C_HEADS, C_KVLORA, tm), lambda i: (i, 0, 0, 0)),
                  _full_spec(wuv, j), _full_spec(w_out, j),
                  _vec_spec(layer, D_MODEL), _vec_spec(layer, D_MODEL)],
        out_specs=_row_spec(tm, D_MODEL),
        compiler_params=_cparams("arbitrary"),
        name="mla_out",
    )(x, mod, o_lat, wuv, w_out, ln_g, ln_b)


def _rope_tables(pos, reps, width):
    half = C_ROPE // 2
    inv = ROPE_THETA ** (-jnp.arange(half, dtype=F32) / half)
    ang = pos.astype(F32)[:, None] * inv
    cos, sin = jnp.cos(ang), jnp.sin(ang)
    cc = jnp.tile(jnp.concatenate([cos, cos], axis=1), (1, reps))
    ss = jnp.tile(jnp.concatenate([-sin, sin], axis=1), (1, reps))
    pad = width - cc.shape[1]
    return jnp.pad(cc, ((0, 0), (0, pad))), jnp.pad(ss, ((0, 0), (0, pad)))


def _prepare_params(p):
    vec = lambda a: a.reshape(a.shape[0], 1, a.shape[1])
    w_uq = p['c_w_uq']
    n_c = w_uq.shape[0]
    c_w_in = jnp.pad(p['c_w_in'], ((0, 0), (0, 0), (0, LANES - C_ROPE)))
    lb_all = jnp.cumsum(jax.nn.softmax(p['b_lb'].astype(F32), axis=0), axis=0)
    lb_all = lb_all - lb_all[:1]
    return dict(
        ln1_g=vec(p['ln1_g']), ln1_b=vec(p['ln1_b']), ln2_g=vec(p['ln2_g']), ln2_b=vec(p['ln2_b']),
        ffn_w1=p['ffn_w1'].astype(BF16), ffn_w2=p['ffn_w2'].astype(BF16),
        a_w_in=p['a_w_in'].astype(BF16), a_ln_g=vec(p['a_ln_g']), a_ln_b=vec(p['a_ln_b']),
        a_w_out=p['a_w_out'].astype(BF16),
        b_w_in=p['b_w_in'].astype(BF16), b_w_out=p['b_w_out'].astype(BF16), lb_all=lb_all,
        c_w_in=c_w_in.astype(BF16), c_g_q=vec(p['c_g_q']), c_g_kv=vec(p['c_g_kv']),
        c_wn=w_uq[..., :C_NOPE].reshape(n_c, C_QLORA, C_HEADS * C_NOPE).astype(BF16),
        c_wr=w_uq[..., C_NOPE:].reshape(n_c, C_QLORA, C_HEADS * C_ROPE).astype(BF16),
        c_wuk=jnp.transpose(p['c_w_uk'], (0, 2, 3, 1)).astype(BF16),
        c_wuv=jnp.transpose(p['c_w_uv'], (0, 2, 1, 3)).astype(BF16),
        c_w_out=p['c_w_out'].astype(BF16),
    )


def _sgu_mixing(w_s, b_s, chunk):
    reps = CHUNK_A // chunk
    causal = jnp.tril(jnp.ones((chunk, chunk), dtype=bool))
    ws = jnp.where(causal, w_s[:, :, :chunk, :chunk], 0)
    eye = jnp.eye(reps, dtype=w_s.dtype)
    ws = jnp.einsum('ab,jgts->jgatbs', eye, ws).reshape(w_s.shape[0], A_GROUPS, CHUNK_A, CHUNK_A)
    bias = jnp.tile(jnp.transpose(b_s[:, :, :chunk], (0, 2, 1)), (1, reps, 1))
    bias = jnp.repeat(bias, A_GDIM, axis=2)
    return ws.astype(BF16), bias


def _run_trunk(x, mod, n_seq, seq_len, q_pos, hgrn_state0, mla_cache, prm, raw, tm):
    T = x.shape[0]
    per_seq_mod = mod.shape[3] == 1
    tps = (seq_len // tm) if per_seq_mod else 1
    sgu_chunk = min(CHUNK_A, seq_len)
    ws, bias = _sgu_mixing(raw['a_w_s'], raw['a_b_s'], sgu_chunk)
    chunk_v, hgrn_states, lat_rows, rope_rows = [], [], [], []
    for i in range(DEPTH):
        kind, j = i % N_MIXERS, i // N_MIXERS
        if kind == 0:
            x, v_rows = _sgu_layer(x, mod, tps, i, j, prm['a_w_in'], prm['a_ln_g'], prm['a_ln_b'], ws, bias,
                                   prm['a_w_out'], prm['ln1_g'], prm['ln1_b'], tm, emit_v=mla_cache is not None)
            chunk_v.append(v_rows)
        elif kind == 1:
            lb = prm['lb_all'][i].reshape(1, D_MODEL)
            q, k, lf, v, gs = _hgrn_proj(x, mod, tps, i, j, prm['b_w_in'], lb, tm)
            if seq_len % 64 == 0:
                C, lpad = 64, seq_len
                rec_tm = min(seq_len, 512)
                rec_in = (q, k, lf, v)
            else:
                C = lpad = rec_tm = SUBLANES
                padseq = lambda a: jnp.pad(a.reshape(n_seq, seq_len, D_MODEL),
                                           ((0, 0), (0, lpad - seq_len), (0, 0))).reshape(n_seq * lpad, D_MODEL)
                rec_in = tuple(padseq(a) for a in (q, k, lf, v))
            s0 = None if hgrn_state0 is None else hgrn_state0[j]
            o, S = _hgrn_rec(*rec_in, s0, n_seq, lpad, rec_tm, C)
            if lpad != seq_len:
                o = o.reshape(n_seq, lpad, D_MODEL)[:, :seq_len].reshape(T, D_MODEL)
            hgrn_states.append(S)
            x = _hgrn_out(x, mod, tps, i, j, o, gs, prm['b_w_out'], prm['ln1_g'], prm['ln1_b'], tm)
        else:
            tq = min(tm, 256)
            tps_q = (seq_len // tq) if per_seq_mod else 1
            pos_rows = q_pos if tq <= seq_len else jnp.tile(q_pos, tq // seq_len)
            ccq, ssq = _rope_tables(pos_rows, C_HEADS, C_HEADS * C_ROPE)
            cck, ssk = _rope_tables(pos_rows, 1, LANES)
            qcat, kcat, klat_t, lat, kr = _mla_proj(x, mod, tps_q, pos_rows.shape[0] // tq, i, j, prm['c_w_in'],
                                            prm['c_g_q'], prm['c_g_kv'], prm['c_wn'], prm['c_wr'], prm['c_wuk'],
                                            ccq, ssq, cck, ssk, tq)
            if mla_cache is None:
                o_lat = _attn_prompt(qcat, kcat, klat_t, n_seq, seq_len, tq, min(seq_len, 512))
            else:
                pool_lat, pool_rope_t, pt = mla_cache
                qs = qcat.reshape(C_HEADS, n_seq, seq_len, C_QK).transpose(1, 0, 2, 3)
                qs = qs.reshape(n_seq, C_HEADS * seq_len, C_QK)
                padk = lambda a: jnp.pad(a.reshape(n_seq, seq_len, a.shape[1]), ((0, 0), (0, 16 - seq_len), (0, 0)))
                o_s = _attn_paged(qs, padk(lat), padk(kr), pool_lat, pool_rope_t, pt, j, seq_len)
                o_lat = o_s.reshape(n_seq, C_HEADS, seq_len, C_KVLORA).transpose(1, 3, 0, 2)
                o_lat = o_lat.reshape(1, C_HEADS, C_KVLORA, T)
            x = _mla_out(x, mod, tps_q, i, j, o_lat, prm['c_wuv'], prm['c_w_out'], prm['ln1_g'], prm['ln1_b'], tq)
            lat_rows.append(lat.reshape(n_seq, seq_len, C_KVLORA))
            rope_rows.append(kr.reshape(n_seq, seq_len, C_ROPE))
        x = _ffn_layer(x, mod, tps, i, prm['ffn_w1'], prm['ffn_w2'], prm['ln2_g'], prm['ln2_b'], tm)
    stack = lambda xs: jnp.stack(xs) if xs and xs[0] is not None else None
    return x, stack(chunk_v), jnp.stack(hgrn_states), jnp.stack(lat_rows), jnp.stack(rope_rows)


def kernel(x_prompt, x_sample, cache_kv_latent, cache_k_rope, state_hgrn, page_table, c_prompt, c_sample,
           w_ada, b_ada, ln1_g, ln1_b, ln2_g, ln2_b, ffn_w1, ffn_w2, a_w_in, a_ln_g, a_ln_b, a_w_s, a_b_s,
           a_w_out, b_w_in, b_lb, b_w_out, c_w_in, c_g_q, c_g_kv, c_w_uq, c_w_uk, c_w_uv, c_w_out):
    raw = dict(ln1_g=ln1_g, ln1_b=ln1_b, ln2_g=ln2_g, ln2_b=ln2_b, ffn_w1=ffn_w1, ffn_w2=ffn_w2,
               a_w_in=a_w_in, a_ln_g=a_ln_g, a_ln_b=a_ln_b, a_w_s=a_w_s, a_b_s=a_b_s, a_w_out=a_w_out,
               b_w_in=b_w_in, b_lb=b_lb, b_w_out=b_w_out, c_w_in=c_w_in, c_g_q=c_g_q, c_g_kv=c_g_kv,
               c_w_uq=c_w_uq, c_w_uk=c_w_uk, c_w_uv=c_w_uv, c_w_out=c_w_out)
    prm = _prepare_params(raw)
    nb, seq, d = x_prompt.shape
    ns, sseq, _ = x_sample.shape
    past_len = page_table.shape[1] * cache_kv_latent.shape[2]
    pos_prompt = jnp.arange(seq, dtype=jnp.int32)
    pos_sample = past_len + jnp.arange(sseq, dtype=jnp.int32)

    mod = _modulation(jnp.concatenate([c_prompt, c_sample], axis=0), w_ada, b_ada)
    mod = mod.reshape(DEPTH, nb + ns, 6, d).transpose(0, 2, 1, 3)
    mod_p = mod[:, :, :nb].reshape(DEPTH, 6, nb, 1, d)
    mod_s = jnp.repeat(mod[:, :, nb:], sseq, axis=2).reshape(DEPTH, 6, 1, ns * sseq, d)

    tm_p = 512
    y_p, _, hs_p, lat_p, rope_p = _run_trunk(x_prompt.reshape(nb * seq, d), mod_p, nb, seq, pos_prompt,
                                             None, None, prm, raw, tm_p)
    y_s, v_s, hs_s, lat_s, rope_s = _run_trunk(x_sample.reshape(ns * sseq, d), mod_s, ns, sseq, pos_sample,
                                               state_hgrn,
                                               (cache_kv_latent, jnp.swapaxes(cache_k_rope, 2, 3), page_table),
                                               prm, raw, ns * sseq)
    return (y_p.reshape(nb, seq, d), y_s.reshape(ns, sseq, d), hs_p, hs_s, lat_p, rope_p, lat_s, rope_s,
            v_s.reshape(v_s.shape[0], ns, sseq, d))
```

```python
import functools
import math

import jax
import jax.numpy as jnp
from jax import lax
from jax.experimental import pallas as pl
from jax.experimental.pallas import tpu as pltpu

F32 = jnp.float32
BF16 = jnp.bfloat16

D_MODEL = 1024
DEPTH = 4
N_MIXERS = 3
CHUNK_A = 128
A_GROUPS = 8
A_GDIM = D_MODEL // A_GROUPS
B_HEADS = 8
B_DK = 128
B_DV = D_MODEL // B_HEADS
C_HEADS = 8
C_NOPE = 128
C_ROPE = 64
C_V = 128
C_QLORA = 512
C_KVLORA = 256
ROPE_THETA = 10000.0
D_FF = 4 * D_MODEL
ALPHA = (2.0 * DEPTH) ** 0.25
EPS = 1e-6

LANES = 128
SUBLANES = 8
C_QK = C_KVLORA + LANES
VMEM_LIMIT = 56 * 1024 * 1024


def _cparams(*sem):
    return pltpu.CompilerParams(dimension_semantics=sem, vmem_limit_bytes=VMEM_LIMIT)


def _dot(a, b):
    return jnp.dot(a, b, preferred_element_type=F32)


def _dot_nt(a, b):
    return lax.dot_general(a, b, (((1,), (1,)), ((), ())), preferred_element_type=F32)


def _dot_tn(a, b):
    return lax.dot_general(a, b, (((0,), (0,)), ((), ())), preferred_element_type=F32)


def _layer_norm(y, g, b):
    mu = jnp.mean(y, axis=-1, keepdims=True)
    yc = y - mu
    var = jnp.mean(yc * yc, axis=-1, keepdims=True)
    return yc * lax.rsqrt(var + EPS) * g + b


def _rms(y):
    return y * lax.rsqrt(jnp.mean(y * y, axis=-1, keepdims=True) + EPS)


def _silu(x):
    return x * jax.nn.sigmoid(x)


def _gelu_tanh(x):
    return 0.5 * x * (1.0 + jnp.tanh(math.sqrt(2.0 / math.pi) * (x + 0.044715 * (x * x * x))))


def _modulate(x, sh_ref, sc_ref):
    return x * (1.0 + sc_ref[...]) + sh_ref[...]


def _residual_ln(x, gate_ref, out, lg_ref, lb_ref):
    return _layer_norm(ALPHA * x + gate_ref[...] * out, lg_ref[...], lb_ref[...])


def _mod_spec(mod, layer, which, tiles_per_seq):
    rows = mod.shape[3]
    return pl.BlockSpec((None, None, None, rows, D_MODEL),
                        lambda i: (layer, which, i // tiles_per_seq, 0, 0))


def _vec_spec(layer, width):
    return pl.BlockSpec((None, 1, width), lambda i: (layer, 0, 0))


def _full_spec(arr, layer=None):
    if layer is None:
        nd = arr.ndim
        return pl.BlockSpec(arr.shape, lambda i: (0,) * nd)
    nd = arr.ndim - 1
    return pl.BlockSpec((None,) + arr.shape[1:], lambda i: (layer,) + (0,) * nd)


def _row_spec(tm, width):
    return pl.BlockSpec((tm, width), lambda i: (i, 0))


def _mod_body(c_ref, w_ref, b_ref, o_ref):
    sc = _silu(c_ref[...]).astype(BF16)
    o_ref[...] = _dot(sc, w_ref[...].astype(BF16)) + b_ref[...]


def _modulation(c_all, w_ada, b_ada):
    n = c_all.shape[0]
    tn = 1536
    width = w_ada.shape[2]
    return pl.pallas_call(
        _mod_body,
        out_shape=jax.ShapeDtypeStruct((DEPTH, n, width), F32),
        grid=(DEPTH, width // tn),
        in_specs=[pl.BlockSpec((n, D_MODEL), lambda l, j: (0, 0)),
                  pl.BlockSpec((None, D_MODEL, tn), lambda l, j: (l, 0, j)),
                  pl.BlockSpec((None, 1, tn), lambda l, j: (l, 0, j))],
        out_specs=pl.BlockSpec((None, n, tn), lambda l, j: (l, 0, j)),
        compiler_params=_cparams("arbitrary", "arbitrary"),
        name="adaln_modulation",
    )(c_all, w_ada, b_ada.reshape(DEPTH, 1, width))


def _ffn_body(x_ref, sh_ref, sc_ref, g_ref, w1_ref, w2_ref, lg_ref, lb_ref, o_ref, acc_ref, *, fc):
    x = x_ref[...]
    h = _modulate(x, sh_ref, sc_ref).astype(BF16)
    for c in range(D_FF // fc):
        a = _dot(h, w1_ref[:, c * fc:(c + 1) * fc])
        a = jnp.square(jnp.maximum(a, 0.0)).astype(BF16)
        d = _dot(a, w2_ref[c * fc:(c + 1) * fc, :])
        if c == 0:
            acc_ref[...] = d
        else:
            acc_ref[...] += d
    o_ref[...] = _residual_ln(x, g_ref, acc_ref[...], lg_ref, lb_ref)


def _ffn_layer(x, mod, tps, layer, w1, w2, ln_g, ln_b, tm):
    T = x.shape[0]
    return pl.pallas_call(
        functools.partial(_ffn_body, fc=1024),
        out_shape=jax.ShapeDtypeStruct((T, D_MODEL), F32),
        grid=(T // tm,),
        in_specs=[_row_spec(tm, D_MODEL),
                  _mod_spec(mod, layer, 3, tps), _mod_spec(mod, layer, 4, tps), _mod_spec(mod, layer, 5, tps),
                  _full_spec(w1, layer), _full_spec(w2, layer),
                  _vec_spec(layer, D_MODEL), _vec_spec(layer, D_MODEL)],
        out_specs=_row_spec(tm, D_MODEL),
        scratch_shapes=[pltpu.VMEM((tm, D_MODEL), F32)],
        compiler_params=_cparams("arbitrary"),
        name="ffn_sublayer",
    )(x, mod, mod, mod, w1, w2, ln_g, ln_b)


def _sgu_body(x_ref, sh_ref, sc_ref, g_ref, win_ref, lng_ref, lnb_ref, ws_ref, bias_ref, wout_ref,
              lg_ref, lb_ref, o_ref, *rest, tm, emit_v):
    if emit_v:
        v_ref, gated_ref = rest
    else:
        (gated_ref,) = rest
    x = x_ref[...]
    h = _modulate(x, sh_ref, sc_ref).astype(BF16)
    u = _gelu_tanh(_dot(h, win_ref[:, :D_MODEL]))
    v = _gelu_tanh(_dot(h, win_ref[:, D_MODEL:]))
    v = _layer_norm(v, lng_ref[...], lnb_ref[...])
    if emit_v:
        v_ref[...] = v
    vb = v.astype(BF16)
    for n in range(tm // CHUNK_A):
        r = slice(n * CHUNK_A, (n + 1) * CHUNK_A)
        cols = [_dot(ws_ref[g], vb[r, g * A_GDIM:(g + 1) * A_GDIM]) for g in range(A_GROUPS)]
        mixed = jnp.concatenate(cols, axis=1) + bias_ref[...]
        gated_ref[r, :] = (u[r, :] * mixed).astype(BF16)
    out = _dot(gated_ref[...], wout_ref[...])
    o_ref[...] = _residual_ln(x, g_ref, out, lg_ref, lb_ref)


def _sgu_layer(x, mod, tps, layer, j, w_in, ln_g, ln_b, ws, bias, w_out, ln1_g, ln1_b, tm, emit_v):
    T = x.shape[0]
    out_shape = [jax.ShapeDtypeStruct((T, D_MODEL), F32)]
    out_specs = [_row_spec(tm, D_MODEL)]
    if emit_v:
        out_shape.append(jax.ShapeDtypeStruct((T, D_MODEL), F32))
        out_specs.append(_row_spec(tm, D_MODEL))
    res = pl.pallas_call(
        functools.partial(_sgu_body, tm=tm, emit_v=emit_v),
        out_shape=out_shape,
        grid=(T // tm,),
        in_specs=[_row_spec(tm, D_MODEL),
                  _mod_spec(mod, layer, 0, tps), _mod_spec(mod, layer, 1, tps), _mod_spec(mod, layer, 2, tps),
                  _full_spec(w_in, j), _vec_spec(j, D_MODEL), _vec_spec(j, D_MODEL),
                  _full_spec(ws, j), _full_spec(bias, j), _full_spec(w_out, j),
                  _vec_spec(layer, D_MODEL), _vec_spec(layer, D_MODEL)],
        out_specs=out_specs,
        scratch_shapes=[pltpu.VMEM((tm, D_MODEL), BF16)],
        compiler_params=_cparams("arbitrary"),
        name="sgu_sublayer",
    )(x, mod, mod, mod, w_in, ln_g, ln_b, ws, bias, w_out, ln1_g, ln1_b)
    return (res[0], res[1]) if emit_v else (res[0], None)


def _hgrn_proj_body(x_ref, sh_ref, sc_ref, win_ref, lb_ref, q_ref, k_ref, lf_ref, v_ref, gs_ref):
    h = _modulate(x_ref[...], sh_ref, sc_ref).astype(BF16)
    d = D_MODEL
    q_ref[...] = _silu(_dot(h, win_ref[:, 0:d]))
    fz = _dot(h, win_ref[:, d:2 * d])
    lb = lb_ref[...]
    t = jnp.log1p(jnp.exp(-jnp.abs(fz)))
    a = jnp.log(lb)
    b = jnp.log1p(-lb) + (jnp.minimum(fz, 0.0) - t)
    lf_ref[...] = (jnp.maximum(a, b) + jnp.log1p(jnp.exp(-jnp.abs(a - b)))) * math.log2(math.e)
    k_ref[...] = (1.0 - lb) * jax.nn.sigmoid(-fz)
    v_ref[...] = _dot(h, win_ref[:, 2 * d:3 * d])
    gs_ref[...] = _silu(_dot(h, win_ref[:, 3 * d:4 * d]))


def _hgrn_proj(x, mod, tps, layer, j, w_in, lb, tm):
    T = x.shape[0]
    shp = jax.ShapeDtypeStruct((T, D_MODEL), F32)
    return pl.pallas_call(
        _hgrn_proj_body,
        out_shape=[shp] * 5,
        grid=(T // tm,),
        in_specs=[_row_spec(tm, D_MODEL), _mod_spec(mod, layer, 0, tps), _mod_spec(mod, layer, 1, tps),
                  _full_spec(w_in, j), _full_spec(lb)],
        out_specs=[_row_spec(tm, D_MODEL)] * 5,
        compiler_params=_cparams("arbitrary"),
        name="hgrn_proj",
    )(x, mod, mod, w_in, lb)


def _hgrn_rec_body(*refs, C, nchunk, hb, has_s0):
    if has_s0:
        q_ref, k_ref, g_ref, v_ref, s0_ref, o_ref, sout_ref, st_ref = refs
    else:
        q_ref, k_ref, g_ref, v_ref, o_ref, sout_ref, st_ref = refs
    t = pl.program_id(2)

    @pl.when(t == 0)
    def _():
        for hd in range(hb):
            st_ref[hd] = s0_ref[hd].T if has_s0 else jnp.zeros((B_DV, B_DK), F32)

    row = lax.broadcasted_iota(jnp.int32, (C, B_DK), 0)
    row_a = lax.broadcasted_iota(jnp.int32, (C, C), 0)
    col_a = lax.broadcasted_iota(jnp.int32, (C, C), 1)
    tri = jnp.where(row_a >= col_a, 1.0, 0.0).astype(BF16)
    band = [(col_a == row_a - r) & ((row_a & (SUBLANES - 1)) >= r) for r in range(SUBLANES)]
    levels = []
    for m in (8, 16, 32, 64, 128):
        if 2 * m <= C:
            shift = int(math.log2(2 * m))
            pair = (((row_a >> shift) == (col_a >> shift)) & ((row_a & (2 * m - 1)) >= m)
                    & ((col_a & (2 * m - 1)) < m))
            levels.append((m, (row & (2 * m - 1)) >= m, pair))

    states = [st_ref[hd] for hd in range(hb)]
    for c, hd in [(c, hd) for c in range(nchunk) for hd in range(hb)]:
        rows = slice(c * C, (c + 1) * C)
        cols = slice(hd * B_DK, (hd + 1) * B_DK)
        st = states[hd]
        qc = q_ref[rows, cols]
        kc = k_ref[rows, cols]
        gc = g_ref[rows, cols]
        vb = v_ref[rows, cols].astype(BF16)
        g_hi = gc.astype(BF16)
        r1 = gc - g_hi.astype(F32)
        g_mid = r1.astype(BF16)
        g_lo = (r1 - g_mid.astype(F32)).astype(BF16)
        b3 = _dot(tri, jnp.concatenate([g_hi, g_mid, g_lo], axis=1))
        b = b3[:, 0:B_DK] + b3[:, B_DK:2 * B_DK] + b3[:, 2 * B_DK:3 * B_DK]
        b_last = b[C - 1:C, :]
        o = _dot_nt((qc * jnp.exp2(b)).astype(BF16), st.astype(BF16))
        a_mat = jnp.zeros((C, C), F32)
        for m, upper, pair in levels:
            bref = jnp.concatenate(
                [jnp.broadcast_to(b[i * 2 * m + m - 1:i * 2 * m + m, :], (2 * m, B_DK)) for i in range(C // (2 * m))],
                axis=0)
            q_up = jnp.where(upper, qc * jnp.exp2(b - bref), 0.0)
            k_lo = jnp.where(upper, 0.0, kc * jnp.exp2(bref - b))
            a_mat = jnp.where(pair, _dot_nt(q_up.astype(BF16), k_lo.astype(BF16)), a_mat)
        k3 = kc.reshape(C // SUBLANES, SUBLANES, B_DK)
        b3d = b.reshape(C // SUBLANES, SUBLANES, B_DK)
        for r in range(SUBLANES):
            k_r = kc if r == 0 else pltpu.roll(k3, r, 1).reshape(C, B_DK)
            b_r = b if r == 0 else pltpu.roll(b3d, r, 1).reshape(C, B_DK)
            d_r = jnp.sum(qc * k_r * jnp.exp2(b - b_r), axis=1, keepdims=True)
            a_mat = jnp.where(band[r], d_r, a_mat)
        o_ref[rows, cols] = o + _dot(a_mat.astype(BF16), vb)
        k_dec = (kc * jnp.exp2(b_last - b)).astype(BF16)
        states[hd] = st * jnp.exp2(b_last) + _dot_tn(vb, k_dec)
    for hd in range(hb):
        st_ref[hd] = states[hd]

    @pl.when(t == pl.num_programs(2) - 1)
    def _():
        for hd in range(hb):
            sout_ref[hd] = states[hd].T


def _hgrn_rec(q, k, lf, v, s0, n_seq, seq_len, tm, C, hb):
    T = q.shape[0]
    nt = seq_len // tm
    blk = pl.BlockSpec((tm, hb * B_DK), lambda b, h, t: (b * nt + t, h))
    st_spec = pl.BlockSpec((None, hb, B_DK, B_DV), lambda b, h, t: (b, h, 0, 0))
    has_s0 = s0 is not None
    in_specs = [blk] * 4 + ([st_spec] if has_s0 else [])
    args = (q, k, lf, v) + ((s0,) if has_s0 else ())
    return pl.pallas_call(
        functools.partial(_hgrn_rec_body, C=C, nchunk=tm // C, hb=hb, has_s0=has_s0),
        out_shape=[jax.ShapeDtypeStruct((T, D_MODEL), F32),
                   jax.ShapeDtypeStruct((n_seq, B_HEADS, B_DK, B_DV), F32)],
        grid=(n_seq, B_HEADS // hb, nt),
        in_specs=in_specs,
        out_specs=[blk, st_spec],
        scratch_shapes=[pltpu.VMEM((hb, B_DV, B_DK), F32)],
        compiler_params=_cparams("arbitrary", "arbitrary", "arbitrary"),
        name="hgrn_recurrence",
    )(*args)


def _hgrn_out_body(x_ref, g_ref, o_ref, gs_ref, wout_ref, lg_ref, lb_ref, y_ref):
    o = o_ref[...]
    parts = [_rms(o[:, h * B_DV:(h + 1) * B_DV]) for h in range(B_HEADS)]
    y = (jnp.concatenate(parts, axis=1) * gs_ref[...]).astype(BF16)
    y_ref[...] = _residual_ln(x_ref[...], g_ref, _dot(y, wout_ref[...]), lg_ref, lb_ref)


def _hgrn_out(x, mod, tps, layer, j, o, gs, w_out, ln_g, ln_b, tm):
    T = x.shape[0]
    return pl.pallas_call(
        _hgrn_out_body,
        out_shape=jax.ShapeDtypeStruct((T, D_MODEL), F32),
        grid=(T // tm,),
        in_specs=[_row_spec(tm, D_MODEL), _mod_spec(mod, layer, 2, tps),
                  _row_spec(tm, D_MODEL), _row_spec(tm, D_MODEL),
                  _full_spec(w_out, j), _vec_spec(layer, D_MODEL), _vec_spec(layer, D_MODEL)],
        out_specs=_row_spec(tm, D_MODEL),
        compiler_params=_cparams("arbitrary"),
        name="hgrn_out",
    )(x, mod, o, gs, w_out, ln_g, ln_b)


def _rope_lanes(x, cc_ref, ss_ref, period_first_half):
    n = x.shape[1]
    half = C_ROPE // 2
    rot = jnp.where(period_first_half, pltpu.roll(x, n - half, 1), pltpu.roll(x, half, 1))
    return x * cc_ref[...] + rot * ss_ref[...]


def _mla_proj_body(x_ref, sh_ref, sc_ref, win_ref, gq_ref, gkv_ref, wn_ref, wr_ref, wuk_ref,
                   ccq_ref, ssq_ref, cck_ref, ssk_ref, q_ref, kcat_ref, klt_ref, lat_ref, kr_ref):
    h = _modulate(x_ref[...], sh_ref, sc_ref).astype(BF16)
    a = _dot(h, win_ref[...])
    cq = (_rms(a[:, :C_QLORA]) * gq_ref[...]).astype(BF16)
    ckv = _rms(a[:, C_QLORA:C_QLORA + C_KVLORA]) * gkv_ref[...]
    kr_slab = a[:, C_QLORA + C_KVLORA:]
    lane_k = lax.broadcasted_iota(jnp.int32, kr_slab.shape, 1)
    kr_slab = _rope_lanes(kr_slab, cck_ref, ssk_ref, (lane_k & (C_ROPE - 1)) < C_ROPE // 2)
    lat_ref[...] = ckv
    kr_ref[...] = kr_slab[:, :C_ROPE]
    kcat_ref[...] = jnp.concatenate([ckv, kr_slab], axis=1).astype(BF16)
    klt_ref[...] = ckv.T.astype(BF16)
    qn = _dot(cq, wn_ref[...]).astype(BF16)
    qr = _dot(cq, wr_ref[...])
    lane_q = lax.broadcasted_iota(jnp.int32, qr.shape, 1)
    qr = _rope_lanes(qr, ccq_ref, ssq_ref, (lane_q & (C_ROPE - 1)) < C_ROPE // 2).astype(BF16)
    zeros = jnp.zeros((qr.shape[0], LANES - C_ROPE), BF16)
    for hd in range(C_HEADS):
        ql = _dot(qn[:, hd * C_NOPE:(hd + 1) * C_NOPE], wuk_ref[hd]).astype(BF16)
        q_ref[hd] = jnp.concatenate([ql, qr[:, hd * C_ROPE:(hd + 1) * C_ROPE], zeros], axis=1)


def _mla_proj(x, mod, tps, tab_tiles, layer, j, w_in, g_q, g_kv, wn, wr, wuk, ccq, ssq, cck, ssk, tm):
    T = x.shape[0]
    nt = T // tm
    tab = lambda w: pl.BlockSpec((tm, w), lambda i: (i % tab_tiles, 0))
    return pl.pallas_call(
        _mla_proj_body,
        out_shape=[jax.ShapeDtypeStruct((nt, C_HEADS, tm, C_QK), BF16),
                   jax.ShapeDtypeStruct((T, C_QK), BF16),
                   jax.ShapeDtypeStruct((C_KVLORA, T), BF16),
                   jax.ShapeDtypeStruct((T, C_KVLORA), F32),
                   jax.ShapeDtypeStruct((T, C_ROPE), F32)],
        grid=(nt,),
        in_specs=[_row_spec(tm, D_MODEL), _mod_spec(mod, layer, 0, tps), _mod_spec(mod, layer, 1, tps),
                  _full_spec(w_in, j), _vec_spec(j, C_QLORA), _vec_spec(j, C_KVLORA),
                  _full_spec(wn, j), _full_spec(wr, j), _full_spec(wuk, j),
                  tab(C_HEADS * C_ROPE), tab(C_HEADS * C_ROPE), tab(LANES), tab(LANES)],
        out_specs=[pl.BlockSpec((None, C_HEADS, tm, C_QK), lambda i: (i, 0, 0, 0)),
                   _row_spec(tm, C_QK), pl.BlockSpec((C_KVLORA, tm), lambda i: (0, i)),
                   _row_spec(tm, C_KVLORA), _row_spec(tm, C_ROPE)],
        compiler_params=_cparams("arbitrary"),
        name="mla_proj",
    )(x, mod, mod, w_in, g_q, g_kv, wn, wr, wuk, ccq, ssq, cck, ssk)


def _softmax_update(s, m_ref, l_ref, acc_ref, values):
    m_prev = m_ref[...]
    m_new = jnp.maximum(m_prev, jnp.max(s, axis=-1, keepdims=True))
    alpha = jnp.exp(m_prev - m_new)
    p = jnp.exp(s - m_new)
    l_ref[...] = alpha * l_ref[...] + jnp.sum(p, axis=-1, keepdims=True)
    acc_ref[...] = alpha * acc_ref[...] + _dot(p.astype(BF16), values)
    m_ref[...] = m_new


def _softmax_init(m_ref, l_ref, acc_ref):
    m_ref[...] = jnp.full_like(m_ref, -jnp.inf)
    l_ref[...] = jnp.zeros_like(l_ref)
    acc_ref[...] = jnp.zeros_like(acc_ref)


def _attn_body(qi_ref, kj_ref, last_ref, q_ref, k_ref, kt_ref, o_ref, m_ref, l_ref, acc_ref, *, tq, tk, scale2):
    p_id = pl.program_id(1)
    qi = qi_ref[p_id]
    kj = kj_ref[p_id]

    @pl.when(kj == 0)
    def _():
        _softmax_init(m_ref, l_ref, acc_ref)

    def step(masked):
        k = k_ref[...]
        kt = kt_ref[...]
        if masked:
            key = lax.broadcasted_iota(jnp.int32, (tk, tq), 0) + kj * tk
            tok = lax.broadcasted_iota(jnp.int32, (tk, tq), 1) + qi * tq
            keep = key <= tok
        for hd in range(C_HEADS):
            t = _dot_nt(k, q_ref[hd]) * scale2
            if masked:
                t = jnp.where(keep, t, -jnp.inf)
            m_prev = m_ref[hd]
            m_new = jnp.maximum(m_prev, jnp.max(t, axis=0, keepdims=True))
            alpha = jnp.exp2(m_prev - m_new)
            p = jnp.exp2(t - m_new)
            l_ref[hd] = alpha * l_ref[hd] + jnp.sum(p, axis=0, keepdims=True)
            acc_ref[hd] = alpha * acc_ref[hd] + _dot(kt, p.astype(BF16))
            m_ref[hd] = m_new

    fully_visible = (kj + 1) * tk - 1 <= qi * tq
    pl.when(fully_visible)(lambda: step(False))
    pl.when(jnp.logical_not(fully_visible))(lambda: step(True))

    @pl.when(last_ref[p_id] == 1)
    def _():
        for hd in range(C_HEADS):
            o_ref[hd] = (acc_ref[hd] / l_ref[hd]).astype(BF16)


def _attn_prompt(q, kcat, klat_t, n_seq, seq_len, tq, tk):
    nq, nk = seq_len // tq, seq_len // tk
    pairs = [(i, j) for i in range(nq) for j in range((i * tq + tq - 1) // tk + 1)]
    qi = jnp.asarray([p[0] for p in pairs], jnp.int32)
    kj = jnp.asarray([p[1] for p in pairs], jnp.int32)
    last = jnp.asarray([int(n + 1 == len(pairs) or pairs[n + 1][0] != p[0]) for n, p in enumerate(pairs)], jnp.int32)
    scale2 = (C_NOPE + C_ROPE) ** -0.5 * math.log2(math.e)
    grid_spec = pltpu.PrefetchScalarGridSpec(
        num_scalar_prefetch=3,
        grid=(n_seq, len(pairs)),
        in_specs=[pl.BlockSpec((None, C_HEADS, tq, C_QK), lambda b, p, qi, kj, last: (b * nq + qi[p], 0, 0, 0)),
                  pl.BlockSpec((tk, C_QK), lambda b, p, qi, kj, last: (b * nk + kj[p], 0)),
                  pl.BlockSpec((C_KVLORA, tk), lambda b, p, qi, kj, last: (0, b * nk + kj[p]))],
        out_specs=pl.BlockSpec((None, C_HEADS, C_KVLORA, tq), lambda b, p, qi, kj, last: (b * nq + qi[p], 0, 0, 0)),
        scratch_shapes=[pltpu.VMEM((C_HEADS, 1, tq), F32), pltpu.VMEM((C_HEADS, 1, tq), F32),
                        pltpu.VMEM((C_HEADS, C_KVLORA, tq), F32)])
    return pl.pallas_call(
        functools.partial(_attn_body, tq=tq, tk=tk, scale2=scale2),
        out_shape=jax.ShapeDtypeStruct((n_seq * nq, C_HEADS, C_KVLORA, tq), BF16),
        grid_spec=grid_spec,
        compiler_params=_cparams("arbitrary", "arbitrary"),
        name="mla_attention_prompt",
    )(qi, kj, last, q, kcat, klat_t)


def _attn_paged_body(pt_ref, q_ref, nlat_ref, nrope_ref, lat_hbm, rope_hbm, o_ref,
                     lat_buf, rope_buf, sem, m_ref, l_ref, acc_ref, *, pages, sub, j, seq_new, scale2):
    b = pl.program_id(0)
    g = pl.program_id(1)
    n_groups = pl.num_programs(1)
    step = b * n_groups + g
    slot = step & 1

    def page_copy(kind, page_id, sl, i):
        src, dst = (lat_hbm, lat_buf) if kind == 0 else (rope_hbm, rope_buf)
        return pltpu.make_async_copy(src.at[j, page_id], dst.at[sl, i], sem.at[kind, sl])

    def start_group(bb, gg, sl):
        for i in range(pages):
            page_id = pt_ref[bb, gg * pages + i]
            page_copy(0, page_id, sl, i).start()
            page_copy(1, page_id, sl, i).start()

    @pl.when(step == 0)
    def _():
        start_group(0, 0, 0)

    @pl.when(step + 1 < pl.num_programs(0) * n_groups)
    def _():
        wrap = g == n_groups - 1
        start_group(jnp.where(wrap, b + 1, b), jnp.where(wrap, 0, g + 1), 1 - slot)

    @pl.when(g == 0)
    def _():
        _softmax_init(m_ref, l_ref, acc_ref)

    for i in range(pages):
        page_copy(0, 0, slot, i).wait()
        page_copy(1, 0, slot, i).wait()

    q = q_ref[...]
    ql = q[:, :C_KVLORA]
    qr = q[:, C_KVLORA:C_KVLORA + C_ROPE]

    def update(carry, t, values):
        m_prev, l_prev, acc = carry
        m_new = jnp.maximum(m_prev, jnp.max(t, axis=-1, keepdims=True))
        alpha = jnp.exp2(m_prev - m_new)
        p = jnp.exp2(t - m_new)
        return (m_new, alpha * l_prev + jnp.sum(p, axis=-1, keepdims=True),
                alpha * acc + _dot(p.astype(BF16), values))

    carry = (m_ref[...], l_ref[...], acc_ref[...])
    page = lat_buf.shape[2]
    for u in range(pages // sub):
        lat = lat_buf[slot, u * sub:(u + 1) * sub].reshape(sub * page, C_KVLORA).astype(BF16)
        rp_t = jnp.concatenate([rope_buf[slot, u * sub + i] for i in range(sub)], axis=1).astype(BF16)
        carry = update(carry, (_dot_nt(ql, lat) + _dot(qr, rp_t)) * scale2, lat)

    @pl.when(g < n_groups - 1)
    def _():
        m_ref[...], l_ref[...], acc_ref[...] = carry

    @pl.when(g == n_groups - 1)
    def _():
        nlat = nlat_ref[...].astype(BF16)
        t2 = (_dot_nt(ql, nlat) + _dot_nt(qr, nrope_ref[...].astype(BF16))) * scale2
        tok = lax.broadcasted_iota(jnp.int32, t2.shape, 0) & (seq_new - 1)
        key = lax.broadcasted_iota(jnp.int32, t2.shape, 1)
        _, l_fin, acc = update(carry, jnp.where(key <= tok, t2, -jnp.inf), nlat)
        o_ref[...] = (acc / l_fin).astype(BF16)


def _attn_paged(q, new_lat, new_rope, pool_lat, pool_rope_t, page_table, j, seq_new):
    n_seq, n_pages = page_table.shape
    page = pool_lat.shape[2]
    pages, sub = 16, 16
    rows = q.shape[1]
    scale2 = (C_NOPE + C_ROPE) ** -0.5 * math.log2(math.e)
    grid_spec = pltpu.PrefetchScalarGridSpec(
        num_scalar_prefetch=1,
        grid=(n_seq, n_pages // pages),
        in_specs=[pl.BlockSpec((None, rows, C_QK), lambda b, s, pt: (b, 0, 0)),
                  pl.BlockSpec((None,) + new_lat.shape[1:], lambda b, s, pt: (b, 0, 0)),
                  pl.BlockSpec((None,) + new_rope.shape[1:], lambda b, s, pt: (b, 0, 0)),
                  pl.BlockSpec(memory_space=pl.ANY), pl.BlockSpec(memory_space=pl.ANY)],
        out_specs=pl.BlockSpec((None, rows, C_KVLORA), lambda b, s, pt: (b, 0, 0)),
        scratch_shapes=[pltpu.VMEM((2, pages, page, C_KVLORA), F32),
                        pltpu.VMEM((2, pages, C_ROPE, page), F32),
                        pltpu.SemaphoreType.DMA((2, 2)),
                        pltpu.VMEM((rows, 1), F32), pltpu.VMEM((rows, 1), F32),
                        pltpu.VMEM((rows, C_KVLORA), F32)])
    return pl.pallas_call(
        functools.partial(_attn_paged_body, pages=pages, sub=sub, j=j, seq_new=seq_new, scale2=scale2),
        out_shape=jax.ShapeDtypeStruct((n_seq, rows, C_KVLORA), BF16),
        grid_spec=grid_spec,
        compiler_params=_cparams("arbitrary", "arbitrary"),
        name="mla_attention_paged",
    )(page_table, q, new_lat, new_rope, pool_lat, pool_rope_t)


def _mla_out_body(x_ref, g_ref, o_ref, wuv_ref, wout_ref, lg_ref, lb_ref, y_ref):
    parts = [_dot_tn(o_ref[hd], wuv_ref[hd]) for hd in range(C_HEADS)]
    o = jnp.concatenate(parts, axis=1).astype(BF16)
    y_ref[...] = _residual_ln(x_ref[...], g_ref, _dot(o, wout_ref[...]), lg_ref, lb_ref)


def _mla_out(x, mod, tps, layer, j, o_lat, wuv, w_out, ln_g, ln_b, tm):
    T = x.shape[0]
    return pl.pallas_call(
        _mla_out_body,
        out_shape=jax.ShapeDtypeStruct((T, D_MODEL), F32),
        grid=(T // tm,),
        in_specs=[_row_spec(tm, D_MODEL), _mod_spec(mod, layer, 2, tps),
                  pl.BlockSpec((None, C_HEADS, C_KVLORA, tm), lambda i: (i, 0, 0, 0)),
                  _full_spec(wuv, j), _full_spec(w_out, j),
                  _vec_spec(layer, D_MODEL), _vec_spec(layer, D_MODEL)],
        out_specs=_row_spec(tm, D_MODEL),
        compiler_params=_cparams("arbitrary"),
        name="mla_out",
    )(x, mod, o_lat, wuv, w_out, ln_g, ln_b)


def _rope_tables(pos, reps, width):
    half = C_ROPE // 2
    inv = ROPE_THETA ** (-jnp.arange(half, dtype=F32) / half)
    ang = pos.astype(F32)[:, None] * inv
    cos, sin = jnp.cos(ang), jnp.sin(ang)
    cc = jnp.tile(jnp.concatenate([cos, cos], axis=1), (1, reps))
    ss = jnp.tile(jnp.concatenate([-sin, sin], axis=1), (1, reps))
    pad = width - cc.shape[1]
    return jnp.pad(cc, ((0, 0), (0, pad))), jnp.pad(ss, ((0, 0), (0, pad)))


def _prepare_params(p):
    vec = lambda a: a.reshape(a.shape[0], 1, a.shape[1])
    w_uq = p['c_w_uq']
    n_c = w_uq.shape[0]
    c_w_in = jnp.pad(p['c_w_in'], ((0, 0), (0, 0), (0, LANES - C_ROPE)))
    lb_all = jnp.cumsum(jax.nn.softmax(p['b_lb'].astype(F32), axis=0), axis=0)
    lb_all = lb_all - lb_all[:1]
    return dict(
        ln1_g=vec(p['ln1_g']), ln1_b=vec(p['ln1_b']), ln2_g=vec(p['ln2_g']), ln2_b=vec(p['ln2_b']),
        ffn_w1=p['ffn_w1'].astype(BF16), ffn_w2=p['ffn_w2'].astype(BF16),
        a_w_in=p['a_w_in'].astype(BF16), a_ln_g=vec(p['a_ln_g']), a_ln_b=vec(p['a_ln_b']),
        a_w_out=p['a_w_out'].astype(BF16),
        b_w_in=p['b_w_in'].astype(BF16), b_w_out=p['b_w_out'].astype(BF16), lb_all=lb_all,
        c_w_in=c_w_in.astype(BF16), c_g_q=vec(p['c_g_q']), c_g_kv=vec(p['c_g_kv']),
        c_wn=w_uq[..., :C_NOPE].reshape(n_c, C_QLORA, C_HEADS * C_NOPE).astype(BF16),
        c_wr=w_uq[..., C_NOPE:].reshape(n_c, C_QLORA, C_HEADS * C_ROPE).astype(BF16),
        c_wuk=jnp.transpose(p['c_w_uk'], (0, 2, 3, 1)).astype(BF16),
        c_wuv=jnp.transpose(p['c_w_uv'], (0, 2, 1, 3)).astype(BF16),
        c_w_out=p['c_w_out'].astype(BF16),
    )


def _sgu_mixing(w_s, b_s, chunk):
    reps = CHUNK_A // chunk
    causal = jnp.tril(jnp.ones((chunk, chunk), dtype=bool))
    ws = jnp.where(causal, w_s[:, :, :chunk, :chunk], 0)
    eye = jnp.eye(reps, dtype=w_s.dtype)
    ws = jnp.einsum('ab,jgts->jgatbs', eye, ws).reshape(w_s.shape[0], A_GROUPS, CHUNK_A, CHUNK_A)
    bias = jnp.tile(jnp.transpose(b_s[:, :, :chunk], (0, 2, 1)), (1, reps, 1))
    bias = jnp.repeat(bias, A_GDIM, axis=2)
    return ws.astype(BF16), bias


def _run_trunk(x, mod, n_seq, seq_len, q_pos, hgrn_state0, mla_cache, prm, raw, tm):
    T = x.shape[0]
    per_seq_mod = mod.shape[3] == 1
    tps = (seq_len // tm) if per_seq_mod else 1
    sgu_chunk = min(CHUNK_A, seq_len)
    ws, bias = _sgu_mixing(raw['a_w_s'], raw['a_b_s'], sgu_chunk)
    chunk_v, hgrn_states, lat_rows, rope_rows = [], [], [], []
    for i in range(DEPTH):
        kind, j = i % N_MIXERS, i // N_MIXERS
        if kind == 0:
            x, v_rows = _sgu_layer(x, mod, tps, i, j, prm['a_w_in'], prm['a_ln_g'], prm['a_ln_b'], ws, bias,
                                   prm['a_w_out'], prm['ln1_g'], prm['ln1_b'], tm, emit_v=mla_cache is not None)
            chunk_v.append(v_rows)
        elif kind == 1:
            lb = prm['lb_all'][i].reshape(1, D_MODEL)
            q, k, lf, v, gs = _hgrn_proj(x, mod, tps, i, j, prm['b_w_in'], lb, tm)
            if seq_len % 64 == 0:
                C, lpad, hb = 64, seq_len, 2
                rec_tm = min(seq_len, 512)
                rec_in = (q, k, lf, v)
            else:
                C = lpad = rec_tm = SUBLANES
                hb = B_HEADS
                padseq = lambda a: jnp.pad(a.reshape(n_seq, seq_len, D_MODEL),
                                           ((0, 0), (0, lpad - seq_len), (0, 0))).reshape(n_seq * lpad, D_MODEL)
                rec_in = tuple(padseq(a) for a in (q, k, lf, v))
            s0 = None if hgrn_state0 is None else hgrn_state0[j]
            o, S = _hgrn_rec(*rec_in, s0, n_seq, lpad, rec_tm, C, hb)
            if lpad != seq_len:
                o = o.reshape(n_seq, lpad, D_MODEL)[:, :seq_len].reshape(T, D_MODEL)
            hgrn_states.append(S)
            x = _hgrn_out(x, mod, tps, i, j, o, gs, prm['b_w_out'], prm['ln1_g'], prm['ln1_b'], tm)
        else:
            tq = min(tm, 256)
            tps_q = (seq_len // tq) if per_seq_mod else 1
            pos_rows = q_pos if tq <= seq_len else jnp.tile(q_pos, tq // seq_len)
            ccq, ssq = _rope_tables(pos_rows, C_HEADS, C_HEADS * C_ROPE)
            cck, ssk = _rope_tables(pos_rows, 1, LANES)
            qcat, kcat, klat_t, lat, kr = _mla_proj(x, mod, tps_q, pos_rows.shape[0] // tq, i, j, prm['c_w_in'],
                                            prm['c_g_q'], prm['c_g_kv'], prm['c_wn'], prm['c_wr'], prm['c_wuk'],
                                            ccq, ssq, cck, ssk, tq)
            if mla_cache is None:
                o_lat = _attn_prompt(qcat, kcat, klat_t, n_seq, seq_len, tq, min(seq_len, 512))
            else:
                pool_lat, pool_rope_t, pt = mla_cache
                qs = qcat.reshape(C_HEADS, n_seq, seq_len, C_QK).transpose(1, 0, 2, 3)
                qs = qs.reshape(n_seq, C_HEADS * seq_len, C_QK)
                padk = lambda a: jnp.pad(a.reshape(n_seq, seq_len, a.shape[1]), ((0, 0), (0, 16 - seq_len), (0, 0)))
                o_s = _attn_paged(qs, padk(lat), padk(kr), pool_lat, pool_rope_t, pt, j, seq_len)
                o_lat = o_s.reshape(n_seq, C_HEADS, seq_len, C_KVLORA).transpose(1, 3, 0, 2)
                o_lat = o_lat.reshape(1, C_HEADS, C_KVLORA, T)
            x = _mla_out(x, mod, tps_q, i, j, o_lat, prm['c_wuv'], prm['c_w_out'], prm['ln1_g'], prm['ln1_b'], tq)
            lat_rows.append(lat.reshape(n_seq, seq_len, C_KVLORA))
            rope_rows.append(kr.reshape(n_seq, seq_len, C_ROPE))
        x = _ffn_layer(x, mod, tps, i, prm['ffn_w1'], prm['ffn_w2'], prm['ln2_g'], prm['ln2_b'], tm)
    stack = lambda xs: jnp.stack(xs) if xs and xs[0] is not None else None
    return x, stack(chunk_v), jnp.stack(hgrn_states), jnp.stack(lat_rows), jnp.stack(rope_rows)


def kernel(x_prompt, x_sample, cache_kv_latent, cache_k_rope, state_hgrn, page_table, c_prompt, c_sample,
           w_ada, b_ada, ln1_g, ln1_b, ln2_g, ln2_b, ffn_w1, ffn_w2, a_w_in, a_ln_g, a_ln_b, a_w_s, a_b_s,
           a_w_out, b_w_in, b_lb, b_w_out, c_w_in, c_g_q, c_g_kv, c_w_uq, c_w_uk, c_w_uv, c_w_out):
    raw = dict(ln1_g=ln1_g, ln1_b=ln1_b, ln2_g=ln2_g, ln2_b=ln2_b, ffn_w1=ffn_w1, ffn_w2=ffn_w2,
               a_w_in=a_w_in, a_ln_g=a_ln_g, a_ln_b=a_ln_b, a_w_s=a_w_s, a_b_s=a_b_s, a_w_out=a_w_out,
               b_w_in=b_w_in, b_lb=b_lb, b_w_out=b_w_out, c_w_in=c_w_in, c_g_q=c_g_q, c_g_kv=c_g_kv,
               c_w_uq=c_w_uq, c_w_uk=c_w_uk, c_w_uv=c_w_uv, c_w_out=c_w_out)
    prm = _prepare_params(raw)
    nb, seq, d = x_prompt.shape
    ns, sseq, _ = x_sample.shape
    past_len = page_table.shape[1] * cache_kv_latent.shape[2]
    pos_prompt = jnp.arange(seq, dtype=jnp.int32)
    pos_sample = past_len + jnp.arange(sseq, dtype=jnp.int32)

    mod = _modulation(jnp.concatenate([c_prompt, c_sample], axis=0), w_ada, b_ada)
    mod = mod.reshape(DEPTH, nb + ns, 6, d).transpose(0, 2, 1, 3)
    mod_p = mod[:, :, :nb].reshape(DEPTH, 6, nb, 1, d)
    mod_s = jnp.repeat(mod[:, :, nb:], sseq, axis=2).reshape(DEPTH, 6, 1, ns * sseq, d)

    tm_p = 512
    y_p, _, hs_p, lat_p, rope_p = _run_trunk(x_prompt.reshape(nb * seq, d), mod_p, nb, seq, pos_prompt,
                                             None, None, prm, raw, tm_p)
    y_s, v_s, hs_s, lat_s, rope_s = _run_trunk(x_sample.reshape(ns * sseq, d), mod_s, ns, sseq, pos_sample,
                                               state_hgrn,
                                               (cache_kv_latent, jnp.swapaxes(cache_k_rope, 2, 3), page_table),
                                               prm, raw, ns * sseq)
    return (y_p.reshape(nb, seq, d), y_s.reshape(ns, sseq, d), hs_p, hs_s, lat_p, rope_p, lat_s, rope_s,
            v_s.reshape(v_s.shape[0], ns, sseq, d))
```

```python
import functools
import math

import jax
import jax.numpy as jnp
from jax import lax
from jax.experimental import pallas as pl
from jax.experimental.pallas import tpu as pltpu

F32 = jnp.float32
BF16 = jnp.bfloat16

D_MODEL = 1024
DEPTH = 4
N_MIXERS = 3
CHUNK_A = 128
A_GROUPS = 8
A_GDIM = D_MODEL // A_GROUPS
B_HEADS = 8
B_DK = 128
B_DV = D_MODEL // B_HEADS
C_HEADS = 8
C_NOPE = 128
C_ROPE = 64
C_V = 128
C_QLORA = 512
C_KVLORA = 256
ROPE_THETA = 10000.0
D_FF = 4 * D_MODEL
ALPHA = (2.0 * DEPTH) ** 0.25
EPS = 1e-6

LANES = 128
SUBLANES = 8
C_QK = C_KVLORA + LANES
VMEM_LIMIT = 56 * 1024 * 1024


def _cparams(*sem):
    return pltpu.CompilerParams(dimension_semantics=sem, vmem_limit_bytes=VMEM_LIMIT)


def _dot(a, b):
    return jnp.dot(a, b, preferred_element_type=F32)


def _dot_nt(a, b):
    return lax.dot_general(a, b, (((1,), (1,)), ((), ())), preferred_element_type=F32)


def _dot_tn(a, b):
    return lax.dot_general(a, b, (((0,), (0,)), ((), ())), preferred_element_type=F32)


def _layer_norm(y, g, b):
    mu = jnp.mean(y, axis=-1, keepdims=True)
    yc = y - mu
    var = jnp.mean(yc * yc, axis=-1, keepdims=True)
    return yc * lax.rsqrt(var + EPS) * g + b


def _rms(y):
    return y * lax.rsqrt(jnp.mean(y * y, axis=-1, keepdims=True) + EPS)


def _silu(x):
    return x * jax.nn.sigmoid(x)


def _gelu_tanh(x):
    return 0.5 * x * (1.0 + jnp.tanh(math.sqrt(2.0 / math.pi) * (x + 0.044715 * (x * x * x))))


def _modulate(x, sh_ref, sc_ref):
    return x * (1.0 + sc_ref[...]) + sh_ref[...]


def _residual_ln(x, gate_ref, out, lg_ref, lb_ref):
    return _layer_norm(ALPHA * x + gate_ref[...] * out, lg_ref[...], lb_ref[...])


def _mod_spec(mod, layer, which, tiles_per_seq):
    rows = mod.shape[3]
    return pl.BlockSpec((None, None, None, rows, D_MODEL),
                        lambda i: (layer, which, i // tiles_per_seq, 0, 0))


def _vec_spec(layer, width):
    return pl.BlockSpec((None, 1, width), lambda i: (layer, 0, 0))


def _full_spec(arr, layer=None):
    if layer is None:
        nd = arr.ndim
        return pl.BlockSpec(arr.shape, lambda i: (0,) * nd)
    nd = arr.ndim - 1
    return pl.BlockSpec((None,) + arr.shape[1:], lambda i: (layer,) + (0,) * nd)


def _row_spec(tm, width):
    return pl.BlockSpec((tm, width), lambda i: (i, 0))


def _mod_body(c_ref, w_ref, b_ref, o_ref):
    sc = _silu(c_ref[...]).astype(BF16)
    o_ref[...] = _dot(sc, w_ref[...].astype(BF16)) + b_ref[...]


def _modulation(c_all, w_ada, b_ada):
    n = c_all.shape[0]
    tn = 1536
    width = w_ada.shape[2]
    return pl.pallas_call(
        _mod_body,
        out_shape=jax.ShapeDtypeStruct((DEPTH, n, width), F32),
        grid=(DEPTH, width // tn),
        in_specs=[pl.BlockSpec((n, D_MODEL), lambda l, j: (0, 0)),
                  pl.BlockSpec((None, D_MODEL, tn), lambda l, j: (l, 0, j)),
                  pl.BlockSpec((None, 1, tn), lambda l, j: (l, 0, j))],
        out_specs=pl.BlockSpec((None, n, tn), lambda l, j: (l, 0, j)),
        compiler_params=_cparams("arbitrary", "arbitrary"),
        name="adaln_modulation",
    )(c_all, w_ada, b_ada.reshape(DEPTH, 1, width))


def _ffn_body(x_ref, sh_ref, sc_ref, g_ref, w1_ref, w2_ref, lg_ref, lb_ref, o_ref, acc_ref, *, fc):
    x = x_ref[...]
    h = _modulate(x, sh_ref, sc_ref).astype(BF16)
    for c in range(D_FF // fc):
        a = _dot(h, w1_ref[:, c * fc:(c + 1) * fc])
        a = jnp.square(jnp.maximum(a, 0.0)).astype(BF16)
        d = _dot(a, w2_ref[c * fc:(c + 1) * fc, :])
        if c == 0:
            acc_ref[...] = d
        else:
            acc_ref[...] += d
    o_ref[...] = _residual_ln(x, g_ref, acc_ref[...], lg_ref, lb_ref)


def _ffn_layer(x, mod, tps, layer, w1, w2, ln_g, ln_b, tm):
    T = x.shape[0]
    return pl.pallas_call(
        functools.partial(_ffn_body, fc=1024),
        out_shape=jax.ShapeDtypeStruct((T, D_MODEL), F32),
        grid=(T // tm,),
        in_specs=[_row_spec(tm, D_MODEL),
                  _mod_spec(mod, layer, 3, tps), _mod_spec(mod, layer, 4, tps), _mod_spec(mod, layer, 5, tps),
                  _full_spec(w1, layer), _full_spec(w2, layer),
                  _vec_spec(layer, D_MODEL), _vec_spec(layer, D_MODEL)],
        out_specs=_row_spec(tm, D_MODEL),
        scratch_shapes=[pltpu.VMEM((tm, D_MODEL), F32)],
        compiler_params=_cparams("arbitrary"),
        name="ffn_sublayer",
    )(x, mod, mod, mod, w1, w2, ln_g, ln_b)


def _sgu_body(x_ref, sh_ref, sc_ref, g_ref, win_ref, lng_ref, lnb_ref, ws_ref, bias_ref, wout_ref,
              lg_ref, lb_ref, o_ref, *rest, tm, emit_v):
    if emit_v:
        v_ref, gated_ref = rest
    else:
        (gated_ref,) = rest
    x = x_ref[...]
    h = _modulate(x, sh_ref, sc_ref).astype(BF16)
    u = _gelu_tanh(_dot(h, win_ref[:, :D_MODEL]))
    v = _gelu_tanh(_dot(h, win_ref[:, D_MODEL:]))
    v = _layer_norm(v, lng_ref[...], lnb_ref[...])
    if emit_v:
        v_ref[...] = v
    vb = v.astype(BF16)
    nch = tm // CHUNK_A
    for g in range(A_GROUPS):
        c = slice(g * A_GDIM, (g + 1) * A_GDIM)
        rhs = jnp.concatenate([vb[n * CHUNK_A:(n + 1) * CHUNK_A, c] for n in range(nch)], axis=1)
        mixed = _dot(ws_ref[g], rhs)
        for n in range(nch):
            r = slice(n * CHUNK_A, (n + 1) * CHUNK_A)
            gated_ref[r, c] = (u[r, c] * (mixed[:, n * A_GDIM:(n + 1) * A_GDIM] + bias_ref[:, c])).astype(BF16)
    out = _dot(gated_ref[...], wout_ref[...])
    o_ref[...] = _residual_ln(x, g_ref, out, lg_ref, lb_ref)


def _sgu_layer(x, mod, tps, layer, j, w_in, ln_g, ln_b, ws, bias, w_out, ln1_g, ln1_b, tm, emit_v):
    T = x.shape[0]
    out_shape = [jax.ShapeDtypeStruct((T, D_MODEL), F32)]
    out_specs = [_row_spec(tm, D_MODEL)]
    if emit_v:
        out_shape.append(jax.ShapeDtypeStruct((T, D_MODEL), F32))
        out_specs.append(_row_spec(tm, D_MODEL))
    res = pl.pallas_call(
        functools.partial(_sgu_body, tm=tm, emit_v=emit_v),
        out_shape=out_shape,
        grid=(T // tm,),
        in_specs=[_row_spec(tm, D_MODEL),
                  _mod_spec(mod, layer, 0, tps), _mod_spec(mod, layer, 1, tps), _mod_spec(mod, layer, 2, tps),
                  _full_spec(w_in, j), _vec_spec(j, D_MODEL), _vec_spec(j, D_MODEL),
                  _full_spec(ws, j), _full_spec(bias, j), _full_spec(w_out, j),
                  _vec_spec(layer, D_MODEL), _vec_spec(layer, D_MODEL)],
        out_specs=out_specs,
        scratch_shapes=[pltpu.VMEM((tm, D_MODEL), BF16)],
        compiler_params=_cparams("arbitrary"),
        name="sgu_sublayer",
    )(x, mod, mod, mod, w_in, ln_g, ln_b, ws, bias, w_out, ln1_g, ln1_b)
    return (res[0], res[1]) if emit_v else (res[0], None)


def _hgrn_proj_body(x_ref, sh_ref, sc_ref, win_ref, lb_ref, q_ref, k_ref, lf_ref, v_ref, gs_ref):
    h = _modulate(x_ref[...], sh_ref, sc_ref).astype(BF16)
    d = D_MODEL
    q_ref[...] = _silu(_dot(h, win_ref[:, 0:d]))
    fz = _dot(h, win_ref[:, d:2 * d])
    lb = lb_ref[...]
    t = jnp.log1p(jnp.exp(-jnp.abs(fz)))
    a = jnp.log(lb)
    b = jnp.log1p(-lb) + (jnp.minimum(fz, 0.0) - t)
    lf_ref[...] = (jnp.maximum(a, b) + jnp.log1p(jnp.exp(-jnp.abs(a - b)))) * math.log2(math.e)
    k_ref[...] = (1.0 - lb) * jax.nn.sigmoid(-fz)
    v_ref[...] = _dot(h, win_ref[:, 2 * d:3 * d])
    gs_ref[...] = _silu(_dot(h, win_ref[:, 3 * d:4 * d]))


def _hgrn_proj(x, mod, tps, layer, j, w_in, lb, tm):
    T = x.shape[0]
    shp = jax.ShapeDtypeStruct((T, D_MODEL), F32)
    return pl.pallas_call(
        _hgrn_proj_body,
        out_shape=[shp] * 5,
        grid=(T // tm,),
        in_specs=[_row_spec(tm, D_MODEL), _mod_spec(mod, layer, 0, tps), _mod_spec(mod, layer, 1, tps),
                  _full_spec(w_in, j), _full_spec(lb)],
        out_specs=[_row_spec(tm, D_MODEL)] * 5,
        compiler_params=_cparams("arbitrary"),
        name="hgrn_proj",
    )(x, mod, mod, w_in, lb)


def _hgrn_rec_body(*refs, C, nchunk, hb, has_s0):
    if has_s0:
        q_ref, k_ref, g_ref, v_ref, s0_ref, o_ref, sout_ref, st_ref = refs
    else:
        q_ref, k_ref, g_ref, v_ref, o_ref, sout_ref, st_ref = refs
    t = pl.program_id(2)

    @pl.when(t == 0)
    def _():
        for hd in range(hb):
            st_ref[hd] = s0_ref[hd].T if has_s0 else jnp.zeros((B_DV, B_DK), F32)

    row = lax.broadcasted_iota(jnp.int32, (C, B_DK), 0)
    row_a = lax.broadcasted_iota(jnp.int32, (C, C), 0)
    col_a = lax.broadcasted_iota(jnp.int32, (C, C), 1)
    tri = jnp.where(row_a >= col_a, 1.0, 0.0).astype(BF16)
    band = [(col_a == row_a - r) & ((row_a & (SUBLANES - 1)) >= r) for r in range(SUBLANES)]
    levels = []
    for m in (8, 16, 32, 64, 128):
        if 2 * m <= C:
            shift = int(math.log2(2 * m))
            pair = (((row_a >> shift) == (col_a >> shift)) & ((row_a & (2 * m - 1)) >= m)
                    & ((col_a & (2 * m - 1)) < m))
            levels.append((m, (row & (2 * m - 1)) >= m, pair))

    states = [st_ref[hd] for hd in range(hb)]
    for c, hd in [(c, hd) for c in range(nchunk) for hd in range(hb)]:
        rows = slice(c * C, (c + 1) * C)
        cols = slice(hd * B_DK, (hd + 1) * B_DK)
        st = states[hd]
        qc = q_ref[rows, cols]
        kc = k_ref[rows, cols]
        gc = g_ref[rows, cols]
        vb = v_ref[rows, cols].astype(BF16)
        g_hi = gc.astype(BF16)
        r1 = gc - g_hi.astype(F32)
        g_mid = r1.astype(BF16)
        g_lo = (r1 - g_mid.astype(F32)).astype(BF16)
        b3 = _dot(tri, jnp.concatenate([g_hi, g_mid, g_lo], axis=1))
        b = b3[:, 0:B_DK] + b3[:, B_DK:2 * B_DK] + b3[:, 2 * B_DK:3 * B_DK]
        b_last = b[C - 1:C, :]
        o = _dot_nt((qc * jnp.exp2(b)).astype(BF16), st.astype(BF16))
        a_mat = jnp.zeros((C, C), F32)
        for m, upper, pair in levels:
            bref = jnp.concatenate(
                [jnp.broadcast_to(b[i * 2 * m + m - 1:i * 2 * m + m, :], (2 * m, B_DK)) for i in range(C // (2 * m))],
                axis=0)
            q_up = jnp.where(upper, qc * jnp.exp2(b - bref), 0.0)
            k_lo = jnp.where(upper, 0.0, kc * jnp.exp2(bref - b))
            a_mat = jnp.where(pair, _dot_nt(q_up.astype(BF16), k_lo.astype(BF16)), a_mat)
        k3 = kc.reshape(C // SUBLANES, SUBLANES, B_DK)
        b3d = b.reshape(C // SUBLANES, SUBLANES, B_DK)
        for r in range(SUBLANES):
            k_r = kc if r == 0 else pltpu.roll(k3, r, 1).reshape(C, B_DK)
            b_r = b if r == 0 else pltpu.roll(b3d, r, 1).reshape(C, B_DK)
            d_r = jnp.sum(qc * k_r * jnp.exp2(b - b_r), axis=1, keepdims=True)
            a_mat = jnp.where(band[r], d_r, a_mat)
        o_ref[rows, cols] = o + _dot(a_mat.astype(BF16), vb)
        k_dec = (kc * jnp.exp2(b_last - b)).astype(BF16)
        states[hd] = st * jnp.exp2(b_last) + _dot_tn(vb, k_dec)
    for hd in range(hb):
        st_ref[hd] = states[hd]

    @pl.when(t == pl.num_programs(2) - 1)
    def _():
        for hd in range(hb):
            sout_ref[hd] = states[hd].T


def _hgrn_rec(q, k, lf, v, s0, n_seq, seq_len, tm, C, hb):
    T = q.shape[0]
    nt = seq_len // tm
    blk = pl.BlockSpec((tm, hb * B_DK), lambda b, h, t: (b * nt + t, h))
    st_spec = pl.BlockSpec((None, hb, B_DK, B_DV), lambda b, h, t: (b, h, 0, 0))
    has_s0 = s0 is not None
    in_specs = [blk] * 4 + ([st_spec] if has_s0 else [])
    args = (q, k, lf, v) + ((s0,) if has_s0 else ())
    return pl.pallas_call(
        functools.partial(_hgrn_rec_body, C=C, nchunk=tm // C, hb=hb, has_s0=has_s0),
        out_shape=[jax.ShapeDtypeStruct((T, D_MODEL), F32),
                   jax.ShapeDtypeStruct((n_seq, B_HEADS, B_DK, B_DV), F32)],
        grid=(n_seq, B_HEADS // hb, nt),
        in_specs=in_specs,
        out_specs=[blk, st_spec],
        scratch_shapes=[pltpu.VMEM((hb, B_DV, B_DK), F32)],
        compiler_params=_cparams("arbitrary", "arbitrary", "arbitrary"),
        name="hgrn_recurrence",
    )(*args)


def _hgrn_out_body(x_ref, g_ref, o_ref, gs_ref, wout_ref, lg_ref, lb_ref, y_ref):
    o = o_ref[...]
    parts = [_rms(o[:, h * B_DV:(h + 1) * B_DV]) for h in range(B_HEADS)]
    y = (jnp.concatenate(parts, axis=1) * gs_ref[...]).astype(BF16)
    y_ref[...] = _residual_ln(x_ref[...], g_ref, _dot(y, wout_ref[...]), lg_ref, lb_ref)


def _hgrn_out(x, mod, tps, layer, j, o, gs, w_out, ln_g, ln_b, tm):
    T = x.shape[0]
    return pl.pallas_call(
        _hgrn_out_body,
        out_shape=jax.ShapeDtypeStruct((T, D_MODEL), F32),
        grid=(T // tm,),
        in_specs=[_row_spec(tm, D_MODEL), _mod_spec(mod, layer, 2, tps),
                  _row_spec(tm, D_MODEL), _row_spec(tm, D_MODEL),
                  _full_spec(w_out, j), _vec_spec(layer, D_MODEL), _vec_spec(layer, D_MODEL)],
        out_specs=_row_spec(tm, D_MODEL),
        compiler_params=_cparams("arbitrary"),
        name="hgrn_out",
    )(x, mod, o, gs, w_out, ln_g, ln_b)


def _rope_lanes(x, cc_ref, ss_ref, period_first_half):
    n = x.shape[1]
    half = C_ROPE // 2
    rot = jnp.where(period_first_half, pltpu.roll(x, n - half, 1), pltpu.roll(x, half, 1))
    return x * cc_ref[...] + rot * ss_ref[...]


def _mla_proj_body(x_ref, sh_ref, sc_ref, win_ref, gq_ref, gkv_ref, wn_ref, wr_ref, wuk_ref,
                   ccq_ref, ssq_ref, cck_ref, ssk_ref, q_ref, kcat_ref, klt_ref, lat_ref, kr_ref):
    h = _modulate(x_ref[...], sh_ref, sc_ref).astype(BF16)
    a = _dot(h, win_ref[...])
    cq = (_rms(a[:, :C_QLORA]) * gq_ref[...]).astype(BF16)
    ckv = _rms(a[:, C_QLORA:C_QLORA + C_KVLORA]) * gkv_ref[...]
    kr_slab = a[:, C_QLORA + C_KVLORA:]
    lane_k = lax.broadcasted_iota(jnp.int32, kr_slab.shape, 1)
    kr_slab = _rope_lanes(kr_slab, cck_ref, ssk_ref, (lane_k & (C_ROPE - 1)) < C_ROPE // 2)
    lat_ref[...] = ckv
    kr_ref[...] = kr_slab[:, :C_ROPE]
    kcat_ref[...] = jnp.concatenate([ckv, kr_slab], axis=1).astype(BF16)
    klt_ref[...] = ckv.T.astype(BF16)
    qn = _dot(cq, wn_ref[...]).astype(BF16)
    qr = _dot(cq, wr_ref[...])
    lane_q = lax.broadcasted_iota(jnp.int32, qr.shape, 1)
    qr = _rope_lanes(qr, ccq_ref, ssq_ref, (lane_q & (C_ROPE - 1)) < C_ROPE // 2).astype(BF16)
    zeros = jnp.zeros((qr.shape[0], LANES - C_ROPE), BF16)
    for hd in range(C_HEADS):
        ql = _dot(qn[:, hd * C_NOPE:(hd + 1) * C_NOPE], wuk_ref[hd]).astype(BF16)
        q_ref[hd] = jnp.concatenate([ql, qr[:, hd * C_ROPE:(hd + 1) * C_ROPE], zeros], axis=1)


def _mla_proj(x, mod, tps, tab_tiles, layer, j, w_in, g_q, g_kv, wn, wr, wuk, ccq, ssq, cck, ssk, tm):
    T = x.shape[0]
    nt = T // tm
    tab = lambda w: pl.BlockSpec((tm, w), lambda i: (i % tab_tiles, 0))
    return pl.pallas_call(
        _mla_proj_body,
        out_shape=[jax.ShapeDtypeStruct((nt, C_HEADS, tm, C_QK), BF16),
                   jax.ShapeDtypeStruct((T, C_QK), BF16),
                   jax.ShapeDtypeStruct((C_KVLORA, T), BF16),
                   jax.ShapeDtypeStruct((T, C_KVLORA), F32),
                   jax.ShapeDtypeStruct((T, C_ROPE), F32)],
        grid=(nt,),
        in_specs=[_row_spec(tm, D_MODEL), _mod_spec(mod, layer, 0, tps), _mod_spec(mod, layer, 1, tps),
                  _full_spec(w_in, j), _vec_spec(j, C_QLORA), _vec_spec(j, C_KVLORA),
                  _full_spec(wn, j), _full_spec(wr, j), _full_spec(wuk, j),
                  tab(C_HEADS * C_ROPE), tab(C_HEADS * C_ROPE), tab(LANES), tab(LANES)],
        out_specs=[pl.BlockSpec((None, C_HEADS, tm, C_QK), lambda i: (i, 0, 0, 0)),
                   _row_spec(tm, C_QK), pl.BlockSpec((C_KVLORA, tm), lambda i: (0, i)),
                   _row_spec(tm, C_KVLORA), _row_spec(tm, C_ROPE)],
        compiler_params=_cparams("arbitrary"),
        name="mla_proj",
    )(x, mod, mod, w_in, g_q, g_kv, wn, wr, wuk, ccq, ssq, cck, ssk)


def _softmax_update(s, m_ref, l_ref, acc_ref, values):
    m_prev = m_ref[...]
    m_new = jnp.maximum(m_prev, jnp.max(s, axis=-1, keepdims=True))
    alpha = jnp.exp(m_prev - m_new)
    p = jnp.exp(s - m_new)
    l_ref[...] = alpha * l_ref[...] + jnp.sum(p, axis=-1, keepdims=True)
    acc_ref[...] = alpha * acc_ref[...] + _dot(p.astype(BF16), values)
    m_ref[...] = m_new


def _softmax_init(m_ref, l_ref, acc_ref):
    m_ref[...] = jnp.full_like(m_ref, -jnp.inf)
    l_ref[...] = jnp.zeros_like(l_ref)
    acc_ref[...] = jnp.zeros_like(acc_ref)


def _attn_body(qi_ref, kj_ref, last_ref, q_ref, k_ref, kt_ref, o_ref, m_ref, l_ref, acc_ref, *, tq, tk, scale2):
    p_id = pl.program_id(1)
    qi = qi_ref[p_id]
    kj = kj_ref[p_id]

    @pl.when(kj == 0)
    def _():
        _softmax_init(m_ref, l_ref, acc_ref)

    def step(masked):
        k = k_ref[...]
        kt = kt_ref[...]
        if masked:
            key = lax.broadcasted_iota(jnp.int32, (tk, tq), 0) + kj * tk
            tok = lax.broadcasted_iota(jnp.int32, (tk, tq), 1) + qi * tq
            keep = key <= tok
        for hd in range(C_HEADS):
            t = _dot_nt(k, q_ref[hd]) * scale2
            if masked:
                t = jnp.where(keep, t, -jnp.inf)
            m_prev = m_ref[hd]
            m_new = jnp.maximum(m_prev, jnp.max(t, axis=0, keepdims=True))
            alpha = jnp.exp2(m_prev - m_new)
            p = jnp.exp2(t - m_new)
            l_ref[hd] = alpha * l_ref[hd] + jnp.sum(p, axis=0, keepdims=True)
            acc_ref[hd] = alpha * acc_ref[hd] + _dot(kt, p.astype(BF16))
            m_ref[hd] = m_new

    fully_visible = (kj + 1) * tk - 1 <= qi * tq
    pl.when(fully_visible)(lambda: step(False))
    pl.when(jnp.logical_not(fully_visible))(lambda: step(True))

    @pl.when(last_ref[p_id] == 1)
    def _():
        for hd in range(C_HEADS):
            o_ref[hd] = (acc_ref[hd] / l_ref[hd]).astype(BF16)


def _attn_prompt(q, kcat, klat_t, n_seq, seq_len, tq, tk):
    nq, nk = seq_len // tq, seq_len // tk
    pairs = [(i, j) for i in range(nq) for j in range((i * tq + tq - 1) // tk + 1)]
    qi = jnp.asarray([p[0] for p in pairs], jnp.int32)
    kj = jnp.asarray([p[1] for p in pairs], jnp.int32)
    last = jnp.asarray([int(n + 1 == len(pairs) or pairs[n + 1][0] != p[0]) for n, p in enumerate(pairs)], jnp.int32)
    scale2 = (C_NOPE + C_ROPE) ** -0.5 * math.log2(math.e)
    grid_spec = pltpu.PrefetchScalarGridSpec(
        num_scalar_prefetch=3,
        grid=(n_seq, len(pairs)),
        in_specs=[pl.BlockSpec((None, C_HEADS, tq, C_QK), lambda b, p, qi, kj, last: (b * nq + qi[p], 0, 0, 0)),
                  pl.BlockSpec((tk, C_QK), lambda b, p, qi, kj, last: (b * nk + kj[p], 0)),
                  pl.BlockSpec((C_KVLORA, tk), lambda b, p, qi, kj, last: (0, b * nk + kj[p]))],
        out_specs=pl.BlockSpec((None, C_HEADS, C_KVLORA, tq), lambda b, p, qi, kj, last: (b * nq + qi[p], 0, 0, 0)),
        scratch_shapes=[pltpu.VMEM((C_HEADS, 1, tq), F32), pltpu.VMEM((C_HEADS, 1, tq), F32),
                        pltpu.VMEM((C_HEADS, C_KVLORA, tq), F32)])
    return pl.pallas_call(
        functools.partial(_attn_body, tq=tq, tk=tk, scale2=scale2),
        out_shape=jax.ShapeDtypeStruct((n_seq * nq, C_HEADS, C_KVLORA, tq), BF16),
        grid_spec=grid_spec,
        compiler_params=_cparams("arbitrary", "arbitrary"),
        name="mla_attention_prompt",
    )(qi, kj, last, q, kcat, klat_t)


def _attn_paged_body(pt_ref, q_ref, nlat_ref, nrope_ref, lat_hbm, rope_hbm, o_ref,
                     lat_buf, rope_buf, sem, m_ref, l_ref, acc_ref, *, pages, sub, j, seq_new, scale2):
    b = pl.program_id(0)
    g = pl.program_id(1)
    n_groups = pl.num_programs(1)
    step = b * n_groups + g
    slot = step & 1

    def page_copy(kind, page_id, sl, i):
        src, dst = (lat_hbm, lat_buf) if kind == 0 else (rope_hbm, rope_buf)
        return pltpu.make_async_copy(src.at[j, page_id], dst.at[sl, i], sem.at[kind, sl])

    def start_group(bb, gg, sl):
        for i in range(pages):
            page_id = pt_ref[bb, gg * pages + i]
            page_copy(0, page_id, sl, i).start(priority=i % 2)
            page_copy(1, page_id, sl, i).start(priority=(i + 1) % 2)

    @pl.when(step == 0)
    def _():
        start_group(0, 0, 0)

    @pl.when(step + 1 < pl.num_programs(0) * n_groups)
    def _():
        wrap = g == n_groups - 1
        start_group(jnp.where(wrap, b + 1, b), jnp.where(wrap, 0, g + 1), 1 - slot)

    @pl.when(g == 0)
    def _():
        _softmax_init(m_ref, l_ref, acc_ref)

    for i in range(pages):
        page_copy(0, 0, slot, i).wait()
        page_copy(1, 0, slot, i).wait()

    q = q_ref[...]
    ql = q[:, :C_KVLORA]
    qr = q[:, C_KVLORA:C_KVLORA + C_ROPE]

    def update(carry, t, values):
        m_prev, l_prev, acc = carry
        m_new = jnp.maximum(m_prev, jnp.max(t, axis=-1, keepdims=True))
        alpha = jnp.exp2(m_prev - m_new)
        p = jnp.exp2(t - m_new)
        return (m_new, alpha * l_prev + jnp.sum(p, axis=-1, keepdims=True),
                alpha * acc + _dot(p.astype(BF16), values))

    carry = (m_ref[...], l_ref[...], acc_ref[...])
    page = lat_buf.shape[2]
    for u in range(pages // sub):
        lat = lat_buf[slot, u * sub:(u + 1) * sub].reshape(sub * page, C_KVLORA).astype(BF16)
        rp_t = jnp.concatenate([rope_buf[slot, u * sub + i] for i in range(sub)], axis=1).astype(BF16)
        carry = update(carry, (_dot_nt(ql, lat) + _dot(qr, rp_t)) * scale2, lat)

    @pl.when(g < n_groups - 1)
    def _():
        m_ref[...], l_ref[...], acc_ref[...] = carry

    @pl.when(g == n_groups - 1)
    def _():
        nlat = nlat_ref[...].astype(BF16)
        t2 = (_dot_nt(ql, nlat) + _dot_nt(qr, nrope_ref[...].astype(BF16))) * scale2
        tok = lax.broadcasted_iota(jnp.int32, t2.shape, 0) & (seq_new - 1)
        key = lax.broadcasted_iota(jnp.int32, t2.shape, 1)
        _, l_fin, acc = update(carry, jnp.where(key <= tok, t2, -jnp.inf), nlat)
        o_ref[...] = (acc / l_fin).astype(BF16)


def _attn_paged(q, new_lat, new_rope, pool_lat, pool_rope_t, page_table, j, seq_new):
    n_seq, n_pages = page_table.shape
    page = pool_lat.shape[2]
    pages, sub = 16, 16
    rows = q.shape[1]
    scale2 = (C_NOPE + C_ROPE) ** -0.5 * math.log2(math.e)
    grid_spec = pltpu.PrefetchScalarGridSpec(
        num_scalar_prefetch=1,
        grid=(n_seq, n_pages // pages),
        in_specs=[pl.BlockSpec((None, rows, C_QK), lambda b, s, pt: (b, 0, 0)),
                  pl.BlockSpec((None,) + new_lat.shape[1:], lambda b, s, pt: (b, 0, 0)),
                  pl.BlockSpec((None,) + new_rope.shape[1:], lambda b, s, pt: (b, 0, 0)),
                  pl.BlockSpec(memory_space=pl.ANY), pl.BlockSpec(memory_space=pl.ANY)],
        out_specs=pl.BlockSpec((None, rows, C_KVLORA), lambda b, s, pt: (b, 0, 0)),
        scratch_shapes=[pltpu.VMEM((2, pages, page, C_KVLORA), F32),
                        pltpu.VMEM((2, pages, C_ROPE, page), F32),
                        pltpu.SemaphoreType.DMA((2, 2)),
                        pltpu.VMEM((rows, 1), F32), pltpu.VMEM((rows, 1), F32),
                        pltpu.VMEM((rows, C_KVLORA), F32)])
    return pl.pallas_call(
        functools.partial(_attn_paged_body, pages=pages, sub=sub, j=j, seq_new=seq_new, scale2=scale2),
        out_shape=jax.ShapeDtypeStruct((n_seq, rows, C_KVLORA), BF16),
        grid_spec=grid_spec,
        compiler_params=_cparams("arbitrary", "arbitrary"),
        name="mla_attention_paged",
    )(page_table, q, new_lat, new_rope, pool_lat, pool_rope_t)


def _mla_out_body(x_ref, g_ref, o_ref, wuv_ref, wout_ref, lg_ref, lb_ref, y_ref):
    parts = [_dot_tn(o_ref[hd], wuv_ref[hd]) for hd in range(C_HEADS)]
    o = jnp.concatenate(parts, axis=1).astype(BF16)
    y_ref[...] = _residual_ln(x_ref[...], g_ref, _dot(o, wout_ref[...]), lg_ref, lb_ref)


def _mla_out(x, mod, tps, layer, j, o_lat, wuv, w_out, ln_g, ln_b, tm):
    T = x.shape[0]
    return pl.pallas_call(
        _mla_out_body,
        out_shape=jax.ShapeDtypeStruct((T, D_MODEL), F32),
        grid=(T // tm,),
        in_specs=[_row_spec(tm, D_MODEL), _mod_spec(mod, layer, 2, tps),
                  pl.BlockSpec((None, C_HEADS, C_KVLORA, tm), lambda i: (i, 0, 0, 0)),
                  _full_spec(wuv, j), _full_spec(w_out, j),
                  _vec_spec(layer, D_MODEL), _vec_spec(layer, D_MODEL)],
        out_specs=_row_spec(tm, D_MODEL),
        compiler_params=_cparams("arbitrary"),
        name="mla_out",
    )(x, mod, o_lat, wuv, w_out, ln_g, ln_b)


def _rope_tables(pos, reps, width):
    half = C_ROPE // 2
    inv = ROPE_THETA ** (-jnp.arange(half, dtype=F32) / half)
    ang = pos.astype(F32)[:, None] * inv
    cos, sin = jnp.cos(ang), jnp.sin(ang)
    cc = jnp.tile(jnp.concatenate([cos, cos], axis=1), (1, reps))
    ss = jnp.tile(jnp.concatenate([-sin, sin], axis=1), (1, reps))
    pad = width - cc.shape[1]
    return jnp.pad(cc, ((0, 0), (0, pad))), jnp.pad(ss, ((0, 0), (0, pad)))


def _prepare_params(p):
    vec = lambda a: a.reshape(a.shape[0], 1, a.shape[1])
    w_uq = p['c_w_uq']
    n_c = w_uq.shape[0]
    c_w_in = jnp.pad(p['c_w_in'], ((0, 0), (0, 0), (0, LANES - C_ROPE)))
    lb_all = jnp.cumsum(jax.nn.softmax(p['b_lb'].astype(F32), axis=0), axis=0)
    lb_all = lb_all - lb_all[:1]
    return dict(
        ln1_g=vec(p['ln1_g']), ln1_b=vec(p['ln1_b']), ln2_g=vec(p['ln2_g']), ln2_b=vec(p['ln2_b']),
        ffn_w1=p['ffn_w1'].astype(BF16), ffn_w2=p['ffn_w2'].astype(BF16),
        a_w_in=p['a_w_in'].astype(BF16), a_ln_g=vec(p['a_ln_g']), a_ln_b=vec(p['a_ln_b']),
        a_w_out=p['a_w_out'].astype(BF16),
        b_w_in=p['b_w_in'].astype(BF16), b_w_out=p['b_w_out'].astype(BF16), lb_all=lb_all,
        c_w_in=c_w_in.astype(BF16), c_g_q=vec(p['c_g_q']), c_g_kv=vec(p['c_g_kv']),
        c_wn=w_uq[..., :C_NOPE].reshape(n_c, C_QLORA, C_HEADS * C_NOPE).astype(BF16),
        c_wr=w_uq[..., C_NOPE:].reshape(n_c, C_QLORA, C_HEADS * C_ROPE).astype(BF16),
        c_wuk=jnp.transpose(p['c_w_uk'], (0, 2, 3, 1)).astype(BF16),
        c_wuv=jnp.transpose(p['c_w_uv'], (0, 2, 1, 3)).astype(BF16),
        c_w_out=p['c_w_out'].astype(BF16),
    )


def _sgu_mixing(w_s, b_s, chunk):
    reps = CHUNK_A // chunk
    causal = jnp.tril(jnp.ones((chunk, chunk), dtype=bool))
    ws = jnp.where(causal, w_s[:, :, :chunk, :chunk], 0)
    eye = jnp.eye(reps, dtype=w_s.dtype)
    ws = jnp.einsum('ab,jgts->jgatbs', eye, ws).reshape(w_s.shape[0], A_GROUPS, CHUNK_A, CHUNK_A)
    bias = jnp.tile(jnp.transpose(b_s[:, :, :chunk], (0, 2, 1)), (1, reps, 1))
    bias = jnp.repeat(bias, A_GDIM, axis=2)
    return ws.astype(BF16), bias


def _run_trunk(x, mod, n_seq, seq_len, q_pos, hgrn_state0, mla_cache, prm, raw, tm):
    T = x.shape[0]
    per_seq_mod = mod.shape[3] == 1
    tps = (seq_len // tm) if per_seq_mod else 1
    sgu_chunk = min(CHUNK_A, seq_len)
    ws, bias = _sgu_mixing(raw['a_w_s'], raw['a_b_s'], sgu_chunk)
    chunk_v, hgrn_states, lat_rows, rope_rows = [], [], [], []
    for i in range(DEPTH):
        kind, j = i % N_MIXERS, i // N_MIXERS
        if kind == 0:
            x, v_rows = _sgu_layer(x, mod, tps, i, j, prm['a_w_in'], prm['a_ln_g'], prm['a_ln_b'], ws, bias,
                                   prm['a_w_out'], prm['ln1_g'], prm['ln1_b'], tm, emit_v=mla_cache is not None)
            chunk_v.append(v_rows)
        elif kind == 1:
            lb = prm['lb_all'][i].reshape(1, D_MODEL)
            q, k, lf, v, gs = _hgrn_proj(x, mod, tps, i, j, prm['b_w_in'], lb, tm)
            if seq_len % 64 == 0:
                C, lpad, hb = 64, seq_len, 4
                rec_tm = min(seq_len, 512)
                rec_in = (q, k, lf, v)
            else:
                C = lpad = rec_tm = SUBLANES
                hb = B_HEADS
                padseq = lambda a: jnp.pad(a.reshape(n_seq, seq_len, D_MODEL),
                                           ((0, 0), (0, lpad - seq_len), (0, 0))).reshape(n_seq * lpad, D_MODEL)
                rec_in = tuple(padseq(a) for a in (q, k, lf, v))
            s0 = None if hgrn_state0 is None else hgrn_state0[j]
            o, S = _hgrn_rec(*rec_in, s0, n_seq, lpad, rec_tm, C, hb)
            if lpad != seq_len:
                o = o.reshape(n_seq, lpad, D_MODEL)[:, :seq_len].reshape(T, D_MODEL)
            hgrn_states.append(S)
            x = _hgrn_out(x, mod, tps, i, j, o, gs, prm['b_w_out'], prm['ln1_g'], prm['ln1_b'], tm)
        else:
            tq = min(tm, 512)
            tps_q = (seq_len // tq) if per_seq_mod else 1
            pos_rows = q_pos if tq <= seq_len else jnp.tile(q_pos, tq // seq_len)
            ccq, ssq = _rope_tables(pos_rows, C_HEADS, C_HEADS * C_ROPE)
            cck, ssk = _rope_tables(pos_rows, 1, LANES)
            qcat, kcat, klat_t, lat, kr = _mla_proj(x, mod, tps_q, pos_rows.shape[0] // tq, i, j, prm['c_w_in'],
                                            prm['c_g_q'], prm['c_g_kv'], prm['c_wn'], prm['c_wr'], prm['c_wuk'],
                                            ccq, ssq, cck, ssk, tq)
            if mla_cache is None:
                o_lat = _attn_prompt(qcat, kcat, klat_t, n_seq, seq_len, tq, min(seq_len, 512))
            else:
                pool_lat, pool_rope_t, pt = mla_cache
                qs = qcat.reshape(C_HEADS, n_seq, seq_len, C_QK).transpose(1, 0, 2, 3)
                qs = qs.reshape(n_seq, C_HEADS * seq_len, C_QK)
                padk = lambda a: jnp.pad(a.reshape(n_seq, seq_len, a.shape[1]), ((0, 0), (0, 16 - seq_len), (0, 0)))
                o_s = _attn_paged(qs, padk(lat), padk(kr), pool_lat, pool_rope_t, pt, j, seq_len)
                o_lat = o_s.reshape(n_seq, C_HEADS, seq_len, C_KVLORA).transpose(1, 3, 0, 2)
                o_lat = o_lat.reshape(1, C_HEADS, C_KVLORA, T)
            x = _mla_out(x, mod, tps_q, i, j, o_lat, prm['c_wuv'], prm['c_w_out'], prm['ln1_g'], prm['ln1_b'], tq)
            lat_rows.append(lat.reshape(n_seq, seq_len, C_KVLORA))
            rope_rows.append(kr.reshape(n_seq, seq_len, C_ROPE))
        x = _ffn_layer(x, mod, tps, i, prm['ffn_w1'], prm['ffn_w2'], prm['ln2_g'], prm['ln2_b'], tm)
    stack = lambda xs: jnp.stack(xs) if xs and xs[0] is not None else None
    return x, stack(chunk_v), jnp.stack(hgrn_states), jnp.stack(lat_rows), jnp.stack(rope_rows)


def kernel(x_prompt, x_sample, cache_kv_latent, cache_k_rope, state_hgrn, page_table, c_prompt, c_sample,
           w_ada, b_ada, ln1_g, ln1_b, ln2_g, ln2_b, ffn_w1, ffn_w2, a_w_in, a_ln_g, a_ln_b, a_w_s, a_b_s,
           a_w_out, b_w_in, b_lb, b_w_out, c_w_in, c_g_q, c_g_kv, c_w_uq, c_w_uk, c_w_uv, c_w_out):
    raw = dict(ln1_g=ln1_g, ln1_b=ln1_b, ln2_g=ln2_g, ln2_b=ln2_b, ffn_w1=ffn_w1, ffn_w2=ffn_w2,
               a_w_in=a_w_in, a_ln_g=a_ln_g, a_ln_b=a_ln_b, a_w_s=a_w_s, a_b_s=a_b_s, a_w_out=a_w_out,
               b_w_in=b_w_in, b_lb=b_lb, b_w_out=b_w_out, c_w_in=c_w_in, c_g_q=c_g_q, c_g_kv=c_g_kv,
               c_w_uq=c_w_uq, c_w_uk=c_w_uk, c_w_uv=c_w_uv, c_w_out=c_w_out)
    prm = _prepare_params(raw)
    nb, seq, d = x_prompt.shape
    ns, sseq, _ = x_sample.shape
    past_len = page_table.shape[1] * cache_kv_latent.shape[2]
    pos_prompt = jnp.arange(seq, dtype=jnp.int32)
    pos_sample = past_len + jnp.arange(sseq, dtype=jnp.int32)

    mod = _modulation(jnp.concatenate([c_prompt, c_sample], axis=0), w_ada, b_ada)
    mod = mod.reshape(DEPTH, nb + ns, 6, d).transpose(0, 2, 1, 3)
    mod_p = mod[:, :, :nb].reshape(DEPTH, 6, nb, 1, d)
    mod_s = jnp.repeat(mod[:, :, nb:], sseq, axis=2).reshape(DEPTH, 6, 1, ns * sseq, d)

    tm_p = 512
    y_p, _, hs_p, lat_p, rope_p = _run_trunk(x_prompt.reshape(nb * seq, d), mod_p, nb, seq, pos_prompt,
                                             None, None, prm, raw, tm_p)
    y_s, v_s, hs_s, lat_s, rope_s = _run_trunk(x_sample.reshape(ns * sseq, d), mod_s, ns, sseq, pos_sample,
                                               state_hgrn,
                                               (cache_kv_latent, jnp.swapaxes(cache_k_rope, 2, 3), page_table),
                                               prm, raw, ns * sseq)
    return (y_p.reshape(nb, seq, d), y_s.reshape(ns, sseq, d), hs_p, hs_s, lat_p, rope_p, lat_s, rope_s,
            v_s.reshape(v_s.shape[0], ns, sseq, d))
```

```python
import functools
import math

import jax
import jax.numpy as jnp
from jax import lax
from jax.experimental import pallas as pl
from jax.experimental.pallas import tpu as pltpu

F32 = jnp.float32
BF16 = jnp.bfloat16

D_MODEL = 1024
DEPTH = 4
N_MIXERS = 3
CHUNK_A = 128
A_GROUPS = 8
A_GDIM = D_MODEL // A_GROUPS
B_HEADS = 8
B_DK = 128
B_DV = D_MODEL // B_HEADS
C_HEADS = 8
C_NOPE = 128
C_ROPE = 64
C_V = 128
C_QLORA = 512
C_KVLORA = 256
ROPE_THETA = 10000.0
D_FF = 4 * D_MODEL
ALPHA = (2.0 * DEPTH) ** 0.25
EPS = 1e-6

LANES = 128
SUBLANES = 8
C_QK = C_KVLORA + LANES
VMEM_LIMIT = 56 * 1024 * 1024


def _cparams(*sem):
    return pltpu.CompilerParams(dimension_semantics=sem, vmem_limit_bytes=VMEM_LIMIT)


def _dot(a, b):
    return jnp.dot(a, b, preferred_element_type=F32)


def _dot_nt(a, b):
    return lax.dot_general(a, b, (((1,), (1,)), ((), ())), preferred_element_type=F32)


def _dot_tn(a, b):
    return lax.dot_general(a, b, (((0,), (0,)), ((), ())), preferred_element_type=F32)


def _layer_norm(y, g, b):
    mu = jnp.mean(y, axis=-1, keepdims=True)
    yc = y - mu
    var = jnp.mean(yc * yc, axis=-1, keepdims=True)
    return yc * lax.rsqrt(var + EPS) * g + b


def _rms(y):
    return y * lax.rsqrt(jnp.mean(y * y, axis=-1, keepdims=True) + EPS)


def _silu(x):
    return x * jax.nn.sigmoid(x)


def _gelu_tanh(x):
    return 0.5 * x * (1.0 + jnp.tanh(math.sqrt(2.0 / math.pi) * (x + 0.044715 * (x * x * x))))


def _modulate(x, sh_ref, sc_ref):
    return x * (1.0 + sc_ref[...]) + sh_ref[...]


def _residual_ln(x, gate_ref, out, lg_ref, lb_ref):
    return _layer_norm(ALPHA * x + gate_ref[...] * out, lg_ref[...], lb_ref[...])


def _mod_spec(mod, layer, which, tiles_per_seq):
    rows = mod.shape[3]
    return pl.BlockSpec((None, None, None, rows, D_MODEL),
                        lambda i: (layer, which, i // tiles_per_seq, 0, 0))


def _vec_spec(layer, width):
    return pl.BlockSpec((None, 1, width), lambda i: (layer, 0, 0))


def _full_spec(arr, layer=None):
    if layer is None:
        nd = arr.ndim
        return pl.BlockSpec(arr.shape, lambda i: (0,) * nd)
    nd = arr.ndim - 1
    return pl.BlockSpec((None,) + arr.shape[1:], lambda i: (layer,) + (0,) * nd)


def _row_spec(tm, width):
    return pl.BlockSpec((tm, width), lambda i: (i, 0))


def _mod_body(c_ref, w_ref, b_ref, o_ref):
    sc = _silu(c_ref[...]).astype(BF16)
    o_ref[...] = _dot(sc, w_ref[...].astype(BF16)) + b_ref[...]


def _modulation(c_all, w_ada, b_ada):
    n = c_all.shape[0]
    tn = 1536
    width = w_ada.shape[2]
    return pl.pallas_call(
        _mod_body,
        out_shape=jax.ShapeDtypeStruct((DEPTH, n, width), F32),
        grid=(DEPTH, width // tn),
        in_specs=[pl.BlockSpec((n, D_MODEL), lambda l, j: (0, 0)),
                  pl.BlockSpec((None, D_MODEL, tn), lambda l, j: (l, 0, j)),
                  pl.BlockSpec((None, 1, tn), lambda l, j: (l, 0, j))],
        out_specs=pl.BlockSpec((None, n, tn), lambda l, j: (l, 0, j)),
        compiler_params=_cparams("arbitrary", "arbitrary"),
        name="adaln_modulation",
    )(c_all, w_ada, b_ada.reshape(DEPTH, 1, width))


def _ffn_body(x_ref, sh_ref, sc_ref, g_ref, w1_ref, w2_ref, lg_ref, lb_ref, o_ref, acc_ref, *, fc):
    x = x_ref[...]
    h = _modulate(x, sh_ref, sc_ref).astype(BF16)
    for c in range(D_FF // fc):
        a = _dot(h, w1_ref[:, c * fc:(c + 1) * fc])
        a = jnp.square(jnp.maximum(a, 0.0)).astype(BF16)
        d = _dot(a, w2_ref[c * fc:(c + 1) * fc, :])
        if c == 0:
            acc_ref[...] = d
        else:
            acc_ref[...] += d
    o_ref[...] = _residual_ln(x, g_ref, acc_ref[...], lg_ref, lb_ref)


def _ffn_layer(x, mod, tps, layer, w1, w2, ln_g, ln_b, tm):
    T = x.shape[0]
    return pl.pallas_call(
        functools.partial(_ffn_body, fc=1024),
        out_shape=jax.ShapeDtypeStruct((T, D_MODEL), F32),
        grid=(T // tm,),
        in_specs=[_row_spec(tm, D_MODEL),
                  _mod_spec(mod, layer, 3, tps), _mod_spec(mod, layer, 4, tps), _mod_spec(mod, layer, 5, tps),
                  _full_spec(w1, layer), _full_spec(w2, layer),
                  _vec_spec(layer, D_MODEL), _vec_spec(layer, D_MODEL)],
        out_specs=_row_spec(tm, D_MODEL),
        scratch_shapes=[pltpu.VMEM((tm, D_MODEL), F32)],
        compiler_params=_cparams("arbitrary"),
        name="ffn_sublayer",
    )(x, mod, mod, mod, w1, w2, ln_g, ln_b)


def _sgu_body(x_ref, sh_ref, sc_ref, g_ref, win_ref, lng_ref, lnb_ref, ws_ref, bias_ref, wout_ref,
              lg_ref, lb_ref, o_ref, *rest, tm, emit_v):
    if emit_v:
        v_ref, gated_ref = rest
    else:
        (gated_ref,) = rest
    x = x_ref[...]
    h = _modulate(x, sh_ref, sc_ref).astype(BF16)
    u = _gelu_tanh(_dot(h, win_ref[:, :D_MODEL]))
    v = _gelu_tanh(_dot(h, win_ref[:, D_MODEL:]))
    v = _layer_norm(v, lng_ref[...], lnb_ref[...])
    if emit_v:
        v_ref[...] = v
    vb = v.astype(BF16)
    nch = tm // CHUNK_A
    for g in range(A_GROUPS):
        c = slice(g * A_GDIM, (g + 1) * A_GDIM)
        rhs = jnp.concatenate([vb[n * CHUNK_A:(n + 1) * CHUNK_A, c] for n in range(nch)], axis=1)
        mixed = _dot(ws_ref[g], rhs)
        for n in range(nch):
            r = slice(n * CHUNK_A, (n + 1) * CHUNK_A)
            gated_ref[r, c] = (u[r, c] * (mixed[:, n * A_GDIM:(n + 1) * A_GDIM] + bias_ref[:, c])).astype(BF16)
    out = _dot(gated_ref[...], wout_ref[...])
    o_ref[...] = _residual_ln(x, g_ref, out, lg_ref, lb_ref)


def _sgu_layer(x, mod, tps, layer, j, w_in, ln_g, ln_b, ws, bias, w_out, ln1_g, ln1_b, tm, emit_v):
    T = x.shape[0]
    out_shape = [jax.ShapeDtypeStruct((T, D_MODEL), F32)]
    out_specs = [_row_spec(tm, D_MODEL)]
    if emit_v:
        out_shape.append(jax.ShapeDtypeStruct((T, D_MODEL), F32))
        out_specs.append(_row_spec(tm, D_MODEL))
    res = pl.pallas_call(
        functools.partial(_sgu_body, tm=tm, emit_v=emit_v),
        out_shape=out_shape,
        grid=(T // tm,),
        in_specs=[_row_spec(tm, D_MODEL),
                  _mod_spec(mod, layer, 0, tps), _mod_spec(mod, layer, 1, tps), _mod_spec(mod, layer, 2, tps),
                  _full_spec(w_in, j), _vec_spec(j, D_MODEL), _vec_spec(j, D_MODEL),
                  _full_spec(ws, j), _full_spec(bias, j), _full_spec(w_out, j),
                  _vec_spec(layer, D_MODEL), _vec_spec(layer, D_MODEL)],
        out_specs=out_specs,
        scratch_shapes=[pltpu.VMEM((tm, D_MODEL), BF16)],
        compiler_params=_cparams("arbitrary"),
        name="sgu_sublayer",
    )(x, mod, mod, mod, w_in, ln_g, ln_b, ws, bias, w_out, ln1_g, ln1_b)
    return (res[0], res[1]) if emit_v else (res[0], None)


def _hgrn_proj_body(x_ref, sh_ref, sc_ref, win_ref, lb_ref, q_ref, k_ref, lf_ref, v_ref, gs_ref):
    h = _modulate(x_ref[...], sh_ref, sc_ref).astype(BF16)
    d = D_MODEL
    q_ref[...] = _silu(_dot(h, win_ref[:, 0:d]))
    fz = _dot(h, win_ref[:, d:2 * d])
    lb = lb_ref[...]
    t = jnp.log1p(jnp.exp(-jnp.abs(fz)))
    a = jnp.log(lb)
    b = jnp.log1p(-lb) + (jnp.minimum(fz, 0.0) - t)
    lf_ref[...] = (jnp.maximum(a, b) + jnp.log1p(jnp.exp(-jnp.abs(a - b)))) * math.log2(math.e)
    k_ref[...] = (1.0 - lb) * jax.nn.sigmoid(-fz)
    v_ref[...] = _dot(h, win_ref[:, 2 * d:3 * d])
    gs_ref[...] = _silu(_dot(h, win_ref[:, 3 * d:4 * d]))


def _hgrn_proj(x, mod, tps, layer, j, w_in, lb, tm):
    T = x.shape[0]
    shp = jax.ShapeDtypeStruct((T, D_MODEL), F32)
    return pl.pallas_call(
        _hgrn_proj_body,
        out_shape=[shp] * 5,
        grid=(T // tm,),
        in_specs=[_row_spec(tm, D_MODEL), _mod_spec(mod, layer, 0, tps), _mod_spec(mod, layer, 1, tps),
                  _full_spec(w_in, j), _full_spec(lb)],
        out_specs=[_row_spec(tm, D_MODEL)] * 5,
        compiler_params=_cparams("arbitrary"),
        name="hgrn_proj",
    )(x, mod, mod, w_in, lb)


def _hgrn_rec_body(*refs, C, nchunk, hb, has_s0):
    if has_s0:
        q_ref, k_ref, g_ref, v_ref, s0_ref, o_ref, sout_ref, st_ref = refs
    else:
        q_ref, k_ref, g_ref, v_ref, o_ref, sout_ref, st_ref = refs
    t = pl.program_id(2)

    @pl.when(t == 0)
    def _():
        for hd in range(hb):
            st_ref[hd] = s0_ref[hd].T if has_s0 else jnp.zeros((B_DV, B_DK), F32)

    row = lax.broadcasted_iota(jnp.int32, (C, B_DK), 0)
    row_a = lax.broadcasted_iota(jnp.int32, (C, C), 0)
    col_a = lax.broadcasted_iota(jnp.int32, (C, C), 1)
    tri = jnp.where(row_a >= col_a, 1.0, 0.0).astype(BF16)
    band = [(col_a == row_a - r) & ((row_a & (SUBLANES - 1)) >= r) for r in range(SUBLANES)]
    levels = []
    for m in (8, 16, 32, 64, 128):
        if 2 * m <= C:
            shift = int(math.log2(2 * m))
            pair = (((row_a >> shift) == (col_a >> shift)) & ((row_a & (2 * m - 1)) >= m)
                    & ((col_a & (2 * m - 1)) < m))
            levels.append((m, (row & (2 * m - 1)) >= m, pair))

    def prefix(c, hd):
        rows = slice(c * C, (c + 1) * C)
        cols = slice(hd * B_DK, (hd + 1) * B_DK)
        gc = g_ref[rows, cols]
        g_hi = gc.astype(BF16)
        r1 = gc - g_hi.astype(F32)
        g_mid = r1.astype(BF16)
        g_lo = (r1 - g_mid.astype(F32)).astype(BF16)
        b3 = _dot(tri, jnp.concatenate([g_hi, g_mid, g_lo], axis=1))
        return rows, cols, b3

    def products(pre, st):
        rows, cols, b3 = pre
        qc = q_ref[rows, cols]
        kc = k_ref[rows, cols]
        b = b3[:, 0:B_DK] + b3[:, B_DK:2 * B_DK] + b3[:, 2 * B_DK:3 * B_DK]
        o_inter = _dot_nt((qc * jnp.exp2(b)).astype(BF16), st.astype(BF16))
        level_dots = []
        for m, upper, pair in levels:
            bref = jnp.concatenate(
                [jnp.broadcast_to(b[i * 2 * m + m - 1:i * 2 * m + m, :], (2 * m, B_DK)) for i in range(C // (2 * m))],
                axis=0)
            q_up = jnp.where(upper, qc * jnp.exp2(b - bref), 0.0)
            k_lo = jnp.where(upper, 0.0, kc * jnp.exp2(bref - b))
            level_dots.append(_dot_nt(q_up.astype(BF16), k_lo.astype(BF16)))
        k3 = kc.reshape(C // SUBLANES, SUBLANES, B_DK)
        b3d = b.reshape(C // SUBLANES, SUBLANES, B_DK)
        band_sums = []
        for r in range(SUBLANES):
            k_r = kc if r == 0 else pltpu.roll(k3, r, 1).reshape(C, B_DK)
            b_r = b if r == 0 else pltpu.roll(b3d, r, 1).reshape(C, B_DK)
            band_sums.append(jnp.sum(qc * k_r * jnp.exp2(b - b_r), axis=1, keepdims=True))
        b_last = b[C - 1:C, :]
        k_dec = (kc * jnp.exp2(b_last - b)).astype(BF16)
        return rows, cols, o_inter, level_dots, band_sums, k_dec, jnp.exp2(b_last)

    def finish(prod, st):
        rows, cols, o_inter, level_dots, band_sums, k_dec, decay = prod
        vb = v_ref[rows, cols].astype(BF16)
        a_mat = jnp.zeros((C, C), F32)
        for (m, upper, pair), a_m in zip(levels, level_dots):
            a_mat = jnp.where(pair, a_m, a_mat)
        for r in range(SUBLANES):
            a_mat = jnp.where(band[r], band_sums[r], a_mat)
        o_ref[rows, cols] = o_inter + _dot(a_mat.astype(BF16), vb)
        return st * decay + _dot_tn(vb, k_dec)

    states = [st_ref[hd] for hd in range(hb)]
    pre = [prefix(0, hd) for hd in range(hb)]
    for c in range(nchunk):
        prods = [products(pre[hd], states[hd]) for hd in range(hb)]
        if c + 1 < nchunk:
            pre = [prefix(c + 1, hd) for hd in range(hb)]
        states = [finish(prods[hd], states[hd]) for hd in range(hb)]
    for hd in range(hb):
        st_ref[hd] = states[hd]

    @pl.when(t == pl.num_programs(2) - 1)
    def _():
        for hd in range(hb):
            sout_ref[hd] = states[hd].T


def _hgrn_rec(q, k, lf, v, s0, n_seq, seq_len, tm, C, hb):
    T = q.shape[0]
    nt = seq_len // tm
    blk = pl.BlockSpec((tm, hb * B_DK), lambda b, h, t: (b * nt + t, h))
    st_spec = pl.BlockSpec((None, hb, B_DK, B_DV), lambda b, h, t: (b, h, 0, 0))
    has_s0 = s0 is not None
    in_specs = [blk] * 4 + ([st_spec] if has_s0 else [])
    args = (q, k, lf, v) + ((s0,) if has_s0 else ())
    return pl.pallas_call(
        functools.partial(_hgrn_rec_body, C=C, nchunk=tm // C, hb=hb, has_s0=has_s0),
        out_shape=[jax.ShapeDtypeStruct((T, D_MODEL), F32),
                   jax.ShapeDtypeStruct((n_seq, B_HEADS, B_DK, B_DV), F32)],
        grid=(n_seq, B_HEADS // hb, nt),
        in_specs=in_specs,
        out_specs=[blk, st_spec],
        scratch_shapes=[pltpu.VMEM((hb, B_DV, B_DK), F32)],
        compiler_params=_cparams("arbitrary", "arbitrary", "arbitrary"),
        name="hgrn_recurrence",
    )(*args)


def _hgrn_out_body(x_ref, g_ref, o_ref, gs_ref, wout_ref, lg_ref, lb_ref, y_ref):
    o = o_ref[...]
    parts = [_rms(o[:, h * B_DV:(h + 1) * B_DV]) for h in range(B_HEADS)]
    y = (jnp.concatenate(parts, axis=1) * gs_ref[...]).astype(BF16)
    y_ref[...] = _residual_ln(x_ref[...], g_ref, _dot(y, wout_ref[...]), lg_ref, lb_ref)


def _hgrn_out(x, mod, tps, layer, j, o, gs, w_out, ln_g, ln_b, tm):
    T = x.shape[0]
    return pl.pallas_call(
        _hgrn_out_body,
        out_shape=jax.ShapeDtypeStruct((T, D_MODEL), F32),
        grid=(T // tm,),
        in_specs=[_row_spec(tm, D_MODEL), _mod_spec(mod, layer, 2, tps),
                  _row_spec(tm, D_MODEL), _row_spec(tm, D_MODEL),
                  _full_spec(w_out, j), _vec_spec(layer, D_MODEL), _vec_spec(layer, D_MODEL)],
        out_specs=_row_spec(tm, D_MODEL),
        compiler_params=_cparams("arbitrary"),
        name="hgrn_out",
    )(x, mod, o, gs, w_out, ln_g, ln_b)


def _rope_lanes(x, cc_ref, ss_ref, period_first_half):
    n = x.shape[1]
    half = C_ROPE // 2
    rot = jnp.where(period_first_half, pltpu.roll(x, n - half, 1), pltpu.roll(x, half, 1))
    return x * cc_ref[...] + rot * ss_ref[...]


def _mla_proj_body(x_ref, sh_ref, sc_ref, win_ref, gq_ref, gkv_ref, wn_ref, wr_ref, wuk_ref,
                   ccq_ref, ssq_ref, cck_ref, ssk_ref, q_ref, kcat_ref, klt_ref, lat_ref, kr_ref):
    h = _modulate(x_ref[...], sh_ref, sc_ref).astype(BF16)
    a = _dot(h, win_ref[...])
    cq = (_rms(a[:, :C_QLORA]) * gq_ref[...]).astype(BF16)
    ckv = _rms(a[:, C_QLORA:C_QLORA + C_KVLORA]) * gkv_ref[...]
    kr_slab = a[:, C_QLORA + C_KVLORA:]
    lane_k = lax.broadcasted_iota(jnp.int32, kr_slab.shape, 1)
    kr_slab = _rope_lanes(kr_slab, cck_ref, ssk_ref, (lane_k & (C_ROPE - 1)) < C_ROPE // 2)
    lat_ref[...] = ckv
    kr_ref[...] = kr_slab[:, :C_ROPE]
    kcat_ref[...] = jnp.concatenate([ckv, kr_slab], axis=1).astype(BF16)
    klt_ref[...] = ckv.T.astype(BF16)
    qn = _dot(cq, wn_ref[...]).astype(BF16)
    qr = _dot(cq, wr_ref[...])
    lane_q = lax.broadcasted_iota(jnp.int32, qr.shape, 1)
    qr = _rope_lanes(qr, ccq_ref, ssq_ref, (lane_q & (C_ROPE - 1)) < C_ROPE // 2).astype(BF16)
    zeros = jnp.zeros((qr.shape[0], LANES - C_ROPE), BF16)
    for hd in range(C_HEADS):
        ql = _dot(qn[:, hd * C_NOPE:(hd + 1) * C_NOPE], wuk_ref[hd]).astype(BF16)
        q_ref[hd] = jnp.concatenate([ql, qr[:, hd * C_ROPE:(hd + 1) * C_ROPE], zeros], axis=1)


def _mla_proj(x, mod, tps, tab_tiles, layer, j, w_in, g_q, g_kv, wn, wr, wuk, ccq, ssq, cck, ssk, tm):
    T = x.shape[0]
    nt = T // tm
    tab = lambda w: pl.BlockSpec((tm, w), lambda i: (i % tab_tiles, 0))
    return pl.pallas_call(
        _mla_proj_body,
        out_shape=[jax.ShapeDtypeStruct((nt, C_HEADS, tm, C_QK), BF16),
                   jax.ShapeDtypeStruct((T, C_QK), BF16),
                   jax.ShapeDtypeStruct((C_KVLORA, T), BF16),
                   jax.ShapeDtypeStruct((T, C_KVLORA), F32),
                   jax.ShapeDtypeStruct((T, C_ROPE), F32)],
        grid=(nt,),
        in_specs=[_row_spec(tm, D_MODEL), _mod_spec(mod, layer, 0, tps), _mod_spec(mod, layer, 1, tps),
                  _full_spec(w_in, j), _vec_spec(j, C_QLORA), _vec_spec(j, C_KVLORA),
                  _full_spec(wn, j), _full_spec(wr, j), _full_spec(wuk, j),
                  tab(C_HEADS * C_ROPE), tab(C_HEADS * C_ROPE), tab(LANES), tab(LANES)],
        out_specs=[pl.BlockSpec((None, C_HEADS, tm, C_QK), lambda i: (i, 0, 0, 0)),
                   _row_spec(tm, C_QK), pl.BlockSpec((C_KVLORA, tm), lambda i: (0, i)),
                   _row_spec(tm, C_KVLORA), _row_spec(tm, C_ROPE)],
        compiler_params=_cparams("arbitrary"),
        name="mla_proj",
    )(x, mod, mod, w_in, g_q, g_kv, wn, wr, wuk, ccq, ssq, cck, ssk)


def _softmax_update(s, m_ref, l_ref, acc_ref, values):
    m_prev = m_ref[...]
    m_new = jnp.maximum(m_prev, jnp.max(s, axis=-1, keepdims=True))
    alpha = jnp.exp(m_prev - m_new)
    p = jnp.exp(s - m_new)
    l_ref[...] = alpha * l_ref[...] + jnp.sum(p, axis=-1, keepdims=True)
    acc_ref[...] = alpha * acc_ref[...] + _dot(p.astype(BF16), values)
    m_ref[...] = m_new


def _softmax_init(m_ref, l_ref, acc_ref):
    m_ref[...] = jnp.full_like(m_ref, -jnp.inf)
    l_ref[...] = jnp.zeros_like(l_ref)
    acc_ref[...] = jnp.zeros_like(acc_ref)


def _attn_body(qi_ref, kj_ref, last_ref, q_ref, k_ref, kt_ref, o_ref, m_ref, l_ref, acc_ref, *, tq, tk, scale2):
    p_id = pl.program_id(1)
    qi = qi_ref[p_id]
    kj = kj_ref[p_id]

    @pl.when(kj == 0)
    def _():
        _softmax_init(m_ref, l_ref, acc_ref)

    def step(masked):
        k = k_ref[...]
        kt = kt_ref[...]
        if masked:
            key = lax.broadcasted_iota(jnp.int32, (tk, tq), 0) + kj * tk
            tok = lax.broadcasted_iota(jnp.int32, (tk, tq), 1) + qi * tq
            keep = key <= tok
        def scores(hd):
            t = _dot_nt(k, q_ref[hd]) * scale2
            return jnp.where(keep, t, -jnp.inf) if masked else t

        ahead = 2
        queue = [scores(hd) for hd in range(ahead)]
        for hd in range(C_HEADS):
            t = queue.pop(0)
            if hd + ahead < C_HEADS:
                queue.append(scores(hd + ahead))
            m_prev = m_ref[hd]
            m_new = jnp.maximum(m_prev, jnp.max(t, axis=0, keepdims=True))
            alpha = jnp.exp2(m_prev - m_new)
            p = jnp.exp2(t - m_new)
            l_ref[hd] = alpha * l_ref[hd] + jnp.sum(p, axis=0, keepdims=True)
            acc_ref[hd] = alpha * acc_ref[hd] + _dot(kt, p.astype(BF16))
            m_ref[hd] = m_new

    fully_visible = (kj + 1) * tk - 1 <= qi * tq
    pl.when(fully_visible)(lambda: step(False))
    pl.when(jnp.logical_not(fully_visible))(lambda: step(True))

    @pl.when(last_ref[p_id] == 1)
    def _():
        for hd in range(C_HEADS):
            o_ref[hd] = (acc_ref[hd] / l_ref[hd]).astype(BF16)


def _attn_prompt(q, kcat, klat_t, n_seq, seq_len, tq, tk):
    nq, nk = seq_len // tq, seq_len // tk
    pairs = [(i, j) for i in range(nq) for j in range((i * tq + tq - 1) // tk + 1)]
    qi = jnp.asarray([p[0] for p in pairs], jnp.int32)
    kj = jnp.asarray([p[1] for p in pairs], jnp.int32)
    last = jnp.asarray([int(n + 1 == len(pairs) or pairs[n + 1][0] != p[0]) for n, p in enumerate(pairs)], jnp.int32)
    scale2 = (C_NOPE + C_ROPE) ** -0.5 * math.log2(math.e)
    grid_spec = pltpu.PrefetchScalarGridSpec(
        num_scalar_prefetch=3,
        grid=(n_seq, len(pairs)),
        in_specs=[pl.BlockSpec((None, C_HEADS, tq, C_QK), lambda b, p, qi, kj, last: (b * nq + qi[p], 0, 0, 0)),
                  pl.BlockSpec((tk, C_QK), lambda b, p, qi, kj, last: (b * nk + kj[p], 0)),
                  pl.BlockSpec((C_KVLORA, tk), lambda b, p, qi, kj, last: (0, b * nk + kj[p]))],
        out_specs=pl.BlockSpec((None, C_HEADS, C_KVLORA, tq), lambda b, p, qi, kj, last: (b * nq + qi[p], 0, 0, 0)),
        scratch_shapes=[pltpu.VMEM((C_HEADS, 1, tq), F32), pltpu.VMEM((C_HEADS, 1, tq), F32),
                        pltpu.VMEM((C_HEADS, C_KVLORA, tq), F32)])
    return pl.pallas_call(
        functools.partial(_attn_body, tq=tq, tk=tk, scale2=scale2),
        out_shape=jax.ShapeDtypeStruct((n_seq * nq, C_HEADS, C_KVLORA, tq), BF16),
        grid_spec=grid_spec,
        compiler_params=_cparams("arbitrary", "arbitrary"),
        name="mla_attention_prompt",
    )(qi, kj, last, q, kcat, klat_t)


def _attn_paged_body(pt_ref, q_ref, nlat_ref, nrope_ref, lat_hbm, rope_hbm, o_ref,
                     lat_buf, rope_buf, sem, m_ref, l_ref, acc_ref, *, pages, sub, j, seq_new, scale2):
    b = pl.program_id(0)
    g = pl.program_id(1)
    n_groups = pl.num_programs(1)
    step = b * n_groups + g
    slot = step & 1

    def page_copy(kind, page_id, sl, i):
        src, dst = (lat_hbm, lat_buf) if kind == 0 else (rope_hbm, rope_buf)
        return pltpu.make_async_copy(src.at[j, page_id], dst.at[sl, i], sem.at[kind, sl])

    def start_group(bb, gg, sl):
        for i in range(pages):
            page_id = pt_ref[bb, gg * pages + i]
            page_copy(0, page_id, sl, i).start(priority=i % 2)
            page_copy(1, page_id, sl, i).start(priority=(i + 1) % 2)

    @pl.when(step == 0)
    def _():
        start_group(0, 0, 0)

    def wait_group(sl):
        for i in range(pages):
            page_copy(0, 0, sl, i).wait()
            page_copy(1, 0, sl, i).wait()

    wait_group(slot)

    q = q_ref[...]
    ql = q[:, :C_KVLORA]
    qr = q[:, C_KVLORA:C_KVLORA + C_ROPE]

    def update(carry, t, values):
        m_prev, l_prev, acc = carry
        m_new = jnp.maximum(m_prev, jnp.max(t, axis=-1, keepdims=True))
        alpha = jnp.exp2(m_prev - m_new)
        p = jnp.exp2(t - m_new)
        return (m_new, alpha * l_prev + jnp.sum(p, axis=-1, keepdims=True),
                alpha * acc + _dot(p.astype(BF16), values))

    first = g == 0
    carry = (jnp.where(first, -jnp.inf, m_ref[...]), jnp.where(first, 0.0, l_ref[...]),
             jnp.where(first, 0.0, acc_ref[...]))
    page = lat_buf.shape[2]
    lats, scores = [], []
    for u in range(pages // sub):
        lat = lat_buf[slot, u * sub:(u + 1) * sub].reshape(sub * page, C_KVLORA).astype(BF16)
        rp_t = jnp.concatenate([rope_buf[slot, u * sub + i] for i in range(sub)], axis=1).astype(BF16)
        lats.append(lat)
        scores.append((_dot_nt(ql, lat) + _dot(qr, rp_t)) * scale2)
    last = step == pl.num_programs(0) * n_groups - 1
    wrap = g == n_groups - 1
    start_group(jnp.where(last, b, jnp.where(wrap, b + 1, b)),
                jnp.where(last, g, jnp.where(wrap, 0, g + 1)), 1 - slot)
    for lat, t in zip(lats, scores):
        carry = update(carry, t, lat)

    @pl.when(last)
    def _():
        wait_group(1 - slot)

    @pl.when(g < n_groups - 1)
    def _():
        m_ref[...], l_ref[...], acc_ref[...] = carry

    @pl.when(g == n_groups - 1)
    def _():
        nlat = nlat_ref[...].astype(BF16)
        t2 = (_dot_nt(ql, nlat) + _dot_nt(qr, nrope_ref[...].astype(BF16))) * scale2
        tok = lax.broadcasted_iota(jnp.int32, t2.shape, 0) & (seq_new - 1)
        key = lax.broadcasted_iota(jnp.int32, t2.shape, 1)
        _, l_fin, acc = update(carry, jnp.where(key <= tok, t2, -jnp.inf), nlat)
        o_ref[...] = (acc / l_fin).astype(BF16)


def _attn_paged(q, new_lat, new_rope, pool_lat, pool_rope_t, page_table, j, seq_new):
    n_seq, n_pages = page_table.shape
    page = pool_lat.shape[2]
    pages, sub = 16, 4
    rows = q.shape[1]
    scale2 = (C_NOPE + C_ROPE) ** -0.5 * math.log2(math.e)
    grid_spec = pltpu.PrefetchScalarGridSpec(
        num_scalar_prefetch=1,
        grid=(n_seq, n_pages // pages),
        in_specs=[pl.BlockSpec((None, rows, C_QK), lambda b, s, pt: (b, 0, 0)),
                  pl.BlockSpec((None,) + new_lat.shape[1:], lambda b, s, pt: (b, 0, 0)),
                  pl.BlockSpec((None,) + new_rope.shape[1:], lambda b, s, pt: (b, 0, 0)),
                  pl.BlockSpec(memory_space=pl.ANY), pl.BlockSpec(memory_space=pl.ANY)],
        out_specs=pl.BlockSpec((None, rows, C_KVLORA), lambda b, s, pt: (b, 0, 0)),
        scratch_shapes=[pltpu.VMEM((2, pages, page, C_KVLORA), F32),
                        pltpu.VMEM((2, pages, C_ROPE, page), F32),
                        pltpu.SemaphoreType.DMA((2, 2)),
                        pltpu.VMEM((rows, 1), F32), pltpu.VMEM((rows, 1), F32),
                        pltpu.VMEM((rows, C_KVLORA), F32)])
    return pl.pallas_call(
        functools.partial(_attn_paged_body, pages=pages, sub=sub, j=j, seq_new=seq_new, scale2=scale2),
        out_shape=jax.ShapeDtypeStruct((n_seq, rows, C_KVLORA), BF16),
        grid_spec=grid_spec,
        compiler_params=_cparams("arbitrary", "arbitrary"),
        name="mla_attention_paged",
    )(page_table, q, new_lat, new_rope, pool_lat, pool_rope_t)


def _mla_out_body(x_ref, g_ref, o_ref, wuv_ref, wout_ref, lg_ref, lb_ref, y_ref):
    parts = [_dot_tn(o_ref[hd], wuv_ref[hd]) for hd in range(C_HEADS)]
    o = jnp.concatenate(parts, axis=1).astype(BF16)
    y_ref[...] = _residual_ln(x_ref[...], g_ref, _dot(o, wout_ref[...]), lg_ref, lb_ref)


def _mla_out(x, mod, tps, layer, j, o_lat, wuv, w_out, ln_g, ln_b, tm):
    T = x.shape[0]
    return pl.pallas_call(
        _mla_out_body,
        out_shape=jax.ShapeDtypeStruct((T, D_MODEL), F32),
        grid=(T // tm,),
        in_specs=[_row_spec(tm, D_MODEL), _mod_spec(mod, layer, 2, tps),
                  pl.BlockSpec((None, C_HEADS, C_KVLORA, tm), lambda i: (i, 0, 0, 0)),
                  _full_spec(wuv, j), _full_spec(w_out, j),
                  _vec_spec(layer, D_MODEL), _vec_spec(layer, D_MODEL)],
        out_specs=_row_spec(tm, D_MODEL),
        compiler_params=_cparams("arbitrary"),
        name="mla_out",
    )(x, mod, o_lat, wuv, w_out, ln_g, ln_b)


def _rope_tables(pos, reps, width):
    half = C_ROPE // 2
    inv = ROPE_THETA ** (-jnp.arange(half, dtype=F32) / half)
    ang = pos.astype(F32)[:, None] * inv
    cos, sin = jnp.cos(ang), jnp.sin(ang)
    cc = jnp.tile(jnp.concatenate([cos, cos], axis=1), (1, reps))
    ss = jnp.tile(jnp.concatenate([-sin, sin], axis=1), (1, reps))
    pad = width - cc.shape[1]
    return jnp.pad(cc, ((0, 0), (0, pad))), jnp.pad(ss, ((0, 0), (0, pad)))


def _prepare_params(p):
    vec = lambda a: a.reshape(a.shape[0], 1, a.shape[1])
    w_uq = p['c_w_uq']
    n_c = w_uq.shape[0]
    c_w_in = jnp.pad(p['c_w_in'], ((0, 0), (0, 0), (0, LANES - C_ROPE)))
    lb_all = jnp.cumsum(jax.nn.softmax(p['b_lb'].astype(F32), axis=0), axis=0)
    lb_all = lb_all - lb_all[:1]
    return dict(
        ln1_g=vec(p['ln1_g']), ln1_b=vec(p['ln1_b']), ln2_g=vec(p['ln2_g']), ln2_b=vec(p['ln2_b']),
        ffn_w1=p['ffn_w1'].astype(BF16), ffn_w2=p['ffn_w2'].astype(BF16),
        a_w_in=p['a_w_in'].astype(BF16), a_ln_g=vec(p['a_ln_g']), a_ln_b=vec(p['a_ln_b']),
        a_w_out=p['a_w_out'].astype(BF16),
        b_w_in=p['b_w_in'].astype(BF16), b_w_out=p['b_w_out'].astype(BF16), lb_all=lb_all,
        c_w_in=c_w_in.astype(BF16), c_g_q=vec(p['c_g_q']), c_g_kv=vec(p['c_g_kv']),
        c_wn=w_uq[..., :C_NOPE].reshape(n_c, C_QLORA, C_HEADS * C_NOPE).astype(BF16),
        c_wr=w_uq[..., C_NOPE:].reshape(n_c, C_QLORA, C_HEADS * C_ROPE).astype(BF16),
        c_wuk=jnp.transpose(p['c_w_uk'], (0, 2, 3, 1)).astype(BF16),
        c_wuv=jnp.transpose(p['c_w_uv'], (0, 2, 1, 3)).astype(BF16),
        c_w_out=p['c_w_out'].astype(BF16),
    )


def _sgu_mixing(w_s, b_s, chunk):
    reps = CHUNK_A // chunk
    causal = jnp.tril(jnp.ones((chunk, chunk), dtype=bool))
    ws = jnp.where(causal, w_s[:, :, :chunk, :chunk], 0)
    eye = jnp.eye(reps, dtype=w_s.dtype)
    ws = jnp.einsum('ab,jgts->jgatbs', eye, ws).reshape(w_s.shape[0], A_GROUPS, CHUNK_A, CHUNK_A)
    bias = jnp.tile(jnp.transpose(b_s[:, :, :chunk], (0, 2, 1)), (1, reps, 1))
    bias = jnp.repeat(bias, A_GDIM, axis=2)
    return ws.astype(BF16), bias


def _run_trunk(x, mod, n_seq, seq_len, q_pos, hgrn_state0, mla_cache, prm, raw, tm):
    T = x.shape[0]
    per_seq_mod = mod.shape[3] == 1
    tps = (seq_len // tm) if per_seq_mod else 1
    sgu_chunk = min(CHUNK_A, seq_len)
    ws, bias = _sgu_mixing(raw['a_w_s'], raw['a_b_s'], sgu_chunk)
    chunk_v, hgrn_states, lat_rows, rope_rows = [], [], [], []
    for i in range(DEPTH):
        kind, j = i % N_MIXERS, i // N_MIXERS
        if kind == 0:
            x, v_rows = _sgu_layer(x, mod, tps, i, j, prm['a_w_in'], prm['a_ln_g'], prm['a_ln_b'], ws, bias,
                                   prm['a_w_out'], prm['ln1_g'], prm['ln1_b'], tm, emit_v=mla_cache is not None)
            chunk_v.append(v_rows)
        elif kind == 1:
            lb = prm['lb_all'][i].reshape(1, D_MODEL)
            q, k, lf, v, gs = _hgrn_proj(x, mod, tps, i, j, prm['b_w_in'], lb, tm)
            if seq_len % 64 == 0:
                C, lpad, hb = 64, seq_len, 4
                rec_tm = min(seq_len, 512)
                rec_in = (q, k, lf, v)
            else:
                C = lpad = rec_tm = SUBLANES
                hb = B_HEADS
                padseq = lambda a: jnp.pad(a.reshape(n_seq, seq_len, D_MODEL),
                                           ((0, 0), (0, lpad - seq_len), (0, 0))).reshape(n_seq * lpad, D_MODEL)
                rec_in = tuple(padseq(a) for a in (q, k, lf, v))
            s0 = None if hgrn_state0 is None else hgrn_state0[j]
            o, S = _hgrn_rec(*rec_in, s0, n_seq, lpad, rec_tm, C, hb)
            if lpad != seq_len:
                o = o.reshape(n_seq, lpad, D_MODEL)[:, :seq_len].reshape(T, D_MODEL)
            hgrn_states.append(S)
            x = _hgrn_out(x, mod, tps, i, j, o, gs, prm['b_w_out'], prm['ln1_g'], prm['ln1_b'], tm)
        else:
            tq = min(tm, 512)
            tps_q = (seq_len // tq) if per_seq_mod else 1
            pos_rows = q_pos if tq <= seq_len else jnp.tile(q_pos, tq // seq_len)
            ccq, ssq = _rope_tables(pos_rows, C_HEADS, C_HEADS * C_ROPE)
            cck, ssk = _rope_tables(pos_rows, 1, LANES)
            qcat, kcat, klat_t, lat, kr = _mla_proj(x, mod, tps_q, pos_rows.shape[0] // tq, i, j, prm['c_w_in'],
                                            prm['c_g_q'], prm['c_g_kv'], prm['c_wn'], prm['c_wr'], prm['c_wuk'],
                                            ccq, ssq, cck, ssk, tq)
            if mla_cache is None:
                o_lat = _attn_prompt(qcat, kcat, klat_t, n_seq, seq_len, tq, min(seq_len, 512))
            else:
                pool_lat, pool_rope_t, pt = mla_cache
                qs = qcat.reshape(C_HEADS, n_seq, seq_len, C_QK).transpose(1, 0, 2, 3)
                qs = qs.reshape(n_seq, C_HEADS * seq_len, C_QK)
                padk = lambda a: jnp.pad(a.reshape(n_seq, seq_len, a.shape[1]), ((0, 0), (0, 16 - seq_len), (0, 0)))
                o_s = _attn_paged(qs, padk(lat), padk(kr), pool_lat, pool_rope_t, pt, j, seq_len)
                o_lat = o_s.reshape(n_seq, C_HEADS, seq_len, C_KVLORA).transpose(1, 3, 0, 2)
                o_lat = o_lat.reshape(1, C_HEADS, C_KVLORA, T)
            x = _mla_out(x, mod, tps_q, i, j, o_lat, prm['c_wuv'], prm['c_w_out'], prm['ln1_g'], prm['ln1_b'], tq)
            lat_rows.append(lat.reshape(n_seq, seq_len, C_KVLORA))
            rope_rows.append(kr.reshape(n_seq, seq_len, C_ROPE))
        x = _ffn_layer(x, mod, tps, i, prm['ffn_w1'], prm['ffn_w2'], prm['ln2_g'], prm['ln2_b'], tm)
    stack = lambda xs: jnp.stack(xs) if xs and xs[0] is not None else None
    return x, stack(chunk_v), jnp.stack(hgrn_states), jnp.stack(lat_rows), jnp.stack(rope_rows)


def kernel(x_prompt, x_sample, cache_kv_latent, cache_k_rope, state_hgrn, page_table, c_prompt, c_sample,
           w_ada, b_ada, ln1_g, ln1_b, ln2_g, ln2_b, ffn_w1, ffn_w2, a_w_in, a_ln_g, a_ln_b, a_w_s, a_b_s,
           a_w_out, b_w_in, b_lb, b_w_out, c_w_in, c_g_q, c_g_kv, c_w_uq, c_w_uk, c_w_uv, c_w_out):
    raw = dict(ln1_g=ln1_g, ln1_b=ln1_b, ln2_g=ln2_g, ln2_b=ln2_b, ffn_w1=ffn_w1, ffn_w2=ffn_w2,
               a_w_in=a_w_in, a_ln_g=a_ln_g, a_ln_b=a_ln_b, a_w_s=a_w_s, a_b_s=a_b_s, a_w_out=a_w_out,
               b_w_in=b_w_in, b_lb=b_lb, b_w_out=b_w_out, c_w_in=c_w_in, c_g_q=c_g_q, c_g_kv=c_g_kv,
               c_w_uq=c_w_uq, c_w_uk=c_w_uk, c_w_uv=c_w_uv, c_w_out=c_w_out)
    prm = _prepare_params(raw)
    nb, seq, d = x_prompt.shape
    ns, sseq, _ = x_sample.shape
    past_len = page_table.shape[1] * cache_kv_latent.shape[2]
    pos_prompt = jnp.arange(seq, dtype=jnp.int32)
    pos_sample = past_len + jnp.arange(sseq, dtype=jnp.int32)

    mod = _modulation(jnp.concatenate([c_prompt, c_sample], axis=0), w_ada, b_ada)
    mod = mod.reshape(DEPTH, nb + ns, 6, d).transpose(0, 2, 1, 3)
    mod_p = mod[:, :, :nb].reshape(DEPTH, 6, nb, 1, d)
    mod_s = jnp.repeat(mod[:, :, nb:], sseq, axis=2).reshape(DEPTH, 6, 1, ns * sseq, d)

    tm_p = 512
    y_p, _, hs_p, lat_p, rope_p = _run_trunk(x_prompt.reshape(nb * seq, d), mod_p, nb, seq, pos_prompt,
                                             None, None, prm, raw, tm_p)
    y_s, v_s, hs_s, lat_s, rope_s = _run_trunk(x_sample.reshape(ns * sseq, d), mod_s, ns, sseq, pos_sample,
                                               state_hgrn,
                                               (cache_kv_latent, jnp.swapaxes(cache_k_rope, 2, 3), page_table),
                                               prm, raw, ns * sseq)
    return (y_p.reshape(nb, seq, d), y_s.reshape(ns, sseq, d), hs_p, hs_s, lat_p, rope_p, lat_s, rope_s,
            v_s.reshape(v_s.shape[0], ns, sseq, d))
```

```python
import functools
import math

import jax
import jax.numpy as jnp
from jax import lax
from jax.experimental import pallas as pl
from jax.experimental.pallas import tpu as pltpu

F32 = jnp.float32
BF16 = jnp.bfloat16

D_MODEL = 1024
DEPTH = 4
N_MIXERS = 3
CHUNK_A = 128
A_GROUPS = 8
A_GDIM = D_MODEL // A_GROUPS
B_HEADS = 8
B_DK = 128
B_DV = D_MODEL // B_HEADS
C_HEADS = 8
C_NOPE = 128
C_ROPE = 64
C_V = 128
C_QLORA = 512
C_KVLORA = 256
ROPE_THETA = 10000.0
D_FF = 4 * D_MODEL
ALPHA = (2.0 * DEPTH) ** 0.25
EPS = 1e-6

LANES = 128
SUBLANES = 8
C_QK = C_KVLORA + LANES
VMEM_LIMIT = 56 * 1024 * 1024
PAGED_SLOTS = 3


def _cparams(*sem):
    return pltpu.CompilerParams(dimension_semantics=sem, vmem_limit_bytes=VMEM_LIMIT)


def _dot(a, b):
    return jnp.dot(a, b, preferred_element_type=F32)


def _dot_nt(a, b):
    return lax.dot_general(a, b, (((1,), (1,)), ((), ())), preferred_element_type=F32)


def _dot_tn(a, b):
    return lax.dot_general(a, b, (((0,), (0,)), ((), ())), preferred_element_type=F32)


def _layer_norm(y, g, b):
    mu = jnp.mean(y, axis=-1, keepdims=True)
    yc = y - mu
    var = jnp.mean(yc * yc, axis=-1, keepdims=True)
    return yc * lax.rsqrt(var + EPS) * g + b


def _rms(y):
    return y * lax.rsqrt(jnp.mean(y * y, axis=-1, keepdims=True) + EPS)


def _silu(x):
    return x * jax.nn.sigmoid(x)


def _gelu_tanh(x):
    return 0.5 * x * (1.0 + jnp.tanh(math.sqrt(2.0 / math.pi) * (x + 0.044715 * (x * x * x))))


def _modulate(x, sh_ref, sc_ref):
    return x * (1.0 + sc_ref[...]) + sh_ref[...]


def _residual_ln(x, gate_ref, out, lg_ref, lb_ref):
    return _layer_norm(ALPHA * x + gate_ref[...] * out, lg_ref[...], lb_ref[...])


def _mod_spec(mod, layer, which, tiles_per_seq):
    rows = mod.shape[3]
    return pl.BlockSpec((None, None, None, rows, D_MODEL),
                        lambda i: (layer, which, i // tiles_per_seq, 0, 0))


def _vec_spec(layer, width):
    return pl.BlockSpec((None, 1, width), lambda i: (layer, 0, 0))


def _full_spec(arr, layer=None):
    if layer is None:
        nd = arr.ndim
        return pl.BlockSpec(arr.shape, lambda i: (0,) * nd)
    nd = arr.ndim - 1
    return pl.BlockSpec((None,) + arr.shape[1:], lambda i: (layer,) + (0,) * nd)


def _row_spec(tm, width):
    return pl.BlockSpec((tm, width), lambda i: (i, 0))


def _mod_body(c_ref, w_ref, b_ref, o_ref):
    sc = _silu(c_ref[...]).astype(BF16)
    o_ref[...] = _dot(sc, w_ref[...].astype(BF16)) + b_ref[...]


def _modulation(c_all, w_ada, b_ada):
    n = c_all.shape[0]
    tn = 1536
    width = w_ada.shape[2]
    return pl.pallas_call(
        _mod_body,
        out_shape=jax.ShapeDtypeStruct((DEPTH, n, width), F32),
        grid=(DEPTH, width // tn),
        in_specs=[pl.BlockSpec((n, D_MODEL), lambda l, j: (0, 0)),
                  pl.BlockSpec((None, D_MODEL, tn), lambda l, j: (l, 0, j)),
                  pl.BlockSpec((None, 1, tn), lambda l, j: (l, 0, j))],
        out_specs=pl.BlockSpec((None, n, tn), lambda l, j: (l, 0, j)),
        compiler_params=_cparams("arbitrary", "arbitrary"),
        name="adaln_modulation",
    )(c_all, w_ada, b_ada.reshape(DEPTH, 1, width))


def _ffn_body(x_ref, sh_ref, sc_ref, g_ref, w1_ref, w2_ref, lg_ref, lb_ref, o_ref, acc_ref, *, fc):
    x = x_ref[...]
    h = _modulate(x, sh_ref, sc_ref).astype(BF16)
    for c in range(D_FF // fc):
        a = _dot(h, w1_ref[:, c * fc:(c + 1) * fc])
        a = jnp.square(jnp.maximum(a, 0.0)).astype(BF16)
        d = _dot(a, w2_ref[c * fc:(c + 1) * fc, :])
        if c == 0:
            acc_ref[...] = d
        else:
            acc_ref[...] += d
    o_ref[...] = _residual_ln(x, g_ref, acc_ref[...], lg_ref, lb_ref)


def _ffn_layer(x, mod, tps, layer, w1, w2, ln_g, ln_b, tm):
    T = x.shape[0]
    return pl.pallas_call(
        functools.partial(_ffn_body, fc=1024),
        out_shape=jax.ShapeDtypeStruct((T, D_MODEL), F32),
        grid=(T // tm,),
        in_specs=[_row_spec(tm, D_MODEL),
                  _mod_spec(mod, layer, 3, tps), _mod_spec(mod, layer, 4, tps), _mod_spec(mod, layer, 5, tps),
                  _full_spec(w1, layer), _full_spec(w2, layer),
                  _vec_spec(layer, D_MODEL), _vec_spec(layer, D_MODEL)],
        out_specs=_row_spec(tm, D_MODEL),
        scratch_shapes=[pltpu.VMEM((tm, D_MODEL), F32)],
        compiler_params=_cparams("arbitrary"),
        name="ffn_sublayer",
    )(x, mod, mod, mod, w1, w2, ln_g, ln_b)


def _sgu_body(x_ref, sh_ref, sc_ref, g_ref, win_ref, lng_ref, lnb_ref, ws_ref, bias_ref, wout_ref,
              lg_ref, lb_ref, o_ref, *rest, tm, emit_v):
    if emit_v:
        v_ref, gated_ref = rest
    else:
        (gated_ref,) = rest
    x = x_ref[...]
    h = _modulate(x, sh_ref, sc_ref).astype(BF16)
    u = _gelu_tanh(_dot(h, win_ref[:, :D_MODEL]))
    v = _gelu_tanh(_dot(h, win_ref[:, D_MODEL:]))
    v = _layer_norm(v, lng_ref[...], lnb_ref[...])
    if emit_v:
        v_ref[...] = v
    vb = v.astype(BF16)
    nch = tm // CHUNK_A
    for g in range(A_GROUPS):
        c = slice(g * A_GDIM, (g + 1) * A_GDIM)
        rhs = jnp.concatenate([vb[n * CHUNK_A:(n + 1) * CHUNK_A, c] for n in range(nch)], axis=1)
        mixed = _dot(ws_ref[g], rhs)
        for n in range(nch):
            r = slice(n * CHUNK_A, (n + 1) * CHUNK_A)
            gated_ref[r, c] = (u[r, c] * (mixed[:, n * A_GDIM:(n + 1) * A_GDIM] + bias_ref[:, c])).astype(BF16)
    out = _dot(gated_ref[...], wout_ref[...])
    o_ref[...] = _residual_ln(x, g_ref, out, lg_ref, lb_ref)


def _sgu_layer(x, mod, tps, layer, j, w_in, ln_g, ln_b, ws, bias, w_out, ln1_g, ln1_b, tm, emit_v):
    T = x.shape[0]
    out_shape = [jax.ShapeDtypeStruct((T, D_MODEL), F32)]
    out_specs = [_row_spec(tm, D_MODEL)]
    if emit_v:
        out_shape.append(jax.ShapeDtypeStruct((T, D_MODEL), F32))
        out_specs.append(_row_spec(tm, D_MODEL))
    res = pl.pallas_call(
        functools.partial(_sgu_body, tm=tm, emit_v=emit_v),
        out_shape=out_shape,
        grid=(T // tm,),
        in_specs=[_row_spec(tm, D_MODEL),
                  _mod_spec(mod, layer, 0, tps), _mod_spec(mod, layer, 1, tps), _mod_spec(mod, layer, 2, tps),
                  _full_spec(w_in, j), _vec_spec(j, D_MODEL), _vec_spec(j, D_MODEL),
                  _full_spec(ws, j), _full_spec(bias, j), _full_spec(w_out, j),
                  _vec_spec(layer, D_MODEL), _vec_spec(layer, D_MODEL)],
        out_specs=out_specs,
        scratch_shapes=[pltpu.VMEM((tm, D_MODEL), BF16)],
        compiler_params=_cparams("arbitrary"),
        name="sgu_sublayer",
    )(x, mod, mod, mod, w_in, ln_g, ln_b, ws, bias, w_out, ln1_g, ln1_b)
    return (res[0], res[1]) if emit_v else (res[0], None)


def _hgrn_proj_body(x_ref, sh_ref, sc_ref, win_ref, lb_ref, q_ref, k_ref, lf_ref, v_ref, gs_ref):
    h = _modulate(x_ref[...], sh_ref, sc_ref).astype(BF16)
    d = D_MODEL
    fz = _dot(h, win_ref[:, d:2 * d])
    zq = _dot(h, win_ref[:, 0:d])
    zv = _dot(h, win_ref[:, 2 * d:3 * d])
    zg = _dot(h, win_ref[:, 3 * d:4 * d])
    lb = lb_ref[...]
    e = jnp.exp(-jnp.abs(fz))
    a = jnp.log(lb)
    b = jnp.log1p(-lb) + (jnp.minimum(fz, 0.0) - jnp.log1p(e))
    lf_ref[...] = (jnp.maximum(a, b) + jnp.log1p(jnp.exp(-jnp.abs(a - b)))) * math.log2(math.e)
    k_ref[...] = (1.0 - lb) * (jnp.where(fz >= 0.0, e, 1.0) / (1.0 + e))
    q_ref[...] = _silu(zq)
    v_ref[...] = zv
    gs_ref[...] = _silu(zg)


def _hgrn_proj(x, mod, tps, layer, j, w_in, lb, tm):
    T = x.shape[0]
    shp = jax.ShapeDtypeStruct((T, D_MODEL), F32)
    return pl.pallas_call(
        _hgrn_proj_body,
        out_shape=[shp] * 5,
        grid=(T // tm,),
        in_specs=[_row_spec(tm, D_MODEL), _mod_spec(mod, layer, 0, tps), _mod_spec(mod, layer, 1, tps),
                  _full_spec(w_in, j), _full_spec(lb)],
        out_specs=[_row_spec(tm, D_MODEL)] * 5,
        compiler_params=_cparams("arbitrary"),
        name="hgrn_proj",
    )(x, mod, mod, w_in, lb)


def _hgrn_rec_body(*refs, C, nchunk, hb, has_s0):
    if has_s0:
        q_ref, k_ref, g_ref, v_ref, s0_ref, o_ref, sout_ref, st_ref = refs
    else:
        q_ref, k_ref, g_ref, v_ref, o_ref, sout_ref, st_ref = refs
    t = pl.program_id(2)

    @pl.when(t == 0)
    def _():
        for hd in range(hb):
            st_ref[hd] = s0_ref[hd].T if has_s0 else jnp.zeros((B_DV, B_DK), F32)

    row = lax.broadcasted_iota(jnp.int32, (C, B_DK), 0)
    row_a = lax.broadcasted_iota(jnp.int32, (C, C), 0)
    col_a = lax.broadcasted_iota(jnp.int32, (C, C), 1)
    tri = jnp.where(row_a >= col_a, 1.0, 0.0).astype(BF16)
    band = [(col_a == row_a - r) & ((row_a & (SUBLANES - 1)) >= r) for r in range(SUBLANES)]
    levels = []
    for m in (8, 16, 32, 64, 128):
        if 2 * m <= C:
            shift = int(math.log2(2 * m))
            pair = (((row_a >> shift) == (col_a >> shift)) & ((row_a & (2 * m - 1)) >= m)
                    & ((col_a & (2 * m - 1)) < m))
            levels.append((m, (row & (2 * m - 1)) >= m, pair))

    def prefix(c, hd):
        rows = slice(c * C, (c + 1) * C)
        cols = slice(hd * B_DK, (hd + 1) * B_DK)
        gc = g_ref[rows, cols]
        g_hi = gc.astype(BF16)
        r1 = gc - g_hi.astype(F32)
        g_mid = r1.astype(BF16)
        g_lo = (r1 - g_mid.astype(F32)).astype(BF16)
        b3 = _dot(tri, jnp.concatenate([g_hi, g_mid, g_lo], axis=1))
        return rows, cols, b3

    def products(pre, st):
        rows, cols, b3 = pre
        qc = q_ref[rows, cols]
        kc = k_ref[rows, cols]
        b = b3[:, 0:B_DK] + b3[:, B_DK:2 * B_DK] + b3[:, 2 * B_DK:3 * B_DK]
        o_inter = _dot_nt((qc * jnp.exp2(b)).astype(BF16), st.astype(BF16))
        level_dots = []
        for m, upper, pair in levels:
            bref = jnp.concatenate(
                [jnp.broadcast_to(b[i * 2 * m + m - 1:i * 2 * m + m, :], (2 * m, B_DK)) for i in range(C // (2 * m))],
                axis=0)
            q_up = jnp.where(upper, qc * jnp.exp2(b - bref), 0.0)
            k_lo = jnp.where(upper, 0.0, kc * jnp.exp2(bref - b))
            level_dots.append(_dot_nt(q_up.astype(BF16), k_lo.astype(BF16)))
        k3 = kc.reshape(C // SUBLANES, SUBLANES, B_DK)
        b3d = b.reshape(C // SUBLANES, SUBLANES, B_DK)
        band_sums = []
        for r in range(SUBLANES):
            k_r = kc if r == 0 else pltpu.roll(k3, r, 1).reshape(C, B_DK)
            b_r = b if r == 0 else pltpu.roll(b3d, r, 1).reshape(C, B_DK)
            band_sums.append(jnp.sum(qc * k_r * jnp.exp2(b - b_r), axis=1, keepdims=True))
        b_last = b[C - 1:C, :]
        k_dec = (kc * jnp.exp2(b_last - b)).astype(BF16)
        return rows, cols, o_inter, level_dots, band_sums, k_dec, jnp.exp2(b_last)

    def finish(prod, st):
        rows, cols, o_inter, level_dots, band_sums, k_dec, decay = prod
        vb = v_ref[rows, cols].astype(BF16)
        a_mat = jnp.zeros((C, C), F32)
        for (m, upper, pair), a_m in zip(levels, level_dots):
            a_mat = jnp.where(pair, a_m, a_mat)
        for r in range(SUBLANES):
            a_mat = jnp.where(band[r], band_sums[r], a_mat)
        o_ref[rows, cols] = o_inter + _dot(a_mat.astype(BF16), vb)
        return st * decay + _dot_tn(vb, k_dec)

    states = [st_ref[hd] for hd in range(hb)]
    pre = [prefix(0, hd) for hd in range(hb)]
    for c in range(nchunk):
        prods = [products(pre[hd], states[hd]) for hd in range(hb)]
        if c + 1 < nchunk:
            pre = [prefix(c + 1, hd) for hd in range(hb)]
        states = [finish(prods[hd], states[hd]) for hd in range(hb)]
    for hd in range(hb):
        st_ref[hd] = states[hd]

    @pl.when(t == pl.num_programs(2) - 1)
    def _():
        for hd in range(hb):
            sout_ref[hd] = states[hd].T


def _hgrn_rec(q, k, lf, v, s0, n_seq, seq_len, tm, C, hb):
    T = q.shape[0]
    nt = seq_len // tm
    blk = pl.BlockSpec((tm, hb * B_DK), lambda b, h, t: (b * nt + t, h))
    st_spec = pl.BlockSpec((None, hb, B_DK, B_DV), lambda b, h, t: (b, h, 0, 0))
    has_s0 = s0 is not None
    in_specs = [blk] * 4 + ([st_spec] if has_s0 else [])
    args = (q, k, lf, v) + ((s0,) if has_s0 else ())
    return pl.pallas_call(
        functools.partial(_hgrn_rec_body, C=C, nchunk=tm // C, hb=hb, has_s0=has_s0),
        out_shape=[jax.ShapeDtypeStruct((T, D_MODEL), F32),
                   jax.ShapeDtypeStruct((n_seq, B_HEADS, B_DK, B_DV), F32)],
        grid=(n_seq, B_HEADS // hb, nt),
        in_specs=in_specs,
        out_specs=[blk, st_spec],
        scratch_shapes=[pltpu.VMEM((hb, B_DV, B_DK), F32)],
        compiler_params=_cparams("arbitrary", "arbitrary", "arbitrary"),
        name="hgrn_recurrence",
    )(*args)


def _hgrn_out_body(x_ref, g_ref, o_ref, gs_ref, wout_ref, lg_ref, lb_ref, y_ref):
    o = o_ref[...]
    parts = [_rms(o[:, h * B_DV:(h + 1) * B_DV]) for h in range(B_HEADS)]
    y = (jnp.concatenate(parts, axis=1) * gs_ref[...]).astype(BF16)
    y_ref[...] = _residual_ln(x_ref[...], g_ref, _dot(y, wout_ref[...]), lg_ref, lb_ref)


def _hgrn_out(x, mod, tps, layer, j, o, gs, w_out, ln_g, ln_b, tm):
    T = x.shape[0]
    return pl.pallas_call(
        _hgrn_out_body,
        out_shape=jax.ShapeDtypeStruct((T, D_MODEL), F32),
        grid=(T // tm,),
        in_specs=[_row_spec(tm, D_MODEL), _mod_spec(mod, layer, 2, tps),
                  _row_spec(tm, D_MODEL), _row_spec(tm, D_MODEL),
                  _full_spec(w_out, j), _vec_spec(layer, D_MODEL), _vec_spec(layer, D_MODEL)],
        out_specs=_row_spec(tm, D_MODEL),
        compiler_params=_cparams("arbitrary"),
        name="hgrn_out",
    )(x, mod, o, gs, w_out, ln_g, ln_b)


def _rope_lanes(x, cc_ref, ss_ref, period_first_half):
    n = x.shape[1]
    half = C_ROPE // 2
    rot = jnp.where(period_first_half, pltpu.roll(x, n - half, 1), pltpu.roll(x, half, 1))
    return x * cc_ref[...] + rot * ss_ref[...]


def _mla_proj_body(x_ref, sh_ref, sc_ref, win_ref, gq_ref, gkv_ref, wn_ref, wr_ref, wuk_ref,
                   ccq_ref, ssq_ref, cck_ref, ssk_ref, q_ref, kcat_ref, klt_ref, lat_ref, kr_ref):
    h = _modulate(x_ref[...], sh_ref, sc_ref).astype(BF16)
    a = _dot(h, win_ref[...])
    cq = (_rms(a[:, :C_QLORA]) * gq_ref[...]).astype(BF16)
    ckv = _rms(a[:, C_QLORA:C_QLORA + C_KVLORA]) * gkv_ref[...]
    kr_slab = a[:, C_QLORA + C_KVLORA:]
    lane_k = lax.broadcasted_iota(jnp.int32, kr_slab.shape, 1)
    kr_slab = _rope_lanes(kr_slab, cck_ref, ssk_ref, (lane_k & (C_ROPE - 1)) < C_ROPE // 2)
    lat_ref[...] = ckv
    kr_ref[...] = kr_slab[:, :C_ROPE]
    kcat_ref[...] = jnp.concatenate([ckv, kr_slab], axis=1).astype(BF16)
    klt_ref[...] = ckv.T.astype(BF16)
    qn = _dot(cq, wn_ref[...]).astype(BF16)
    qr = _dot(cq, wr_ref[...])
    lane_q = lax.broadcasted_iota(jnp.int32, qr.shape, 1)
    qr = _rope_lanes(qr, ccq_ref, ssq_ref, (lane_q & (C_ROPE - 1)) < C_ROPE // 2).astype(BF16)
    zeros = jnp.zeros((qr.shape[0], LANES - C_ROPE), BF16)
    for hd in range(C_HEADS):
        ql = _dot(qn[:, hd * C_NOPE:(hd + 1) * C_NOPE], wuk_ref[hd]).astype(BF16)
        q_ref[hd] = jnp.concatenate([ql, qr[:, hd * C_ROPE:(hd + 1) * C_ROPE], zeros], axis=1)


def _mla_proj(x, mod, tps, tab_tiles, layer, j, w_in, g_q, g_kv, wn, wr, wuk, ccq, ssq, cck, ssk, tm):
    T = x.shape[0]
    nt = T // tm
    tab = lambda w: pl.BlockSpec((tm, w), lambda i: (i % tab_tiles, 0))
    return pl.pallas_call(
        _mla_proj_body,
        out_shape=[jax.ShapeDtypeStruct((nt, C_HEADS, tm, C_QK), BF16),
                   jax.ShapeDtypeStruct((T, C_QK), BF16),
                   jax.ShapeDtypeStruct((C_KVLORA, T), BF16),
                   jax.ShapeDtypeStruct((T, C_KVLORA), F32),
                   jax.ShapeDtypeStruct((T, C_ROPE), F32)],
        grid=(nt,),
        in_specs=[_row_spec(tm, D_MODEL), _mod_spec(mod, layer, 0, tps), _mod_spec(mod, layer, 1, tps),
                  _full_spec(w_in, j), _vec_spec(j, C_QLORA), _vec_spec(j, C_KVLORA),
                  _full_spec(wn, j), _full_spec(wr, j), _full_spec(wuk, j),
                  tab(C_HEADS * C_ROPE), tab(C_HEADS * C_ROPE), tab(LANES), tab(LANES)],
        out_specs=[pl.BlockSpec((None, C_HEADS, tm, C_QK), lambda i: (i, 0, 0, 0)),
                   _row_spec(tm, C_QK), pl.BlockSpec((C_KVLORA, tm), lambda i: (0, i)),
                   _row_spec(tm, C_KVLORA), _row_spec(tm, C_ROPE)],
        compiler_params=_cparams("arbitrary"),
        name="mla_proj",
    )(x, mod, mod, w_in, g_q, g_kv, wn, wr, wuk, ccq, ssq, cck, ssk)


def _softmax_update(s, m_ref, l_ref, acc_ref, values):
    m_prev = m_ref[...]
    m_new = jnp.maximum(m_prev, jnp.max(s, axis=-1, keepdims=True))
    alpha = jnp.exp(m_prev - m_new)
    p = jnp.exp(s - m_new)
    l_ref[...] = alpha * l_ref[...] + jnp.sum(p, axis=-1, keepdims=True)
    acc_ref[...] = alpha * acc_ref[...] + _dot(p.astype(BF16), values)
    m_ref[...] = m_new


def _softmax_init(m_ref, l_ref, acc_ref):
    m_ref[...] = jnp.full_like(m_ref, -jnp.inf)
    l_ref[...] = jnp.zeros_like(l_ref)
    acc_ref[...] = jnp.zeros_like(acc_ref)


def _attn_body(qi_ref, kj_ref, last_ref, q_ref, k_ref, kt_ref, o_ref, m_ref, l_ref, acc_ref, *, tq, tk, scale2):
    p_id = pl.program_id(1)
    qi = qi_ref[p_id]
    kj = kj_ref[p_id]

    @pl.when(kj == 0)
    def _():
        _softmax_init(m_ref, l_ref, acc_ref)

    def step(masked):
        k = k_ref[...]
        kt = kt_ref[...]
        if masked:
            key = lax.broadcasted_iota(jnp.int32, (tk, tq), 0) + kj * tk
            tok = lax.broadcasted_iota(jnp.int32, (tk, tq), 1) + qi * tq
            keep = key <= tok
        def scores(hd):
            t = _dot_nt(k, q_ref[hd]) * scale2
            return jnp.where(keep, t, -jnp.inf) if masked else t

        ahead = 2
        queue = [scores(hd) for hd in range(ahead)]
        for hd in range(C_HEADS):
            t = queue.pop(0)
            if hd + ahead < C_HEADS:
                queue.append(scores(hd + ahead))
            m_prev = m_ref[hd]
            m_new = jnp.maximum(m_prev, jnp.max(t, axis=0, keepdims=True))
            alpha = jnp.exp2(m_prev - m_new)
            p = jnp.exp2(t - m_new)
            l_ref[hd] = alpha * l_ref[hd] + jnp.sum(p, axis=0, keepdims=True)
            acc_ref[hd] = alpha * acc_ref[hd] + _dot(kt, p.astype(BF16))
            m_ref[hd] = m_new

    fully_visible = (kj + 1) * tk - 1 <= qi * tq
    pl.when(fully_visible)(lambda: step(False))
    pl.when(jnp.logical_not(fully_visible))(lambda: step(True))

    @pl.when(last_ref[p_id] == 1)
    def _():
        for hd in range(C_HEADS):
            o_ref[hd] = (acc_ref[hd] / l_ref[hd]).astype(BF16)


def _attn_prompt(q, kcat, klat_t, n_seq, seq_len, tq, tk):
    nq, nk = seq_len // tq, seq_len // tk
    pairs = [(i, j) for i in range(nq) for j in range((i * tq + tq - 1) // tk + 1)]
    qi = jnp.asarray([p[0] for p in pairs], jnp.int32)
    kj = jnp.asarray([p[1] for p in pairs], jnp.int32)
    last = jnp.asarray([int(n + 1 == len(pairs) or pairs[n + 1][0] != p[0]) for n, p in enumerate(pairs)], jnp.int32)
    scale2 = (C_NOPE + C_ROPE) ** -0.5 * math.log2(math.e)
    grid_spec = pltpu.PrefetchScalarGridSpec(
        num_scalar_prefetch=3,
        grid=(n_seq, len(pairs)),
        in_specs=[pl.BlockSpec((None, C_HEADS, tq, C_QK), lambda b, p, qi, kj, last: (b * nq + qi[p], 0, 0, 0)),
                  pl.BlockSpec((tk, C_QK), lambda b, p, qi, kj, last: (b * nk + kj[p], 0)),
                  pl.BlockSpec((C_KVLORA, tk), lambda b, p, qi, kj, last: (0, b * nk + kj[p]))],
        out_specs=pl.BlockSpec((None, C_HEADS, C_KVLORA, tq), lambda b, p, qi, kj, last: (b * nq + qi[p], 0, 0, 0)),
        scratch_shapes=[pltpu.VMEM((C_HEADS, 1, tq), F32), pltpu.VMEM((C_HEADS, 1, tq), F32),
                        pltpu.VMEM((C_HEADS, C_KVLORA, tq), F32)])
    return pl.pallas_call(
        functools.partial(_attn_body, tq=tq, tk=tk, scale2=scale2),
        out_shape=jax.ShapeDtypeStruct((n_seq * nq, C_HEADS, C_KVLORA, tq), BF16),
        grid_spec=grid_spec,
        compiler_params=_cparams("arbitrary", "arbitrary"),
        name="mla_attention_prompt",
    )(qi, kj, last, q, kcat, klat_t)


def _attn_paged_body(pt_ref, q_ref, nlat_ref, nrope_ref, lat_hbm, rope_hbm, o_ref,
                     lat_buf, rope_buf, sem, m_ref, l_ref, acc_ref, *, pages, sub, j, seq_new, scale2):
    b = pl.program_id(0)
    g = pl.program_id(1)
    n_groups = pl.num_programs(1)
    n_steps = pl.num_programs(0) * n_groups
    step = b * n_groups + g
    slot = lax.rem(step, PAGED_SLOTS)
    lookahead = PAGED_SLOTS - 1

    def page_copy(kind, page_id, sl, i):
        src, dst = (lat_hbm, lat_buf) if kind == 0 else (rope_hbm, rope_buf)
        return pltpu.make_async_copy(src.at[j, page_id], dst.at[sl, i], sem.at[kind, sl])

    def start_group(bb, gg, sl):
        for i in range(pages):
            page_id = pt_ref[bb, gg * pages + i]
            page_copy(0, page_id, sl, i).start(priority=i % 2)
            page_copy(1, page_id, sl, i).start(priority=(i + 1) % 2)

    def start_step(s, sl):
        s = jnp.minimum(s, n_steps - 1)
        start_group(s // n_groups, lax.rem(s, n_groups), sl)

    @pl.when(step == 0)
    def _():
        for s in range(lookahead):
            start_step(s, s)

    def wait_group(sl):
        for i in range(pages):
            page_copy(0, 0, sl, i).wait()
            page_copy(1, 0, sl, i).wait()

    wait_group(slot)

    q = q_ref[...]
    ql = q[:, :C_KVLORA]
    qr = q[:, C_KVLORA:C_KVLORA + C_ROPE]

    def update(carry, t, values):
        m_prev, l_prev, acc = carry
        m_new = jnp.maximum(m_prev, jnp.max(t, axis=-1, keepdims=True))
        alpha = jnp.exp2(m_prev - m_new)
        p = jnp.exp2(t - m_new)
        return (m_new, alpha * l_prev + jnp.sum(p, axis=-1, keepdims=True),
                alpha * acc + _dot(p.astype(BF16), values))

    first = g == 0
    carry = (jnp.where(first, -jnp.inf, m_ref[...]), jnp.where(first, 0.0, l_ref[...]),
             jnp.where(first, 0.0, acc_ref[...]))
    page = lat_buf.shape[2]
    lats, scores = [], []
    for u in range(pages // sub):
        lat = lat_buf[slot, u * sub:(u + 1) * sub].reshape(sub * page, C_KVLORA).astype(BF16)
        rp_t = jnp.concatenate([rope_buf[slot, u * sub + i] for i in range(sub)], axis=1).astype(BF16)
        lats.append(lat)
        scores.append((_dot_nt(ql, lat) + _dot(qr, rp_t)) * scale2)
    start_step(step + lookahead, lax.rem(step + lookahead, PAGED_SLOTS))
    for lat, t in zip(lats, scores):
        carry = update(carry, t, lat)

    @pl.when(step == n_steps - 1)
    def _():
        for ahead in range(1, PAGED_SLOTS):
            wait_group(lax.rem(step + ahead, PAGED_SLOTS))

    @pl.when(g < n_groups - 1)
    def _():
        m_ref[...], l_ref[...], acc_ref[...] = carry

    @pl.when(g == n_groups - 1)
    def _():
        nlat = nlat_ref[...].astype(BF16)
        t2 = (_dot_nt(ql, nlat) + _dot_nt(qr, nrope_ref[...].astype(BF16))) * scale2
        tok = lax.broadcasted_iota(jnp.int32, t2.shape, 0) & (seq_new - 1)
        key = lax.broadcasted_iota(jnp.int32, t2.shape, 1)
        _, l_fin, acc = update(carry, jnp.where(key <= tok, t2, -jnp.inf), nlat)
        o_ref[...] = (acc / l_fin).astype(BF16)


def _attn_paged(q, new_lat, new_rope, pool_lat, pool_rope_t, page_table, j, seq_new):
    n_seq, n_pages = page_table.shape
    page = pool_lat.shape[2]
    pages, sub = 16, 4
    rows = q.shape[1]
    scale2 = (C_NOPE + C_ROPE) ** -0.5 * math.log2(math.e)
    grid_spec = pltpu.PrefetchScalarGridSpec(
        num_scalar_prefetch=1,
        grid=(n_seq, n_pages // pages),
        in_specs=[pl.BlockSpec((None, rows, C_QK), lambda b, s, pt: (b, 0, 0)),
                  pl.BlockSpec((None,) + new_lat.shape[1:], lambda b, s, pt: (b, 0, 0)),
                  pl.BlockSpec((None,) + new_rope.shape[1:], lambda b, s, pt: (b, 0, 0)),
                  pl.BlockSpec(memory_space=pl.ANY), pl.BlockSpec(memory_space=pl.ANY)],
        out_specs=pl.BlockSpec((None, rows, C_KVLORA), lambda b, s, pt: (b, 0, 0)),
        scratch_shapes=[pltpu.VMEM((PAGED_SLOTS, pages, page, C_KVLORA), F32),
                        pltpu.VMEM((PAGED_SLOTS, pages, C_ROPE, page), F32),
                        pltpu.SemaphoreType.DMA((2, PAGED_SLOTS)),
                        pltpu.VMEM((rows, 1), F32), pltpu.VMEM((rows, 1), F32),
                        pltpu.VMEM((rows, C_KVLORA), F32)])
    return pl.pallas_call(
        functools.partial(_attn_paged_body, pages=pages, sub=sub, j=j, seq_new=seq_new, scale2=scale2),
        out_shape=jax.ShapeDtypeStruct((n_seq, rows, C_KVLORA), BF16),
        grid_spec=grid_spec,
        compiler_params=_cparams("arbitrary", "arbitrary"),
        name="mla_attention_paged",
    )(page_table, q, new_lat, new_rope, pool_lat, pool_rope_t)


def _mla_out_body(x_ref, g_ref, o_ref, wuv_ref, wout_ref, lg_ref, lb_ref, y_ref):
    parts = [_dot_tn(o_ref[hd], wuv_ref[hd]) for hd in range(C_HEADS)]
    o = jnp.concatenate(parts, axis=1).astype(BF16)
    y_ref[...] = _residual_ln(x_ref[...], g_ref, _dot(o, wout_ref[...]), lg_ref, lb_ref)


def _mla_out(x, mod, tps, layer, j, o_lat, wuv, w_out, ln_g, ln_b, tm):
    T = x.shape[0]
    return pl.pallas_call(
        _mla_out_body,
        out_shape=jax.ShapeDtypeStruct((T, D_MODEL), F32),
        grid=(T // tm,),
        in_specs=[_row_spec(tm, D_MODEL), _mod_spec(mod, layer, 2, tps),
                  pl.BlockSpec((None, C_HEADS, C_KVLORA, tm), lambda i: (i, 0, 0, 0)),
                  _full_spec(wuv, j), _full_spec(w_out, j),
                  _vec_spec(layer, D_MODEL), _vec_spec(layer, D_MODEL)],
        out_specs=_row_spec(tm, D_MODEL),
        compiler_params=_cparams("arbitrary"),
        name="mla_out",
    )(x, mod, o_lat, wuv, w_out, ln_g, ln_b)


def _rope_tables(pos, reps, width):
    half = C_ROPE // 2
    inv = ROPE_THETA ** (-jnp.arange(half, dtype=F32) / half)
    ang = pos.astype(F32)[:, None] * inv
    cos, sin = jnp.cos(ang), jnp.sin(ang)
    cc = jnp.tile(jnp.concatenate([cos, cos], axis=1), (1, reps))
    ss = jnp.tile(jnp.concatenate([-sin, sin], axis=1), (1, reps))
    pad = width - cc.shape[1]
    return jnp.pad(cc, ((0, 0), (0, pad))), jnp.pad(ss, ((0, 0), (0, pad)))


def _prepare_params(p):
    vec = lambda a: a.reshape(a.shape[0], 1, a.shape[1])
    w_uq = p['c_w_uq']
    n_c = w_uq.shape[0]
    c_w_in = jnp.pad(p['c_w_in'], ((0, 0), (0, 0), (0, LANES - C_ROPE)))
    lb_all = jnp.cumsum(jax.nn.softmax(p['b_lb'].astype(F32), axis=0), axis=0)
    lb_all = lb_all - lb_all[:1]
    return dict(
        ln1_g=vec(p['ln1_g']), ln1_b=vec(p['ln1_b']), ln2_g=vec(p['ln2_g']), ln2_b=vec(p['ln2_b']),
        ffn_w1=p['ffn_w1'].astype(BF16), ffn_w2=p['ffn_w2'].astype(BF16),
        a_w_in=p['a_w_in'].astype(BF16), a_ln_g=vec(p['a_ln_g']), a_ln_b=vec(p['a_ln_b']),
        a_w_out=p['a_w_out'].astype(BF16),
        b_w_in=p['b_w_in'].astype(BF16), b_w_out=p['b_w_out'].astype(BF16), lb_all=lb_all,
        c_w_in=c_w_in.astype(BF16), c_g_q=vec(p['c_g_q']), c_g_kv=vec(p['c_g_kv']),
        c_wn=w_uq[..., :C_NOPE].reshape(n_c, C_QLORA, C_HEADS * C_NOPE).astype(BF16),
        c_wr=w_uq[..., C_NOPE:].reshape(n_c, C_QLORA, C_HEADS * C_ROPE).astype(BF16),
        c_wuk=jnp.transpose(p['c_w_uk'], (0, 2, 3, 1)).astype(BF16),
        c_wuv=jnp.transpose(p['c_w_uv'], (0, 2, 1, 3)).astype(BF16),
        c_w_out=p['c_w_out'].astype(BF16),
    )


def _sgu_mixing(w_s, b_s, chunk):
    reps = CHUNK_A // chunk
    causal = jnp.tril(jnp.ones((chunk, chunk), dtype=bool))
    ws = jnp.where(causal, w_s[:, :, :chunk, :chunk], 0)
    eye = jnp.eye(reps, dtype=w_s.dtype)
    ws = jnp.einsum('ab,jgts->jgatbs', eye, ws).reshape(w_s.shape[0], A_GROUPS, CHUNK_A, CHUNK_A)
    bias = jnp.tile(jnp.transpose(b_s[:, :, :chunk], (0, 2, 1)), (1, reps, 1))
    bias = jnp.repeat(bias, A_GDIM, axis=2)
    return ws.astype(BF16), bias


def _run_trunk(x, mod, n_seq, seq_len, q_pos, hgrn_state0, mla_cache, prm, raw, tm):
    T = x.shape[0]
    per_seq_mod = mod.shape[3] == 1
    tps = (seq_len // tm) if per_seq_mod else 1
    sgu_chunk = min(CHUNK_A, seq_len)
    ws, bias = _sgu_mixing(raw['a_w_s'], raw['a_b_s'], sgu_chunk)
    chunk_v, hgrn_states, lat_rows, rope_rows = [], [], [], []
    for i in range(DEPTH):
        kind, j = i % N_MIXERS, i // N_MIXERS
        if kind == 0:
            x, v_rows = _sgu_layer(x, mod, tps, i, j, prm['a_w_in'], prm['a_ln_g'], prm['a_ln_b'], ws, bias,
                                   prm['a_w_out'], prm['ln1_g'], prm['ln1_b'], tm, emit_v=mla_cache is not None)
            chunk_v.append(v_rows)
        elif kind == 1:
            lb = prm['lb_all'][i].reshape(1, D_MODEL)
            q, k, lf, v, gs = _hgrn_proj(x, mod, tps, i, j, prm['b_w_in'], lb, tm)
            if seq_len % 64 == 0:
                C, lpad, hb = 64, seq_len, 4
                rec_tm = min(seq_len, 512)
                rec_in = (q, k, lf, v)
            else:
                C = lpad = rec_tm = SUBLANES
                hb = B_HEADS
                padseq = lambda a: jnp.pad(a.reshape(n_seq, seq_len, D_MODEL),
                                           ((0, 0), (0, lpad - seq_len), (0, 0))).reshape(n_seq * lpad, D_MODEL)
                rec_in = tuple(padseq(a) for a in (q, k, lf, v))
            s0 = None if hgrn_state0 is None else hgrn_state0[j]
            o, S = _hgrn_rec(*rec_in, s0, n_seq, lpad, rec_tm, C, hb)
            if lpad != seq_len:
                o = o.reshape(n_seq, lpad, D_MODEL)[:, :seq_len].reshape(T, D_MODEL)
            hgrn_states.append(S)
            x = _hgrn_out(x, mod, tps, i, j, o, gs, prm['b_w_out'], prm['ln1_g'], prm['ln1_b'], tm)
        else:
            tq = min(tm, 512)
            tps_q = (seq_len // tq) if per_seq_mod else 1
            pos_rows = q_pos if tq <= seq_len else jnp.tile(q_pos, tq // seq_len)
            ccq, ssq = _rope_tables(pos_rows, C_HEADS, C_HEADS * C_ROPE)
            cck, ssk = _rope_tables(pos_rows, 1, LANES)
            qcat, kcat, klat_t, lat, kr = _mla_proj(x, mod, tps_q, pos_rows.shape[0] // tq, i, j, prm['c_w_in'],
                                            prm['c_g_q'], prm['c_g_kv'], prm['c_wn'], prm['c_wr'], prm['c_wuk'],
                                            ccq, ssq, cck, ssk, tq)
            if mla_cache is None:
                o_lat = _attn_prompt(qcat, kcat, klat_t, n_seq, seq_len, tq, min(seq_len, 512))
            else:
                pool_lat, pool_rope_t, pt = mla_cache
                qs = qcat.reshape(C_HEADS, n_seq, seq_len, C_QK).transpose(1, 0, 2, 3)
                qs = qs.reshape(n_seq, C_HEADS * seq_len, C_QK)
                padk = lambda a: jnp.pad(a.reshape(n_seq, seq_len, a.shape[1]), ((0, 0), (0, 16 - seq_len), (0, 0)))
                o_s = _attn_paged(qs, padk(lat), padk(kr), pool_lat, pool_rope_t, pt, j, seq_len)
                o_lat = o_s.reshape(n_seq, C_HEADS, seq_len, C_KVLORA).transpose(1, 3, 0, 2)
                o_lat = o_lat.reshape(1, C_HEADS, C_KVLORA, T)
            x = _mla_out(x, mod, tps_q, i, j, o_lat, prm['c_wuv'], prm['c_w_out'], prm['ln1_g'], prm['ln1_b'], tq)
            lat_rows.append(lat.reshape(n_seq, seq_len, C_KVLORA))
            rope_rows.append(kr.reshape(n_seq, seq_len, C_ROPE))
        x = _ffn_layer(x, mod, tps, i, prm['ffn_w1'], prm['ffn_w2'], prm['ln2_g'], prm['ln2_b'], tm)
    stack = lambda xs: jnp.stack(xs) if xs and xs[0] is not None else None
    return x, stack(chunk_v), jnp.stack(hgrn_states), jnp.stack(lat_rows), jnp.stack(rope_rows)


def kernel(x_prompt, x_sample, cache_kv_latent, cache_k_rope, state_hgrn, page_table, c_prompt, c_sample,
           w_ada, b_ada, ln1_g, ln1_b, ln2_g, ln2_b, ffn_w1, ffn_w2, a_w_in, a_ln_g, a_ln_b, a_w_s, a_b_s,
           a_w_out, b_w_in, b_lb, b_w_out, c_w_in, c_g_q, c_g_kv, c_w_uq, c_w_uk, c_w_uv, c_w_out):
    raw = dict(ln1_g=ln1_g, ln1_b=ln1_b, ln2_g=ln2_g, ln2_b=ln2_b, ffn_w1=ffn_w1, ffn_w2=ffn_w2,
               a_w_in=a_w_in, a_ln_g=a_ln_g, a_ln_b=a_ln_b, a_w_s=a_w_s, a_b_s=a_b_s, a_w_out=a_w_out,
               b_w_in=b_w_in, b_lb=b_lb, b_w_out=b_w_out, c_w_in=c_w_in, c_g_q=c_g_q, c_g_kv=c_g_kv,
               c_w_uq=c_w_uq, c_w_uk=c_w_uk, c_w_uv=c_w_uv, c_w_out=c_w_out)
    prm = _prepare_params(raw)
    nb, seq, d = x_prompt.shape
    ns, sseq, _ = x_sample.shape
    past_len = page_table.shape[1] * cache_kv_latent.shape[2]
    pos_prompt = jnp.arange(seq, dtype=jnp.int32)
    pos_sample = past_len + jnp.arange(sseq, dtype=jnp.int32)

    mod = _modulation(jnp.concatenate([c_prompt, c_sample], axis=0), w_ada, b_ada)
    mod = mod.reshape(DEPTH, nb + ns, 6, d).transpose(0, 2, 1, 3)
    mod_p = mod[:, :, :nb].reshape(DEPTH, 6, nb, 1, d)
    mod_s = jnp.repeat(mod[:, :, nb:], sseq, axis=2).reshape(DEPTH, 6, 1, ns * sseq, d)

    tm_p = 512
    y_p, _, hs_p, lat_p, rope_p = _run_trunk(x_prompt.reshape(nb * seq, d), mod_p, nb, seq, pos_prompt,
                                             None, None, prm, raw, tm_p)
    y_s, v_s, hs_s, lat_s, rope_s = _run_trunk(x_sample.reshape(ns * sseq, d), mod_s, ns, sseq, pos_sample,
                                               state_hgrn,
                                               (cache_kv_latent, jnp.swapaxes(cache_k_rope, 2, 3), page_table),
                                               prm, raw, ns * sseq)
    return (y_p.reshape(nb, seq, d), y_s.reshape(ns, sseq, d), hs_p, hs_s, lat_p, rope_p, lat_s, rope_s,
            v_s.reshape(v_s.shape[0], ns, sseq, d))
```

```python
import functools
import math

import jax
import jax.numpy as jnp
from jax import lax
from jax.experimental import pallas as pl
from jax.experimental.pallas import tpu as pltpu

F32 = jnp.float32
BF16 = jnp.bfloat16

D_MODEL = 1024
DEPTH = 4
N_MIXERS = 3
CHUNK_A = 128
A_GROUPS = 8
A_GDIM = D_MODEL // A_GROUPS
B_HEADS = 8
B_DK = 128
B_DV = D_MODEL // B_HEADS
C_HEADS = 8
C_NOPE = 128
C_ROPE = 64
C_V = 128
C_QLORA = 512
C_KVLORA = 256
ROPE_THETA = 10000.0
D_FF = 4 * D_MODEL
ALPHA = (2.0 * DEPTH) ** 0.25
EPS = 1e-6

LANES = 128
SUBLANES = 8
C_QK = C_KVLORA + LANES
VMEM_LIMIT = 56 * 1024 * 1024
PAGED_SLOTS = 3


def _cparams(*sem):
    return pltpu.CompilerParams(dimension_semantics=sem, vmem_limit_bytes=VMEM_LIMIT)


def _dot(a, b):
    return jnp.dot(a, b, preferred_element_type=F32)


def _dot_nt(a, b):
    return lax.dot_general(a, b, (((1,), (1,)), ((), ())), preferred_element_type=F32)


def _dot_tn(a, b):
    return lax.dot_general(a, b, (((0,), (0,)), ((), ())), preferred_element_type=F32)


def _layer_norm(y, g, b):
    mu = jnp.mean(y, axis=-1, keepdims=True)
    yc = y - mu
    var = jnp.mean(yc * yc, axis=-1, keepdims=True)
    return yc * lax.rsqrt(var + EPS) * g + b


def _rms(y):
    return y * lax.rsqrt(jnp.mean(y * y, axis=-1, keepdims=True) + EPS)


def _silu(x):
    return x * jax.nn.sigmoid(x)


def _gelu_tanh(x):
    return 0.5 * x * (1.0 + jnp.tanh(math.sqrt(2.0 / math.pi) * (x + 0.044715 * (x * x * x))))


def _modulate(x, sh_ref, sc_ref):
    return x * (1.0 + sc_ref[...]) + sh_ref[...]


def _residual_ln(x, gate_ref, out, lg_ref, lb_ref):
    return _layer_norm(ALPHA * x + gate_ref[...] * out, lg_ref[...], lb_ref[...])


def _mod_spec(mod, layer, which, tiles_per_seq):
    if mod.ndim == 5:
        return pl.BlockSpec((None, None, None, 1, D_MODEL), lambda i: (layer, i // tiles_per_seq, which, 0, 0))
    return pl.BlockSpec((None, mod.shape[1], D_MODEL), lambda i: (layer, 0, which))


def _vec_spec(layer, width):
    return pl.BlockSpec((None, 1, width), lambda i: (layer, 0, 0))


def _full_spec(arr, layer=None):
    if layer is None:
        nd = arr.ndim
        return pl.BlockSpec(arr.shape, lambda i: (0,) * nd)
    nd = arr.ndim - 1
    return pl.BlockSpec((None,) + arr.shape[1:], lambda i: (layer,) + (0,) * nd)


def _row_spec(tm, width):
    return pl.BlockSpec((tm, width), lambda i: (i, 0))


def _mod_body(ca_ref, cb_ref, w_ref, b_ref, oa_ref, ob_ref):
    w = w_ref[...].astype(BF16)
    oa_ref[...] = _dot(_silu(ca_ref[...]).astype(BF16), w) + b_ref[...]
    ob_ref[...] = _dot(_silu(cb_ref[...]).astype(BF16), w) + b_ref[...]


def _modulation(c_a, c_b, w_ada, b_ada):
    tn = 1536
    width = w_ada.shape[2]
    rows = lambda c: pl.BlockSpec((c.shape[0], D_MODEL), lambda l, j: (0, 0))
    out = lambda c: pl.BlockSpec((None, c.shape[0], tn), lambda l, j: (l, 0, j))
    return pl.pallas_call(
        _mod_body,
        out_shape=[jax.ShapeDtypeStruct((DEPTH, c.shape[0], width), F32) for c in (c_a, c_b)],
        grid=(DEPTH, width // tn),
        in_specs=[rows(c_a), rows(c_b),
                  pl.BlockSpec((None, D_MODEL, tn), lambda l, j: (l, 0, j)),
                  pl.BlockSpec((None, 1, tn), lambda l, j: (l, 0, j))],
        out_specs=[out(c_a), out(c_b)],
        compiler_params=_cparams("arbitrary", "arbitrary"),
        name="adaln_modulation",
    )(c_a, c_b, w_ada, b_ada.reshape(DEPTH, 1, width))


def _ffn_body(x_ref, sh_ref, sc_ref, g_ref, w1_ref, w2_ref, lg_ref, lb_ref, o_ref, acc_ref, *, fc):
    x = x_ref[...]
    h = _modulate(x, sh_ref, sc_ref).astype(BF16)
    for c in range(D_FF // fc):
        a = _dot(h, w1_ref[:, c * fc:(c + 1) * fc])
        a = jnp.square(jnp.maximum(a, 0.0)).astype(BF16)
        d = _dot(a, w2_ref[c * fc:(c + 1) * fc, :])
        if c == 0:
            acc_ref[...] = d
        else:
            acc_ref[...] += d
    o_ref[...] = _residual_ln(x, g_ref, acc_ref[...], lg_ref, lb_ref)


def _ffn_layer(x, mod, tps, layer, w1, w2, ln_g, ln_b, tm):
    T = x.shape[0]
    return pl.pallas_call(
        functools.partial(_ffn_body, fc=1024),
        out_shape=jax.ShapeDtypeStruct((T, D_MODEL), F32),
        grid=(T // tm,),
        in_specs=[_row_spec(tm, D_MODEL),
                  _mod_spec(mod, layer, 3, tps), _mod_spec(mod, layer, 4, tps), _mod_spec(mod, layer, 5, tps),
                  _full_spec(w1, layer), _full_spec(w2, layer),
                  _vec_spec(layer, D_MODEL), _vec_spec(layer, D_MODEL)],
        out_specs=_row_spec(tm, D_MODEL),
        scratch_shapes=[pltpu.VMEM((tm, D_MODEL), F32)],
        compiler_params=_cparams("arbitrary"),
        name="ffn_sublayer",
    )(x, mod, mod, mod, w1, w2, ln_g, ln_b)


def _sgu_body(x_ref, sh_ref, sc_ref, g_ref, win_ref, lng_ref, lnb_ref, ws_ref, bias_ref, wout_ref,
              lg_ref, lb_ref, o_ref, *rest, tm, pieces, emit_v):
    if emit_v:
        v_ref, gated_ref = rest
    else:
        (gated_ref,) = rest
    rows = [slice(p * tm // pieces, (p + 1) * tm // pieces) for p in range(pieces)]
    per_row = lambda ref, r: ref[...] if ref.shape[0] == 1 else ref[r, :]

    def in_products(r):
        h = (x_ref[r, :] * (1.0 + per_row(sc_ref, r)) + per_row(sh_ref, r)).astype(BF16)
        return _dot(h, win_ref[:, :D_MODEL]), _dot(h, win_ref[:, D_MODEL:])

    def gate(r, zu, zv):
        u = _gelu_tanh(zu)
        v = _layer_norm(_gelu_tanh(zv), lng_ref[...], lnb_ref[...])
        if emit_v:
            v_ref[r, :] = v
        vb = v.astype(BF16)
        nch = (r.stop - r.start) // CHUNK_A
        for g in range(A_GROUPS):
            c = slice(g * A_GDIM, (g + 1) * A_GDIM)
            rhs = jnp.concatenate([vb[n * CHUNK_A:(n + 1) * CHUNK_A, c] for n in range(nch)], axis=1)
            mixed = _dot(ws_ref[g], rhs)
            for n in range(nch):
                rn = slice(n * CHUNK_A, (n + 1) * CHUNK_A)
                gated = u[rn, c] * (mixed[:, n * A_GDIM:(n + 1) * A_GDIM] + bias_ref[:, c])
                gated_ref[r.start + n * CHUNK_A:r.start + (n + 1) * CHUNK_A, c] = gated.astype(BF16)
        return _dot(gated_ref[r, :], wout_ref[...])

    def finish(r, out):
        y = ALPHA * x_ref[r, :] + per_row(g_ref, r) * out
        o_ref[r, :] = _layer_norm(y, lg_ref[...], lb_ref[...])

    z = in_products(rows[0])
    outs = []
    for p, r in enumerate(rows):
        z_next = in_products(rows[p + 1]) if p + 1 < pieces else None
        outs.append(gate(r, *z))
        z = z_next
        if p > 0:
            finish(rows[p - 1], outs[p - 1])
    finish(rows[-1], outs[-1])


def _sgu_layer(x, mod, tps, layer, j, w_in, ln_g, ln_b, ws, bias, w_out, ln1_g, ln1_b, tm, emit_v):
    T = x.shape[0]
    out_shape = [jax.ShapeDtypeStruct((T, D_MODEL), F32)]
    out_specs = [_row_spec(tm, D_MODEL)]
    if emit_v:
        out_shape.append(jax.ShapeDtypeStruct((T, D_MODEL), F32))
        out_specs.append(_row_spec(tm, D_MODEL))
    res = pl.pallas_call(
        functools.partial(_sgu_body, tm=tm, pieces=2 if tm >= 4 * CHUNK_A else 1, emit_v=emit_v),
        out_shape=out_shape,
        grid=(T // tm,),
        in_specs=[_row_spec(tm, D_MODEL),
                  _mod_spec(mod, layer, 0, tps), _mod_spec(mod, layer, 1, tps), _mod_spec(mod, layer, 2, tps),
                  _full_spec(w_in, j), _vec_spec(j, D_MODEL), _vec_spec(j, D_MODEL),
                  _full_spec(ws, j), _full_spec(bias, j), _full_spec(w_out, j),
                  _vec_spec(layer, D_MODEL), _vec_spec(layer, D_MODEL)],
        out_specs=out_specs,
        scratch_shapes=[pltpu.VMEM((tm, D_MODEL), BF16)],
        compiler_params=_cparams("arbitrary"),
        name="sgu_sublayer",
    )(x, mod, mod, mod, w_in, ln_g, ln_b, ws, bias, w_out, ln1_g, ln1_b)
    return (res[0], res[1]) if emit_v else (res[0], None)


def _hgrn_proj_body(x_ref, sh_ref, sc_ref, win_ref, lb_ref, q_ref, k_ref, lf_ref, v_ref, gs_ref):
    h = _modulate(x_ref[...], sh_ref, sc_ref).astype(BF16)
    d = D_MODEL
    fz = _dot(h, win_ref[:, d:2 * d])
    zq = _dot(h, win_ref[:, 0:d])
    zv = _dot(h, win_ref[:, 2 * d:3 * d])
    zg = _dot(h, win_ref[:, 3 * d:4 * d])
    lb = lb_ref[...]
    e = jnp.exp(-jnp.abs(fz))
    a = jnp.log(lb)
    b = jnp.log1p(-lb) + (jnp.minimum(fz, 0.0) - jnp.log1p(e))
    lf_ref[...] = (jnp.maximum(a, b) + jnp.log1p(jnp.exp(-jnp.abs(a - b)))) * math.log2(math.e)
    k_ref[...] = (1.0 - lb) * (jnp.where(fz >= 0.0, e, 1.0) / (1.0 + e))
    q_ref[...] = _silu(zq)
    v_ref[...] = zv
    gs_ref[...] = _silu(zg)


def _hgrn_proj(x, mod, tps, layer, j, w_in, lb, tm):
    T = x.shape[0]
    shp = jax.ShapeDtypeStruct((T, D_MODEL), F32)
    return pl.pallas_call(
        _hgrn_proj_body,
        out_shape=[shp] * 5,
        grid=(T // tm,),
        in_specs=[_row_spec(tm, D_MODEL), _mod_spec(mod, layer, 0, tps), _mod_spec(mod, layer, 1, tps),
                  _full_spec(w_in, j), _full_spec(lb)],
        out_specs=[_row_spec(tm, D_MODEL)] * 5,
        compiler_params=_cparams("arbitrary"),
        name="hgrn_proj",
    )(x, mod, mod, w_in, lb)


def _hgrn_rec_body(*refs, C, nchunk, hb, has_s0):
    if has_s0:
        q_ref, k_ref, g_ref, v_ref, s0_ref, o_ref, sout_ref, st_ref = refs
    else:
        q_ref, k_ref, g_ref, v_ref, o_ref, sout_ref, st_ref = refs
    t = pl.program_id(2)

    @pl.when(t == 0)
    def _():
        for hd in range(hb):
            st_ref[hd] = s0_ref[hd].T if has_s0 else jnp.zeros((B_DV, B_DK), F32)

    row = lax.broadcasted_iota(jnp.int32, (C, B_DK), 0)
    row_a = lax.broadcasted_iota(jnp.int32, (C, C), 0)
    col_a = lax.broadcasted_iota(jnp.int32, (C, C), 1)
    tri = jnp.where(row_a >= col_a, 1.0, 0.0).astype(BF16)
    band = [(col_a == row_a - r) & ((row_a & (SUBLANES - 1)) >= r) for r in range(SUBLANES)]
    levels = []
    for m in (8, 16, 32, 64, 128):
        if 2 * m <= C:
            shift = int(math.log2(2 * m))
            pair = (((row_a >> shift) == (col_a >> shift)) & ((row_a & (2 * m - 1)) >= m)
                    & ((col_a & (2 * m - 1)) < m))
            levels.append((m, (row & (2 * m - 1)) >= m, pair))

    def prefix(c, hd):
        rows = slice(c * C, (c + 1) * C)
        cols = slice(hd * B_DK, (hd + 1) * B_DK)
        gc = g_ref[rows, cols]
        g_hi = gc.astype(BF16)
        r1 = gc - g_hi.astype(F32)
        g_mid = r1.astype(BF16)
        g_lo = (r1 - g_mid.astype(F32)).astype(BF16)
        b3 = _dot(tri, jnp.concatenate([g_hi, g_mid, g_lo], axis=1))
        return rows, cols, b3

    def products(pre, st):
        rows, cols, b3 = pre
        qc = q_ref[rows, cols]
        kc = k_ref[rows, cols]
        b = b3[:, 0:B_DK] + b3[:, B_DK:2 * B_DK] + b3[:, 2 * B_DK:3 * B_DK]
        o_inter = _dot_nt((qc * jnp.exp2(b)).astype(BF16), st.astype(BF16))
        level_dots = []
        for m, upper, pair in levels:
            bref = jnp.concatenate(
                [jnp.broadcast_to(b[i * 2 * m + m - 1:i * 2 * m + m, :], (2 * m, B_DK)) for i in range(C // (2 * m))],
                axis=0)
            q_up = jnp.where(upper, qc * jnp.exp2(b - bref), 0.0)
            k_lo = jnp.where(upper, 0.0, kc * jnp.exp2(bref - b))
            level_dots.append(_dot_nt(q_up.astype(BF16), k_lo.astype(BF16)))
        k3 = kc.reshape(C // SUBLANES, SUBLANES, B_DK)
        b3d = b.reshape(C // SUBLANES, SUBLANES, B_DK)
        band_sums = []
        for r in range(SUBLANES):
            k_r = kc if r == 0 else pltpu.roll(k3, r, 1).reshape(C, B_DK)
            b_r = b if r == 0 else pltpu.roll(b3d, r, 1).reshape(C, B_DK)
            band_sums.append(jnp.sum(qc * k_r * jnp.exp2(b - b_r), axis=1, keepdims=True))
        b_last = b[C - 1:C, :]
        k_dec = (kc * jnp.exp2(b_last - b)).astype(BF16)
        return rows, cols, o_inter, level_dots, band_sums, k_dec, jnp.exp2(b_last)

    def finish(prod, st):
        rows, cols, o_inter, level_dots, band_sums, k_dec, decay = prod
        vb = v_ref[rows, cols].astype(BF16)
        a_mat = jnp.zeros((C, C), F32)
        for (m, upper, pair), a_m in zip(levels, level_dots):
            a_mat = jnp.where(pair, a_m, a_mat)
        for r in range(SUBLANES):
            a_mat = jnp.where(band[r], band_sums[r], a_mat)
        o_ref[rows, cols] = o_inter + _dot(a_mat.astype(BF16), vb)
        return st * decay + _dot_tn(vb, k_dec)

    states = [st_ref[hd] for hd in range(hb)]
    pre = [prefix(0, hd) for hd in range(hb)]
    for c in range(nchunk):
        prods = [products(pre[hd], states[hd]) for hd in range(hb)]
        if c + 1 < nchunk:
            pre = [prefix(c + 1, hd) for hd in range(hb)]
        states = [finish(prods[hd], states[hd]) for hd in range(hb)]
    for hd in range(hb):
        st_ref[hd] = states[hd]

    @pl.when(t == pl.num_programs(2) - 1)
    def _():
        for hd in range(hb):
            sout_ref[hd] = states[hd].T


def _hgrn_rec(q, k, lf, v, s0, n_seq, seq_len, tm, C, hb):
    T = q.shape[0]
    nt = seq_len // tm
    blk = pl.BlockSpec((tm, hb * B_DK), lambda b, h, t: (b * nt + t, h))
    st_spec = pl.BlockSpec((None, hb, B_DK, B_DV), lambda b, h, t: (b, h, 0, 0))
    has_s0 = s0 is not None
    in_specs = [blk] * 4 + ([st_spec] if has_s0 else [])
    args = (q, k, lf, v) + ((s0,) if has_s0 else ())
    return pl.pallas_call(
        functools.partial(_hgrn_rec_body, C=C, nchunk=tm // C, hb=hb, has_s0=has_s0),
        out_shape=[jax.ShapeDtypeStruct((T, D_MODEL), F32),
                   jax.ShapeDtypeStruct((n_seq, B_HEADS, B_DK, B_DV), F32)],
        grid=(n_seq, B_HEADS // hb, nt),
        in_specs=in_specs,
        out_specs=[blk, st_spec],
        scratch_shapes=[pltpu.VMEM((hb, B_DV, B_DK), F32)],
        compiler_params=_cparams("arbitrary", "arbitrary", "arbitrary"),
        name="hgrn_recurrence",
    )(*args)


def _hgrn_out_body(x_ref, g_ref, o_ref, gs_ref, wout_ref, lg_ref, lb_ref, y_ref):
    o = o_ref[...]
    parts = [_rms(o[:, h * B_DV:(h + 1) * B_DV]) for h in range(B_HEADS)]
    y = (jnp.concatenate(parts, axis=1) * gs_ref[...]).astype(BF16)
    y_ref[...] = _residual_ln(x_ref[...], g_ref, _dot(y, wout_ref[...]), lg_ref, lb_ref)


def _hgrn_out(x, mod, tps, layer, j, o, gs, w_out, ln_g, ln_b, tm):
    T = x.shape[0]
    return pl.pallas_call(
        _hgrn_out_body,
        out_shape=jax.ShapeDtypeStruct((T, D_MODEL), F32),
        grid=(T // tm,),
        in_specs=[_row_spec(tm, D_MODEL), _mod_spec(mod, layer, 2, tps),
                  _row_spec(tm, D_MODEL), _row_spec(tm, D_MODEL),
                  _full_spec(w_out, j), _vec_spec(layer, D_MODEL), _vec_spec(layer, D_MODEL)],
        out_specs=_row_spec(tm, D_MODEL),
        compiler_params=_cparams("arbitrary"),
        name="hgrn_out",
    )(x, mod, o, gs, w_out, ln_g, ln_b)


def _rope_lanes(x, cc_ref, ss_ref, period_first_half):
    n = x.shape[1]
    half = C_ROPE // 2
    rot = jnp.where(period_first_half, pltpu.roll(x, n - half, 1), pltpu.roll(x, half, 1))
    return x * cc_ref[...] + rot * ss_ref[...]


def _mla_proj_body(x_ref, sh_ref, sc_ref, win_ref, gq_ref, gkv_ref, wn_ref, wr_ref, wuk_ref,
                   ccq_ref, ssq_ref, cck_ref, ssk_ref, q_ref, kcat_ref, klt_ref, lat_ref, kr_ref):
    h = _modulate(x_ref[...], sh_ref, sc_ref).astype(BF16)
    a = _dot(h, win_ref[...])
    cq = (_rms(a[:, :C_QLORA]) * gq_ref[...]).astype(BF16)
    ckv = _rms(a[:, C_QLORA:C_QLORA + C_KVLORA]) * gkv_ref[...]
    kr_slab = a[:, C_QLORA + C_KVLORA:]
    lane_k = lax.broadcasted_iota(jnp.int32, kr_slab.shape, 1)
    kr_slab = _rope_lanes(kr_slab, cck_ref, ssk_ref, (lane_k & (C_ROPE - 1)) < C_ROPE // 2)
    lat_ref[...] = ckv
    kr_ref[...] = kr_slab[:, :C_ROPE]
    kcat_ref[...] = jnp.concatenate([ckv, kr_slab], axis=1).astype(BF16)
    klt_ref[...] = ckv.T.astype(BF16)
    qn = _dot(cq, wn_ref[...]).astype(BF16)
    qr = _dot(cq, wr_ref[...])
    lane_q = lax.broadcasted_iota(jnp.int32, qr.shape, 1)
    qr = _rope_lanes(qr, ccq_ref, ssq_ref, (lane_q & (C_ROPE - 1)) < C_ROPE // 2).astype(BF16)
    zeros = jnp.zeros((qr.shape[0], LANES - C_ROPE), BF16)
    for hd in range(C_HEADS):
        ql = _dot(qn[:, hd * C_NOPE:(hd + 1) * C_NOPE], wuk_ref[hd]).astype(BF16)
        q_ref[hd] = jnp.concatenate([ql, qr[:, hd * C_ROPE:(hd + 1) * C_ROPE], zeros], axis=1)


def _mla_proj(x, mod, tps, tab_tiles, layer, j, w_in, g_q, g_kv, wn, wr, wuk, ccq, ssq, cck, ssk, tm):
    T = x.shape[0]
    nt = T // tm
    tab = lambda w: pl.BlockSpec((tm, w), lambda i: (i % tab_tiles, 0))
    return pl.pallas_call(
        _mla_proj_body,
        out_shape=[jax.ShapeDtypeStruct((nt, C_HEADS, tm, C_QK), BF16),
                   jax.ShapeDtypeStruct((T, C_QK), BF16),
                   jax.ShapeDtypeStruct((C_KVLORA, T), BF16),
                   jax.ShapeDtypeStruct((T, C_KVLORA), F32),
                   jax.ShapeDtypeStruct((T, C_ROPE), F32)],
        grid=(nt,),
        in_specs=[_row_spec(tm, D_MODEL), _mod_spec(mod, layer, 0, tps), _mod_spec(mod, layer, 1, tps),
                  _full_spec(w_in, j), _vec_spec(j, C_QLORA), _vec_spec(j, C_KVLORA),
                  _full_spec(wn, j), _full_spec(wr, j), _full_spec(wuk, j),
                  tab(C_HEADS * C_ROPE), tab(C_HEADS * C_ROPE), tab(LANES), tab(LANES)],
        out_specs=[pl.BlockSpec((None, C_HEADS, tm, C_QK), lambda i: (i, 0, 0, 0)),
                   _row_spec(tm, C_QK), pl.BlockSpec((C_KVLORA, tm), lambda i: (0, i)),
                   _row_spec(tm, C_KVLORA), _row_spec(tm, C_ROPE)],
        compiler_params=_cparams("arbitrary"),
        name="mla_proj",
    )(x, mod, mod, w_in, g_q, g_kv, wn, wr, wuk, ccq, ssq, cck, ssk)


def _softmax_update(s, m_ref, l_ref, acc_ref, values):
    m_prev = m_ref[...]
    m_new = jnp.maximum(m_prev, jnp.max(s, axis=-1, keepdims=True))
    alpha = jnp.exp(m_prev - m_new)
    p = jnp.exp(s - m_new)
    l_ref[...] = alpha * l_ref[...] + jnp.sum(p, axis=-1, keepdims=True)
    acc_ref[...] = alpha * acc_ref[...] + _dot(p.astype(BF16), values)
    m_ref[...] = m_new


def _softmax_init(m_ref, l_ref, acc_ref):
    m_ref[...] = jnp.full_like(m_ref, -jnp.inf)
    l_ref[...] = jnp.zeros_like(l_ref)
    acc_ref[...] = jnp.zeros_like(acc_ref)


def _attn_body(qi_ref, kj_ref, last_ref, q_ref, k_ref, kt_ref, o_ref, m_ref, l_ref, acc_ref, *, tq, tk, scale2):
    p_id = pl.program_id(1)
    qi = qi_ref[p_id]
    kj = kj_ref[p_id]

    @pl.when(kj == 0)
    def _():
        _softmax_init(m_ref, l_ref, acc_ref)

    def step(masked):
        k = k_ref[...]
        kt = kt_ref[...]
        if masked:
            key = lax.broadcasted_iota(jnp.int32, (tk, tq), 0) + kj * tk
            tok = lax.broadcasted_iota(jnp.int32, (tk, tq), 1) + qi * tq
            keep = key <= tok
        def scores(hd):
            t = _dot_nt(k, q_ref[hd]) * scale2
            return jnp.where(keep, t, -jnp.inf) if masked else t

        ahead = 2
        queue = [scores(hd) for hd in range(ahead)]
        for hd in range(C_HEADS):
            t = queue.pop(0)
            if hd + ahead < C_HEADS:
                queue.append(scores(hd + ahead))
            m_prev = m_ref[hd]
            m_new = jnp.maximum(m_prev, jnp.max(t, axis=0, keepdims=True))
            alpha = jnp.exp2(m_prev - m_new)
            p = jnp.exp2(t - m_new)
            l_ref[hd] = alpha * l_ref[hd] + jnp.sum(p, axis=0, keepdims=True)
            acc_ref[hd] = alpha * acc_ref[hd] + _dot(kt, p.astype(BF16))
            m_ref[hd] = m_new

    fully_visible = (kj + 1) * tk - 1 <= qi * tq
    pl.when(fully_visible)(lambda: step(False))
    pl.when(jnp.logical_not(fully_visible))(lambda: step(True))

    @pl.when(last_ref[p_id] == 1)
    def _():
        for hd in range(C_HEADS):
            o_ref[hd] = (acc_ref[hd] / l_ref[hd]).astype(BF16)


def _attn_prompt(q, kcat, klat_t, n_seq, seq_len, tq, tk):
    nq, nk = seq_len // tq, seq_len // tk
    pairs = [(i, j) for i in range(nq) for j in range((i * tq + tq - 1) // tk + 1)]
    qi = jnp.asarray([p[0] for p in pairs], jnp.int32)
    kj = jnp.asarray([p[1] for p in pairs], jnp.int32)
    last = jnp.asarray([int(n + 1 == len(pairs) or pairs[n + 1][0] != p[0]) for n, p in enumerate(pairs)], jnp.int32)
    scale2 = (C_NOPE + C_ROPE) ** -0.5 * math.log2(math.e)
    grid_spec = pltpu.PrefetchScalarGridSpec(
        num_scalar_prefetch=3,
        grid=(n_seq, len(pairs)),
        in_specs=[pl.BlockSpec((None, C_HEADS, tq, C_QK), lambda b, p, qi, kj, last: (b * nq + qi[p], 0, 0, 0)),
                  pl.BlockSpec((tk, C_QK), lambda b, p, qi, kj, last: (b * nk + kj[p], 0)),
                  pl.BlockSpec((C_KVLORA, tk), lambda b, p, qi, kj, last: (0, b * nk + kj[p]))],
        out_specs=pl.BlockSpec((None, C_HEADS, C_KVLORA, tq), lambda b, p, qi, kj, last: (b * nq + qi[p], 0, 0, 0)),
        scratch_shapes=[pltpu.VMEM((C_HEADS, 1, tq), F32), pltpu.VMEM((C_HEADS, 1, tq), F32),
                        pltpu.VMEM((C_HEADS, C_KVLORA, tq), F32)])
    return pl.pallas_call(
        functools.partial(_attn_body, tq=tq, tk=tk, scale2=scale2),
        out_shape=jax.ShapeDtypeStruct((n_seq * nq, C_HEADS, C_KVLORA, tq), BF16),
        grid_spec=grid_spec,
        compiler_params=_cparams("arbitrary", "arbitrary"),
        name="mla_attention_prompt",
    )(qi, kj, last, q, kcat, klat_t)


def _attn_paged_body(pt_ref, q_ref, nlat_ref, nrope_ref, lat_hbm, rope_hbm, o_ref,
                     lat_buf, rope_buf, sem, m_ref, l_ref, acc_ref, *, nsq, pages, sub, j, seq_new, scale2):
    b = pl.program_id(0)
    g = pl.program_id(1)
    n_groups = pl.num_programs(1)
    n_steps = pl.num_programs(0) * n_groups
    step = b * n_groups + g
    slot = lax.rem(step, PAGED_SLOTS)
    lookahead = PAGED_SLOTS - 1

    def page_copy(kind, page_id, sl, i):
        src, dst = (lat_hbm, lat_buf) if kind == 0 else (rope_hbm, rope_buf)
        return pltpu.make_async_copy(src.at[j, page_id], dst.at[sl, i], sem.at[kind, sl])

    def start_group(bb, gg, sl):
        for i in range(nsq * pages):
            page_id = pt_ref[bb * nsq + i // pages, gg * pages + i % pages]
            page_copy(0, page_id, sl, i).start(priority=i % 2)
            page_copy(1, page_id, sl, i).start(priority=(i + 1) % 2)

    def start_step(s, sl):
        s = jnp.minimum(s, n_steps - 1)
        start_group(s // n_groups, lax.rem(s, n_groups), sl)

    @pl.when(step == 0)
    def _():
        for s in range(lookahead):
            start_step(s, s)

    def wait_group(sl):
        for i in range(nsq * pages):
            page_copy(0, 0, sl, i).wait()
            page_copy(1, 0, sl, i).wait()

    wait_group(slot)

    def update(carry, t, values):
        m_prev, l_prev, acc = carry
        m_new = jnp.maximum(m_prev, jnp.max(t, axis=-1, keepdims=True))
        alpha = jnp.exp2(m_prev - m_new)
        p = jnp.exp2(t - m_new)
        return (m_new, alpha * l_prev + jnp.sum(p, axis=-1, keepdims=True),
                alpha * acc + _dot(p.astype(BF16), values))

    first = g == 0
    page = lat_buf.shape[2]
    qls = [q_ref[s][:, :C_KVLORA] for s in range(nsq)]
    qrs = [q_ref[s][:, C_KVLORA:C_KVLORA + C_ROPE] for s in range(nsq)]
    work = []
    for s in range(nsq):
        for u in range(pages // sub):
            e0 = s * pages + u * sub
            lat = lat_buf[slot, e0:e0 + sub].reshape(sub * page, C_KVLORA).astype(BF16)
            rp_t = jnp.concatenate([rope_buf[slot, e0 + i] for i in range(sub)], axis=1).astype(BF16)
            work.append((s, lat, (_dot_nt(qls[s], lat) + _dot(qrs[s], rp_t)) * scale2))
    start_step(step + lookahead, lax.rem(step + lookahead, PAGED_SLOTS))
    carries = [(jnp.where(first, -jnp.inf, m_ref[s]), jnp.where(first, 0.0, l_ref[s]),
                jnp.where(first, 0.0, acc_ref[s])) for s in range(nsq)]
    for s, lat, t in work:
        carries[s] = update(carries[s], t, lat)

    @pl.when(step == n_steps - 1)
    def _():
        for ahead in range(1, PAGED_SLOTS):
            wait_group(lax.rem(step + ahead, PAGED_SLOTS))

    @pl.when(g < n_groups - 1)
    def _():
        for s in range(nsq):
            m_ref[s], l_ref[s], acc_ref[s] = carries[s]

    @pl.when(g == n_groups - 1)
    def _():
        for s in range(nsq):
            nlat = nlat_ref[s].astype(BF16)
            t2 = (_dot_nt(qls[s], nlat) + _dot_nt(qrs[s], nrope_ref[s].astype(BF16))) * scale2
            tok = lax.broadcasted_iota(jnp.int32, t2.shape, 0) & (seq_new - 1)
            key = lax.broadcasted_iota(jnp.int32, t2.shape, 1)
            _, l_fin, acc = update(carries[s], jnp.where(key <= tok, t2, -jnp.inf), nlat)
            o_ref[s] = (acc / l_fin).astype(BF16)


def _attn_paged(q, new_lat, new_rope, pool_lat, pool_rope_t, page_table, j, seq_new):
    n_seq, n_pages = page_table.shape
    page = pool_lat.shape[2]
    nsq, pages, sub = 2, 16, 4
    rows = q.shape[1]
    scale2 = (C_NOPE + C_ROPE) ** -0.5 * math.log2(math.e)
    grid_spec = pltpu.PrefetchScalarGridSpec(
        num_scalar_prefetch=1,
        grid=(n_seq // nsq, n_pages // pages),
        in_specs=[pl.BlockSpec((nsq, rows, C_QK), lambda b, s, pt: (b, 0, 0)),
                  pl.BlockSpec((nsq,) + new_lat.shape[1:], lambda b, s, pt: (b, 0, 0)),
                  pl.BlockSpec((nsq,) + new_rope.shape[1:], lambda b, s, pt: (b, 0, 0)),
                  pl.BlockSpec(memory_space=pl.ANY), pl.BlockSpec(memory_space=pl.ANY)],
        out_specs=pl.BlockSpec((nsq, rows, C_KVLORA), lambda b, s, pt: (b, 0, 0)),
        scratch_shapes=[pltpu.VMEM((PAGED_SLOTS, nsq * pages, page, C_KVLORA), F32),
                        pltpu.VMEM((PAGED_SLOTS, nsq * pages, C_ROPE, page), F32),
                        pltpu.SemaphoreType.DMA((2, PAGED_SLOTS)),
                        pltpu.VMEM((nsq, rows, 1), F32), pltpu.VMEM((nsq, rows, 1), F32),
                        pltpu.VMEM((nsq, rows, C_KVLORA), F32)])
    return pl.pallas_call(
        functools.partial(_attn_paged_body, nsq=nsq, pages=pages, sub=sub, j=j, seq_new=seq_new, scale2=scale2),
        out_shape=jax.ShapeDtypeStruct((n_seq, rows, C_KVLORA), BF16),
        grid_spec=grid_spec,
        compiler_params=_cparams("arbitrary", "arbitrary"),
        name="mla_attention_paged",
    )(page_table, q, new_lat, new_rope, pool_lat, pool_rope_t)


def _mla_out_body(x_ref, g_ref, o_ref, wuv_ref, wout_ref, lg_ref, lb_ref, y_ref):
    parts = [_dot_tn(o_ref[hd], wuv_ref[hd]) for hd in range(C_HEADS)]
    o = jnp.concatenate(parts, axis=1).astype(BF16)
    y_ref[...] = _residual_ln(x_ref[...], g_ref, _dot(o, wout_ref[...]), lg_ref, lb_ref)


def _mla_out(x, mod, tps, layer, j, o_lat, wuv, w_out, ln_g, ln_b, tm):
    T = x.shape[0]
    return pl.pallas_call(
        _mla_out_body,
        out_shape=jax.ShapeDtypeStruct((T, D_MODEL), F32),
        grid=(T // tm,),
        in_specs=[_row_spec(tm, D_MODEL), _mod_spec(mod, layer, 2, tps),
                  pl.BlockSpec((None, C_HEADS, C_KVLORA, tm), lambda i: (i, 0, 0, 0)),
                  _full_spec(wuv, j), _full_spec(w_out, j),
                  _vec_spec(layer, D_MODEL), _vec_spec(layer, D_MODEL)],
        out_specs=_row_spec(tm, D_MODEL),
        compiler_params=_cparams("arbitrary"),
        name="mla_out",
    )(x, mod, o_lat, wuv, w_out, ln_g, ln_b)


def _rope_tables(pos, reps, width):
    half = C_ROPE // 2
    inv = ROPE_THETA ** (-jnp.arange(half, dtype=F32) / half)
    ang = pos.astype(F32)[:, None] * inv
    cos, sin = jnp.cos(ang), jnp.sin(ang)
    cc = jnp.tile(jnp.concatenate([cos, cos], axis=1), (1, reps))
    ss = jnp.tile(jnp.concatenate([-sin, sin], axis=1), (1, reps))
    pad = width - cc.shape[1]
    return jnp.pad(cc, ((0, 0), (0, pad))), jnp.pad(ss, ((0, 0), (0, pad)))


def _prepare_params(p):
    vec = lambda a: a.reshape(a.shape[0], 1, a.shape[1])
    w_uq = p['c_w_uq']
    n_c = w_uq.shape[0]
    c_w_in = jnp.pad(p['c_w_in'], ((0, 0), (0, 0), (0, LANES - C_ROPE)))
    lb_all = jnp.cumsum(jax.nn.softmax(p['b_lb'].astype(F32), axis=0), axis=0)
    lb_all = lb_all - lb_all[:1]
    return dict(
        ln1_g=vec(p['ln1_g']), ln1_b=vec(p['ln1_b']), ln2_g=vec(p['ln2_g']), ln2_b=vec(p['ln2_b']),
        ffn_w1=p['ffn_w1'].astype(BF16), ffn_w2=p['ffn_w2'].astype(BF16),
        a_w_in=p['a_w_in'].astype(BF16), a_ln_g=vec(p['a_ln_g']), a_ln_b=vec(p['a_ln_b']),
        a_w_out=p['a_w_out'].astype(BF16),
        b_w_in=p['b_w_in'].astype(BF16), b_w_out=p['b_w_out'].astype(BF16), lb_all=lb_all,
        c_w_in=c_w_in.astype(BF16), c_g_q=vec(p['c_g_q']), c_g_kv=vec(p['c_g_kv']),
        c_wn=w_uq[..., :C_NOPE].reshape(n_c, C_QLORA, C_HEADS * C_NOPE).astype(BF16),
        c_wr=w_uq[..., C_NOPE:].reshape(n_c, C_QLORA, C_HEADS * C_ROPE).astype(BF16),
        c_wuk=jnp.transpose(p['c_w_uk'], (0, 2, 3, 1)).astype(BF16),
        c_wuv=jnp.transpose(p['c_w_uv'], (0, 2, 1, 3)).astype(BF16),
        c_w_out=p['c_w_out'].astype(BF16),
    )


def _sgu_mixing(w_s, b_s, chunk):
    reps = CHUNK_A // chunk
    causal = jnp.tril(jnp.ones((chunk, chunk), dtype=bool))
    ws = jnp.where(causal, w_s[:, :, :chunk, :chunk], 0)
    eye = jnp.eye(reps, dtype=w_s.dtype)
    ws = jnp.einsum('ab,jgts->jgatbs', eye, ws).reshape(w_s.shape[0], A_GROUPS, CHUNK_A, CHUNK_A)
    bias = jnp.tile(jnp.transpose(b_s[:, :, :chunk], (0, 2, 1)), (1, reps, 1))
    bias = jnp.repeat(bias, A_GDIM, axis=2)
    return ws.astype(BF16), bias


def _run_trunk(x, mod, n_seq, seq_len, q_pos, hgrn_state0, mla_cache, prm, raw, tm):
    T = x.shape[0]
    per_seq_mod = mod.ndim == 5
    tps = (seq_len // tm) if per_seq_mod else 1
    sgu_chunk = min(CHUNK_A, seq_len)
    ws, bias = _sgu_mixing(raw['a_w_s'], raw['a_b_s'], sgu_chunk)
    chunk_v, hgrn_states, lat_rows, rope_rows = [], [], [], []
    for i in range(DEPTH):
        kind, j = i % N_MIXERS, i // N_MIXERS
        if kind == 0:
            x, v_rows = _sgu_layer(x, mod, tps, i, j, prm['a_w_in'], prm['a_ln_g'], prm['a_ln_b'], ws, bias,
                                   prm['a_w_out'], prm['ln1_g'], prm['ln1_b'], tm, emit_v=mla_cache is not None)
            chunk_v.append(v_rows)
        elif kind == 1:
            lb = prm['lb_all'][i].reshape(1, D_MODEL)
            q, k, lf, v, gs = _hgrn_proj(x, mod, tps, i, j, prm['b_w_in'], lb, tm)
            if seq_len % 64 == 0:
                C, lpad, hb = 64, seq_len, 4
                rec_tm = min(seq_len, 512)
                rec_in = (q, k, lf, v)
            else:
                C = lpad = rec_tm = SUBLANES
                hb = B_HEADS
                padseq = lambda a: jnp.pad(a.reshape(n_seq, seq_len, D_MODEL),
                                           ((0, 0), (0, lpad - seq_len), (0, 0))).reshape(n_seq * lpad, D_MODEL)
                rec_in = tuple(padseq(a) for a in (q, k, lf, v))
            s0 = None if hgrn_state0 is None else hgrn_state0[j]
            o, S = _hgrn_rec(*rec_in, s0, n_seq, lpad, rec_tm, C, hb)
            if lpad != seq_len:
                o = o.reshape(n_seq, lpad, D_MODEL)[:, :seq_len].reshape(T, D_MODEL)
            hgrn_states.append(S)
            x = _hgrn_out(x, mod, tps, i, j, o, gs, prm['b_w_out'], prm['ln1_g'], prm['ln1_b'], tm)
        else:
            tq = min(tm, 512)
            tps_q = (seq_len // tq) if per_seq_mod else 1
            pos_rows = q_pos if tq <= seq_len else jnp.tile(q_pos, tq // seq_len)
            ccq, ssq = _rope_tables(pos_rows, C_HEADS, C_HEADS * C_ROPE)
            cck, ssk = _rope_tables(pos_rows, 1, LANES)
            qcat, kcat, klat_t, lat, kr = _mla_proj(x, mod, tps_q, pos_rows.shape[0] // tq, i, j, prm['c_w_in'],
                                            prm['c_g_q'], prm['c_g_kv'], prm['c_wn'], prm['c_wr'], prm['c_wuk'],
                                            ccq, ssq, cck, ssk, tq)
            if mla_cache is None:
                o_lat = _attn_prompt(qcat, kcat, klat_t, n_seq, seq_len, tq, min(seq_len, 512))
            else:
                pool_lat, pool_rope_t, pt = mla_cache
                qs = qcat.reshape(C_HEADS, n_seq, seq_len, C_QK).transpose(1, 0, 2, 3)
                qs = qs.reshape(n_seq, C_HEADS * seq_len, C_QK)
                padk = lambda a: jnp.pad(a.reshape(n_seq, seq_len, a.shape[1]), ((0, 0), (0, 16 - seq_len), (0, 0)))
                o_s = _attn_paged(qs, padk(lat), padk(kr), pool_lat, pool_rope_t, pt, j, seq_len)
                o_lat = o_s.reshape(n_seq, C_HEADS, seq_len, C_KVLORA).transpose(1, 3, 0, 2)
                o_lat = o_lat.reshape(1, C_HEADS, C_KVLORA, T)
            x = _mla_out(x, mod, tps_q, i, j, o_lat, prm['c_wuv'], prm['c_w_out'], prm['ln1_g'], prm['ln1_b'], tq)
            lat_rows.append(lat.reshape(n_seq, seq_len, C_KVLORA))
            rope_rows.append(kr.reshape(n_seq, seq_len, C_ROPE))
        x = _ffn_layer(x, mod, tps, i, prm['ffn_w1'], prm['ffn_w2'], prm['ln2_g'], prm['ln2_b'], tm)
    stack = lambda xs: jnp.stack(xs) if xs and xs[0] is not None else None
    return x, stack(chunk_v), jnp.stack(hgrn_states), jnp.stack(lat_rows), jnp.stack(rope_rows)


def kernel(x_prompt, x_sample, cache_kv_latent, cache_k_rope, state_hgrn, page_table, c_prompt, c_sample,
           w_ada, b_ada, ln1_g, ln1_b, ln2_g, ln2_b, ffn_w1, ffn_w2, a_w_in, a_ln_g, a_ln_b, a_w_s, a_b_s,
           a_w_out, b_w_in, b_lb, b_w_out, c_w_in, c_g_q, c_g_kv, c_w_uq, c_w_uk, c_w_uv, c_w_out):
    raw = dict(ln1_g=ln1_g, ln1_b=ln1_b, ln2_g=ln2_g, ln2_b=ln2_b, ffn_w1=ffn_w1, ffn_w2=ffn_w2,
               a_w_in=a_w_in, a_ln_g=a_ln_g, a_ln_b=a_ln_b, a_w_s=a_w_s, a_b_s=a_b_s, a_w_out=a_w_out,
               b_w_in=b_w_in, b_lb=b_lb, b_w_out=b_w_out, c_w_in=c_w_in, c_g_q=c_g_q, c_g_kv=c_g_kv,
               c_w_uq=c_w_uq, c_w_uk=c_w_uk, c_w_uv=c_w_uv, c_w_out=c_w_out)
    prm = _prepare_params(raw)
    nb, seq, d = x_prompt.shape
    ns, sseq, _ = x_sample.shape
    past_len = page_table.shape[1] * cache_kv_latent.shape[2]
    pos_prompt = jnp.arange(seq, dtype=jnp.int32)
    pos_sample = past_len + jnp.arange(sseq, dtype=jnp.int32)

    mod_p, mod_s = _modulation(c_prompt, jnp.repeat(c_sample, sseq, axis=0), w_ada, b_ada)
    mod_p = mod_p.reshape(DEPTH, nb, 6, 1, d)

    tm_p = 512
    y_p, _, hs_p, lat_p, rope_p = _run_trunk(x_prompt.reshape(nb * seq, d), mod_p, nb, seq, pos_prompt,
                                             None, None, prm, raw, tm_p)
    y_s, v_s, hs_s, lat_s, rope_s = _run_trunk(x_sample.reshape(ns * sseq, d), mod_s, ns, sseq, pos_sample,
                                               state_hgrn,
                                               (cache_kv_latent, jnp.swapaxes(cache_k_rope, 2, 3), page_table),
                                               prm, raw, ns * sseq)
    return (y_p.reshape(nb, seq, d), y_s.reshape(ns, sseq, d), hs_p, hs_s, lat_p, rope_p, lat_s, rope_s,
            v_s.reshape(v_s.shape[0], ns, sseq, d))
```

```python
import functools
import math

import jax
import jax.numpy as jnp
from jax import lax
from jax.experimental import pallas as pl
from jax.experimental.pallas import tpu as pltpu

F32 = jnp.float32
BF16 = jnp.bfloat16

D_MODEL = 1024
DEPTH = 4
N_MIXERS = 3
CHUNK_A = 128
A_GROUPS = 8
A_GDIM = D_MODEL // A_GROUPS
B_HEADS = 8
B_DK = 128
B_DV = D_MODEL // B_HEADS
C_HEADS = 8
C_NOPE = 128
C_ROPE = 64
C_V = 128
C_QLORA = 512
C_KVLORA = 256
ROPE_THETA = 10000.0
D_FF = 4 * D_MODEL
ALPHA = (2.0 * DEPTH) ** 0.25
EPS = 1e-6

LANES = 128
SUBLANES = 8
C_QK = C_KVLORA + LANES
VMEM_LIMIT = 56 * 1024 * 1024
PAGED_SLOTS = 3


def _cparams(*sem):
    return pltpu.CompilerParams(dimension_semantics=sem, vmem_limit_bytes=VMEM_LIMIT)


def _dot(a, b):
    return jnp.dot(a, b, preferred_element_type=F32)


def _dot_nt(a, b):
    return lax.dot_general(a, b, (((1,), (1,)), ((), ())), preferred_element_type=F32)


def _dot_tn(a, b):
    return lax.dot_general(a, b, (((0,), (0,)), ((), ())), preferred_element_type=F32)


def _layer_norm(y, g, b):
    mu = jnp.mean(y, axis=-1, keepdims=True)
    yc = y - mu
    var = jnp.mean(yc * yc, axis=-1, keepdims=True)
    return yc * lax.rsqrt(var + EPS) * g + b


def _rms(y):
    return y * lax.rsqrt(jnp.mean(y * y, axis=-1, keepdims=True) + EPS)


def _silu(x):
    return x * jax.nn.sigmoid(x)


def _gelu_tanh(x):
    return 0.5 * x * (1.0 + jnp.tanh(math.sqrt(2.0 / math.pi) * (x + 0.044715 * (x * x * x))))


def _modulate(x, sh_ref, sc_ref):
    return x * (1.0 + sc_ref[...]) + sh_ref[...]


def _residual_ln(x, gate_ref, out, lg_ref, lb_ref):
    return _layer_norm(ALPHA * x + gate_ref[...] * out, lg_ref[...], lb_ref[...])


def _mod_spec(mod, layer, which, tiles_per_seq):
    if mod.ndim == 5:
        return pl.BlockSpec((None, None, None, 1, D_MODEL), lambda i: (layer, i // tiles_per_seq, which, 0, 0))
    return pl.BlockSpec((None, mod.shape[1], D_MODEL), lambda i: (layer, 0, which))


def _vec_spec(layer, width):
    return pl.BlockSpec((None, 1, width), lambda i: (layer, 0, 0))


def _full_spec(arr, layer=None):
    if layer is None:
        nd = arr.ndim
        return pl.BlockSpec(arr.shape, lambda i: (0,) * nd)
    nd = arr.ndim - 1
    return pl.BlockSpec((None,) + arr.shape[1:], lambda i: (layer,) + (0,) * nd)


def _row_spec(tm, width):
    return pl.BlockSpec((tm, width), lambda i: (i, 0))


def _mod_body(ca_ref, cb_ref, w_ref, b_ref, oa_ref, ob_ref):
    w = w_ref[...].astype(BF16)
    oa_ref[...] = _dot(_silu(ca_ref[...]).astype(BF16), w) + b_ref[...]
    ob_ref[...] = _dot(_silu(cb_ref[...]).astype(BF16), w) + b_ref[...]


def _modulation(c_a, c_b, w_ada, b_ada):
    tn = 1536
    width = w_ada.shape[2]
    rows = lambda c: pl.BlockSpec((c.shape[0], D_MODEL), lambda l, j: (0, 0))
    out = lambda c: pl.BlockSpec((None, c.shape[0], tn), lambda l, j: (l, 0, j))
    return pl.pallas_call(
        _mod_body,
        out_shape=[jax.ShapeDtypeStruct((DEPTH, c.shape[0], width), F32) for c in (c_a, c_b)],
        grid=(DEPTH, width // tn),
        in_specs=[rows(c_a), rows(c_b),
                  pl.BlockSpec((None, D_MODEL, tn), lambda l, j: (l, 0, j)),
                  pl.BlockSpec((None, 1, tn), lambda l, j: (l, 0, j))],
        out_specs=[out(c_a), out(c_b)],
        compiler_params=_cparams("arbitrary", "arbitrary"),
        name="adaln_modulation",
    )(c_a, c_b, w_ada, b_ada.reshape(DEPTH, 1, width))


def _ffn_body(x_ref, sh_ref, sc_ref, g_ref, w1_ref, w2_ref, lg_ref, lb_ref, o_ref, acc_ref, *, fc):
    x = x_ref[...]
    h = _modulate(x, sh_ref, sc_ref).astype(BF16)
    for c in range(D_FF // fc):
        a = _dot(h, w1_ref[:, c * fc:(c + 1) * fc])
        a = jnp.square(jnp.maximum(a, 0.0)).astype(BF16)
        d = _dot(a, w2_ref[c * fc:(c + 1) * fc, :])
        if c == 0:
            acc_ref[...] = d
        else:
            acc_ref[...] += d
    o_ref[...] = _residual_ln(x, g_ref, acc_ref[...], lg_ref, lb_ref)


def _ffn_layer(x, mod, tps, layer, w1, w2, ln_g, ln_b, tm):
    T = x.shape[0]
    return pl.pallas_call(
        functools.partial(_ffn_body, fc=1024),
        out_shape=jax.ShapeDtypeStruct((T, D_MODEL), F32),
        grid=(T // tm,),
        in_specs=[_row_spec(tm, D_MODEL),
                  _mod_spec(mod, layer, 3, tps), _mod_spec(mod, layer, 4, tps), _mod_spec(mod, layer, 5, tps),
                  _full_spec(w1, layer), _full_spec(w2, layer),
                  _vec_spec(layer, D_MODEL), _vec_spec(layer, D_MODEL)],
        out_specs=_row_spec(tm, D_MODEL),
        scratch_shapes=[pltpu.VMEM((tm, D_MODEL), F32)],
        compiler_params=_cparams("arbitrary"),
        name="ffn_sublayer",
    )(x, mod, mod, mod, w1, w2, ln_g, ln_b)


def _sgu_body(x_ref, sh_ref, sc_ref, g_ref, win_ref, lng_ref, lnb_ref, ws_ref, bias_ref, wout_ref,
              lg_ref, lb_ref, o_ref, *rest, tm, pieces, emit_v):
    if emit_v:
        v_ref, gated_ref = rest
    else:
        (gated_ref,) = rest
    rows = [slice(p * tm // pieces, (p + 1) * tm // pieces) for p in range(pieces)]
    per_row = lambda ref, r: ref[...] if ref.shape[0] == 1 else ref[r, :]

    def in_products(r):
        h = (x_ref[r, :] * (1.0 + per_row(sc_ref, r)) + per_row(sh_ref, r)).astype(BF16)
        return _dot(h, win_ref[:, :D_MODEL]), _dot(h, win_ref[:, D_MODEL:])

    def gate(r, zu, zv):
        u = _gelu_tanh(zu)
        v = _layer_norm(_gelu_tanh(zv), lng_ref[...], lnb_ref[...])
        if emit_v:
            v_ref[r, :] = v
        vb = v.astype(BF16)
        nch = (r.stop - r.start) // CHUNK_A
        for g in range(A_GROUPS):
            c = slice(g * A_GDIM, (g + 1) * A_GDIM)
            rhs = jnp.concatenate([vb[n * CHUNK_A:(n + 1) * CHUNK_A, c] for n in range(nch)], axis=1)
            mixed = _dot(ws_ref[g], rhs)
            for n in range(nch):
                rn = slice(n * CHUNK_A, (n + 1) * CHUNK_A)
                gated = u[rn, c] * (mixed[:, n * A_GDIM:(n + 1) * A_GDIM] + bias_ref[:, c])
                gated_ref[r.start + n * CHUNK_A:r.start + (n + 1) * CHUNK_A, c] = gated.astype(BF16)
        return _dot(gated_ref[r, :], wout_ref[...])

    def finish(r, out):
        y = ALPHA * x_ref[r, :] + per_row(g_ref, r) * out
        o_ref[r, :] = _layer_norm(y, lg_ref[...], lb_ref[...])

    z = in_products(rows[0])
    outs = []
    for p, r in enumerate(rows):
        z_next = in_products(rows[p + 1]) if p + 1 < pieces else None
        outs.append(gate(r, *z))
        z = z_next
        if p > 0:
            finish(rows[p - 1], outs[p - 1])
    finish(rows[-1], outs[-1])


def _sgu_layer(x, mod, tps, layer, j, w_in, ln_g, ln_b, ws, bias, w_out, ln1_g, ln1_b, tm, emit_v):
    T = x.shape[0]
    out_shape = [jax.ShapeDtypeStruct((T, D_MODEL), F32)]
    out_specs = [_row_spec(tm, D_MODEL)]
    if emit_v:
        out_shape.append(jax.ShapeDtypeStruct((T, D_MODEL), F32))
        out_specs.append(_row_spec(tm, D_MODEL))
    res = pl.pallas_call(
        functools.partial(_sgu_body, tm=tm, pieces=2 if tm >= 4 * CHUNK_A else 1, emit_v=emit_v),
        out_shape=out_shape,
        grid=(T // tm,),
        in_specs=[_row_spec(tm, D_MODEL),
                  _mod_spec(mod, layer, 0, tps), _mod_spec(mod, layer, 1, tps), _mod_spec(mod, layer, 2, tps),
                  _full_spec(w_in, j), _vec_spec(j, D_MODEL), _vec_spec(j, D_MODEL),
                  _full_spec(ws, j), _full_spec(bias, j), _full_spec(w_out, j),
                  _vec_spec(layer, D_MODEL), _vec_spec(layer, D_MODEL)],
        out_specs=out_specs,
        scratch_shapes=[pltpu.VMEM((tm, D_MODEL), BF16)],
        compiler_params=_cparams("arbitrary"),
        name="sgu_sublayer",
    )(x, mod, mod, mod, w_in, ln_g, ln_b, ws, bias, w_out, ln1_g, ln1_b)
    return (res[0], res[1]) if emit_v else (res[0], None)


def _hgrn_proj_body(x_ref, sh_ref, sc_ref, win_ref, lb_ref, q_ref, k_ref, lf_ref, v_ref, gs_ref):
    d = D_MODEL
    tm = x_ref.shape[0]
    pieces = 4 if tm % (4 * LANES) == 0 and sh_ref.shape[0] == 1 else 1
    rows = [slice(p * tm // pieces, (p + 1) * tm // pieces) for p in range(pieces)]

    def products(r):
        h = _modulate(x_ref[r, :], sh_ref, sc_ref).astype(BF16)
        return [_dot(h, win_ref[:, part * d:(part + 1) * d]) for part in (1, 0, 2, 3)]

    def gate(r, fz, zq, zv, zg):
        lb = lb_ref[...]
        e = jnp.exp(-jnp.abs(fz))
        a = jnp.log(lb)
        b = jnp.log1p(-lb) + (jnp.minimum(fz, 0.0) - jnp.log1p(e))
        lf_ref[r, :] = (jnp.maximum(a, b) + jnp.log1p(jnp.exp(-jnp.abs(a - b)))) * math.log2(math.e)
        k_ref[r, :] = (1.0 - lb) * (jnp.where(fz >= 0.0, e, 1.0) / (1.0 + e))
        q_ref[r, :] = _silu(zq)
        v_ref[r, :] = zv
        gs_ref[r, :] = _silu(zg)

    z = [products(r) for r in rows]
    for r, zr in zip(rows, z):
        gate(r, *zr)


def _hgrn_proj(x, mod, tps, layer, j, w_in, lb, tm):
    T = x.shape[0]
    shp = jax.ShapeDtypeStruct((T, D_MODEL), F32)
    return pl.pallas_call(
        _hgrn_proj_body,
        out_shape=[shp] * 5,
        grid=(T // tm,),
        in_specs=[_row_spec(tm, D_MODEL), _mod_spec(mod, layer, 0, tps), _mod_spec(mod, layer, 1, tps),
                  _full_spec(w_in, j), _full_spec(lb)],
        out_specs=[_row_spec(tm, D_MODEL)] * 5,
        compiler_params=_cparams("arbitrary"),
        name="hgrn_proj",
    )(x, mod, mod, w_in, lb)


def _hgrn_rec_body(*refs, C, nchunk, hb, has_s0):
    if has_s0:
        q_ref, k_ref, g_ref, v_ref, s0_ref, o_ref, sout_ref, st_ref = refs
    else:
        q_ref, k_ref, g_ref, v_ref, o_ref, sout_ref, st_ref = refs
    t = pl.program_id(2)

    @pl.when(t == 0)
    def _():
        for hd in range(hb):
            st_ref[hd] = s0_ref[hd].T if has_s0 else jnp.zeros((B_DV, B_DK), F32)

    row = lax.broadcasted_iota(jnp.int32, (C, B_DK), 0)
    row_a = lax.broadcasted_iota(jnp.int32, (C, C), 0)
    col_a = lax.broadcasted_iota(jnp.int32, (C, C), 1)
    tri = jnp.where(row_a >= col_a, 1.0, 0.0).astype(BF16)
    band = [(col_a == row_a - r) & ((row_a & (SUBLANES - 1)) >= r) for r in range(SUBLANES)]
    levels = []
    for m in (8, 16, 32, 64, 128):
        if 2 * m <= C:
            shift = int(math.log2(2 * m))
            pair = (((row_a >> shift) == (col_a >> shift)) & ((row_a & (2 * m - 1)) >= m)
                    & ((col_a & (2 * m - 1)) < m))
            levels.append((m, (row & (2 * m - 1)) >= m, pair))

    def prefix(c, hd):
        rows = slice(c * C, (c + 1) * C)
        cols = slice(hd * B_DK, (hd + 1) * B_DK)
        gc = g_ref[rows, cols]
        g_hi = gc.astype(BF16)
        r1 = gc - g_hi.astype(F32)
        g_mid = r1.astype(BF16)
        g_lo = (r1 - g_mid.astype(F32)).astype(BF16)
        b3 = _dot(tri, jnp.concatenate([g_hi, g_mid, g_lo], axis=1))
        return rows, cols, b3

    def products(pre, st):
        rows, cols, b3 = pre
        qc = q_ref[rows, cols]
        kc = k_ref[rows, cols]
        b = b3[:, 0:B_DK] + b3[:, B_DK:2 * B_DK] + b3[:, 2 * B_DK:3 * B_DK]
        o_inter = _dot_nt((qc * jnp.exp2(b)).astype(BF16), st.astype(BF16))
        level_dots = []
        for m, upper, pair in levels:
            bref = jnp.concatenate(
                [jnp.broadcast_to(b[i * 2 * m + m - 1:i * 2 * m + m, :], (2 * m, B_DK)) for i in range(C // (2 * m))],
                axis=0)
            q_up = jnp.where(upper, qc * jnp.exp2(b - bref), 0.0)
            k_lo = jnp.where(upper, 0.0, kc * jnp.exp2(bref - b))
            level_dots.append(_dot_nt(q_up.astype(BF16), k_lo.astype(BF16)))
        k3 = kc.reshape(C // SUBLANES, SUBLANES, B_DK)
        b3d = b.reshape(C // SUBLANES, SUBLANES, B_DK)
        band_sums = []
        for r in range(SUBLANES):
            k_r = kc if r == 0 else pltpu.roll(k3, r, 1).reshape(C, B_DK)
            b_r = b if r == 0 else pltpu.roll(b3d, r, 1).reshape(C, B_DK)
            band_sums.append(jnp.sum(qc * k_r * jnp.exp2(b - b_r), axis=1, keepdims=True))
        b_last = b[C - 1:C, :]
        k_dec = (kc * jnp.exp2(b_last - b)).astype(BF16)
        return rows, cols, o_inter, level_dots, band_sums, k_dec, jnp.exp2(b_last)

    def finish(prod, st):
        rows, cols, o_inter, level_dots, band_sums, k_dec, decay = prod
        vb = v_ref[rows, cols].astype(BF16)
        a_mat = jnp.zeros((C, C), F32)
        for (m, upper, pair), a_m in zip(levels, level_dots):
            a_mat = jnp.where(pair, a_m, a_mat)
        for r in range(SUBLANES):
            a_mat = jnp.where(band[r], band_sums[r], a_mat)
        o_ref[rows, cols] = o_inter + _dot(a_mat.astype(BF16), vb)
        return st * decay + _dot_tn(vb, k_dec)

    states = [st_ref[hd] for hd in range(hb)]
    pre = [prefix(0, hd) for hd in range(hb)]
    for c in range(nchunk):
        prods = [products(pre[hd], states[hd]) for hd in range(hb)]
        if c + 1 < nchunk:
            pre = [prefix(c + 1, hd) for hd in range(hb)]
        states = [finish(prods[hd], states[hd]) for hd in range(hb)]
    for hd in range(hb):
        st_ref[hd] = states[hd]

    @pl.when(t == pl.num_programs(2) - 1)
    def _():
        for hd in range(hb):
            sout_ref[hd] = states[hd].T


def _hgrn_rec(q, k, lf, v, s0, n_seq, seq_len, tm, C, hb):
    T = q.shape[0]
    nt = seq_len // tm
    blk = pl.BlockSpec((tm, hb * B_DK), lambda b, h, t: (b * nt + t, h))
    st_spec = pl.BlockSpec((None, hb, B_DK, B_DV), lambda b, h, t: (b, h, 0, 0))
    has_s0 = s0 is not None
    in_specs = [blk] * 4 + ([st_spec] if has_s0 else [])
    args = (q, k, lf, v) + ((s0,) if has_s0 else ())
    return pl.pallas_call(
        functools.partial(_hgrn_rec_body, C=C, nchunk=tm // C, hb=hb, has_s0=has_s0),
        out_shape=[jax.ShapeDtypeStruct((T, D_MODEL), F32),
                   jax.ShapeDtypeStruct((n_seq, B_HEADS, B_DK, B_DV), F32)],
        grid=(n_seq, B_HEADS // hb, nt),
        in_specs=in_specs,
        out_specs=[blk, st_spec],
        scratch_shapes=[pltpu.VMEM((hb, B_DV, B_DK), F32)],
        compiler_params=_cparams("arbitrary", "arbitrary", "arbitrary"),
        name="hgrn_recurrence",
    )(*args)


def _hgrn_out_body(x_ref, g_ref, o_ref, gs_ref, wout_ref, lg_ref, lb_ref, y_ref):
    o = o_ref[...]
    parts = [_rms(o[:, h * B_DV:(h + 1) * B_DV]) for h in range(B_HEADS)]
    y = (jnp.concatenate(parts, axis=1) * gs_ref[...]).astype(BF16)
    y_ref[...] = _residual_ln(x_ref[...], g_ref, _dot(y, wout_ref[...]), lg_ref, lb_ref)


def _hgrn_out(x, mod, tps, layer, j, o, gs, w_out, ln_g, ln_b, tm):
    T = x.shape[0]
    return pl.pallas_call(
        _hgrn_out_body,
        out_shape=jax.ShapeDtypeStruct((T, D_MODEL), F32),
        grid=(T // tm,),
        in_specs=[_row_spec(tm, D_MODEL), _mod_spec(mod, layer, 2, tps),
                  _row_spec(tm, D_MODEL), _row_spec(tm, D_MODEL),
                  _full_spec(w_out, j), _vec_spec(layer, D_MODEL), _vec_spec(layer, D_MODEL)],
        out_specs=_row_spec(tm, D_MODEL),
        compiler_params=_cparams("arbitrary"),
        name="hgrn_out",
    )(x, mod, o, gs, w_out, ln_g, ln_b)


def _rope_lanes(x, cc_ref, ss_ref, period_first_half):
    n = x.shape[1]
    half = C_ROPE // 2
    rot = jnp.where(period_first_half, pltpu.roll(x, n - half, 1), pltpu.roll(x, half, 1))
    return x * cc_ref[...] + rot * ss_ref[...]


def _mla_proj_body(x_ref, sh_ref, sc_ref, win_ref, gq_ref, gkv_ref, wn_ref, wr_ref, wuk_ref,
                   ccq_ref, ssq_ref, cck_ref, ssk_ref, q_ref, kcat_ref, klt_ref, lat_ref, kr_ref):
    h = _modulate(x_ref[...], sh_ref, sc_ref).astype(BF16)
    a = _dot(h, win_ref[...])
    cq = (_rms(a[:, :C_QLORA]) * gq_ref[...]).astype(BF16)
    ckv = _rms(a[:, C_QLORA:C_QLORA + C_KVLORA]) * gkv_ref[...]
    kr_slab = a[:, C_QLORA + C_KVLORA:]
    lane_k = lax.broadcasted_iota(jnp.int32, kr_slab.shape, 1)
    kr_slab = _rope_lanes(kr_slab, cck_ref, ssk_ref, (lane_k & (C_ROPE - 1)) < C_ROPE // 2)
    lat_ref[...] = ckv
    kr_ref[...] = kr_slab[:, :C_ROPE]
    kcat_ref[...] = jnp.concatenate([ckv, kr_slab], axis=1).astype(BF16)
    klt_ref[...] = ckv.T.astype(BF16)
    qn = _dot(cq, wn_ref[...]).astype(BF16)
    qr = _dot(cq, wr_ref[...])
    lane_q = lax.broadcasted_iota(jnp.int32, qr.shape, 1)
    qr = _rope_lanes(qr, ccq_ref, ssq_ref, (lane_q & (C_ROPE - 1)) < C_ROPE // 2).astype(BF16)
    zeros = jnp.zeros((qr.shape[0], LANES - C_ROPE), BF16)
    for hd in range(C_HEADS):
        ql = _dot(qn[:, hd * C_NOPE:(hd + 1) * C_NOPE], wuk_ref[hd]).astype(BF16)
        q_ref[hd] = jnp.concatenate([ql, qr[:, hd * C_ROPE:(hd + 1) * C_ROPE], zeros], axis=1)


def _mla_proj(x, mod, tps, tab_tiles, layer, j, w_in, g_q, g_kv, wn, wr, wuk, ccq, ssq, cck, ssk, tm):
    T = x.shape[0]
    nt = T // tm
    tab = lambda w: pl.BlockSpec((tm, w), lambda i: (i % tab_tiles, 0))
    return pl.pallas_call(
        _mla_proj_body,
        out_shape=[jax.ShapeDtypeStruct((nt, C_HEADS, tm, C_QK), BF16),
                   jax.ShapeDtypeStruct((T, C_QK), BF16),
                   jax.ShapeDtypeStruct((C_KVLORA, T), BF16),
                   jax.ShapeDtypeStruct((T, C_KVLORA), F32),
                   jax.ShapeDtypeStruct((T, C_ROPE), F32)],
        grid=(nt,),
        in_specs=[_row_spec(tm, D_MODEL), _mod_spec(mod, layer, 0, tps), _mod_spec(mod, layer, 1, tps),
                  _full_spec(w_in, j), _vec_spec(j, C_QLORA), _vec_spec(j, C_KVLORA),
                  _full_spec(wn, j), _full_spec(wr, j), _full_spec(wuk, j),
                  tab(C_HEADS * C_ROPE), tab(C_HEADS * C_ROPE), tab(LANES), tab(LANES)],
        out_specs=[pl.BlockSpec((None, C_HEADS, tm, C_QK), lambda i: (i, 0, 0, 0)),
                   _row_spec(tm, C_QK), pl.BlockSpec((C_KVLORA, tm), lambda i: (0, i)),
                   _row_spec(tm, C_KVLORA), _row_spec(tm, C_ROPE)],
        compiler_params=_cparams("arbitrary"),
        name="mla_proj",
    )(x, mod, mod, w_in, g_q, g_kv, wn, wr, wuk, ccq, ssq, cck, ssk)


def _softmax_update(s, m_ref, l_ref, acc_ref, values):
    m_prev = m_ref[...]
    m_new = jnp.maximum(m_prev, jnp.max(s, axis=-1, keepdims=True))
    alpha = jnp.exp(m_prev - m_new)
    p = jnp.exp(s - m_new)
    l_ref[...] = alpha * l_ref[...] + jnp.sum(p, axis=-1, keepdims=True)
    acc_ref[...] = alpha * acc_ref[...] + _dot(p.astype(BF16), values)
    m_ref[...] = m_new


def _softmax_init(m_ref, l_ref, acc_ref):
    m_ref[...] = jnp.full_like(m_ref, -jnp.inf)
    l_ref[...] = jnp.zeros_like(l_ref)
    acc_ref[...] = jnp.zeros_like(acc_ref)


def _attn_body(qi_ref, kj_ref, last_ref, q_ref, k_ref, kt_ref, o_ref, m_ref, l_ref, acc_ref, *, tq, tk, scale2):
    p_id = pl.program_id(1)
    qi = qi_ref[p_id]
    kj = kj_ref[p_id]

    @pl.when(kj == 0)
    def _():
        _softmax_init(m_ref, l_ref, acc_ref)

    def step(masked):
        k = k_ref[...]
        kt = kt_ref[...]
        if masked:
            key = lax.broadcasted_iota(jnp.int32, (tk, tq), 0) + kj * tk
            tok = lax.broadcasted_iota(jnp.int32, (tk, tq), 1) + qi * tq
            keep = key <= tok
        def scores(hd):
            t = _dot_nt(k, q_ref[hd]) * scale2
            return jnp.where(keep, t, -jnp.inf) if masked else t

        ahead = 2
        queue = [scores(hd) for hd in range(ahead)]
        for hd in range(C_HEADS):
            t = queue.pop(0)
            if hd + ahead < C_HEADS:
                queue.append(scores(hd + ahead))
            m_prev = m_ref[hd]
            m_new = jnp.maximum(m_prev, jnp.max(t, axis=0, keepdims=True))
            alpha = jnp.exp2(m_prev - m_new)
            p = jnp.exp2(t - m_new)
            l_ref[hd] = alpha * l_ref[hd] + jnp.sum(p, axis=0, keepdims=True)
            acc_ref[hd] = alpha * acc_ref[hd] + _dot(kt, p.astype(BF16))
            m_ref[hd] = m_new

    fully_visible = (kj + 1) * tk - 1 <= qi * tq
    pl.when(fully_visible)(lambda: step(False))
    pl.when(jnp.logical_not(fully_visible))(lambda: step(True))

    @pl.when(last_ref[p_id] == 1)
    def _():
        for hd in range(C_HEADS):
            o_ref[hd] = (acc_ref[hd] / l_ref[hd]).astype(BF16)


def _attn_prompt(q, kcat, klat_t, n_seq, seq_len, tq, tk):
    nq, nk = seq_len // tq, seq_len // tk
    pairs = [(i, j) for i in range(nq) for j in range((i * tq + tq - 1) // tk + 1)]
    qi = jnp.asarray([p[0] for p in pairs], jnp.int32)
    kj = jnp.asarray([p[1] for p in pairs], jnp.int32)
    last = jnp.asarray([int(n + 1 == len(pairs) or pairs[n + 1][0] != p[0]) for n, p in enumerate(pairs)], jnp.int32)
    scale2 = (C_NOPE + C_ROPE) ** -0.5 * math.log2(math.e)
    grid_spec = pltpu.PrefetchScalarGridSpec(
        num_scalar_prefetch=3,
        grid=(n_seq, len(pairs)),
        in_specs=[pl.BlockSpec((None, C_HEADS, tq, C_QK), lambda b, p, qi, kj, last: (b * nq + qi[p], 0, 0, 0)),
                  pl.BlockSpec((tk, C_QK), lambda b, p, qi, kj, last: (b * nk + kj[p], 0)),
                  pl.BlockSpec((C_KVLORA, tk), lambda b, p, qi, kj, last: (0, b * nk + kj[p]))],
        out_specs=pl.BlockSpec((None, C_HEADS, C_KVLORA, tq), lambda b, p, qi, kj, last: (b * nq + qi[p], 0, 0, 0)),
        scratch_shapes=[pltpu.VMEM((C_HEADS, 1, tq), F32), pltpu.VMEM((C_HEADS, 1, tq), F32),
                        pltpu.VMEM((C_HEADS, C_KVLORA, tq), F32)])
    return pl.pallas_call(
        functools.partial(_attn_body, tq=tq, tk=tk, scale2=scale2),
        out_shape=jax.ShapeDtypeStruct((n_seq * nq, C_HEADS, C_KVLORA, tq), BF16),
        grid_spec=grid_spec,
        compiler_params=_cparams("arbitrary", "arbitrary"),
        name="mla_attention_prompt",
    )(qi, kj, last, q, kcat, klat_t)


def _attn_paged_body(pt_ref, q_ref, nlat_ref, nrope_ref, lat_hbm, rope_hbm, o_ref,
                     lat_buf, rope_buf, sem, m_ref, l_ref, acc_ref, *, nsq, pages, sub, j, seq_new, scale2):
    b = pl.program_id(0)
    g = pl.program_id(1)
    n_groups = pl.num_programs(1)
    n_steps = pl.num_programs(0) * n_groups
    step = b * n_groups + g
    slot = lax.rem(step, PAGED_SLOTS)
    lookahead = PAGED_SLOTS - 1

    def page_copy(kind, page_id, sl, i):
        src, dst = (lat_hbm, lat_buf) if kind == 0 else (rope_hbm, rope_buf)
        return pltpu.make_async_copy(src.at[j, page_id], dst.at[sl, i], sem.at[kind, sl])

    def start_group(bb, gg, sl):
        for i in range(nsq * pages):
            page_id = pt_ref[bb * nsq + i // pages, gg * pages + i % pages]
            page_copy(0, page_id, sl, i).start(priority=i % 2)
            page_copy(1, page_id, sl, i).start(priority=(i + 1) % 2)

    def start_step(s, sl):
        s = jnp.minimum(s, n_steps - 1)
        start_group(s // n_groups, lax.rem(s, n_groups), sl)

    @pl.when(step == 0)
    def _():
        for s in range(lookahead):
            start_step(s, s)

    def wait_group(sl):
        for i in range(nsq * pages):
            page_copy(0, 0, sl, i).wait()
            page_copy(1, 0, sl, i).wait()

    wait_group(slot)

    def update(carry, t, values):
        m_prev, l_prev, acc = carry
        m_new = jnp.maximum(m_prev, jnp.max(t, axis=-1, keepdims=True))
        alpha = jnp.exp2(m_prev - m_new)
        p = jnp.exp2(t - m_new)
        return (m_new, alpha * l_prev + jnp.sum(p, axis=-1, keepdims=True),
                alpha * acc + _dot(p.astype(BF16), values))

    first = g == 0
    page = lat_buf.shape[2]
    qls = [q_ref[s][:, :C_KVLORA] for s in range(nsq)]
    qrs = [q_ref[s][:, C_KVLORA:C_KVLORA + C_ROPE] for s in range(nsq)]
    work = []
    for s in range(nsq):
        for u in range(pages // sub):
            e0 = s * pages + u * sub
            lat = lat_buf[slot, e0:e0 + sub].reshape(sub * page, C_KVLORA).astype(BF16)
            rp_t = jnp.concatenate([rope_buf[slot, e0 + i] for i in range(sub)], axis=1).astype(BF16)
            work.append((s, lat, (_dot_nt(qls[s], lat) + _dot(qrs[s], rp_t)) * scale2))
    start_step(step + lookahead, lax.rem(step + lookahead, PAGED_SLOTS))
    carries = [(jnp.where(first, -jnp.inf, m_ref[s]), jnp.where(first, 0.0, l_ref[s]),
                jnp.where(first, 0.0, acc_ref[s])) for s in range(nsq)]
    for s, lat, t in work:
        carries[s] = update(carries[s], t, lat)

    @pl.when(step == n_steps - 1)
    def _():
        for ahead in range(1, PAGED_SLOTS):
            wait_group(lax.rem(step + ahead, PAGED_SLOTS))

    @pl.when(g < n_groups - 1)
    def _():
        for s in range(nsq):
            m_ref[s], l_ref[s], acc_ref[s] = carries[s]

    @pl.when(g == n_groups - 1)
    def _():
        for s in range(nsq):
            nlat = nlat_ref[s].astype(BF16)
            t2 = (_dot_nt(qls[s], nlat) + _dot_nt(qrs[s], nrope_ref[s].astype(BF16))) * scale2
            tok = lax.broadcasted_iota(jnp.int32, t2.shape, 0) & (seq_new - 1)
            key = lax.broadcasted_iota(jnp.int32, t2.shape, 1)
            _, l_fin, acc = update(carries[s], jnp.where(key <= tok, t2, -jnp.inf), nlat)
            o_ref[s] = (acc / l_fin).astype(BF16)


def _attn_paged(q, new_lat, new_rope, pool_lat, pool_rope_t, page_table, j, seq_new):
    n_seq, n_pages = page_table.shape
    page = pool_lat.shape[2]
    nsq, pages, sub = 2, 16, 4
    rows = q.shape[1]
    scale2 = (C_NOPE + C_ROPE) ** -0.5 * math.log2(math.e)
    grid_spec = pltpu.PrefetchScalarGridSpec(
        num_scalar_prefetch=1,
        grid=(n_seq // nsq, n_pages // pages),
        in_specs=[pl.BlockSpec((nsq, rows, C_QK), lambda b, s, pt: (b, 0, 0)),
                  pl.BlockSpec((nsq,) + new_lat.shape[1:], lambda b, s, pt: (b, 0, 0)),
                  pl.BlockSpec((nsq,) + new_rope.shape[1:], lambda b, s, pt: (b, 0, 0)),
                  pl.BlockSpec(memory_space=pl.ANY), pl.BlockSpec(memory_space=pl.ANY)],
        out_specs=pl.BlockSpec((nsq, rows, C_KVLORA), lambda b, s, pt: (b, 0, 0)),
        scratch_shapes=[pltpu.VMEM((PAGED_SLOTS, nsq * pages, page, C_KVLORA), F32),
                        pltpu.VMEM((PAGED_SLOTS, nsq * pages, C_ROPE, page), F32),
                        pltpu.SemaphoreType.DMA((2, PAGED_SLOTS)),
                        pltpu.VMEM((nsq, rows, 1), F32), pltpu.VMEM((nsq, rows, 1), F32),
                        pltpu.VMEM((nsq, rows, C_KVLORA), F32)])
    return pl.pallas_call(
        functools.partial(_attn_paged_body, nsq=nsq, pages=pages, sub=sub, j=j, seq_new=seq_new, scale2=scale2),
        out_shape=jax.ShapeDtypeStruct((n_seq, rows, C_KVLORA), BF16),
        grid_spec=grid_spec,
        compiler_params=_cparams("arbitrary", "arbitrary"),
        name="mla_attention_paged",
    )(page_table, q, new_lat, new_rope, pool_lat, pool_rope_t)


def _mla_out_body(x_ref, g_ref, o_ref, wuv_ref, wout_ref, lg_ref, lb_ref, y_ref):
    parts = [_dot_tn(o_ref[hd], wuv_ref[hd]) for hd in range(C_HEADS)]
    o = jnp.concatenate(parts, axis=1).astype(BF16)
    y_ref[...] = _residual_ln(x_ref[...], g_ref, _dot(o, wout_ref[...]), lg_ref, lb_ref)


def _mla_out(x, mod, tps, layer, j, o_lat, wuv, w_out, ln_g, ln_b, tm):
    T = x.shape[0]
    return pl.pallas_call(
        _mla_out_body,
        out_shape=jax.ShapeDtypeStruct((T, D_MODEL), F32),
        grid=(T // tm,),
        in_specs=[_row_spec(tm, D_MODEL), _mod_spec(mod, layer, 2, tps),
                  pl.BlockSpec((None, C_HEADS, C_KVLORA, tm), lambda i: (i, 0, 0, 0)),
                  _full_spec(wuv, j), _full_spec(w_out, j),
                  _vec_spec(layer, D_MODEL), _vec_spec(layer, D_MODEL)],
        out_specs=_row_spec(tm, D_MODEL),
        compiler_params=_cparams("arbitrary"),
        name="mla_out",
    )(x, mod, o_lat, wuv, w_out, ln_g, ln_b)


def _rope_tables(pos, reps, width):
    half = C_ROPE // 2
    inv = ROPE_THETA ** (-jnp.arange(half, dtype=F32) / half)
    ang = pos.astype(F32)[:, None] * inv
    cos, sin = jnp.cos(ang), jnp.sin(ang)
    cc = jnp.tile(jnp.concatenate([cos, cos], axis=1), (1, reps))
    ss = jnp.tile(jnp.concatenate([-sin, sin], axis=1), (1, reps))
    pad = width - cc.shape[1]
    return jnp.pad(cc, ((0, 0), (0, pad))), jnp.pad(ss, ((0, 0), (0, pad)))


def _prepare_params(p):
    vec = lambda a: a.reshape(a.shape[0], 1, a.shape[1])
    w_uq = p['c_w_uq']
    n_c = w_uq.shape[0]
    c_w_in = jnp.pad(p['c_w_in'], ((0, 0), (0, 0), (0, LANES - C_ROPE)))
    lb_all = jnp.cumsum(jax.nn.softmax(p['b_lb'].astype(F32), axis=0), axis=0)
    lb_all = lb_all - lb_all[:1]
    return dict(
        ln1_g=vec(p['ln1_g']), ln1_b=vec(p['ln1_b']), ln2_g=vec(p['ln2_g']), ln2_b=vec(p['ln2_b']),
        ffn_w1=p['ffn_w1'].astype(BF16), ffn_w2=p['ffn_w2'].astype(BF16),
        a_w_in=p['a_w_in'].astype(BF16), a_ln_g=vec(p['a_ln_g']), a_ln_b=vec(p['a_ln_b']),
        a_w_out=p['a_w_out'].astype(BF16),
        b_w_in=p['b_w_in'].astype(BF16), b_w_out=p['b_w_out'].astype(BF16), lb_all=lb_all,
        c_w_in=c_w_in.astype(BF16), c_g_q=vec(p['c_g_q']), c_g_kv=vec(p['c_g_kv']),
        c_wn=w_uq[..., :C_NOPE].reshape(n_c, C_QLORA, C_HEADS * C_NOPE).astype(BF16),
        c_wr=w_uq[..., C_NOPE:].reshape(n_c, C_QLORA, C_HEADS * C_ROPE).astype(BF16),
        c_wuk=jnp.transpose(p['c_w_uk'], (0, 2, 3, 1)).astype(BF16),
        c_wuv=jnp.transpose(p['c_w_uv'], (0, 2, 1, 3)).astype(BF16),
        c_w_out=p['c_w_out'].astype(BF16),
    )


def _sgu_mixing(w_s, b_s, chunk):
    reps = CHUNK_A // chunk
    causal = jnp.tril(jnp.ones((chunk, chunk), dtype=bool))
    ws = jnp.where(causal, w_s[:, :, :chunk, :chunk], 0)
    eye = jnp.eye(reps, dtype=w_s.dtype)
    ws = jnp.einsum('ab,jgts->jgatbs', eye, ws).reshape(w_s.shape[0], A_GROUPS, CHUNK_A, CHUNK_A)
    bias = jnp.tile(jnp.transpose(b_s[:, :, :chunk], (0, 2, 1)), (1, reps, 1))
    bias = jnp.repeat(bias, A_GDIM, axis=2)
    return ws.astype(BF16), bias


def _run_trunk(x, mod, n_seq, seq_len, q_pos, hgrn_state0, mla_cache, prm, raw, tm):
    T = x.shape[0]
    per_seq_mod = mod.ndim == 5
    tps = (seq_len // tm) if per_seq_mod else 1
    sgu_chunk = min(CHUNK_A, seq_len)
    ws, bias = _sgu_mixing(raw['a_w_s'], raw['a_b_s'], sgu_chunk)
    chunk_v, hgrn_states, lat_rows, rope_rows = [], [], [], []
    for i in range(DEPTH):
        kind, j = i % N_MIXERS, i // N_MIXERS
        if kind == 0:
            x, v_rows = _sgu_layer(x, mod, tps, i, j, prm['a_w_in'], prm['a_ln_g'], prm['a_ln_b'], ws, bias,
                                   prm['a_w_out'], prm['ln1_g'], prm['ln1_b'], tm, emit_v=mla_cache is not None)
            chunk_v.append(v_rows)
        elif kind == 1:
            lb = prm['lb_all'][i].reshape(1, D_MODEL)
            q, k, lf, v, gs = _hgrn_proj(x, mod, tps, i, j, prm['b_w_in'], lb, tm)
            if seq_len % 64 == 0:
                C, lpad, hb = 64, seq_len, 4
                rec_tm = min(seq_len, 512)
                rec_in = (q, k, lf, v)
            else:
                C = lpad = rec_tm = SUBLANES
                hb = B_HEADS
                padseq = lambda a: jnp.pad(a.reshape(n_seq, seq_len, D_MODEL),
                                           ((0, 0), (0, lpad - seq_len), (0, 0))).reshape(n_seq * lpad, D_MODEL)
                rec_in = tuple(padseq(a) for a in (q, k, lf, v))
            s0 = None if hgrn_state0 is None else hgrn_state0[j]
            o, S = _hgrn_rec(*rec_in, s0, n_seq, lpad, rec_tm, C, hb)
            if lpad != seq_len:
                o = o.reshape(n_seq, lpad, D_MODEL)[:, :seq_len].reshape(T, D_MODEL)
            hgrn_states.append(S)
            x = _hgrn_out(x, mod, tps, i, j, o, gs, prm['b_w_out'], prm['ln1_g'], prm['ln1_b'], tm)
        else:
            tq = min(tm, 512)
            tps_q = (seq_len // tq) if per_seq_mod else 1
            pos_rows = q_pos if tq <= seq_len else jnp.tile(q_pos, tq // seq_len)
            ccq, ssq = _rope_tables(pos_rows, C_HEADS, C_HEADS * C_ROPE)
            cck, ssk = _rope_tables(pos_rows, 1, LANES)
            qcat, kcat, klat_t, lat, kr = _mla_proj(x, mod, tps_q, pos_rows.shape[0] // tq, i, j, prm['c_w_in'],
                                            prm['c_g_q'], prm['c_g_kv'], prm['c_wn'], prm['c_wr'], prm['c_wuk'],
                                            ccq, ssq, cck, ssk, tq)
            if mla_cache is None:
                o_lat = _attn_prompt(qcat, kcat, klat_t, n_seq, seq_len, tq, min(seq_len, 512))
            else:
                pool_lat, pool_rope_t, pt = mla_cache
                qs = qcat.reshape(C_HEADS, n_seq, seq_len, C_QK).transpose(1, 0, 2, 3)
                qs = qs.reshape(n_seq, C_HEADS * seq_len, C_QK)
                padk = lambda a: jnp.pad(a.reshape(n_seq, seq_len, a.shape[1]), ((0, 0), (0, 16 - seq_len), (0, 0)))
                o_s = _attn_paged(qs, padk(lat), padk(kr), pool_lat, pool_rope_t, pt, j, seq_len)
                o_lat = o_s.reshape(n_seq, C_HEADS, seq_len, C_KVLORA).transpose(1, 3, 0, 2)
                o_lat = o_lat.reshape(1, C_HEADS, C_KVLORA, T)
            x = _mla_out(x, mod, tps_q, i, j, o_lat, prm['c_wuv'], prm['c_w_out'], prm['ln1_g'], prm['ln1_b'], tq)
            lat_rows.append(lat.reshape(n_seq, seq_len, C_KVLORA))
            rope_rows.append(kr.reshape(n_seq, seq_len, C_ROPE))
        x = _ffn_layer(x, mod, tps, i, prm['ffn_w1'], prm['ffn_w2'], prm['ln2_g'], prm['ln2_b'], tm)
    stack = lambda xs: jnp.stack(xs) if xs and xs[0] is not None else None
    return x, stack(chunk_v), jnp.stack(hgrn_states), jnp.stack(lat_rows), jnp.stack(rope_rows)


def kernel(x_prompt, x_sample, cache_kv_latent, cache_k_rope, state_hgrn, page_table, c_prompt, c_sample,
           w_ada, b_ada, ln1_g, ln1_b, ln2_g, ln2_b, ffn_w1, ffn_w2, a_w_in, a_ln_g, a_ln_b, a_w_s, a_b_s,
           a_w_out, b_w_in, b_lb, b_w_out, c_w_in, c_g_q, c_g_kv, c_w_uq, c_w_uk, c_w_uv, c_w_out):
    raw = dict(ln1_g=ln1_g, ln1_b=ln1_b, ln2_g=ln2_g, ln2_b=ln2_b, ffn_w1=ffn_w1, ffn_w2=ffn_w2,
               a_w_in=a_w_in, a_ln_g=a_ln_g, a_ln_b=a_ln_b, a_w_s=a_w_s, a_b_s=a_b_s, a_w_out=a_w_out,
               b_w_in=b_w_in, b_lb=b_lb, b_w_out=b_w_out, c_w_in=c_w_in, c_g_q=c_g_q, c_g_kv=c_g_kv,
               c_w_uq=c_w_uq, c_w_uk=c_w_uk, c_w_uv=c_w_uv, c_w_out=c_w_out)
    prm = _prepare_params(raw)
    nb, seq, d = x_prompt.shape
    ns, sseq, _ = x_sample.shape
    past_len = page_table.shape[1] * cache_kv_latent.shape[2]
    pos_prompt = jnp.arange(seq, dtype=jnp.int32)
    pos_sample = past_len + jnp.arange(sseq, dtype=jnp.int32)

    mod_p, mod_s = _modulation(c_prompt, jnp.repeat(c_sample, sseq, axis=0), w_ada, b_ada)
    mod_p = mod_p.reshape(DEPTH, nb, 6, 1, d)

    tm_p = 512
    y_p, _, hs_p, lat_p, rope_p = _run_trunk(x_prompt.reshape(nb * seq, d), mod_p, nb, seq, pos_prompt,
                                             None, None, prm, raw, tm_p)
    y_s, v_s, hs_s, lat_s, rope_s = _run_trunk(x_sample.reshape(ns * sseq, d), mod_s, ns, sseq, pos_sample,
                                               state_hgrn,
                                               (cache_kv_latent, jnp.swapaxes(cache_k_rope, 2, 3), page_table),
                                               prm, raw, ns * sseq)
    return (y_p.reshape(nb, seq, d), y_s.reshape(ns, sseq, d), hs_p, hs_s, lat_p, rope_p, lat_s, rope_s,
            v_s.reshape(v_s.shape[0], ns, sseq, d))
```

```python
import functools
import math

import jax
import jax.numpy as jnp
from jax import lax
from jax.experimental import pallas as pl
from jax.experimental.pallas import tpu as pltpu

F32 = jnp.float32
BF16 = jnp.bfloat16

D_MODEL = 1024
DEPTH = 4
N_MIXERS = 3
CHUNK_A = 128
A_GROUPS = 8
A_GDIM = D_MODEL // A_GROUPS
B_HEADS = 8
B_DK = 128
B_DV = D_MODEL // B_HEADS
C_HEADS = 8
C_NOPE = 128
C_ROPE = 64
C_V = 128
C_QLORA = 512
C_KVLORA = 256
ROPE_THETA = 10000.0
D_FF = 4 * D_MODEL
ALPHA = (2.0 * DEPTH) ** 0.25
EPS = 1e-6

LANES = 128
SUBLANES = 8
C_QK = C_KVLORA + LANES
VMEM_LIMIT = 56 * 1024 * 1024
PAGED_SLOTS = 3


def _cparams(*sem):
    return pltpu.CompilerParams(dimension_semantics=sem, vmem_limit_bytes=VMEM_LIMIT)


def _dot(a, b):
    return jnp.dot(a, b, preferred_element_type=F32)


def _dot_nt(a, b):
    return lax.dot_general(a, b, (((1,), (1,)), ((), ())), preferred_element_type=F32)


def _dot_tn(a, b):
    return lax.dot_general(a, b, (((0,), (0,)), ((), ())), preferred_element_type=F32)


def _layer_norm(y, g, b):
    mu = jnp.mean(y, axis=-1, keepdims=True)
    yc = y - mu
    var = jnp.mean(yc * yc, axis=-1, keepdims=True)
    return yc * lax.rsqrt(var + EPS) * g + b


def _rms(y):
    return y * lax.rsqrt(jnp.mean(y * y, axis=-1, keepdims=True) + EPS)


def _silu(x):
    return x * jax.nn.sigmoid(x)


def _gelu_tanh(x):
    return 0.5 * x * (1.0 + jnp.tanh(math.sqrt(2.0 / math.pi) * (x + 0.044715 * (x * x * x))))


def _modulate(x, sh_ref, sc_ref):
    return x * (1.0 + sc_ref[...]) + sh_ref[...]


def _residual_ln(x, gate_ref, out, lg_ref, lb_ref):
    return _layer_norm(ALPHA * x + gate_ref[...] * out, lg_ref[...], lb_ref[...])


def _mod_spec(mod, layer, which, tiles_per_seq):
    if mod.ndim == 5:
        return pl.BlockSpec((None, None, None, 1, D_MODEL), lambda i: (layer, i // tiles_per_seq, which, 0, 0))
    return pl.BlockSpec((None, mod.shape[1], D_MODEL), lambda i: (layer, 0, which))


def _vec_spec(layer, width):
    return pl.BlockSpec((None, 1, width), lambda i: (layer, 0, 0))


def _full_spec(arr, layer=None):
    if layer is None:
        nd = arr.ndim
        return pl.BlockSpec(arr.shape, lambda i: (0,) * nd)
    nd = arr.ndim - 1
    return pl.BlockSpec((None,) + arr.shape[1:], lambda i: (layer,) + (0,) * nd)


def _row_spec(tm, width):
    return pl.BlockSpec((tm, width), lambda i: (i, 0))


def _mod_body(ca_ref, cb_ref, w_ref, b_ref, oa_ref, ob_ref):
    w = w_ref[...].astype(BF16)
    oa_ref[...] = _dot(_silu(ca_ref[...]).astype(BF16), w) + b_ref[...]
    ob_ref[...] = _dot(_silu(cb_ref[...]).astype(BF16), w) + b_ref[...]


def _modulation(c_a, c_b, w_ada, b_ada):
    tn = 1536
    width = w_ada.shape[2]
    rows = lambda c: pl.BlockSpec((c.shape[0], D_MODEL), lambda l, j: (0, 0))
    out = lambda c: pl.BlockSpec((None, c.shape[0], tn), lambda l, j: (l, 0, j))
    return pl.pallas_call(
        _mod_body,
        out_shape=[jax.ShapeDtypeStruct((DEPTH, c.shape[0], width), F32) for c in (c_a, c_b)],
        grid=(DEPTH, width // tn),
        in_specs=[rows(c_a), rows(c_b),
                  pl.BlockSpec((None, D_MODEL, tn), lambda l, j: (l, 0, j)),
                  pl.BlockSpec((None, 1, tn), lambda l, j: (l, 0, j))],
        out_specs=[out(c_a), out(c_b)],
        compiler_params=_cparams("arbitrary", "arbitrary"),
        name="adaln_modulation",
    )(c_a, c_b, w_ada, b_ada.reshape(DEPTH, 1, width))


def _ffn_body(x_ref, sh_ref, sc_ref, g_ref, w1_ref, w2_ref, lg_ref, lb_ref, o_ref, acc_ref, *, fc):
    x = x_ref[...]
    h = _modulate(x, sh_ref, sc_ref).astype(BF16)
    for c in range(D_FF // fc):
        a = _dot(h, w1_ref[:, c * fc:(c + 1) * fc])
        a = jnp.square(jnp.maximum(a, 0.0)).astype(BF16)
        d = _dot(a, w2_ref[c * fc:(c + 1) * fc, :])
        if c == 0:
            acc_ref[...] = d
        else:
            acc_ref[...] += d
    o_ref[...] = _residual_ln(x, g_ref, acc_ref[...], lg_ref, lb_ref)


def _ffn_layer(x, mod, tps, layer, w1, w2, ln_g, ln_b, tm):
    T = x.shape[0]
    return pl.pallas_call(
        functools.partial(_ffn_body, fc=1024),
        out_shape=jax.ShapeDtypeStruct((T, D_MODEL), F32),
        grid=(T // tm,),
        in_specs=[_row_spec(tm, D_MODEL),
                  _mod_spec(mod, layer, 3, tps), _mod_spec(mod, layer, 4, tps), _mod_spec(mod, layer, 5, tps),
                  _full_spec(w1, layer), _full_spec(w2, layer),
                  _vec_spec(layer, D_MODEL), _vec_spec(layer, D_MODEL)],
        out_specs=_row_spec(tm, D_MODEL),
        scratch_shapes=[pltpu.VMEM((tm, D_MODEL), F32)],
        compiler_params=_cparams("arbitrary"),
        name="ffn_sublayer",
    )(x, mod, mod, mod, w1, w2, ln_g, ln_b)


def _sgu_body(x_ref, sh_ref, sc_ref, g_ref, win_ref, lng_ref, lnb_ref, ws_ref, bias_ref, wout_ref,
              lg_ref, lb_ref, o_ref, *rest, tm, pieces, emit_v):
    if emit_v:
        v_ref, gated_ref = rest
    else:
        (gated_ref,) = rest
    rows = [slice(p * tm // pieces, (p + 1) * tm // pieces) for p in range(pieces)]
    per_row = lambda ref, r: ref[...] if ref.shape[0] == 1 else ref[r, :]

    def in_products(r):
        h = (x_ref[r, :] * (1.0 + per_row(sc_ref, r)) + per_row(sh_ref, r)).astype(BF16)
        return _dot(h, win_ref[:, :D_MODEL]), _dot(h, win_ref[:, D_MODEL:])

    def gate(r, zu, zv):
        u = _gelu_tanh(zu)
        v = _layer_norm(_gelu_tanh(zv), lng_ref[...], lnb_ref[...])
        if emit_v:
            v_ref[r, :] = v
        vb = v.astype(BF16)
        nch = (r.stop - r.start) // CHUNK_A
        for g in range(A_GROUPS):
            c = slice(g * A_GDIM, (g + 1) * A_GDIM)
            rhs = jnp.concatenate([vb[n * CHUNK_A:(n + 1) * CHUNK_A, c] for n in range(nch)], axis=1)
            mixed = _dot(ws_ref[g], rhs)
            for n in range(nch):
                rn = slice(n * CHUNK_A, (n + 1) * CHUNK_A)
                gated = u[rn, c] * (mixed[:, n * A_GDIM:(n + 1) * A_GDIM] + bias_ref[:, c])
                gated_ref[r.start + n * CHUNK_A:r.start + (n + 1) * CHUNK_A, c] = gated.astype(BF16)
        return _dot(gated_ref[r, :], wout_ref[...])

    def finish(r, out):
        y = ALPHA * x_ref[r, :] + per_row(g_ref, r) * out
        o_ref[r, :] = _layer_norm(y, lg_ref[...], lb_ref[...])

    z = in_products(rows[0])
    outs = []
    for p, r in enumerate(rows):
        z_next = in_products(rows[p + 1]) if p + 1 < pieces else None
        outs.append(gate(r, *z))
        z = z_next
        if p > 0:
            finish(rows[p - 1], outs[p - 1])
    finish(rows[-1], outs[-1])


def _sgu_layer(x, mod, tps, layer, j, w_in, ln_g, ln_b, ws, bias, w_out, ln1_g, ln1_b, tm, emit_v):
    T = x.shape[0]
    out_shape = [jax.ShapeDtypeStruct((T, D_MODEL), F32)]
    out_specs = [_row_spec(tm, D_MODEL)]
    if emit_v:
        out_shape.append(jax.ShapeDtypeStruct((T, D_MODEL), F32))
        out_specs.append(_row_spec(tm, D_MODEL))
    res = pl.pallas_call(
        functools.partial(_sgu_body, tm=tm, pieces=2 if tm >= 4 * CHUNK_A else 1, emit_v=emit_v),
        out_shape=out_shape,
        grid=(T // tm,),
        in_specs=[_row_spec(tm, D_MODEL),
                  _mod_spec(mod, layer, 0, tps), _mod_spec(mod, layer, 1, tps), _mod_spec(mod, layer, 2, tps),
                  _full_spec(w_in, j), _vec_spec(j, D_MODEL), _vec_spec(j, D_MODEL),
                  _full_spec(ws, j), _full_spec(bias, j), _full_spec(w_out, j),
                  _vec_spec(layer, D_MODEL), _vec_spec(layer, D_MODEL)],
        out_specs=out_specs,
        scratch_shapes=[pltpu.VMEM((tm, D_MODEL), BF16)],
        compiler_params=_cparams("arbitrary"),
        name="sgu_sublayer",
    )(x, mod, mod, mod, w_in, ln_g, ln_b, ws, bias, w_out, ln1_g, ln1_b)
    return (res[0], res[1]) if emit_v else (res[0], None)


def _hgrn_proj_body(x_ref, sh_ref, sc_ref, win_ref, lb_ref, q_ref, k_ref, lf_ref, v_ref, gs_ref):
    d = D_MODEL
    tm = x_ref.shape[0]
    pieces = 4 if tm % (4 * LANES) == 0 and sh_ref.shape[0] == 1 else 1
    rows = [slice(p * tm // pieces, (p + 1) * tm // pieces) for p in range(pieces)]

    def products(r):
        h = _modulate(x_ref[r, :], sh_ref, sc_ref).astype(BF16)
        return [_dot(h, win_ref[:, part * d:(part + 1) * d]) for part in (1, 0, 2, 3)]

    def gate(r, fz, zq, zv, zg):
        lb = lb_ref[...]
        e = jnp.exp(-jnp.abs(fz))
        a = jnp.log(lb)
        b = jnp.log1p(-lb) + (jnp.minimum(fz, 0.0) - jnp.log1p(e))
        lf_ref[r, :] = (jnp.maximum(a, b) + jnp.log1p(jnp.exp(-jnp.abs(a - b)))) * math.log2(math.e)
        k_ref[r, :] = (1.0 - lb) * (jnp.where(fz >= 0.0, e, 1.0) / (1.0 + e))
        q_ref[r, :] = _silu(zq)
        v_ref[r, :] = zv
        gs_ref[r, :] = _silu(zg)

    z = [products(r) for r in rows]
    for r, zr in zip(rows, z):
        gate(r, *zr)


def _hgrn_proj(x, mod, tps, layer, j, w_in, lb, tm):
    T = x.shape[0]
    shp = jax.ShapeDtypeStruct((T, D_MODEL), F32)
    return pl.pallas_call(
        _hgrn_proj_body,
        out_shape=[shp] * 5,
        grid=(T // tm,),
        in_specs=[_row_spec(tm, D_MODEL), _mod_spec(mod, layer, 0, tps), _mod_spec(mod, layer, 1, tps),
                  _full_spec(w_in, j), _full_spec(lb)],
        out_specs=[_row_spec(tm, D_MODEL)] * 5,
        compiler_params=_cparams("arbitrary"),
        name="hgrn_proj",
    )(x, mod, mod, w_in, lb)


def _hgrn_rec_body(*refs, C, nchunk, hb, has_s0):
    if has_s0:
        q_ref, k_ref, g_ref, v_ref, s0_ref, o_ref, sout_ref, st_ref = refs
    else:
        q_ref, k_ref, g_ref, v_ref, o_ref, sout_ref, st_ref = refs
    t = pl.program_id(2)

    @pl.when(t == 0)
    def _():
        for hd in range(hb):
            st_ref[hd] = s0_ref[hd].T if has_s0 else jnp.zeros((B_DV, B_DK), F32)

    row = lax.broadcasted_iota(jnp.int32, (C, B_DK), 0)
    row_a = lax.broadcasted_iota(jnp.int32, (C, C), 0)
    col_a = lax.broadcasted_iota(jnp.int32, (C, C), 1)
    tri = jnp.where(row_a >= col_a, 1.0, 0.0).astype(BF16)
    sub = row & (SUBLANES - 1)
    levels = []
    for m in (1, 2, 4, 8, 16, 32, 64, 128):
        if 2 * m <= C:
            shift = int(math.log2(2 * m))
            pair = (((row_a >> shift) == (col_a >> shift)) & ((row_a & (2 * m - 1)) >= m)
                    & ((col_a & (2 * m - 1)) < m))
            levels.append((m, (row & (2 * m - 1)) >= m, pair))

    def prefix(c, hd):
        rows = slice(c * C, (c + 1) * C)
        cols = slice(hd * B_DK, (hd + 1) * B_DK)
        gc = g_ref[rows, cols]
        g_hi = gc.astype(BF16)
        r1 = gc - g_hi.astype(F32)
        g_mid = r1.astype(BF16)
        g_lo = (r1 - g_mid.astype(F32)).astype(BF16)
        b3 = _dot(tri, jnp.concatenate([g_hi, g_mid, g_lo], axis=1))
        return rows, cols, b3

    def products(pre, st):
        rows, cols, b3 = pre
        qc = q_ref[rows, cols]
        kc = k_ref[rows, cols]
        b = b3[:, 0:B_DK] + b3[:, B_DK:2 * B_DK] + b3[:, 2 * B_DK:3 * B_DK]
        o_inter = _dot_nt((qc * jnp.exp2(b)).astype(BF16), st.astype(BF16))
        b3d = b.reshape(C // SUBLANES, SUBLANES, B_DK)
        sub_row = lambda i: jnp.broadcast_to(b3d[:, i:i + 1, :], b3d.shape).reshape(C, B_DK)
        level_dots = []
        for m, upper, pair in levels:
            if m == 1:
                bref = jnp.where(upper, pltpu.roll(b3d, 1, 1).reshape(C, B_DK), b)
            elif m == 2:
                bref = jnp.where(sub < 4, sub_row(1), sub_row(5))
            elif m == 4:
                bref = sub_row(3)
            else:
                bref = jnp.concatenate(
                    [jnp.broadcast_to(b[i * 2 * m + m - 1:i * 2 * m + m, :], (2 * m, B_DK))
                     for i in range(C // (2 * m))], axis=0)
            q_up = jnp.where(upper, qc * jnp.exp2(b - bref), 0.0)
            k_lo = jnp.where(upper, 0.0, kc * jnp.exp2(bref - b))
            level_dots.append(_dot_nt(q_up.astype(BF16), k_lo.astype(BF16)))
        diag = jnp.sum(qc * kc, axis=1, keepdims=True)
        b_last = b[C - 1:C, :]
        k_dec = (kc * jnp.exp2(b_last - b)).astype(BF16)
        return rows, cols, o_inter, level_dots, diag, k_dec, jnp.exp2(b_last)

    def finish(prod, st):
        rows, cols, o_inter, level_dots, diag, k_dec, decay = prod
        vb = v_ref[rows, cols].astype(BF16)
        a_mat = jnp.where(row_a == col_a, diag, 0.0)
        for (m, upper, pair), a_m in zip(levels, level_dots):
            a_mat = jnp.where(pair, a_m, a_mat)
        o_ref[rows, cols] = o_inter + _dot(a_mat.astype(BF16), vb)
        return st * decay + _dot_tn(vb, k_dec)

    states = [st_ref[hd] for hd in range(hb)]
    pre = [prefix(0, hd) for hd in range(hb)]
    for c in range(nchunk):
        prods = [products(pre[hd], states[hd]) for hd in range(hb)]
        if c + 1 < nchunk:
            pre = [prefix(c + 1, hd) for hd in range(hb)]
        states = [finish(prods[hd], states[hd]) for hd in range(hb)]
    for hd in range(hb):
        st_ref[hd] = states[hd]

    @pl.when(t == pl.num_programs(2) - 1)
    def _():
        for hd in range(hb):
            sout_ref[hd] = states[hd].T


def _hgrn_rec(q, k, lf, v, s0, n_seq, seq_len, tm, C, hb):
    T = q.shape[0]
    nt = seq_len // tm
    blk = pl.BlockSpec((tm, hb * B_DK), lambda b, h, t: (b * nt + t, h))
    st_spec = pl.BlockSpec((None, hb, B_DK, B_DV), lambda b, h, t: (b, h, 0, 0))
    has_s0 = s0 is not None
    in_specs = [blk] * 4 + ([st_spec] if has_s0 else [])
    args = (q, k, lf, v) + ((s0,) if has_s0 else ())
    return pl.pallas_call(
        functools.partial(_hgrn_rec_body, C=C, nchunk=tm // C, hb=hb, has_s0=has_s0),
        out_shape=[jax.ShapeDtypeStruct((T, D_MODEL), F32),
                   jax.ShapeDtypeStruct((n_seq, B_HEADS, B_DK, B_DV), F32)],
        grid=(n_seq, B_HEADS // hb, nt),
        in_specs=in_specs,
        out_specs=[blk, st_spec],
        scratch_shapes=[pltpu.VMEM((hb, B_DV, B_DK), F32)],
        compiler_params=_cparams("arbitrary", "arbitrary", "arbitrary"),
        name="hgrn_recurrence",
    )(*args)


def _hgrn_out_body(x_ref, g_ref, o_ref, gs_ref, wout_ref, lg_ref, lb_ref, y_ref):
    o = o_ref[...]
    parts = [_rms(o[:, h * B_DV:(h + 1) * B_DV]) for h in range(B_HEADS)]
    y = (jnp.concatenate(parts, axis=1) * gs_ref[...]).astype(BF16)
    y_ref[...] = _residual_ln(x_ref[...], g_ref, _dot(y, wout_ref[...]), lg_ref, lb_ref)


def _hgrn_out(x, mod, tps, layer, j, o, gs, w_out, ln_g, ln_b, tm):
    T = x.shape[0]
    return pl.pallas_call(
        _hgrn_out_body,
        out_shape=jax.ShapeDtypeStruct((T, D_MODEL), F32),
        grid=(T // tm,),
        in_specs=[_row_spec(tm, D_MODEL), _mod_spec(mod, layer, 2, tps),
                  _row_spec(tm, D_MODEL), _row_spec(tm, D_MODEL),
                  _full_spec(w_out, j), _vec_spec(layer, D_MODEL), _vec_spec(layer, D_MODEL)],
        out_specs=_row_spec(tm, D_MODEL),
        compiler_params=_cparams("arbitrary"),
        name="hgrn_out",
    )(x, mod, o, gs, w_out, ln_g, ln_b)


def _rope_lanes(x, cc_ref, ss_ref, period_first_half):
    n = x.shape[1]
    half = C_ROPE // 2
    rot = jnp.where(period_first_half, pltpu.roll(x, n - half, 1), pltpu.roll(x, half, 1))
    return x * cc_ref[...] + rot * ss_ref[...]


def _mla_proj_body(x_ref, sh_ref, sc_ref, win_ref, gq_ref, gkv_ref, wn_ref, wr_ref, wuk_ref,
                   ccq_ref, ssq_ref, cck_ref, ssk_ref, q_ref, kcat_ref, klt_ref, lat_ref, kr_ref):
    h = _modulate(x_ref[...], sh_ref, sc_ref).astype(BF16)
    a = _dot(h, win_ref[...])
    cq = (_rms(a[:, :C_QLORA]) * gq_ref[...]).astype(BF16)
    ckv = _rms(a[:, C_QLORA:C_QLORA + C_KVLORA]) * gkv_ref[...]
    kr_slab = a[:, C_QLORA + C_KVLORA:]
    lane_k = lax.broadcasted_iota(jnp.int32, kr_slab.shape, 1)
    kr_slab = _rope_lanes(kr_slab, cck_ref, ssk_ref, (lane_k & (C_ROPE - 1)) < C_ROPE // 2)
    lat_ref[...] = ckv
    kr_ref[...] = kr_slab[:, :C_ROPE]
    kcat_ref[...] = jnp.concatenate([ckv, kr_slab], axis=1).astype(BF16)
    klt_ref[...] = ckv.T.astype(BF16)
    qn = _dot(cq, wn_ref[...]).astype(BF16)
    qr = _dot(cq, wr_ref[...])
    lane_q = lax.broadcasted_iota(jnp.int32, qr.shape, 1)
    qr = _rope_lanes(qr, ccq_ref, ssq_ref, (lane_q & (C_ROPE - 1)) < C_ROPE // 2).astype(BF16)
    zeros = jnp.zeros((qr.shape[0], LANES - C_ROPE), BF16)
    for hd in range(C_HEADS):
        ql = _dot(qn[:, hd * C_NOPE:(hd + 1) * C_NOPE], wuk_ref[hd]).astype(BF16)
        q_ref[hd] = jnp.concatenate([ql, qr[:, hd * C_ROPE:(hd + 1) * C_ROPE], zeros], axis=1)


def _mla_proj(x, mod, tps, tab_tiles, layer, j, w_in, g_q, g_kv, wn, wr, wuk, ccq, ssq, cck, ssk, tm):
    T = x.shape[0]
    nt = T // tm
    tab = lambda w: pl.BlockSpec((tm, w), lambda i: (i % tab_tiles, 0))
    return pl.pallas_call(
        _mla_proj_body,
        out_shape=[jax.ShapeDtypeStruct((nt, C_HEADS, tm, C_QK), BF16),
                   jax.ShapeDtypeStruct((T, C_QK), BF16),
                   jax.ShapeDtypeStruct((C_KVLORA, T), BF16),
                   jax.ShapeDtypeStruct((T, C_KVLORA), F32),
                   jax.ShapeDtypeStruct((T, C_ROPE), F32)],
        grid=(nt,),
        in_specs=[_row_spec(tm, D_MODEL), _mod_spec(mod, layer, 0, tps), _mod_spec(mod, layer, 1, tps),
                  _full_spec(w_in, j), _vec_spec(j, C_QLORA), _vec_spec(j, C_KVLORA),
                  _full_spec(wn, j), _full_spec(wr, j), _full_spec(wuk, j),
                  tab(C_HEADS * C_ROPE), tab(C_HEADS * C_ROPE), tab(LANES), tab(LANES)],
        out_specs=[pl.BlockSpec((None, C_HEADS, tm, C_QK), lambda i: (i, 0, 0, 0)),
                   _row_spec(tm, C_QK), pl.BlockSpec((C_KVLORA, tm), lambda i: (0, i)),
                   _row_spec(tm, C_KVLORA), _row_spec(tm, C_ROPE)],
        compiler_params=_cparams("arbitrary"),
        name="mla_proj",
    )(x, mod, mod, w_in, g_q, g_kv, wn, wr, wuk, ccq, ssq, cck, ssk)


def _softmax_update(s, m_ref, l_ref, acc_ref, values):
    m_prev = m_ref[...]
    m_new = jnp.maximum(m_prev, jnp.max(s, axis=-1, keepdims=True))
    alpha = jnp.exp(m_prev - m_new)
    p = jnp.exp(s - m_new)
    l_ref[...] = alpha * l_ref[...] + jnp.sum(p, axis=-1, keepdims=True)
    acc_ref[...] = alpha * acc_ref[...] + _dot(p.astype(BF16), values)
    m_ref[...] = m_new


def _softmax_init(m_ref, l_ref, acc_ref):
    m_ref[...] = jnp.full_like(m_ref, -jnp.inf)
    l_ref[...] = jnp.zeros_like(l_ref)
    acc_ref[...] = jnp.zeros_like(acc_ref)


def _attn_body(qi_ref, kj_ref, last_ref, q_ref, k_ref, kt_ref, o_ref, m_ref, l_ref, acc_ref, *, tq, tk, scale2):
    p_id = pl.program_id(1)
    qi = qi_ref[p_id]
    kj = kj_ref[p_id]

    @pl.when(kj == 0)
    def _():
        _softmax_init(m_ref, l_ref, acc_ref)

    def step(masked):
        k = k_ref[...]
        kt = kt_ref[...]
        if masked:
            key = lax.broadcasted_iota(jnp.int32, (tk, tq), 0) + kj * tk
            tok = lax.broadcasted_iota(jnp.int32, (tk, tq), 1) + qi * tq
            keep = key <= tok
        def scores(hd):
            t = _dot_nt(k, q_ref[hd]) * scale2
            return jnp.where(keep, t, -jnp.inf) if masked else t

        ahead = 2
        queue = [scores(hd) for hd in range(ahead)]
        for hd in range(C_HEADS):
            t = queue.pop(0)
            if hd + ahead < C_HEADS:
                queue.append(scores(hd + ahead))
            m_prev = m_ref[hd]
            m_new = jnp.maximum(m_prev, jnp.max(t, axis=0, keepdims=True))
            alpha = jnp.exp2(m_prev - m_new)
            p = jnp.exp2(t - m_new)
            l_ref[hd] = alpha * l_ref[hd] + jnp.sum(p, axis=0, keepdims=True)
            acc_ref[hd] = alpha * acc_ref[hd] + _dot(kt, p.astype(BF16))
            m_ref[hd] = m_new

    fully_visible = (kj + 1) * tk - 1 <= qi * tq
    pl.when(fully_visible)(lambda: step(False))
    pl.when(jnp.logical_not(fully_visible))(lambda: step(True))

    @pl.when(last_ref[p_id] == 1)
    def _():
        for hd in range(C_HEADS):
            o_ref[hd] = (acc_ref[hd] / l_ref[hd]).astype(BF16)


def _attn_prompt(q, kcat, klat_t, n_seq, seq_len, tq, tk):
    nq, nk = seq_len // tq, seq_len // tk
    pairs = [(i, j) for i in range(nq) for j in range((i * tq + tq - 1) // tk + 1)]
    qi = jnp.asarray([p[0] for p in pairs], jnp.int32)
    kj = jnp.asarray([p[1] for p in pairs], jnp.int32)
    last = jnp.asarray([int(n + 1 == len(pairs) or pairs[n + 1][0] != p[0]) for n, p in enumerate(pairs)], jnp.int32)
    scale2 = (C_NOPE + C_ROPE) ** -0.5 * math.log2(math.e)
    grid_spec = pltpu.PrefetchScalarGridSpec(
        num_scalar_prefetch=3,
        grid=(n_seq, len(pairs)),
        in_specs=[pl.BlockSpec((None, C_HEADS, tq, C_QK), lambda b, p, qi, kj, last: (b * nq + qi[p], 0, 0, 0)),
                  pl.BlockSpec((tk, C_QK), lambda b, p, qi, kj, last: (b * nk + kj[p], 0)),
                  pl.BlockSpec((C_KVLORA, tk), lambda b, p, qi, kj, last: (0, b * nk + kj[p]))],
        out_specs=pl.BlockSpec((None, C_HEADS, C_KVLORA, tq), lambda b, p, qi, kj, last: (b * nq + qi[p], 0, 0, 0)),
        scratch_shapes=[pltpu.VMEM((C_HEADS, 1, tq), F32), pltpu.VMEM((C_HEADS, 1, tq), F32),
                        pltpu.VMEM((C_HEADS, C_KVLORA, tq), F32)])
    return pl.pallas_call(
        functools.partial(_attn_body, tq=tq, tk=tk, scale2=scale2),
        out_shape=jax.ShapeDtypeStruct((n_seq * nq, C_HEADS, C_KVLORA, tq), BF16),
        grid_spec=grid_spec,
        compiler_params=_cparams("arbitrary", "arbitrary"),
        name="mla_attention_prompt",
    )(qi, kj, last, q, kcat, klat_t)


def _attn_paged_body(pt_ref, q_ref, nlat_ref, nrope_ref, lat_hbm, rope_hbm, o_ref,
                     lat_buf, rope_buf, sem, m_ref, l_ref, acc_ref, *, nsq, pages, sub, j, seq_new, scale2):
    b = pl.program_id(0)
    g = pl.program_id(1)
    n_groups = pl.num_programs(1)
    n_steps = pl.num_programs(0) * n_groups
    step = b * n_groups + g
    slot = lax.rem(step, PAGED_SLOTS)
    lookahead = PAGED_SLOTS - 1

    def page_copy(kind, page_id, sl, i):
        src, dst = (lat_hbm, lat_buf) if kind == 0 else (rope_hbm, rope_buf)
        return pltpu.make_async_copy(src.at[j, page_id], dst.at[sl, i], sem.at[kind, sl])

    def start_group(bb, gg, sl):
        for i in range(nsq * pages):
            page_id = pt_ref[bb * nsq + i // pages, gg * pages + i % pages]
            page_copy(0, page_id, sl, i).start(priority=i % 2)
            page_copy(1, page_id, sl, i).start(priority=(i + 1) % 2)

    def start_step(s, sl):
        s = jnp.minimum(s, n_steps - 1)
        start_group(s // n_groups, lax.rem(s, n_groups), sl)

    @pl.when(step == 0)
    def _():
        for s in range(lookahead):
            start_step(s, s)

    def wait_group(sl):
        for i in range(nsq * pages):
            page_copy(0, 0, sl, i).wait()
            page_copy(1, 0, sl, i).wait()

    wait_group(slot)

    def update(carry, t, values):
        m_prev, l_prev, acc = carry
        m_new = jnp.maximum(m_prev, jnp.max(t, axis=-1, keepdims=True))
        alpha = jnp.exp2(m_prev - m_new)
        p = jnp.exp2(t - m_new)
        return (m_new, alpha * l_prev + jnp.sum(p, axis=-1, keepdims=True),
                alpha * acc + _dot(p.astype(BF16), values))

    first = g == 0
    page = lat_buf.shape[2]
    qls = [q_ref[s][:, :C_KVLORA] for s in range(nsq)]
    qrs = [q_ref[s][:, C_KVLORA:C_KVLORA + C_ROPE] for s in range(nsq)]
    work = []
    for s in range(nsq):
        for u in range(pages // sub):
            e0 = s * pages + u * sub
            lat = lat_buf[slot, e0:e0 + sub].reshape(sub * page, C_KVLORA).astype(BF16)
            rp_t = jnp.concatenate([rope_buf[slot, e0 + i] for i in range(sub)], axis=1).astype(BF16)
            work.append((s, lat, (_dot_nt(qls[s], lat) + _dot(qrs[s], rp_t)) * scale2))
    start_step(step + lookahead, lax.rem(step + lookahead, PAGED_SLOTS))
    carries = [(jnp.where(first, -jnp.inf, m_ref[s]), jnp.where(first, 0.0, l_ref[s]),
                jnp.where(first, 0.0, acc_ref[s])) for s in range(nsq)]
    for s, lat, t in work:
        carries[s] = update(carries[s], t, lat)

    @pl.when(step == n_steps - 1)
    def _():
        for ahead in range(1, PAGED_SLOTS):
            wait_group(lax.rem(step + ahead, PAGED_SLOTS))

    @pl.when(g < n_groups - 1)
    def _():
        for s in range(nsq):
            m_ref[s], l_ref[s], acc_ref[s] = carries[s]

    @pl.when(g == n_groups - 1)
    def _():
        for s in range(nsq):
            nlat = nlat_ref[s].astype(BF16)
            t2 = (_dot_nt(qls[s], nlat) + _dot_nt(qrs[s], nrope_ref[s].astype(BF16))) * scale2
            tok = lax.broadcasted_iota(jnp.int32, t2.shape, 0) & (seq_new - 1)
            key = lax.broadcasted_iota(jnp.int32, t2.shape, 1)
            _, l_fin, acc = update(carries[s], jnp.where(key <= tok, t2, -jnp.inf), nlat)
            o_ref[s] = (acc / l_fin).astype(BF16)


def _attn_paged(q, new_lat, new_rope, pool_lat, pool_rope_t, page_table, j, seq_new):
    n_seq, n_pages = page_table.shape
    page = pool_lat.shape[2]
    nsq, pages, sub = 2, 16, 4
    rows = q.shape[1]
    scale2 = (C_NOPE + C_ROPE) ** -0.5 * math.log2(math.e)
    grid_spec = pltpu.PrefetchScalarGridSpec(
        num_scalar_prefetch=1,
        grid=(n_seq // nsq, n_pages // pages),
        in_specs=[pl.BlockSpec((nsq, rows, C_QK), lambda b, s, pt: (b, 0, 0)),
                  pl.BlockSpec((nsq,) + new_lat.shape[1:], lambda b, s, pt: (b, 0, 0)),
                  pl.BlockSpec((nsq,) + new_rope.shape[1:], lambda b, s, pt: (b, 0, 0)),
                  pl.BlockSpec(memory_space=pl.ANY), pl.BlockSpec(memory_space=pl.ANY)],
        out_specs=pl.BlockSpec((nsq, rows, C_KVLORA), lambda b, s, pt: (b, 0, 0)),
        scratch_shapes=[pltpu.VMEM((PAGED_SLOTS, nsq * pages, page, C_KVLORA), F32),
                        pltpu.VMEM((PAGED_SLOTS, nsq * pages, C_ROPE, page), F32),
                        pltpu.SemaphoreType.DMA((2, PAGED_SLOTS)),
                        pltpu.VMEM((nsq, rows, 1), F32), pltpu.VMEM((nsq, rows, 1), F32),
                        pltpu.VMEM((nsq, rows, C_KVLORA), F32)])
    return pl.pallas_call(
        functools.partial(_attn_paged_body, nsq=nsq, pages=pages, sub=sub, j=j, seq_new=seq_new, scale2=scale2),
        out_shape=jax.ShapeDtypeStruct((n_seq, rows, C_KVLORA), BF16),
        grid_spec=grid_spec,
        compiler_params=_cparams("arbitrary", "arbitrary"),
        name="mla_attention_paged",
    )(page_table, q, new_lat, new_rope, pool_lat, pool_rope_t)


def _mla_out_body(x_ref, g_ref, o_ref, wuv_ref, wout_ref, lg_ref, lb_ref, y_ref):
    parts = [_dot_tn(o_ref[hd], wuv_ref[hd]) for hd in range(C_HEADS)]
    o = jnp.concatenate(parts, axis=1).astype(BF16)
    y_ref[...] = _residual_ln(x_ref[...], g_ref, _dot(o, wout_ref[...]), lg_ref, lb_ref)


def _mla_out(x, mod, tps, layer, j, o_lat, wuv, w_out, ln_g, ln_b, tm):
    T = x.shape[0]
    return pl.pallas_call(
        _mla_out_body,
        out_shape=jax.ShapeDtypeStruct((T, D_MODEL), F32),
        grid=(T // tm,),
        in_specs=[_row_spec(tm, D_MODEL), _mod_spec(mod, layer, 2, tps),
                  pl.BlockSpec((None, C_HEADS, C_KVLORA, tm), lambda i: (i, 0, 0, 0)),
                  _full_spec(wuv, j), _full_spec(w_out, j),
                  _vec_spec(layer, D_MODEL), _vec_spec(layer, D_MODEL)],
        out_specs=_row_spec(tm, D_MODEL),
        compiler_params=_cparams("arbitrary"),
        name="mla_out",
    )(x, mod, o_lat, wuv, w_out, ln_g, ln_b)


def _rope_tables(pos, reps, width):
    half = C_ROPE // 2
    inv = ROPE_THETA ** (-jnp.arange(half, dtype=F32) / half)
    ang = pos.astype(F32)[:, None] * inv
    cos, sin = jnp.cos(ang), jnp.sin(ang)
    cc = jnp.tile(jnp.concatenate([cos, cos], axis=1), (1, reps))
    ss = jnp.tile(jnp.concatenate([-sin, sin], axis=1), (1, reps))
    pad = width - cc.shape[1]
    return jnp.pad(cc, ((0, 0), (0, pad))), jnp.pad(ss, ((0, 0), (0, pad)))


def _prepare_params(p):
    vec = lambda a: a.reshape(a.shape[0], 1, a.shape[1])
    w_uq = p['c_w_uq']
    n_c = w_uq.shape[0]
    c_w_in = jnp.pad(p['c_w_in'], ((0, 0), (0, 0), (0, LANES - C_ROPE)))
    lb_all = jnp.cumsum(jax.nn.softmax(p['b_lb'].astype(F32), axis=0), axis=0)
    lb_all = lb_all - lb_all[:1]
    return dict(
        ln1_g=vec(p['ln1_g']), ln1_b=vec(p['ln1_b']), ln2_g=vec(p['ln2_g']), ln2_b=vec(p['ln2_b']),
        ffn_w1=p['ffn_w1'].astype(BF16), ffn_w2=p['ffn_w2'].astype(BF16),
        a_w_in=p['a_w_in'].astype(BF16), a_ln_g=vec(p['a_ln_g']), a_ln_b=vec(p['a_ln_b']),
        a_w_out=p['a_w_out'].astype(BF16),
        b_w_in=p['b_w_in'].astype(BF16), b_w_out=p['b_w_out'].astype(BF16), lb_all=lb_all,
        c_w_in=c_w_in.astype(BF16), c_g_q=vec(p['c_g_q']), c_g_kv=vec(p['c_g_kv']),
        c_wn=w_uq[..., :C_NOPE].reshape(n_c, C_QLORA, C_HEADS * C_NOPE).astype(BF16),
        c_wr=w_uq[..., C_NOPE:].reshape(n_c, C_QLORA, C_HEADS * C_ROPE).astype(BF16),
        c_wuk=jnp.transpose(p['c_w_uk'], (0, 2, 3, 1)).astype(BF16),
        c_wuv=jnp.transpose(p['c_w_uv'], (0, 2, 1, 3)).astype(BF16),
        c_w_out=p['c_w_out'].astype(BF16),
    )


def _sgu_mixing(w_s, b_s, chunk):
    reps = CHUNK_A // chunk
    causal = jnp.tril(jnp.ones((chunk, chunk), dtype=bool))
    ws = jnp.where(causal, w_s[:, :, :chunk, :chunk], 0)
    eye = jnp.eye(reps, dtype=w_s.dtype)
    ws = jnp.einsum('ab,jgts->jgatbs', eye, ws).reshape(w_s.shape[0], A_GROUPS, CHUNK_A, CHUNK_A)
    bias = jnp.tile(jnp.transpose(b_s[:, :, :chunk], (0, 2, 1)), (1, reps, 1))
    bias = jnp.repeat(bias, A_GDIM, axis=2)
    return ws.astype(BF16), bias


def _run_trunk(x, mod, n_seq, seq_len, q_pos, hgrn_state0, mla_cache, prm, raw, tm):
    T = x.shape[0]
    per_seq_mod = mod.ndim == 5
    tps = (seq_len // tm) if per_seq_mod else 1
    sgu_chunk = min(CHUNK_A, seq_len)
    ws, bias = _sgu_mixing(raw['a_w_s'], raw['a_b_s'], sgu_chunk)
    chunk_v, hgrn_states, lat_rows, rope_rows = [], [], [], []
    for i in range(DEPTH):
        kind, j = i % N_MIXERS, i // N_MIXERS
        if kind == 0:
            x, v_rows = _sgu_layer(x, mod, tps, i, j, prm['a_w_in'], prm['a_ln_g'], prm['a_ln_b'], ws, bias,
                                   prm['a_w_out'], prm['ln1_g'], prm['ln1_b'], tm, emit_v=mla_cache is not None)
            chunk_v.append(v_rows)
        elif kind == 1:
            lb = prm['lb_all'][i].reshape(1, D_MODEL)
            q, k, lf, v, gs = _hgrn_proj(x, mod, tps, i, j, prm['b_w_in'], lb, tm)
            if seq_len % 64 == 0:
                C, lpad, hb = 64, seq_len, 4
                rec_tm = min(seq_len, 512)
                rec_in = (q, k, lf, v)
            else:
                C = lpad = rec_tm = SUBLANES
                hb = B_HEADS
                padseq = lambda a: jnp.pad(a.reshape(n_seq, seq_len, D_MODEL),
                                           ((0, 0), (0, lpad - seq_len), (0, 0))).reshape(n_seq * lpad, D_MODEL)
                rec_in = tuple(padseq(a) for a in (q, k, lf, v))
            s0 = None if hgrn_state0 is None else hgrn_state0[j]
            o, S = _hgrn_rec(*rec_in, s0, n_seq, lpad, rec_tm, C, hb)
            if lpad != seq_len:
                o = o.reshape(n_seq, lpad, D_MODEL)[:, :seq_len].reshape(T, D_MODEL)
            hgrn_states.append(S)
            x = _hgrn_out(x, mod, tps, i, j, o, gs, prm['b_w_out'], prm['ln1_g'], prm['ln1_b'], tm)
        else:
            tq = min(tm, 512)
            tps_q = (seq_len // tq) if per_seq_mod else 1
            pos_rows = q_pos if tq <= seq_len else jnp.tile(q_pos, tq // seq_len)
            ccq, ssq = _rope_tables(pos_rows, C_HEADS, C_HEADS * C_ROPE)
            cck, ssk = _rope_tables(pos_rows, 1, LANES)
            qcat, kcat, klat_t, lat, kr = _mla_proj(x, mod, tps_q, pos_rows.shape[0] // tq, i, j, prm['c_w_in'],
                                            prm['c_g_q'], prm['c_g_kv'], prm['c_wn'], prm['c_wr'], prm['c_wuk'],
                                            ccq, ssq, cck, ssk, tq)
            if mla_cache is None:
                o_lat = _attn_prompt(qcat, kcat, klat_t, n_seq, seq_len, tq, min(seq_len, 512))
            else:
                pool_lat, pool_rope_t, pt = mla_cache
                qs = qcat.reshape(C_HEADS, n_seq, seq_len, C_QK).transpose(1, 0, 2, 3)
                qs = qs.reshape(n_seq, C_HEADS * seq_len, C_QK)
                padk = lambda a: jnp.pad(a.reshape(n_seq, seq_len, a.shape[1]), ((0, 0), (0, 16 - seq_len), (0, 0)))
                o_s = _attn_paged(qs, padk(lat), padk(kr), pool_lat, pool_rope_t, pt, j, seq_len)
                o_lat = o_s.reshape(n_seq, C_HEADS, seq_len, C_KVLORA).transpose(1, 3, 0, 2)
                o_lat = o_lat.reshape(1, C_HEADS, C_KVLORA, T)
            x = _mla_out(x, mod, tps_q, i, j, o_lat, prm['c_wuv'], prm['c_w_out'], prm['ln1_g'], prm['ln1_b'], tq)
            lat_rows.append(lat.reshape(n_seq, seq_len, C_KVLORA))
            rope_rows.append(kr.reshape(n_seq, seq_len, C_ROPE))
        x = _ffn_layer(x, mod, tps, i, prm['ffn_w1'], prm['ffn_w2'], prm['ln2_g'], prm['ln2_b'], tm)
    stack = lambda xs: jnp.stack(xs) if xs and xs[0] is not None else None
    return x, stack(chunk_v), jnp.stack(hgrn_states), jnp.stack(lat_rows), jnp.stack(rope_rows)


def kernel(x_prompt, x_sample, cache_kv_latent, cache_k_rope, state_hgrn, page_table, c_prompt, c_sample,
           w_ada, b_ada, ln1_g, ln1_b, ln2_g, ln2_b, ffn_w1, ffn_w2, a_w_in, a_ln_g, a_ln_b, a_w_s, a_b_s,
           a_w_out, b_w_in, b_lb, b_w_out, c_w_in, c_g_q, c_g_kv, c_w_uq, c_w_uk, c_w_uv, c_w_out):
    raw = dict(ln1_g=ln1_g, ln1_b=ln1_b, ln2_g=ln2_g, ln2_b=ln2_b, ffn_w1=ffn_w1, ffn_w2=ffn_w2,
               a_w_in=a_w_in, a_ln_g=a_ln_g, a_ln_b=a_ln_b, a_w_s=a_w_s, a_b_s=a_b_s, a_w_out=a_w_out,
               b_w_in=b_w_in, b_lb=b_lb, b_w_out=b_w_out, c_w_in=c_w_in, c_g_q=c_g_q, c_g_kv=c_g_kv,
               c_w_uq=c_w_uq, c_w_uk=c_w_uk, c_w_uv=c_w_uv, c_w_out=c_w_out)
    prm = _prepare_params(raw)
    nb, seq, d = x_prompt.shape
    ns, sseq, _ = x_sample.shape
    past_len = page_table.shape[1] * cache_kv_latent.shape[2]
    pos_prompt = jnp.arange(seq, dtype=jnp.int32)
    pos_sample = past_len + jnp.arange(sseq, dtype=jnp.int32)

    mod_p, mod_s = _modulation(c_prompt, jnp.repeat(c_sample, sseq, axis=0), w_ada, b_ada)
    mod_p = mod_p.reshape(DEPTH, nb, 6, 1, d)

    tm_p = 512
    y_p, _, hs_p, lat_p, rope_p = _run_trunk(x_prompt.reshape(nb * seq, d), mod_p, nb, seq, pos_prompt,
                                             None, None, prm, raw, tm_p)
    y_s, v_s, hs_s, lat_s, rope_s = _run_trunk(x_sample.reshape(ns * sseq, d), mod_s, ns, sseq, pos_sample,
                                               state_hgrn,
                                               (cache_kv_latent, jnp.swapaxes(cache_k_rope, 2, 3), page_table),
                                               prm, raw, ns * sseq)
    return (y_p.reshape(nb, seq, d), y_s.reshape(ns, sseq, d), hs_p, hs_s, lat_p, rope_p, lat_s, rope_s,
            v_s.reshape(v_s.shape[0], ns, sseq, d))
```

```python
import functools
import math

import jax
import jax.numpy as jnp
from jax import lax
from jax.experimental import pallas as pl
from jax.experimental.pallas import tpu as pltpu

F32 = jnp.float32
BF16 = jnp.bfloat16

D_MODEL = 1024
DEPTH = 4
N_MIXERS = 3
CHUNK_A = 128
A_GROUPS = 8
A_GDIM = D_MODEL // A_GROUPS
B_HEADS = 8
B_DK = 128
B_DV = D_MODEL // B_HEADS
C_HEADS = 8
C_NOPE = 128
C_ROPE = 64
C_V = 128
C_QLORA = 512
C_KVLORA = 256
ROPE_THETA = 10000.0
D_FF = 4 * D_MODEL
ALPHA = (2.0 * DEPTH) ** 0.25
EPS = 1e-6

LANES = 128
SUBLANES = 8
C_QK = C_KVLORA + LANES
VMEM_LIMIT = 56 * 1024 * 1024
PAGED_SLOTS = 3


def _cparams(*sem):
    return pltpu.CompilerParams(dimension_semantics=sem, vmem_limit_bytes=VMEM_LIMIT)


def _dot(a, b):
    return jnp.dot(a, b, preferred_element_type=F32)


def _dot_nt(a, b):
    return lax.dot_general(a, b, (((1,), (1,)), ((), ())), preferred_element_type=F32)


def _dot_tn(a, b):
    return lax.dot_general(a, b, (((0,), (0,)), ((), ())), preferred_element_type=F32)


def _layer_norm(y, g, b):
    mu = jnp.mean(y, axis=-1, keepdims=True)
    yc = y - mu
    var = jnp.mean(yc * yc, axis=-1, keepdims=True)
    return yc * lax.rsqrt(var + EPS) * g + b


def _rms(y):
    return y * lax.rsqrt(jnp.mean(y * y, axis=-1, keepdims=True) + EPS)


def _silu(x):
    return x * jax.nn.sigmoid(x)


def _gelu_tanh(x):
    return 0.5 * x * (1.0 + jnp.tanh(math.sqrt(2.0 / math.pi) * (x + 0.044715 * (x * x * x))))


def _modulate(x, sh_ref, sc_ref):
    return x * (1.0 + sc_ref[...]) + sh_ref[...]


def _residual_ln(x, gate_ref, out, lg_ref, lb_ref):
    return _layer_norm(ALPHA * x + gate_ref[...] * out, lg_ref[...], lb_ref[...])


def _mod_spec(mod, layer, which, tiles_per_seq):
    if mod.ndim == 5:
        return pl.BlockSpec((None, None, None, 1, D_MODEL), lambda i: (layer, i // tiles_per_seq, which, 0, 0))
    return pl.BlockSpec((None, mod.shape[1], D_MODEL), lambda i: (layer, 0, which))


def _vec_spec(layer, width):
    return pl.BlockSpec((None, 1, width), lambda i: (layer, 0, 0))


def _full_spec(arr, layer=None):
    if layer is None:
        nd = arr.ndim
        return pl.BlockSpec(arr.shape, lambda i: (0,) * nd)
    nd = arr.ndim - 1
    return pl.BlockSpec((None,) + arr.shape[1:], lambda i: (layer,) + (0,) * nd)


def _row_spec(tm, width):
    return pl.BlockSpec((tm, width), lambda i: (i, 0))


def _mod_body(ca_ref, cb_ref, w_ref, b_ref, oa_ref, ob_ref):
    w = w_ref[...].astype(BF16)
    oa_ref[...] = _dot(_silu(ca_ref[...]).astype(BF16), w) + b_ref[...]
    ob_ref[...] = _dot(_silu(cb_ref[...]).astype(BF16), w) + b_ref[...]


def _modulation(c_a, c_b, w_ada, b_ada):
    tn = 1536
    width = w_ada.shape[2]
    rows = lambda c: pl.BlockSpec((c.shape[0], D_MODEL), lambda l, j: (0, 0))
    out = lambda c: pl.BlockSpec((None, c.shape[0], tn), lambda l, j: (l, 0, j))
    return pl.pallas_call(
        _mod_body,
        out_shape=[jax.ShapeDtypeStruct((DEPTH, c.shape[0], width), F32) for c in (c_a, c_b)],
        grid=(DEPTH, width // tn),
        in_specs=[rows(c_a), rows(c_b),
                  pl.BlockSpec((None, D_MODEL, tn), lambda l, j: (l, 0, j)),
                  pl.BlockSpec((None, 1, tn), lambda l, j: (l, 0, j))],
        out_specs=[out(c_a), out(c_b)],
        compiler_params=_cparams("arbitrary", "arbitrary"),
        name="adaln_modulation",
    )(c_a, c_b, w_ada, b_ada.reshape(DEPTH, 1, width))


def _ffn_body(x_ref, sh_ref, sc_ref, g_ref, w1_ref, w2_ref, lg_ref, lb_ref, o_ref, acc_ref, *, fc):
    x = x_ref[...]
    h = _modulate(x, sh_ref, sc_ref).astype(BF16)
    for c in range(D_FF // fc):
        a = _dot(h, w1_ref[:, c * fc:(c + 1) * fc])
        a = jnp.square(jnp.maximum(a, 0.0)).astype(BF16)
        d = _dot(a, w2_ref[c * fc:(c + 1) * fc, :])
        if c == 0:
            acc_ref[...] = d
        else:
            acc_ref[...] += d
    o_ref[...] = _residual_ln(x, g_ref, acc_ref[...], lg_ref, lb_ref)


def _ffn_layer(x, mod, tps, layer, w1, w2, ln_g, ln_b, tm):
    T = x.shape[0]
    return pl.pallas_call(
        functools.partial(_ffn_body, fc=1024),
        out_shape=jax.ShapeDtypeStruct((T, D_MODEL), F32),
        grid=(T // tm,),
        in_specs=[_row_spec(tm, D_MODEL),
                  _mod_spec(mod, layer, 3, tps), _mod_spec(mod, layer, 4, tps), _mod_spec(mod, layer, 5, tps),
                  _full_spec(w1, layer), _full_spec(w2, layer),
                  _vec_spec(layer, D_MODEL), _vec_spec(layer, D_MODEL)],
        out_specs=_row_spec(tm, D_MODEL),
        scratch_shapes=[pltpu.VMEM((tm, D_MODEL), F32)],
        compiler_params=_cparams("arbitrary"),
        name="ffn_sublayer",
    )(x, mod, mod, mod, w1, w2, ln_g, ln_b)


def _sgu_body(x_ref, sh_ref, sc_ref, g_ref, win_ref, lng_ref, lnb_ref, ws_ref, bias_ref, wout_ref,
              lg_ref, lb_ref, o_ref, *rest, tm, pieces, emit_v):
    if emit_v:
        v_ref, gated_ref = rest
    else:
        (gated_ref,) = rest
    rows = [slice(p * tm // pieces, (p + 1) * tm // pieces) for p in range(pieces)]
    per_row = lambda ref, r: ref[...] if ref.shape[0] == 1 else ref[r, :]

    def in_products(r):
        h = (x_ref[r, :] * (1.0 + per_row(sc_ref, r)) + per_row(sh_ref, r)).astype(BF16)
        return _dot(h, win_ref[:, :D_MODEL]), _dot(h, win_ref[:, D_MODEL:])

    def gate(r, zu, zv):
        u = _gelu_tanh(zu)
        v = _layer_norm(_gelu_tanh(zv), lng_ref[...], lnb_ref[...])
        if emit_v:
            v_ref[r, :] = v
        vb = v.astype(BF16)
        nch = (r.stop - r.start) // CHUNK_A
        for g in range(A_GROUPS):
            c = slice(g * A_GDIM, (g + 1) * A_GDIM)
            rhs = jnp.concatenate([vb[n * CHUNK_A:(n + 1) * CHUNK_A, c] for n in range(nch)], axis=1)
            mixed = _dot(ws_ref[g], rhs)
            for n in range(nch):
                rn = slice(n * CHUNK_A, (n + 1) * CHUNK_A)
                gated = u[rn, c] * (mixed[:, n * A_GDIM:(n + 1) * A_GDIM] + bias_ref[:, c])
                gated_ref[r.start + n * CHUNK_A:r.start + (n + 1) * CHUNK_A, c] = gated.astype(BF16)
        return _dot(gated_ref[r, :], wout_ref[...])

    def finish(r, out):
        y = ALPHA * x_ref[r, :] + per_row(g_ref, r) * out
        o_ref[r, :] = _layer_norm(y, lg_ref[...], lb_ref[...])

    z = in_products(rows[0])
    outs = []
    for p, r in enumerate(rows):
        z_next = in_products(rows[p + 1]) if p + 1 < pieces else None
        outs.append(gate(r, *z))
        z = z_next
        if p > 0:
            finish(rows[p - 1], outs[p - 1])
    finish(rows[-1], outs[-1])


def _sgu_layer(x, mod, tps, layer, j, w_in, ln_g, ln_b, ws, bias, w_out, ln1_g, ln1_b, tm, emit_v):
    T = x.shape[0]
    out_shape = [jax.ShapeDtypeStruct((T, D_MODEL), F32)]
    out_specs = [_row_spec(tm, D_MODEL)]
    if emit_v:
        out_shape.append(jax.ShapeDtypeStruct((T, D_MODEL), F32))
        out_specs.append(_row_spec(tm, D_MODEL))
    res = pl.pallas_call(
        functools.partial(_sgu_body, tm=tm, pieces=2 if tm >= 4 * CHUNK_A else 1, emit_v=emit_v),
        out_shape=out_shape,
        grid=(T // tm,),
        in_specs=[_row_spec(tm, D_MODEL),
                  _mod_spec(mod, layer, 0, tps), _mod_spec(mod, layer, 1, tps), _mod_spec(mod, layer, 2, tps),
                  _full_spec(w_in, j), _vec_spec(j, D_MODEL), _vec_spec(j, D_MODEL),
                  _full_spec(ws, j), _full_spec(bias, j), _full_spec(w_out, j),
                  _vec_spec(layer, D_MODEL), _vec_spec(layer, D_MODEL)],
        out_specs=out_specs,
        scratch_shapes=[pltpu.VMEM((tm, D_MODEL), BF16)],
        compiler_params=_cparams("arbitrary"),
        name="sgu_sublayer",
    )(x, mod, mod, mod, w_in, ln_g, ln_b, ws, bias, w_out, ln1_g, ln1_b)
    return (res[0], res[1]) if emit_v else (res[0], None)


def _hgrn_proj_body(x_ref, sh_ref, sc_ref, win_ref, lb_ref, q_ref, k_ref, lf_ref, v_ref, gs_ref):
    d = D_MODEL
    tm = x_ref.shape[0]
    pieces = 4 if tm % (4 * LANES) == 0 and sh_ref.shape[0] == 1 else 1
    rows = [slice(p * tm // pieces, (p + 1) * tm // pieces) for p in range(pieces)]

    def products(r):
        h = _modulate(x_ref[r, :], sh_ref, sc_ref).astype(BF16)
        return [_dot(h, win_ref[:, part * d:(part + 1) * d]) for part in (1, 0, 2, 3)]

    def gate(r, fz, zq, zv, zg):
        lb = lb_ref[...]
        e = jnp.exp(-jnp.abs(fz))
        a = jnp.log(lb)
        b = jnp.log1p(-lb) + (jnp.minimum(fz, 0.0) - jnp.log1p(e))
        lf_ref[r, :] = (jnp.maximum(a, b) + jnp.log1p(jnp.exp(-jnp.abs(a - b)))) * math.log2(math.e)
        k_ref[r, :] = (1.0 - lb) * (jnp.where(fz >= 0.0, e, 1.0) / (1.0 + e))
        q_ref[r, :] = _silu(zq)
        v_ref[r, :] = zv
        gs_ref[r, :] = _silu(zg)

    z = [products(r) for r in rows]
    for r, zr in zip(rows, z):
        gate(r, *zr)


def _hgrn_proj(x, mod, tps, layer, j, w_in, lb, tm):
    T = x.shape[0]
    shp = jax.ShapeDtypeStruct((T, D_MODEL), F32)
    return pl.pallas_call(
        _hgrn_proj_body,
        out_shape=[shp] * 5,
        grid=(T // tm,),
        in_specs=[_row_spec(tm, D_MODEL), _mod_spec(mod, layer, 0, tps), _mod_spec(mod, layer, 1, tps),
                  _full_spec(w_in, j), _full_spec(lb)],
        out_specs=[_row_spec(tm, D_MODEL)] * 5,
        compiler_params=_cparams("arbitrary"),
        name="hgrn_proj",
    )(x, mod, mod, w_in, lb)


def _hgrn_rec_body(*refs, C, nchunk, hb, has_s0):
    if has_s0:
        q_ref, k_ref, g_ref, v_ref, s0_ref, o_ref, sout_ref, st_ref = refs
    else:
        q_ref, k_ref, g_ref, v_ref, o_ref, sout_ref, st_ref = refs
    t = pl.program_id(2)

    @pl.when(t == 0)
    def _():
        for hd in range(hb):
            st_ref[hd] = s0_ref[hd].T if has_s0 else jnp.zeros((B_DV, B_DK), F32)

    row = lax.broadcasted_iota(jnp.int32, (C, B_DK), 0)
    row_a = lax.broadcasted_iota(jnp.int32, (C, C), 0)
    col_a = lax.broadcasted_iota(jnp.int32, (C, C), 1)
    tri = jnp.where(row_a >= col_a, 1.0, 0.0).astype(BF16)
    sub = row & (SUBLANES - 1)
    levels = []
    for m in (1, 2, 4, 8, 16, 32, 64, 128):
        if 2 * m <= C:
            shift = int(math.log2(2 * m))
            pair = (((row_a >> shift) == (col_a >> shift)) & ((row_a & (2 * m - 1)) >= m)
                    & ((col_a & (2 * m - 1)) < m))
            levels.append((m, (row & (2 * m - 1)) >= m, pair))

    def prefix(c, hd):
        rows = slice(c * C, (c + 1) * C)
        cols = slice(hd * B_DK, (hd + 1) * B_DK)
        gc = g_ref[rows, cols]
        g_hi = gc.astype(BF16)
        r1 = gc - g_hi.astype(F32)
        g_mid = r1.astype(BF16)
        g_lo = (r1 - g_mid.astype(F32)).astype(BF16)
        b3 = _dot(tri, jnp.concatenate([g_hi, g_mid, g_lo], axis=1))
        return rows, cols, b3

    def products(pre, st):
        rows, cols, b3 = pre
        qc = q_ref[rows, cols]
        kc = k_ref[rows, cols]
        b = b3[:, 0:B_DK] + b3[:, B_DK:2 * B_DK] + b3[:, 2 * B_DK:3 * B_DK]
        o_inter = _dot_nt((qc * jnp.exp2(b)).astype(BF16), st.astype(BF16))
        b3d = b.reshape(C // SUBLANES, SUBLANES, B_DK)
        sub_row = lambda i: jnp.broadcast_to(b3d[:, i:i + 1, :], b3d.shape).reshape(C, B_DK)
        level_dots = []
        for m, upper, pair in levels:
            if m == 1:
                bref = jnp.where(upper, pltpu.roll(b3d, 1, 1).reshape(C, B_DK), b)
            elif m == 2:
                bref = jnp.where(sub < 4, sub_row(1), sub_row(5))
            elif m == 4:
                bref = sub_row(3)
            else:
                bref = jnp.concatenate(
                    [jnp.broadcast_to(b[i * 2 * m + m - 1:i * 2 * m + m, :], (2 * m, B_DK))
                     for i in range(C // (2 * m))], axis=0)
            q_up = jnp.where(upper, qc * jnp.exp2(b - bref), 0.0)
            k_lo = jnp.where(upper, 0.0, kc * jnp.exp2(bref - b))
            level_dots.append(_dot_nt(q_up.astype(BF16), k_lo.astype(BF16)))
        diag = jnp.sum(qc * kc, axis=1, keepdims=True)
        b_last = b[C - 1:C, :]
        k_dec = (kc * jnp.exp2(b_last - b)).astype(BF16)
        return rows, cols, o_inter, level_dots, diag, k_dec, jnp.exp2(b_last)

    def finish(prod, st):
        rows, cols, o_inter, level_dots, diag, k_dec, decay = prod
        vb = v_ref[rows, cols].astype(BF16)
        a_mat = jnp.where(row_a == col_a, diag, 0.0)
        for (m, upper, pair), a_m in zip(levels, level_dots):
            a_mat = jnp.where(pair, a_m, a_mat)
        o_ref[rows, cols] = o_inter + _dot(a_mat.astype(BF16), vb)
        return st * decay + _dot_tn(vb, k_dec)

    states = [st_ref[hd] for hd in range(hb)]
    pre = [prefix(0, hd) for hd in range(hb)]
    for c in range(nchunk):
        prods = [products(pre[hd], states[hd]) for hd in range(hb)]
        if c + 1 < nchunk:
            pre = [prefix(c + 1, hd) for hd in range(hb)]
        states = [finish(prods[hd], states[hd]) for hd in range(hb)]
    for hd in range(hb):
        st_ref[hd] = states[hd]

    @pl.when(t == pl.num_programs(2) - 1)
    def _():
        for hd in range(hb):
            sout_ref[hd] = states[hd].T


def _hgrn_rec(q, k, lf, v, s0, n_seq, seq_len, tm, C, hb):
    T = q.shape[0]
    nt = seq_len // tm
    blk = pl.BlockSpec((tm, hb * B_DK), lambda b, h, t: (b * nt + t, h))
    st_spec = pl.BlockSpec((None, hb, B_DK, B_DV), lambda b, h, t: (b, h, 0, 0))
    has_s0 = s0 is not None
    in_specs = [blk] * 4 + ([st_spec] if has_s0 else [])
    args = (q, k, lf, v) + ((s0,) if has_s0 else ())
    return pl.pallas_call(
        functools.partial(_hgrn_rec_body, C=C, nchunk=tm // C, hb=hb, has_s0=has_s0),
        out_shape=[jax.ShapeDtypeStruct((T, D_MODEL), F32),
                   jax.ShapeDtypeStruct((n_seq, B_HEADS, B_DK, B_DV), F32)],
        grid=(n_seq, B_HEADS // hb, nt),
        in_specs=in_specs,
        out_specs=[blk, st_spec],
        scratch_shapes=[pltpu.VMEM((hb, B_DV, B_DK), F32)],
        compiler_params=_cparams("arbitrary", "arbitrary", "arbitrary"),
        name="hgrn_recurrence",
    )(*args)


def _hgrn_out_body(x_ref, g_ref, o_ref, gs_ref, wout_ref, lg_ref, lb_ref, y_ref):
    o = o_ref[...]
    parts = [_rms(o[:, h * B_DV:(h + 1) * B_DV]) for h in range(B_HEADS)]
    y = (jnp.concatenate(parts, axis=1) * gs_ref[...]).astype(BF16)
    y_ref[...] = _residual_ln(x_ref[...], g_ref, _dot(y, wout_ref[...]), lg_ref, lb_ref)


def _hgrn_out(x, mod, tps, layer, j, o, gs, w_out, ln_g, ln_b, tm):
    T = x.shape[0]
    return pl.pallas_call(
        _hgrn_out_body,
        out_shape=jax.ShapeDtypeStruct((T, D_MODEL), F32),
        grid=(T // tm,),
        in_specs=[_row_spec(tm, D_MODEL), _mod_spec(mod, layer, 2, tps),
                  _row_spec(tm, D_MODEL), _row_spec(tm, D_MODEL),
                  _full_spec(w_out, j), _vec_spec(layer, D_MODEL), _vec_spec(layer, D_MODEL)],
        out_specs=_row_spec(tm, D_MODEL),
        compiler_params=_cparams("arbitrary"),
        name="hgrn_out",
    )(x, mod, o, gs, w_out, ln_g, ln_b)


def _rope_lanes(x, cc_ref, ss_ref, period_first_half):
    n = x.shape[1]
    half = C_ROPE // 2
    rot = jnp.where(period_first_half, pltpu.roll(x, n - half, 1), pltpu.roll(x, half, 1))
    return x * cc_ref[...] + rot * ss_ref[...]


def _mla_proj_body(x_ref, sh_ref, sc_ref, win_ref, gq_ref, gkv_ref, wn_ref, wr_ref, wuk_ref,
                   ccq_ref, ssq_ref, cck_ref, ssk_ref, q_ref, kcat_ref, klt_ref, lat_ref, kr_ref):
    tm = x_ref.shape[0]
    pieces = 2 if tm % (4 * LANES) == 0 and sh_ref.shape[0] == 1 else 1
    rows = [slice(p * tm // pieces, (p + 1) * tm // pieces) for p in range(pieces)]

    def down(r):
        h = _modulate(x_ref[r, :], sh_ref, sc_ref).astype(BF16)
        return _dot(h, win_ref[...])

    def keys_and_queries(r, a):
        cq = (_rms(a[:, :C_QLORA]) * gq_ref[...]).astype(BF16)
        ckv = _rms(a[:, C_QLORA:C_QLORA + C_KVLORA]) * gkv_ref[...]
        kr_slab = a[:, C_QLORA + C_KVLORA:]
        lane_k = lax.broadcasted_iota(jnp.int32, kr_slab.shape, 1)
        kr_slab = _rope_lanes(kr_slab, cck_ref.at[r, :], ssk_ref.at[r, :], (lane_k & (C_ROPE - 1)) < C_ROPE // 2)
        lat_ref[r, :] = ckv
        kr_ref[r, :] = kr_slab[:, :C_ROPE]
        kcat_ref[r, :] = jnp.concatenate([ckv, kr_slab], axis=1).astype(BF16)
        klt_ref[:, r] = ckv.T.astype(BF16)
        return _dot(cq, wn_ref[...]), _dot(cq, wr_ref[...])

    def absorb(r, qn, qr):
        qn = qn.astype(BF16)
        lane_q = lax.broadcasted_iota(jnp.int32, qr.shape, 1)
        qr = _rope_lanes(qr, ccq_ref.at[r, :], ssq_ref.at[r, :], (lane_q & (C_ROPE - 1)) < C_ROPE // 2).astype(BF16)
        zeros = jnp.zeros((qr.shape[0], LANES - C_ROPE), BF16)
        for hd in range(C_HEADS):
            ql = _dot(qn[:, hd * C_NOPE:(hd + 1) * C_NOPE], wuk_ref[hd]).astype(BF16)
            q_ref[hd, r, :] = jnp.concatenate([ql, qr[:, hd * C_ROPE:(hd + 1) * C_ROPE], zeros], axis=1)

    downs = [down(r) for r in rows]
    ups = [keys_and_queries(r, a) for r, a in zip(rows, downs)]
    for r, (qn, qr) in zip(rows, ups):
        absorb(r, qn, qr)


def _mla_proj(x, mod, tps, tab_tiles, layer, j, w_in, g_q, g_kv, wn, wr, wuk, ccq, ssq, cck, ssk, tm):
    T = x.shape[0]
    nt = T // tm
    tab = lambda w: pl.BlockSpec((tm, w), lambda i: (i % tab_tiles, 0))
    return pl.pallas_call(
        _mla_proj_body,
        out_shape=[jax.ShapeDtypeStruct((nt, C_HEADS, tm, C_QK), BF16),
                   jax.ShapeDtypeStruct((T, C_QK), BF16),
                   jax.ShapeDtypeStruct((C_KVLORA, T), BF16),
                   jax.ShapeDtypeStruct((T, C_KVLORA), F32),
                   jax.ShapeDtypeStruct((T, C_ROPE), F32)],
        grid=(nt,),
        in_specs=[_row_spec(tm, D_MODEL), _mod_spec(mod, layer, 0, tps), _mod_spec(mod, layer, 1, tps),
                  _full_spec(w_in, j), _vec_spec(j, C_QLORA), _vec_spec(j, C_KVLORA),
                  _full_spec(wn, j), _full_spec(wr, j), _full_spec(wuk, j),
                  tab(C_HEADS * C_ROPE), tab(C_HEADS * C_ROPE), tab(LANES), tab(LANES)],
        out_specs=[pl.BlockSpec((None, C_HEADS, tm, C_QK), lambda i: (i, 0, 0, 0)),
                   _row_spec(tm, C_QK), pl.BlockSpec((C_KVLORA, tm), lambda i: (0, i)),
                   _row_spec(tm, C_KVLORA), _row_spec(tm, C_ROPE)],
        compiler_params=_cparams("arbitrary"),
        name="mla_proj",
    )(x, mod, mod, w_in, g_q, g_kv, wn, wr, wuk, ccq, ssq, cck, ssk)


def _softmax_update(s, m_ref, l_ref, acc_ref, values):
    m_prev = m_ref[...]
    m_new = jnp.maximum(m_prev, jnp.max(s, axis=-1, keepdims=True))
    alpha = jnp.exp(m_prev - m_new)
    p = jnp.exp(s - m_new)
    l_ref[...] = alpha * l_ref[...] + jnp.sum(p, axis=-1, keepdims=True)
    acc_ref[...] = alpha * acc_ref[...] + _dot(p.astype(BF16), values)
    m_ref[...] = m_new


def _softmax_init(m_ref, l_ref, acc_ref):
    m_ref[...] = jnp.full_like(m_ref, -jnp.inf)
    l_ref[...] = jnp.zeros_like(l_ref)
    acc_ref[...] = jnp.zeros_like(acc_ref)


def _attn_body(qi_ref, kj_ref, last_ref, q_ref, k_ref, kt_ref, o_ref, m_ref, l_ref, acc_ref, *, tq, tk, scale2):
    p_id = pl.program_id(1)
    qi = qi_ref[p_id]
    kj = kj_ref[p_id]

    @pl.when(kj == 0)
    def _():
        _softmax_init(m_ref, l_ref, acc_ref)

    def step(masked):
        k = k_ref[...]
        kt = kt_ref[...]
        if masked:
            key = lax.broadcasted_iota(jnp.int32, (tk, tq), 0) + kj * tk
            tok = lax.broadcasted_iota(jnp.int32, (tk, tq), 1) + qi * tq
            keep = key <= tok
        def scores(hd):
            t = _dot_nt(k, q_ref[hd]) * scale2
            return jnp.where(keep, t, -jnp.inf) if masked else t

        ahead = 2
        queue = [scores(hd) for hd in range(ahead)]
        for hd in range(C_HEADS):
            t = queue.pop(0)
            if hd + ahead < C_HEADS:
                queue.append(scores(hd + ahead))
            m_prev = m_ref[hd]
            m_new = jnp.maximum(m_prev, jnp.max(t, axis=0, keepdims=True))
            alpha = jnp.exp2(m_prev - m_new)
            p = jnp.exp2(t - m_new)
            l_ref[hd] = alpha * l_ref[hd] + jnp.sum(p, axis=0, keepdims=True)
            acc_ref[hd] = alpha * acc_ref[hd] + _dot(kt, p.astype(BF16))
            m_ref[hd] = m_new

    fully_visible = (kj + 1) * tk - 1 <= qi * tq
    pl.when(fully_visible)(lambda: step(False))
    pl.when(jnp.logical_not(fully_visible))(lambda: step(True))

    @pl.when(last_ref[p_id] == 1)
    def _():
        for hd in range(C_HEADS):
            o_ref[hd] = (acc_ref[hd] / l_ref[hd]).astype(BF16)


def _attn_prompt(q, kcat, klat_t, n_seq, seq_len, tq, tk):
    nq, nk = seq_len // tq, seq_len // tk
    pairs = [(i, j) for i in range(nq) for j in range((i * tq + tq - 1) // tk + 1)]
    qi = jnp.asarray([p[0] for p in pairs], jnp.int32)
    kj = jnp.asarray([p[1] for p in pairs], jnp.int32)
    last = jnp.asarray([int(n + 1 == len(pairs) or pairs[n + 1][0] != p[0]) for n, p in enumerate(pairs)], jnp.int32)
    scale2 = (C_NOPE + C_ROPE) ** -0.5 * math.log2(math.e)
    grid_spec = pltpu.PrefetchScalarGridSpec(
        num_scalar_prefetch=3,
        grid=(n_seq, len(pairs)),
        in_specs=[pl.BlockSpec((None, C_HEADS, tq, C_QK), lambda b, p, qi, kj, last: (b * nq + qi[p], 0, 0, 0)),
                  pl.BlockSpec((tk, C_QK), lambda b, p, qi, kj, last: (b * nk + kj[p], 0)),
                  pl.BlockSpec((C_KVLORA, tk), lambda b, p, qi, kj, last: (0, b * nk + kj[p]))],
        out_specs=pl.BlockSpec((None, C_HEADS, C_KVLORA, tq), lambda b, p, qi, kj, last: (b * nq + qi[p], 0, 0, 0)),
        scratch_shapes=[pltpu.VMEM((C_HEADS, 1, tq), F32), pltpu.VMEM((C_HEADS, 1, tq), F32),
                        pltpu.VMEM((C_HEADS, C_KVLORA, tq), F32)])
    return pl.pallas_call(
        functools.partial(_attn_body, tq=tq, tk=tk, scale2=scale2),
        out_shape=jax.ShapeDtypeStruct((n_seq * nq, C_HEADS, C_KVLORA, tq), BF16),
        grid_spec=grid_spec,
        compiler_params=_cparams("arbitrary", "arbitrary"),
        name="mla_attention_prompt",
    )(qi, kj, last, q, kcat, klat_t)


def _attn_paged_body(pt_ref, q_ref, nlat_ref, nrope_ref, lat_hbm, rope_hbm, o_ref,
                     lat_buf, rope_buf, sem, m_ref, l_ref, acc_ref, *, nsq, pages, sub, j, seq_new, scale2):
    b = pl.program_id(0)
    g = pl.program_id(1)
    n_groups = pl.num_programs(1)
    n_steps = pl.num_programs(0) * n_groups
    step = b * n_groups + g
    slot = lax.rem(step, PAGED_SLOTS)
    lookahead = PAGED_SLOTS - 1

    def page_copy(kind, page_id, sl, i):
        src, dst = (lat_hbm, lat_buf) if kind == 0 else (rope_hbm, rope_buf)
        return pltpu.make_async_copy(src.at[j, page_id], dst.at[sl, i], sem.at[kind, sl])

    def start_group(bb, gg, sl):
        for i in range(nsq * pages):
            page_id = pt_ref[bb * nsq + i // pages, gg * pages + i % pages]
            page_copy(0, page_id, sl, i).start(priority=i % 2)
            page_copy(1, page_id, sl, i).start(priority=(i + 1) % 2)

    def start_step(s, sl):
        s = jnp.minimum(s, n_steps - 1)
        start_group(s // n_groups, lax.rem(s, n_groups), sl)

    @pl.when(step == 0)
    def _():
        for s in range(lookahead):
            start_step(s, s)

    def wait_group(sl):
        for i in range(nsq * pages):
            page_copy(0, 0, sl, i).wait()
            page_copy(1, 0, sl, i).wait()

    wait_group(slot)

    def update(carry, t, values):
        m_prev, l_prev, acc = carry
        m_new = jnp.maximum(m_prev, jnp.max(t, axis=-1, keepdims=True))
        alpha = jnp.exp2(m_prev - m_new)
        p = jnp.exp2(t - m_new)
        return (m_new, alpha * l_prev + jnp.sum(p, axis=-1, keepdims=True),
                alpha * acc + _dot(p.astype(BF16), values))

    first = g == 0
    page = lat_buf.shape[2]
    qls = [q_ref[s][:, :C_KVLORA] for s in range(nsq)]
    qrs = [q_ref[s][:, C_KVLORA:C_KVLORA + C_ROPE] for s in range(nsq)]
    work = []
    for s in range(nsq):
        for u in range(pages // sub):
            e0 = s * pages + u * sub
            lat = lat_buf[slot, e0:e0 + sub].reshape(sub * page, C_KVLORA).astype(BF16)
            rp_t = jnp.concatenate([rope_buf[slot, e0 + i] for i in range(sub)], axis=1).astype(BF16)
            work.append((s, lat, (_dot_nt(qls[s], lat) + _dot(qrs[s], rp_t)) * scale2))
    start_step(step + lookahead, lax.rem(step + lookahead, PAGED_SLOTS))
    carries = [(jnp.where(first, -jnp.inf, m_ref[s]), jnp.where(first, 0.0, l_ref[s]),
                jnp.where(first, 0.0, acc_ref[s])) for s in range(nsq)]
    for s, lat, t in work:
        carries[s] = update(carries[s], t, lat)

    @pl.when(step == n_steps - 1)
    def _():
        for ahead in range(1, PAGED_SLOTS):
            wait_group(lax.rem(step + ahead, PAGED_SLOTS))

    @pl.when(g < n_groups - 1)
    def _():
        for s in range(nsq):
            m_ref[s], l_ref[s], acc_ref[s] = carries[s]

    @pl.when(g == n_groups - 1)
    def _():
        for s in range(nsq):
            nlat = nlat_ref[s].astype(BF16)
            t2 = (_dot_nt(qls[s], nlat) + _dot_nt(qrs[s], nrope_ref[s].astype(BF16))) * scale2
            tok = lax.broadcasted_iota(jnp.int32, t2.shape, 0) & (seq_new - 1)
            key = lax.broadcasted_iota(jnp.int32, t2.shape, 1)
            _, l_fin, acc = update(carries[s], jnp.where(key <= tok, t2, -jnp.inf), nlat)
            o_ref[s] = (acc / l_fin).astype(BF16)


def _attn_paged(q, new_lat, new_rope, pool_lat, pool_rope_t, page_table, j, seq_new):
    n_seq, n_pages = page_table.shape
    page = pool_lat.shape[2]
    nsq, pages, sub = 4, 16, 4
    rows = q.shape[1]
    scale2 = (C_NOPE + C_ROPE) ** -0.5 * math.log2(math.e)
    grid_spec = pltpu.PrefetchScalarGridSpec(
        num_scalar_prefetch=1,
        grid=(n_seq // nsq, n_pages // pages),
        in_specs=[pl.BlockSpec((nsq, rows, C_QK), lambda b, s, pt: (b, 0, 0)),
                  pl.BlockSpec((nsq,) + new_lat.shape[1:], lambda b, s, pt: (b, 0, 0)),
                  pl.BlockSpec((nsq,) + new_rope.shape[1:], lambda b, s, pt: (b, 0, 0)),
                  pl.BlockSpec(memory_space=pl.ANY), pl.BlockSpec(memory_space=pl.ANY)],
        out_specs=pl.BlockSpec((nsq, rows, C_KVLORA), lambda b, s, pt: (b, 0, 0)),
        scratch_shapes=[pltpu.VMEM((PAGED_SLOTS, nsq * pages, page, C_KVLORA), F32),
                        pltpu.VMEM((PAGED_SLOTS, nsq * pages, C_ROPE, page), F32),
                        pltpu.SemaphoreType.DMA((2, PAGED_SLOTS)),
                        pltpu.VMEM((nsq, rows, 1), F32), pltpu.VMEM((nsq, rows, 1), F32),
                        pltpu.VMEM((nsq, rows, C_KVLORA), F32)])
    return pl.pallas_call(
        functools.partial(_attn_paged_body, nsq=nsq, pages=pages, sub=sub, j=j, seq_new=seq_new, scale2=scale2),
        out_shape=jax.ShapeDtypeStruct((n_seq, rows, C_KVLORA), BF16),
        grid_spec=grid_spec,
        compiler_params=_cparams("arbitrary", "arbitrary"),
        name="mla_attention_paged",
    )(page_table, q, new_lat, new_rope, pool_lat, pool_rope_t)


def _mla_out_body(x_ref, g_ref, o_ref, wuv_ref, wout_ref, lg_ref, lb_ref, y_ref):
    parts = [_dot_tn(o_ref[hd], wuv_ref[hd]) for hd in range(C_HEADS)]
    o = jnp.concatenate(parts, axis=1).astype(BF16)
    y_ref[...] = _residual_ln(x_ref[...], g_ref, _dot(o, wout_ref[...]), lg_ref, lb_ref)


def _mla_out(x, mod, tps, layer, j, o_lat, wuv, w_out, ln_g, ln_b, tm):
    T = x.shape[0]
    return pl.pallas_call(
        _mla_out_body,
        out_shape=jax.ShapeDtypeStruct((T, D_MODEL), F32),
        grid=(T // tm,),
        in_specs=[_row_spec(tm, D_MODEL), _mod_spec(mod, layer, 2, tps),
                  pl.BlockSpec((None, C_HEADS, C_KVLORA, tm), lambda i: (i, 0, 0, 0)),
                  _full_spec(wuv, j), _full_spec(w_out, j),
                  _vec_spec(layer, D_MODEL), _vec_spec(layer, D_MODEL)],
        out_specs=_row_spec(tm, D_MODEL),
        compiler_params=_cparams("arbitrary"),
        name="mla_out",
    )(x, mod, o_lat, wuv, w_out, ln_g, ln_b)


def _rope_tables(pos, reps, width):
    half = C_ROPE // 2
    inv = ROPE_THETA ** (-jnp.arange(half, dtype=F32) / half)
    ang = pos.astype(F32)[:, None] * inv
    cos, sin = jnp.cos(ang), jnp.sin(ang)
    cc = jnp.tile(jnp.concatenate([cos, cos], axis=1), (1, reps))
    ss = jnp.tile(jnp.concatenate([-sin, sin], axis=1), (1, reps))
    pad = width - cc.shape[1]
    return jnp.pad(cc, ((0, 0), (0, pad))), jnp.pad(ss, ((0, 0), (0, pad)))


def _prepare_params(p):
    vec = lambda a: a.reshape(a.shape[0], 1, a.shape[1])
    w_uq = p['c_w_uq']
    n_c = w_uq.shape[0]
    c_w_in = jnp.pad(p['c_w_in'], ((0, 0), (0, 0), (0, LANES - C_ROPE)))
    lb_all = jnp.cumsum(jax.nn.softmax(p['b_lb'].astype(F32), axis=0), axis=0)
    lb_all = lb_all - lb_all[:1]
    return dict(
        ln1_g=vec(p['ln1_g']), ln1_b=vec(p['ln1_b']), ln2_g=vec(p['ln2_g']), ln2_b=vec(p['ln2_b']),
        ffn_w1=p['ffn_w1'].astype(BF16), ffn_w2=p['ffn_w2'].astype(BF16),
        a_w_in=p['a_w_in'].astype(BF16), a_ln_g=vec(p['a_ln_g']), a_ln_b=vec(p['a_ln_b']),
        a_w_out=p['a_w_out'].astype(BF16),
        b_w_in=p['b_w_in'].astype(BF16), b_w_out=p['b_w_out'].astype(BF16), lb_all=lb_all,
        c_w_in=c_w_in.astype(BF16), c_g_q=vec(p['c_g_q']), c_g_kv=vec(p['c_g_kv']),
        c_wn=w_uq[..., :C_NOPE].reshape(n_c, C_QLORA, C_HEADS * C_NOPE).astype(BF16),
        c_wr=w_uq[..., C_NOPE:].reshape(n_c, C_QLORA, C_HEADS * C_ROPE).astype(BF16),
        c_wuk=jnp.transpose(p['c_w_uk'], (0, 2, 3, 1)).astype(BF16),
        c_wuv=jnp.transpose(p['c_w_uv'], (0, 2, 1, 3)).astype(BF16),
        c_w_out=p['c_w_out'].astype(BF16),
    )


def _sgu_mixing(w_s, b_s, chunk):
    reps = CHUNK_A // chunk
    causal = jnp.tril(jnp.ones((chunk, chunk), dtype=bool))
    ws = jnp.where(causal, w_s[:, :, :chunk, :chunk], 0)
    eye = jnp.eye(reps, dtype=w_s.dtype)
    ws = jnp.einsum('ab,jgts->jgatbs', eye, ws).reshape(w_s.shape[0], A_GROUPS, CHUNK_A, CHUNK_A)
    bias = jnp.tile(jnp.transpose(b_s[:, :, :chunk], (0, 2, 1)), (1, reps, 1))
    bias = jnp.repeat(bias, A_GDIM, axis=2)
    return ws.astype(BF16), bias


def _run_trunk(x, mod, n_seq, seq_len, q_pos, hgrn_state0, mla_cache, prm, raw, tm):
    T = x.shape[0]
    per_seq_mod = mod.ndim == 5
    tps = (seq_len // tm) if per_seq_mod else 1
    sgu_chunk = min(CHUNK_A, seq_len)
    ws, bias = _sgu_mixing(raw['a_w_s'], raw['a_b_s'], sgu_chunk)
    chunk_v, hgrn_states, lat_rows, rope_rows = [], [], [], []
    for i in range(DEPTH):
        kind, j = i % N_MIXERS, i // N_MIXERS
        if kind == 0:
            x, v_rows = _sgu_layer(x, mod, tps, i, j, prm['a_w_in'], prm['a_ln_g'], prm['a_ln_b'], ws, bias,
                                   prm['a_w_out'], prm['ln1_g'], prm['ln1_b'], tm, emit_v=mla_cache is not None)
            chunk_v.append(v_rows)
        elif kind == 1:
            lb = prm['lb_all'][i].reshape(1, D_MODEL)
            q, k, lf, v, gs = _hgrn_proj(x, mod, tps, i, j, prm['b_w_in'], lb, tm)
            if seq_len % 64 == 0:
                C, lpad, hb = 64, seq_len, 4
                rec_tm = min(seq_len, 512)
                rec_in = (q, k, lf, v)
            else:
                C = lpad = rec_tm = SUBLANES
                hb = B_HEADS
                padseq = lambda a: jnp.pad(a.reshape(n_seq, seq_len, D_MODEL),
                                           ((0, 0), (0, lpad - seq_len), (0, 0))).reshape(n_seq * lpad, D_MODEL)
                rec_in = tuple(padseq(a) for a in (q, k, lf, v))
            s0 = None if hgrn_state0 is None else hgrn_state0[j]
            o, S = _hgrn_rec(*rec_in, s0, n_seq, lpad, rec_tm, C, hb)
            if lpad != seq_len:
                o = o.reshape(n_seq, lpad, D_MODEL)[:, :seq_len].reshape(T, D_MODEL)
            hgrn_states.append(S)
            x = _hgrn_out(x, mod, tps, i, j, o, gs, prm['b_w_out'], prm['ln1_g'], prm['ln1_b'], tm)
        else:
            tq = min(tm, 512)
            tps_q = (seq_len // tq) if per_seq_mod else 1
            pos_rows = q_pos if tq <= seq_len else jnp.tile(q_pos, tq // seq_len)
            ccq, ssq = _rope_tables(pos_rows, C_HEADS, C_HEADS * C_ROPE)
            cck, ssk = _rope_tables(pos_rows, 1, LANES)
            qcat, kcat, klat_t, lat, kr = _mla_proj(x, mod, tps_q, pos_rows.shape[0] // tq, i, j, prm['c_w_in'],
                                            prm['c_g_q'], prm['c_g_kv'], prm['c_wn'], prm['c_wr'], prm['c_wuk'],
                                            ccq, ssq, cck, ssk, tq)
            if mla_cache is None:
                o_lat = _attn_prompt(qcat, kcat, klat_t, n_seq, seq_len, tq, min(seq_len, 512))
            else:
                pool_lat, pool_rope_t, pt = mla_cache
                qs = qcat.reshape(C_HEADS, n_seq, seq_len, C_QK).transpose(1, 0, 2, 3)
                qs = qs.reshape(n_seq, C_HEADS * seq_len, C_QK)
                padk = lambda a: jnp.pad(a.reshape(n_seq, seq_len, a.shape[1]), ((0, 0), (0, 16 - seq_len), (0, 0)))
                o_s = _attn_paged(qs, padk(lat), padk(kr), pool_lat, pool_rope_t, pt, j, seq_len)
                o_lat = o_s.reshape(n_seq, C_HEADS, seq_len, C_KVLORA).transpose(1, 3, 0, 2)
                o_lat = o_lat.reshape(1, C_HEADS, C_KVLORA, T)
            x = _mla_out(x, mod, tps_q, i, j, o_lat, prm['c_wuv'], prm['c_w_out'], prm['ln1_g'], prm['ln1_b'], tq)
            lat_rows.append(lat.reshape(n_seq, seq_len, C_KVLORA))
            rope_rows.append(kr.reshape(n_seq, seq_len, C_ROPE))
        x = _ffn_layer(x, mod, tps, i, prm['ffn_w1'], prm['ffn_w2'], prm['ln2_g'], prm['ln2_b'], tm)
    stack = lambda xs: jnp.stack(xs) if xs and xs[0] is not None else None
    return x, stack(chunk_v), jnp.stack(hgrn_states), jnp.stack(lat_rows), jnp.stack(rope_rows)


def kernel(x_prompt, x_sample, cache_kv_latent, cache_k_rope, state_hgrn, page_table, c_prompt, c_sample,
           w_ada, b_ada, ln1_g, ln1_b, ln2_g, ln2_b, ffn_w1, ffn_w2, a_w_in, a_ln_g, a_ln_b, a_w_s, a_b_s,
           a_w_out, b_w_in, b_lb, b_w_out, c_w_in, c_g_q, c_g_kv, c_w_uq, c_w_uk, c_w_uv, c_w_out):
    raw = dict(ln1_g=ln1_g, ln1_b=ln1_b, ln2_g=ln2_g, ln2_b=ln2_b, ffn_w1=ffn_w1, ffn_w2=ffn_w2,
               a_w_in=a_w_in, a_ln_g=a_ln_g, a_ln_b=a_ln_b, a_w_s=a_w_s, a_b_s=a_b_s, a_w_out=a_w_out,
               b_w_in=b_w_in, b_lb=b_lb, b_w_out=b_w_out, c_w_in=c_w_in, c_g_q=c_g_q, c_g_kv=c_g_kv,
               c_w_uq=c_w_uq, c_w_uk=c_w_uk, c_w_uv=c_w_uv, c_w_out=c_w_out)
    prm = _prepare_params(raw)
    nb, seq, d = x_prompt.shape
    ns, sseq, _ = x_sample.shape
    past_len = page_table.shape[1] * cache_kv_latent.shape[2]
    pos_prompt = jnp.arange(seq, dtype=jnp.int32)
    pos_sample = past_len + jnp.arange(sseq, dtype=jnp.int32)

    mod_p, mod_s = _modulation(c_prompt, jnp.repeat(c_sample, sseq, axis=0), w_ada, b_ada)
    mod_p = mod_p.reshape(DEPTH, nb, 6, 1, d)

    tm_p = 512
    y_p, _, hs_p, lat_p, rope_p = _run_trunk(x_prompt.reshape(nb * seq, d), mod_p, nb, seq, pos_prompt,
                                             None, None, prm, raw, tm_p)
    y_s, v_s, hs_s, lat_s, rope_s = _run_trunk(x_sample.reshape(ns * sseq, d), mod_s, ns, sseq, pos_sample,
                                               state_hgrn,
                                               (cache_kv_latent, jnp.swapaxes(cache_k_rope, 2, 3), page_table),
                                               prm, raw, ns * sseq)
    return (y_p.reshape(nb, seq, d), y_s.reshape(ns, sseq, d), hs_p, hs_s, lat_p, rope_p, lat_s, rope_s,
            v_s.reshape(v_s.shape[0], ns, sseq, d))
```

```python
import functools
import math

import jax
import jax.numpy as jnp
from jax import lax
from jax.experimental import pallas as pl
from jax.experimental.pallas import tpu as pltpu

F32 = jnp.float32
BF16 = jnp.bfloat16

D_MODEL = 1024
DEPTH = 4
N_MIXERS = 3
CHUNK_A = 128
A_GROUPS = 8
A_GDIM = D_MODEL // A_GROUPS
B_HEADS = 8
B_DK = 128
B_DV = D_MODEL // B_HEADS
C_HEADS = 8
C_NOPE = 128
C_ROPE = 64
C_V = 128
C_QLORA = 512
C_KVLORA = 256
ROPE_THETA = 10000.0
D_FF = 4 * D_MODEL
ALPHA = (2.0 * DEPTH) ** 0.25
EPS = 1e-6

LANES = 128
SUBLANES = 8
C_QK = C_KVLORA + LANES
VMEM_LIMIT = 56 * 1024 * 1024
PAGED_SLOTS = 3


def _cparams(*sem):
    return pltpu.CompilerParams(dimension_semantics=sem, vmem_limit_bytes=VMEM_LIMIT)


def _dot(a, b):
    return jnp.dot(a, b, preferred_element_type=F32)


def _dot_nt(a, b):
    return lax.dot_general(a, b, (((1,), (1,)), ((), ())), preferred_element_type=F32)


def _dot_tn(a, b):
    return lax.dot_general(a, b, (((0,), (0,)), ((), ())), preferred_element_type=F32)


def _layer_norm(y, g, b):
    mu = jnp.mean(y, axis=-1, keepdims=True)
    yc = y - mu
    var = jnp.mean(yc * yc, axis=-1, keepdims=True)
    return yc * lax.rsqrt(var + EPS) * g + b


def _rms(y):
    return y * lax.rsqrt(jnp.mean(y * y, axis=-1, keepdims=True) + EPS)


def _silu(x):
    return x * jax.nn.sigmoid(x)


def _gelu_tanh(x):
    return 0.5 * x * (1.0 + jnp.tanh(math.sqrt(2.0 / math.pi) * (x + 0.044715 * (x * x * x))))


def _modulate(x, sh_ref, sc_ref):
    return x * (1.0 + sc_ref[...]) + sh_ref[...]


def _residual_ln(x, gate_ref, out, lg_ref, lb_ref):
    return _layer_norm(ALPHA * x + gate_ref[...] * out, lg_ref[...], lb_ref[...])


def _mod_spec(mod, layer, which, tiles_per_seq):
    if mod.ndim == 5:
        return pl.BlockSpec((None, None, None, 1, D_MODEL), lambda i: (layer, i // tiles_per_seq, which, 0, 0))
    return pl.BlockSpec((None, mod.shape[1], D_MODEL), lambda i: (layer, 0, which))


def _vec_spec(layer, width):
    return pl.BlockSpec((None, 1, width), lambda i: (layer, 0, 0))


def _full_spec(arr, layer=None):
    if layer is None:
        nd = arr.ndim
        return pl.BlockSpec(arr.shape, lambda i: (0,) * nd)
    nd = arr.ndim - 1
    return pl.BlockSpec((None,) + arr.shape[1:], lambda i: (layer,) + (0,) * nd)


def _row_spec(tm, width):
    return pl.BlockSpec((tm, width), lambda i: (i, 0))


def _mod_body(ca_ref, cb_ref, w_ref, b_ref, oa_ref, ob_ref):
    w = w_ref[...].astype(BF16)
    oa_ref[...] = _dot(_silu(ca_ref[...]).astype(BF16), w) + b_ref[...]
    ob_ref[...] = _dot(_silu(cb_ref[...]).astype(BF16), w) + b_ref[...]


def _modulation(c_a, c_b, w_ada, b_ada):
    tn = 1536
    width = w_ada.shape[2]
    rows = lambda c: pl.BlockSpec((c.shape[0], D_MODEL), lambda l, j: (0, 0))
    out = lambda c: pl.BlockSpec((None, c.shape[0], tn), lambda l, j: (l, 0, j))
    return pl.pallas_call(
        _mod_body,
        out_shape=[jax.ShapeDtypeStruct((DEPTH, c.shape[0], width), F32) for c in (c_a, c_b)],
        grid=(DEPTH, width // tn),
        in_specs=[rows(c_a), rows(c_b),
                  pl.BlockSpec((None, D_MODEL, tn), lambda l, j: (l, 0, j)),
                  pl.BlockSpec((None, 1, tn), lambda l, j: (l, 0, j))],
        out_specs=[out(c_a), out(c_b)],
        compiler_params=_cparams("arbitrary", "arbitrary"),
        name="adaln_modulation",
    )(c_a, c_b, w_ada, b_ada.reshape(DEPTH, 1, width))


def _ffn_body(x_ref, sh_ref, sc_ref, g_ref, w1_ref, w2_ref, lg_ref, lb_ref, o_ref, acc_ref, *, fc):
    x = x_ref[...]
    h = _modulate(x, sh_ref, sc_ref).astype(BF16)
    for c in range(D_FF // fc):
        a = _dot(h, w1_ref[:, c * fc:(c + 1) * fc])
        a = jnp.square(jnp.maximum(a, 0.0)).astype(BF16)
        d = _dot(a, w2_ref[c * fc:(c + 1) * fc, :])
        if c == 0:
            acc_ref[...] = d
        else:
            acc_ref[...] += d
    o_ref[...] = _residual_ln(x, g_ref, acc_ref[...], lg_ref, lb_ref)


def _ffn_layer(x, mod, tps, layer, w1, w2, ln_g, ln_b, tm):
    T = x.shape[0]
    return pl.pallas_call(
        functools.partial(_ffn_body, fc=1024),
        out_shape=jax.ShapeDtypeStruct((T, D_MODEL), F32),
        grid=(T // tm,),
        in_specs=[_row_spec(tm, D_MODEL),
                  _mod_spec(mod, layer, 3, tps), _mod_spec(mod, layer, 4, tps), _mod_spec(mod, layer, 5, tps),
                  _full_spec(w1, layer), _full_spec(w2, layer),
                  _vec_spec(layer, D_MODEL), _vec_spec(layer, D_MODEL)],
        out_specs=_row_spec(tm, D_MODEL),
        scratch_shapes=[pltpu.VMEM((tm, D_MODEL), F32)],
        compiler_params=_cparams("arbitrary"),
        name="ffn_sublayer",
    )(x, mod, mod, mod, w1, w2, ln_g, ln_b)


def _sgu_body(x_ref, sh_ref, sc_ref, g_ref, win_ref, lng_ref, lnb_ref, ws_ref, bias_ref, wout_ref,
              lg_ref, lb_ref, o_ref, *rest, tm, pieces, emit_v):
    if emit_v:
        v_ref, gated_ref = rest
    else:
        (gated_ref,) = rest
    rows = [slice(p * tm // pieces, (p + 1) * tm // pieces) for p in range(pieces)]
    per_row = lambda ref, r: ref[...] if ref.shape[0] == 1 else ref[r, :]

    def in_products(r):
        h = (x_ref[r, :] * (1.0 + per_row(sc_ref, r)) + per_row(sh_ref, r)).astype(BF16)
        return _dot(h, win_ref[:, :D_MODEL]), _dot(h, win_ref[:, D_MODEL:])

    def gate(r, zu, zv):
        u = _gelu_tanh(zu)
        v = _layer_norm(_gelu_tanh(zv), lng_ref[...], lnb_ref[...])
        if emit_v:
            v_ref[r, :] = v
        vb = v.astype(BF16)
        nch = (r.stop - r.start) // CHUNK_A
        for g in range(A_GROUPS):
            c = slice(g * A_GDIM, (g + 1) * A_GDIM)
            rhs = jnp.concatenate([vb[n * CHUNK_A:(n + 1) * CHUNK_A, c] for n in range(nch)], axis=1)
            mixed = _dot(ws_ref[g], rhs)
            for n in range(nch):
                rn = slice(n * CHUNK_A, (n + 1) * CHUNK_A)
                gated = u[rn, c] * (mixed[:, n * A_GDIM:(n + 1) * A_GDIM] + bias_ref[:, c])
                gated_ref[r.start + n * CHUNK_A:r.start + (n + 1) * CHUNK_A, c] = gated.astype(BF16)
        return _dot(gated_ref[r, :], wout_ref[...])

    def finish(r, out):
        y = ALPHA * x_ref[r, :] + per_row(g_ref, r) * out
        o_ref[r, :] = _layer_norm(y, lg_ref[...], lb_ref[...])

    z = in_products(rows[0])
    outs = []
    for p, r in enumerate(rows):
        z_next = in_products(rows[p + 1]) if p + 1 < pieces else None
        outs.append(gate(r, *z))
        z = z_next
        if p > 0:
            finish(rows[p - 1], outs[p - 1])
    finish(rows[-1], outs[-1])


def _sgu_layer(x, mod, tps, layer, j, w_in, ln_g, ln_b, ws, bias, w_out, ln1_g, ln1_b, tm, emit_v):
    T = x.shape[0]
    out_shape = [jax.ShapeDtypeStruct((T, D_MODEL), F32)]
    out_specs = [_row_spec(tm, D_MODEL)]
    if emit_v:
        out_shape.append(jax.ShapeDtypeStruct((T, D_MODEL), F32))
        out_specs.append(_row_spec(tm, D_MODEL))
    res = pl.pallas_call(
        functools.partial(_sgu_body, tm=tm, pieces=2 if tm >= 4 * CHUNK_A else 1, emit_v=emit_v),
        out_shape=out_shape,
        grid=(T // tm,),
        in_specs=[_row_spec(tm, D_MODEL),
                  _mod_spec(mod, layer, 0, tps), _mod_spec(mod, layer, 1, tps), _mod_spec(mod, layer, 2, tps),
                  _full_spec(w_in, j), _vec_spec(j, D_MODEL), _vec_spec(j, D_MODEL),
                  _full_spec(ws, j), _full_spec(bias, j), _full_spec(w_out, j),
                  _vec_spec(layer, D_MODEL), _vec_spec(layer, D_MODEL)],
        out_specs=out_specs,
        scratch_shapes=[pltpu.VMEM((tm, D_MODEL), BF16)],
        compiler_params=_cparams("arbitrary"),
        name="sgu_sublayer",
    )(x, mod, mod, mod, w_in, ln_g, ln_b, ws, bias, w_out, ln1_g, ln1_b)
    return (res[0], res[1]) if emit_v else (res[0], None)


def _hgrn_proj_body(x_ref, sh_ref, sc_ref, win_ref, lb_ref, q_ref, k_ref, lf_ref, v_ref, gs_ref):
    d = D_MODEL
    tm = x_ref.shape[0]
    pieces = 4 if tm % (4 * LANES) == 0 and sh_ref.shape[0] == 1 else 1
    rows = [slice(p * tm // pieces, (p + 1) * tm // pieces) for p in range(pieces)]

    def products(r):
        h = _modulate(x_ref[r, :], sh_ref, sc_ref).astype(BF16)
        return [_dot(h, win_ref[:, part * d:(part + 1) * d]) for part in (1, 0, 2, 3)]

    def gate(r, fz, zq, zv, zg):
        lb = lb_ref[...]
        e = jnp.exp(-jnp.abs(fz))
        a = jnp.log(lb)
        b = jnp.log1p(-lb) + (jnp.minimum(fz, 0.0) - jnp.log1p(e))
        lf_ref[r, :] = (jnp.maximum(a, b) + jnp.log1p(jnp.exp(-jnp.abs(a - b)))) * math.log2(math.e)
        k_ref[r, :] = (1.0 - lb) * (jnp.where(fz >= 0.0, e, 1.0) / (1.0 + e))
        q_ref[r, :] = _silu(zq)
        v_ref[r, :] = zv
        gs_ref[r, :] = _silu(zg)

    z = [products(r) for r in rows]
    for r, zr in zip(rows, z):
        gate(r, *zr)


def _hgrn_proj(x, mod, tps, layer, j, w_in, lb, tm):
    T = x.shape[0]
    shp = jax.ShapeDtypeStruct((T, D_MODEL), F32)
    return pl.pallas_call(
        _hgrn_proj_body,
        out_shape=[shp] * 5,
        grid=(T // tm,),
        in_specs=[_row_spec(tm, D_MODEL), _mod_spec(mod, layer, 0, tps), _mod_spec(mod, layer, 1, tps),
                  _full_spec(w_in, j), _full_spec(lb)],
        out_specs=[_row_spec(tm, D_MODEL)] * 5,
        compiler_params=_cparams("arbitrary"),
        name="hgrn_proj",
    )(x, mod, mod, w_in, lb)


def _hgrn_rec_body(*refs, C, nchunk, hb, has_s0):
    if has_s0:
        q_ref, k_ref, g_ref, v_ref, s0_ref, o_ref, sout_ref, st_ref = refs
    else:
        q_ref, k_ref, g_ref, v_ref, o_ref, sout_ref, st_ref = refs
    t = pl.program_id(2)

    @pl.when(t == 0)
    def _():
        for hd in range(hb):
            st_ref[hd] = s0_ref[hd].T if has_s0 else jnp.zeros((B_DV, B_DK), F32)

    row = lax.broadcasted_iota(jnp.int32, (C, B_DK), 0)
    row_a = lax.broadcasted_iota(jnp.int32, (C, C), 0)
    col_a = lax.broadcasted_iota(jnp.int32, (C, C), 1)
    tri = jnp.where(row_a >= col_a, 1.0, 0.0).astype(BF16)
    sub = row & (SUBLANES - 1)
    levels = []
    for m in (1, 2, 4, 8, 16, 32, 64, 128):
        if 2 * m <= C:
            shift = int(math.log2(2 * m))
            pair = (((row_a >> shift) == (col_a >> shift)) & ((row_a & (2 * m - 1)) >= m)
                    & ((col_a & (2 * m - 1)) < m))
            levels.append((m, (row & (2 * m - 1)) >= m, pair))

    def prefix(c, hd):
        rows = slice(c * C, (c + 1) * C)
        cols = slice(hd * B_DK, (hd + 1) * B_DK)
        gc = g_ref[rows, cols]
        g_hi = gc.astype(BF16)
        r1 = gc - g_hi.astype(F32)
        g_mid = r1.astype(BF16)
        g_lo = (r1 - g_mid.astype(F32)).astype(BF16)
        b3 = _dot(tri, jnp.concatenate([g_hi, g_mid, g_lo], axis=1))
        return rows, cols, b3

    def products(pre, st):
        rows, cols, b3 = pre
        qc = q_ref[rows, cols]
        kc = k_ref[rows, cols]
        b = b3[:, 0:B_DK] + b3[:, B_DK:2 * B_DK] + b3[:, 2 * B_DK:3 * B_DK]
        o_inter = _dot_nt((qc * jnp.exp2(b)).astype(BF16), st.astype(BF16))
        b3d = b.reshape(C // SUBLANES, SUBLANES, B_DK)
        sub_row = lambda i: jnp.broadcast_to(b3d[:, i:i + 1, :], b3d.shape).reshape(C, B_DK)
        level_dots = []
        for m, upper, pair in levels:
            if m == 1:
                bref = jnp.where(upper, pltpu.roll(b3d, 1, 1).reshape(C, B_DK), b)
            elif m == 2:
                bref = jnp.where(sub < 4, sub_row(1), sub_row(5))
            elif m == 4:
                bref = sub_row(3)
            else:
                bref = jnp.concatenate(
                    [jnp.broadcast_to(b[i * 2 * m + m - 1:i * 2 * m + m, :], (2 * m, B_DK))
                     for i in range(C // (2 * m))], axis=0)
            q_up = jnp.where(upper, qc * jnp.exp2(b - bref), 0.0)
            k_lo = jnp.where(upper, 0.0, kc * jnp.exp2(bref - b))
            level_dots.append(_dot_nt(q_up.astype(BF16), k_lo.astype(BF16)))
        diag = jnp.sum(qc * kc, axis=1, keepdims=True)
        b_last = b[C - 1:C, :]
        k_dec = (kc * jnp.exp2(b_last - b)).astype(BF16)
        return rows, cols, o_inter, level_dots, diag, k_dec, jnp.exp2(b_last)

    def finish(prod, st):
        rows, cols, o_inter, level_dots, diag, k_dec, decay = prod
        vb = v_ref[rows, cols].astype(BF16)
        a_mat = jnp.where(row_a == col_a, diag, 0.0)
        for (m, upper, pair), a_m in zip(levels, level_dots):
            a_mat = jnp.where(pair, a_m, a_mat)
        o_ref[rows, cols] = o_inter + _dot(a_mat.astype(BF16), vb)
        return st * decay + _dot_tn(vb, k_dec)

    states = [st_ref[hd] for hd in range(hb)]
    pre = [prefix(0, hd) for hd in range(hb)]
    for c in range(nchunk):
        prods = [products(pre[hd], states[hd]) for hd in range(hb)]
        if c + 1 < nchunk:
            pre = [prefix(c + 1, hd) for hd in range(hb)]
        states = [finish(prods[hd], states[hd]) for hd in range(hb)]
    for hd in range(hb):
        st_ref[hd] = states[hd]

    @pl.when(t == pl.num_programs(2) - 1)
    def _():
        for hd in range(hb):
            sout_ref[hd] = states[hd].T


def _hgrn_rec(q, k, lf, v, s0, n_seq, seq_len, tm, C, hb):
    T = q.shape[0]
    nt = seq_len // tm
    blk = pl.BlockSpec((tm, hb * B_DK), lambda b, h, t: (b * nt + t, h))
    st_spec = pl.BlockSpec((None, hb, B_DK, B_DV), lambda b, h, t: (b, h, 0, 0))
    has_s0 = s0 is not None
    in_specs = [blk] * 4 + ([st_spec] if has_s0 else [])
    args = (q, k, lf, v) + ((s0,) if has_s0 else ())
    return pl.pallas_call(
        functools.partial(_hgrn_rec_body, C=C, nchunk=tm // C, hb=hb, has_s0=has_s0),
        out_shape=[jax.ShapeDtypeStruct((T, D_MODEL), F32),
                   jax.ShapeDtypeStruct((n_seq, B_HEADS, B_DK, B_DV), F32)],
        grid=(n_seq, B_HEADS // hb, nt),
        in_specs=in_specs,
        out_specs=[blk, st_spec],
        scratch_shapes=[pltpu.VMEM((hb, B_DV, B_DK), F32)],
        compiler_params=_cparams("arbitrary", "arbitrary", "arbitrary"),
        name="hgrn_recurrence",
    )(*args)


def _hgrn_out_body(x_ref, g_ref, o_ref, gs_ref, wout_ref, lg_ref, lb_ref, y_ref):
    o = o_ref[...]
    parts = [_rms(o[:, h * B_DV:(h + 1) * B_DV]) for h in range(B_HEADS)]
    y = (jnp.concatenate(parts, axis=1) * gs_ref[...]).astype(BF16)
    y_ref[...] = _residual_ln(x_ref[...], g_ref, _dot(y, wout_ref[...]), lg_ref, lb_ref)


def _hgrn_out(x, mod, tps, layer, j, o, gs, w_out, ln_g, ln_b, tm):
    T = x.shape[0]
    return pl.pallas_call(
        _hgrn_out_body,
        out_shape=jax.ShapeDtypeStruct((T, D_MODEL), F32),
        grid=(T // tm,),
        in_specs=[_row_spec(tm, D_MODEL), _mod_spec(mod, layer, 2, tps),
                  _row_spec(tm, D_MODEL), _row_spec(tm, D_MODEL),
                  _full_spec(w_out, j), _vec_spec(layer, D_MODEL), _vec_spec(layer, D_MODEL)],
        out_specs=_row_spec(tm, D_MODEL),
        compiler_params=_cparams("arbitrary"),
        name="hgrn_out",
    )(x, mod, o, gs, w_out, ln_g, ln_b)


def _rope_lanes(x, cc_ref, ss_ref, period_first_half):
    n = x.shape[1]
    half = C_ROPE // 2
    rot = jnp.where(period_first_half, pltpu.roll(x, n - half, 1), pltpu.roll(x, half, 1))
    return x * cc_ref[...] + rot * ss_ref[...]


def _mla_proj_body(x_ref, sh_ref, sc_ref, win_ref, gq_ref, gkv_ref, wn_ref, wr_ref, wuk_ref,
                   ccq_ref, ssq_ref, cck_ref, ssk_ref, q_ref, kcat_ref, klt_ref, lat_ref, kr_ref):
    tm = x_ref.shape[0]
    pieces = 2 if tm % (4 * LANES) == 0 and sh_ref.shape[0] == 1 else 1
    rows = [slice(p * tm // pieces, (p + 1) * tm // pieces) for p in range(pieces)]

    def down(r):
        h = _modulate(x_ref[r, :], sh_ref, sc_ref).astype(BF16)
        return _dot(h, win_ref[...])

    def keys_and_queries(r, a):
        cq = (_rms(a[:, :C_QLORA]) * gq_ref[...]).astype(BF16)
        ckv = _rms(a[:, C_QLORA:C_QLORA + C_KVLORA]) * gkv_ref[...]
        kr_slab = a[:, C_QLORA + C_KVLORA:]
        lane_k = lax.broadcasted_iota(jnp.int32, kr_slab.shape, 1)
        kr_slab = _rope_lanes(kr_slab, cck_ref.at[r, :], ssk_ref.at[r, :], (lane_k & (C_ROPE - 1)) < C_ROPE // 2)
        lat_ref[r, :] = ckv
        kr_ref[r, :] = kr_slab[:, :C_ROPE]
        kcat_ref[r, :] = jnp.concatenate([ckv, kr_slab], axis=1).astype(BF16)
        klt_ref[:, r] = ckv.T.astype(BF16)
        return _dot(cq, wn_ref[...]), _dot(cq, wr_ref[...])

    def absorb(r, qn, qr):
        qn = qn.astype(BF16)
        lane_q = lax.broadcasted_iota(jnp.int32, qr.shape, 1)
        qr = _rope_lanes(qr, ccq_ref.at[r, :], ssq_ref.at[r, :], (lane_q & (C_ROPE - 1)) < C_ROPE // 2).astype(BF16)
        zeros = jnp.zeros((qr.shape[0], LANES - C_ROPE), BF16)
        for hd in range(C_HEADS):
            ql = _dot(qn[:, hd * C_NOPE:(hd + 1) * C_NOPE], wuk_ref[hd]).astype(BF16)
            q_ref[hd, r, :] = jnp.concatenate([ql, qr[:, hd * C_ROPE:(hd + 1) * C_ROPE], zeros], axis=1)

    downs = [down(r) for r in rows]
    ups = [keys_and_queries(r, a) for r, a in zip(rows, downs)]
    for r, (qn, qr) in zip(rows, ups):
        absorb(r, qn, qr)


def _mla_proj(x, mod, tps, tab_tiles, layer, j, w_in, g_q, g_kv, wn, wr, wuk, ccq, ssq, cck, ssk, tm):
    T = x.shape[0]
    nt = T // tm
    tab = lambda w: pl.BlockSpec((tm, w), lambda i: (i % tab_tiles, 0))
    return pl.pallas_call(
        _mla_proj_body,
        out_shape=[jax.ShapeDtypeStruct((nt, C_HEADS, tm, C_QK), BF16),
                   jax.ShapeDtypeStruct((T, C_QK), BF16),
                   jax.ShapeDtypeStruct((C_KVLORA, T), BF16),
                   jax.ShapeDtypeStruct((T, C_KVLORA), F32),
                   jax.ShapeDtypeStruct((T, C_ROPE), F32)],
        grid=(nt,),
        in_specs=[_row_spec(tm, D_MODEL), _mod_spec(mod, layer, 0, tps), _mod_spec(mod, layer, 1, tps),
                  _full_spec(w_in, j), _vec_spec(j, C_QLORA), _vec_spec(j, C_KVLORA),
                  _full_spec(wn, j), _full_spec(wr, j), _full_spec(wuk, j),
                  tab(C_HEADS * C_ROPE), tab(C_HEADS * C_ROPE), tab(LANES), tab(LANES)],
        out_specs=[pl.BlockSpec((None, C_HEADS, tm, C_QK), lambda i: (i, 0, 0, 0)),
                   _row_spec(tm, C_QK), pl.BlockSpec((C_KVLORA, tm), lambda i: (0, i)),
                   _row_spec(tm, C_KVLORA), _row_spec(tm, C_ROPE)],
        compiler_params=_cparams("arbitrary"),
        name="mla_proj",
    )(x, mod, mod, w_in, g_q, g_kv, wn, wr, wuk, ccq, ssq, cck, ssk)


def _softmax_init(m_ref, l_ref, acc_ref):
    m_ref[...] = jnp.full_like(m_ref, -jnp.inf)
    l_ref[...] = jnp.zeros_like(l_ref)
    acc_ref[...] = jnp.zeros_like(acc_ref)


def _attn_body(qi_ref, kj_ref, last_ref, q_ref, k_ref, kt_ref, o_ref, m_ref, l_ref, acc_ref, *, tq, tk, scale2):
    p_id = pl.program_id(1)
    qi = qi_ref[p_id]
    kj = kj_ref[p_id]

    @pl.when(kj == 0)
    def _():
        _softmax_init(m_ref, l_ref, acc_ref)

    def step(masked):
        k = k_ref[...]
        kt = kt_ref[...]
        if masked:
            key = lax.broadcasted_iota(jnp.int32, (tk, tq), 0) + kj * tk
            tok = lax.broadcasted_iota(jnp.int32, (tk, tq), 1) + qi * tq
            keep = key <= tok
        def scores(hd):
            t = _dot_nt(k, q_ref[hd]) * scale2
            return jnp.where(keep, t, -jnp.inf) if masked else t

        ahead = 2
        queue = [scores(hd) for hd in range(ahead)]
        for hd in range(C_HEADS):
            t = queue.pop(0)
            if hd + ahead < C_HEADS:
                queue.append(scores(hd + ahead))
            m_prev = m_ref[hd]
            m_new = jnp.maximum(m_prev, jnp.max(t, axis=0, keepdims=True))
            alpha = jnp.exp2(m_prev - m_new)
            p = jnp.exp2(t - m_new)
            l_ref[hd] = alpha * l_ref[hd] + jnp.sum(p, axis=0, keepdims=True)
            acc_ref[hd] = alpha * acc_ref[hd] + _dot(kt, p.astype(BF16))
            m_ref[hd] = m_new

    fully_visible = (kj + 1) * tk - 1 <= qi * tq
    pl.when(fully_visible)(lambda: step(False))
    pl.when(jnp.logical_not(fully_visible))(lambda: step(True))

    @pl.when(last_ref[p_id] == 1)
    def _():
        for hd in range(C_HEADS):
            o_ref[hd] = (acc_ref[hd] / l_ref[hd]).astype(BF16)


def _attn_prompt(q, kcat, klat_t, n_seq, seq_len, tq, tk):
    assert seq_len % tq == 0 and seq_len % tk == 0
    nq, nk = seq_len // tq, seq_len // tk
    pairs =[(i, j) for i in range(nq) for j in range((i * tq + tq - 1) // tk + 1)]
    qi = jnp.asarray([p[0] for p in pairs], jnp.int32)
    kj = jnp.asarray([p[1] for p in pairs], jnp.int32)
    last = jnp.asarray([int(n + 1 == len(pairs) or pairs[n + 1][0] != p[0]) for n, p in enumerate(pairs)], jnp.int32)
    scale2 = (C_NOPE + C_ROPE) ** -0.5 * math.log2(math.e)
    grid_spec = pltpu.PrefetchScalarGridSpec(
        num_scalar_prefetch=3,
        grid=(n_seq, len(pairs)),
        in_specs=[pl.BlockSpec((None, C_HEADS, tq, C_QK), lambda b, p, qi, kj, last: (b * nq + qi[p], 0, 0, 0)),
                  pl.BlockSpec((tk, C_QK), lambda b, p, qi, kj, last: (b * nk + kj[p], 0)),
                  pl.BlockSpec((C_KVLORA, tk), lambda b, p, qi, kj, last: (0, b * nk + kj[p]))],
        out_specs=pl.BlockSpec((None, C_HEADS, C_KVLORA, tq), lambda b, p, qi, kj, last: (b * nq + qi[p], 0, 0, 0)),
        scratch_shapes=[pltpu.VMEM((C_HEADS, 1, tq), F32), pltpu.VMEM((C_HEADS, 1, tq), F32),
                        pltpu.VMEM((C_HEADS, C_KVLORA, tq), F32)])
    return pl.pallas_call(
        functools.partial(_attn_body, tq=tq, tk=tk, scale2=scale2),
        out_shape=jax.ShapeDtypeStruct((n_seq * nq, C_HEADS, C_KVLORA, tq), BF16),
        grid_spec=grid_spec,
        compiler_params=_cparams("arbitrary", "arbitrary"),
        name="mla_attention_prompt",
    )(qi, kj, last, q, kcat, klat_t)


def _attn_paged_body(pt_ref, q_ref, nlat_ref, nrope_ref, lat_hbm, rope_hbm, o_ref,
                     lat_buf, rope_buf, sem, m_ref, l_ref, acc_ref, *, nsq, pages, sub, j, seq_new, scale2):
    b = pl.program_id(0)
    g = pl.program_id(1)
    n_groups = pl.num_programs(1)
    n_steps = pl.num_programs(0) * n_groups
    step = b * n_groups + g
    slot = lax.rem(step, PAGED_SLOTS)
    lookahead = PAGED_SLOTS - 1

    def page_copy(kind, page_id, sl, i):
        src, dst = (lat_hbm, lat_buf) if kind == 0 else (rope_hbm, rope_buf)
        return pltpu.make_async_copy(src.at[j, page_id], dst.at[sl, i], sem.at[kind, sl])

    def start_group(bb, gg, sl):
        for i in range(nsq * pages):
            page_id = pt_ref[bb * nsq + i // pages, gg * pages + i % pages]
            page_copy(0, page_id, sl, i).start(priority=i % 2)
            page_copy(1, page_id, sl, i).start(priority=(i + 1) % 2)

    def start_step(s, sl):
        s = jnp.minimum(s, n_steps - 1)
        start_group(s // n_groups, lax.rem(s, n_groups), sl)

    @pl.when(step == 0)
    def _():
        for s in range(lookahead):
            start_step(s, s)

    def wait_group(sl):
        for i in range(nsq * pages):
            page_copy(0, 0, sl, i).wait()
            page_copy(1, 0, sl, i).wait()

    wait_group(slot)

    def update(carry, t, values):
        m_prev, l_prev, acc = carry
        m_new = jnp.maximum(m_prev, jnp.max(t, axis=-1, keepdims=True))
        alpha = jnp.exp2(m_prev - m_new)
        p = jnp.exp2(t - m_new)
        return (m_new, alpha * l_prev + jnp.sum(p, axis=-1, keepdims=True),
                alpha * acc + _dot(p.astype(BF16), values))

    first = g == 0
    page = lat_buf.shape[2]
    qls = [q_ref[s][:, :C_KVLORA] for s in range(nsq)]
    qrs = [q_ref[s][:, C_KVLORA:C_KVLORA + C_ROPE] for s in range(nsq)]
    work = []
    for s in range(nsq):
        for u in range(pages // sub):
            e0 = s * pages + u * sub
            lat = lat_buf[slot, e0:e0 + sub].reshape(sub * page, C_KVLORA).astype(BF16)
            rp_t = jnp.concatenate([rope_buf[slot, e0 + i] for i in range(sub)], axis=1).astype(BF16)
            work.append((s, lat, (_dot_nt(qls[s], lat) + _dot(qrs[s], rp_t)) * scale2))
    start_step(step + lookahead, lax.rem(step + lookahead, PAGED_SLOTS))
    carries = [(jnp.where(first, -jnp.inf, m_ref[s]), jnp.where(first, 0.0, l_ref[s]),
                jnp.where(first, 0.0, acc_ref[s])) for s in range(nsq)]
    for s, lat, t in work:
        carries[s] = update(carries[s], t, lat)

    @pl.when(step == n_steps - 1)
    def _():
        for ahead in range(1, PAGED_SLOTS):
            wait_group(lax.rem(step + ahead, PAGED_SLOTS))

    @pl.when(g < n_groups - 1)
    def _():
        for s in range(nsq):
            m_ref[s], l_ref[s], acc_ref[s] = carries[s]

    @pl.when(g == n_groups - 1)
    def _():
        for s in range(nsq):
            nlat = nlat_ref[s].astype(BF16)
            t2 = (_dot_nt(qls[s], nlat) + _dot_nt(qrs[s], nrope_ref[s].astype(BF16))) * scale2
            tok = lax.broadcasted_iota(jnp.int32, t2.shape, 0) & (seq_new - 1)
            key = lax.broadcasted_iota(jnp.int32, t2.shape, 1)
            _, l_fin, acc = update(carries[s], jnp.where(key <= tok, t2, -jnp.inf), nlat)
            o_ref[s] = (acc / l_fin).astype(BF16)


def _attn_paged(q, new_lat, new_rope, pool_lat, pool_rope_t, page_table, j, seq_new):
    n_seq, n_pages = page_table.shape
    page = pool_lat.shape[2]
    nsq, pages, sub = 4, 16, 4
    assert n_seq % nsq == 0 and n_pages % pages == 0 and seq_new & (seq_new - 1) == 0
    rows = q.shape[1]
    scale2 = (C_NOPE + C_ROPE) ** -0.5 * math.log2(math.e)
    grid_spec = pltpu.PrefetchScalarGridSpec(
        num_scalar_prefetch=1,
        grid=(n_seq // nsq, n_pages // pages),
        in_specs=[pl.BlockSpec((nsq, rows, C_QK), lambda b, s, pt: (b, 0, 0)),
                  pl.BlockSpec((nsq,) + new_lat.shape[1:], lambda b, s, pt: (b, 0, 0)),
                  pl.BlockSpec((nsq,) + new_rope.shape[1:], lambda b, s, pt: (b, 0, 0)),
                  pl.BlockSpec(memory_space=pl.ANY), pl.BlockSpec(memory_space=pl.ANY)],
        out_specs=pl.BlockSpec((nsq, rows, C_KVLORA), lambda b, s, pt: (b, 0, 0)),
        scratch_shapes=[pltpu.VMEM((PAGED_SLOTS, nsq * pages, page, C_KVLORA), F32),
                        pltpu.VMEM((PAGED_SLOTS, nsq * pages, C_ROPE, page), F32),
                        pltpu.SemaphoreType.DMA((2, PAGED_SLOTS)),
                        pltpu.VMEM((nsq, rows, 1), F32), pltpu.VMEM((nsq, rows, 1), F32),
                        pltpu.VMEM((nsq, rows, C_KVLORA), F32)])
    return pl.pallas_call(
        functools.partial(_attn_paged_body, nsq=nsq, pages=pages, sub=sub, j=j, seq_new=seq_new, scale2=scale2),
        out_shape=jax.ShapeDtypeStruct((n_seq, rows, C_KVLORA), BF16),
        grid_spec=grid_spec,
        compiler_params=_cparams("arbitrary", "arbitrary"),
        name="mla_attention_paged",
    )(page_table, q, new_lat, new_rope, pool_lat, pool_rope_t)


def _mla_out_body(x_ref, g_ref, o_ref, wuv_ref, wout_ref, lg_ref, lb_ref, y_ref):
    parts = [_dot_tn(o_ref[hd], wuv_ref[hd]) for hd in range(C_HEADS)]
    o = jnp.concatenate(parts, axis=1).astype(BF16)
    y_ref[...] = _residual_ln(x_ref[...], g_ref, _dot(o, wout_ref[...]), lg_ref, lb_ref)


def _mla_out(x, mod, tps, layer, j, o_lat, wuv, w_out, ln_g, ln_b, tm):
    T = x.shape[0]
    return pl.pallas_call(
        _mla_out_body,
        out_shape=jax.ShapeDtypeStruct((T, D_MODEL), F32),
        grid=(T // tm,),
        in_specs=[_row_spec(tm, D_MODEL), _mod_spec(mod, layer, 2, tps),
                  pl.BlockSpec((None, C_HEADS, C_KVLORA, tm), lambda i: (i, 0, 0, 0)),
                  _full_spec(wuv, j), _full_spec(w_out, j),
                  _vec_spec(layer, D_MODEL), _vec_spec(layer, D_MODEL)],
        out_specs=_row_spec(tm, D_MODEL),
        compiler_params=_cparams("arbitrary"),
        name="mla_out",
    )(x, mod, o_lat, wuv, w_out, ln_g, ln_b)


def _rope_tables(pos, reps, width):
    half = C_ROPE // 2
    inv = ROPE_THETA ** (-jnp.arange(half, dtype=F32) / half)
    ang = pos.astype(F32)[:, None] * inv
    cos, sin = jnp.cos(ang), jnp.sin(ang)
    cc = jnp.tile(jnp.concatenate([cos, cos], axis=1), (1, reps))
    ss = jnp.tile(jnp.concatenate([-sin, sin], axis=1), (1, reps))
    pad = width - cc.shape[1]
    return jnp.pad(cc, ((0, 0), (0, pad))), jnp.pad(ss, ((0, 0), (0, pad)))


def _prepare_params(p):
    vec = lambda a: a.reshape(a.shape[0], 1, a.shape[1])
    w_uq = p['c_w_uq']
    n_c = w_uq.shape[0]
    c_w_in = jnp.pad(p['c_w_in'], ((0, 0), (0, 0), (0, LANES - C_ROPE)))
    lb_all = jnp.cumsum(jax.nn.softmax(p['b_lb'].astype(F32), axis=0), axis=0)
    lb_all = lb_all - lb_all[:1]
    return dict(
        ln1_g=vec(p['ln1_g']), ln1_b=vec(p['ln1_b']), ln2_g=vec(p['ln2_g']), ln2_b=vec(p['ln2_b']),
        ffn_w1=p['ffn_w1'].astype(BF16), ffn_w2=p['ffn_w2'].astype(BF16),
        a_w_in=p['a_w_in'].astype(BF16), a_ln_g=vec(p['a_ln_g']), a_ln_b=vec(p['a_ln_b']),
        a_w_out=p['a_w_out'].astype(BF16),
        b_w_in=p['b_w_in'].astype(BF16), b_w_out=p['b_w_out'].astype(BF16), lb_all=lb_all,
        c_w_in=c_w_in.astype(BF16), c_g_q=vec(p['c_g_q']), c_g_kv=vec(p['c_g_kv']),
        c_wn=w_uq[..., :C_NOPE].reshape(n_c, C_QLORA, C_HEADS * C_NOPE).astype(BF16),
        c_wr=w_uq[..., C_NOPE:].reshape(n_c, C_QLORA, C_HEADS * C_ROPE).astype(BF16),
        c_wuk=jnp.transpose(p['c_w_uk'], (0, 2, 3, 1)).astype(BF16),
        c_wuv=jnp.transpose(p['c_w_uv'], (0, 2, 1, 3)).astype(BF16),
        c_w_out=p['c_w_out'].astype(BF16),
    )


def _sgu_mixing(w_s, b_s, chunk):
    reps = CHUNK_A // chunk
    causal = jnp.tril(jnp.ones((chunk, chunk), dtype=bool))
    ws = jnp.where(causal, w_s[:, :, :chunk, :chunk], 0)
    eye = jnp.eye(reps, dtype=w_s.dtype)
    ws = jnp.einsum('ab,jgts->jgatbs', eye, ws).reshape(w_s.shape[0], A_GROUPS, CHUNK_A, CHUNK_A)
    bias = jnp.tile(jnp.transpose(b_s[:, :, :chunk], (0, 2, 1)), (1, reps, 1))
    bias = jnp.repeat(bias, A_GDIM, axis=2)
    return ws.astype(BF16), bias


def _run_trunk(x, mod, n_seq, seq_len, q_pos, hgrn_state0, mla_cache, prm, raw, tm):
    T = x.shape[0]
    assert T % tm == 0 and (tm % seq_len == 0 or seq_len % tm == 0) and tm % CHUNK_A == 0
    per_seq_mod = mod.ndim == 5
    tps = (seq_len // tm) if per_seq_mod else 1
    sgu_chunk = min(CHUNK_A, seq_len)
    ws, bias = _sgu_mixing(raw['a_w_s'], raw['a_b_s'], sgu_chunk)
    chunk_v, hgrn_states, lat_rows, rope_rows = [], [], [], []
    for i in range(DEPTH):
        kind, j = i % N_MIXERS, i // N_MIXERS
        if kind == 0:
            x, v_rows = _sgu_layer(x, mod, tps, i, j, prm['a_w_in'], prm['a_ln_g'], prm['a_ln_b'], ws, bias,
                                   prm['a_w_out'], prm['ln1_g'], prm['ln1_b'], tm, emit_v=mla_cache is not None)
            chunk_v.append(v_rows)
        elif kind == 1:
            lb = prm['lb_all'][i].reshape(1, D_MODEL)
            q, k, lf, v, gs = _hgrn_proj(x, mod, tps, i, j, prm['b_w_in'], lb, tm)
            if seq_len % 64 == 0:
                C, lpad, hb = 64, seq_len, 8
                rec_tm = min(seq_len, 512)
                rec_in = (q, k, lf, v)
            else:
                C = lpad = rec_tm = SUBLANES
                hb = B_HEADS
                padseq = lambda a: jnp.pad(a.reshape(n_seq, seq_len, D_MODEL),
                                           ((0, 0), (0, lpad - seq_len), (0, 0))).reshape(n_seq * lpad, D_MODEL)
                rec_in = tuple(padseq(a) for a in (q, k, lf, v))
            s0 = None if hgrn_state0 is None else hgrn_state0[j]
            o, S = _hgrn_rec(*rec_in, s0, n_seq, lpad, rec_tm, C, hb)
            if lpad != seq_len:
                o = o.reshape(n_seq, lpad, D_MODEL)[:, :seq_len].reshape(T, D_MODEL)
            hgrn_states.append(S)
            x = _hgrn_out(x, mod, tps, i, j, o, gs, prm['b_w_out'], prm['ln1_g'], prm['ln1_b'], tm)
        else:
            tq = min(tm, 512)
            tps_q = (seq_len // tq) if per_seq_mod else 1
            pos_rows = q_pos if tq <= seq_len else jnp.tile(q_pos, tq // seq_len)
            ccq, ssq = _rope_tables(pos_rows, C_HEADS, C_HEADS * C_ROPE)
            cck, ssk = _rope_tables(pos_rows, 1, LANES)
            qcat, kcat, klat_t, lat, kr = _mla_proj(x, mod, tps_q, pos_rows.shape[0] // tq, i, j, prm['c_w_in'],
                                            prm['c_g_q'], prm['c_g_kv'], prm['c_wn'], prm['c_wr'], prm['c_wuk'],
                                            ccq, ssq, cck, ssk, tq)
            if mla_cache is None:
                o_lat = _attn_prompt(qcat, kcat, klat_t, n_seq, seq_len, tq, min(seq_len, 512))
            else:
                pool_lat, pool_rope_t, pt = mla_cache
                qs = qcat.reshape(C_HEADS, n_seq, seq_len, C_QK).transpose(1, 0, 2, 3)
                qs = qs.reshape(n_seq, C_HEADS * seq_len, C_QK)
                padk = lambda a: jnp.pad(a.reshape(n_seq, seq_len, a.shape[1]), ((0, 0), (0, 16 - seq_len), (0, 0)))
                o_s = _attn_paged(qs, padk(lat), padk(kr), pool_lat, pool_rope_t, pt, j, seq_len)
                o_lat = o_s.reshape(n_seq, C_HEADS, seq_len, C_KVLORA).transpose(1, 3, 0, 2)
                o_lat = o_lat.reshape(1, C_HEADS, C_KVLORA, T)
            x = _mla_out(x, mod, tps_q, i, j, o_lat, prm['c_wuv'], prm['c_w_out'], prm['ln1_g'], prm['ln1_b'], tq)
            lat_rows.append(lat.reshape(n_seq, seq_len, C_KVLORA))
            rope_rows.append(kr.reshape(n_seq, seq_len, C_ROPE))
        x = _ffn_layer(x, mod, tps, i, prm['ffn_w1'], prm['ffn_w2'], prm['ln2_g'], prm['ln2_b'], tm)
    stack = lambda xs: jnp.stack(xs) if xs and xs[0] is not None else None
    return x, stack(chunk_v), jnp.stack(hgrn_states), jnp.stack(lat_rows), jnp.stack(rope_rows)


def kernel(x_prompt, x_sample, cache_kv_latent, cache_k_rope, state_hgrn, page_table, c_prompt, c_sample,
           w_ada, b_ada, ln1_g, ln1_b, ln2_g, ln2_b, ffn_w1, ffn_w2, a_w_in, a_ln_g, a_ln_b, a_w_s, a_b_s,
           a_w_out, b_w_in, b_lb, b_w_out, c_w_in, c_g_q, c_g_kv, c_w_uq, c_w_uk, c_w_uv, c_w_out):
    raw = dict(ln1_g=ln1_g, ln1_b=ln1_b, ln2_g=ln2_g, ln2_b=ln2_b, ffn_w1=ffn_w1, ffn_w2=ffn_w2,
               a_w_in=a_w_in, a_ln_g=a_ln_g, a_ln_b=a_ln_b, a_w_s=a_w_s, a_b_s=a_b_s, a_w_out=a_w_out,
               b_w_in=b_w_in, b_lb=b_lb, b_w_out=b_w_out, c_w_in=c_w_in, c_g_q=c_g_q, c_g_kv=c_g_kv,
               c_w_uq=c_w_uq, c_w_uk=c_w_uk, c_w_uv=c_w_uv, c_w_out=c_w_out)
    prm = _prepare_params(raw)
    nb, seq, d = x_prompt.shape
    ns, sseq, _ = x_sample.shape
    past_len = page_table.shape[1] * cache_kv_latent.shape[2]
    pos_prompt = jnp.arange(seq, dtype=jnp.int32)
    pos_sample = past_len + jnp.arange(sseq, dtype=jnp.int32)

    mod_p, mod_s = _modulation(c_prompt, jnp.repeat(c_sample, sseq, axis=0), w_ada, b_ada)
    mod_p = mod_p.reshape(DEPTH, nb, 6, 1, d)

    tm_p = 512
    y_p, _, hs_p, lat_p, rope_p = _run_trunk(x_prompt.reshape(nb * seq, d), mod_p, nb, seq, pos_prompt,
                                             None, None, prm, raw, tm_p)
    y_s, v_s, hs_s, lat_s, rope_s = _run_trunk(x_sample.reshape(ns * sseq, d), mod_s, ns, sseq, pos_sample,
                                               state_hgrn,
                                               (cache_kv_latent, jnp.swapaxes(cache_k_rope, 2, 3), page_table),
                                               prm, raw, ns * sseq)
    return (y_p.reshape(nb, seq, d), y_s.reshape(ns, sseq, d), hs_p, hs_s, lat_p, rope_p, lat_s, rope_s,
            v_s.reshape(v_s.shape[0], ns, sseq, d))
```

```python
import functools
import math

import jax
import jax.numpy as jnp
from jax import lax
from jax.experimental import pallas as pl
from jax.experimental.pallas import tpu as pltpu

F32 = jnp.float32
BF16 = jnp.bfloat16

D_MODEL = 1024
DEPTH = 4
N_MIXERS = 3
CHUNK_A = 128
A_GROUPS = 8
A_GDIM = D_MODEL // A_GROUPS
B_HEADS = 8
B_DK = 128
B_DV = D_MODEL // B_HEADS
C_HEADS = 8
C_NOPE = 128
C_ROPE = 64
C_V = 128
C_QLORA = 512
C_KVLORA = 256
ROPE_THETA = 10000.0
D_FF = 4 * D_MODEL
ALPHA = (2.0 * DEPTH) ** 0.25
EPS = 1e-6

LANES = 128
SUBLANES = 8
C_QK = C_KVLORA + LANES
VMEM_LIMIT = 56 * 1024 * 1024
PAGED_SLOTS = 3


def _cparams(*sem):
    return pltpu.CompilerParams(dimension_semantics=sem, vmem_limit_bytes=VMEM_LIMIT)


def _dot(a, b):
    return jnp.dot(a, b, preferred_element_type=F32)


def _dot_nt(a, b):
    return lax.dot_general(a, b, (((1,), (1,)), ((), ())), preferred_element_type=F32)


def _dot_tn(a, b):
    return lax.dot_general(a, b, (((0,), (0,)), ((), ())), preferred_element_type=F32)


def _layer_norm(y, g, b):
    mu = jnp.mean(y, axis=-1, keepdims=True)
    yc = y - mu
    var = jnp.mean(yc * yc, axis=-1, keepdims=True)
    return yc * lax.rsqrt(var + EPS) * g + b


def _rms(y):
    return y * lax.rsqrt(jnp.mean(y * y, axis=-1, keepdims=True) + EPS)


def _silu(x):
    return x * jax.nn.sigmoid(x)


def _gelu_tanh(x):
    c = math.sqrt(2.0 / math.pi)
    half = 0.5 * x
    return half + half * jnp.tanh(x * (c + (c * 0.044715) * (x * x)))


def _modulate(x, sh_ref, sc_ref):
    return x * (1.0 + sc_ref[...]) + sh_ref[...]


def _residual_ln(x, gate_ref, out, lg_ref, lb_ref):
    return _layer_norm(ALPHA * x + gate_ref[...] * out, lg_ref[...], lb_ref[...])


def _mod_spec(mod, layer, which, tiles_per_seq):
    if mod.ndim == 5:
        return pl.BlockSpec((None, None, None, 1, D_MODEL), lambda i: (layer, i // tiles_per_seq, which, 0, 0))
    return pl.BlockSpec((None, mod.shape[1], D_MODEL), lambda i: (layer, 0, which))


def _vec_spec(layer, width):
    return pl.BlockSpec((None, 1, width), lambda i: (layer, 0, 0))


def _full_spec(arr, layer=None):
    if layer is None:
        nd = arr.ndim
        return pl.BlockSpec(arr.shape, lambda i: (0,) * nd)
    nd = arr.ndim - 1
    return pl.BlockSpec((None,) + arr.shape[1:], lambda i: (layer,) + (0,) * nd)


def _row_spec(tm, width):
    return pl.BlockSpec((tm, width), lambda i: (i, 0))


def _mod_body(ca_ref, cb_ref, w_ref, b_ref, oa_ref, ob_ref):
    w = w_ref[...].astype(BF16)
    oa_ref[...] = _dot(_silu(ca_ref[...]).astype(BF16), w) + b_ref[...]
    ob_ref[...] = _dot(_silu(cb_ref[...]).astype(BF16), w) + b_ref[...]


def _modulation(c_a, c_b, w_ada, b_ada):
    tn = 1536
    width = w_ada.shape[2]
    rows = lambda c: pl.BlockSpec((c.shape[0], D_MODEL), lambda l, j: (0, 0))
    out = lambda c: pl.BlockSpec((None, c.shape[0], tn), lambda l, j: (l, 0, j))
    return pl.pallas_call(
        _mod_body,
        out_shape=[jax.ShapeDtypeStruct((DEPTH, c.shape[0], width), F32) for c in (c_a, c_b)],
        grid=(DEPTH, width // tn),
        in_specs=[rows(c_a), rows(c_b),
                  pl.BlockSpec((None, D_MODEL, tn), lambda l, j: (l, 0, j)),
                  pl.BlockSpec((None, 1, tn), lambda l, j: (l, 0, j))],
        out_specs=[out(c_a), out(c_b)],
        compiler_params=_cparams("arbitrary", "arbitrary"),
        name="adaln_modulation",
    )(c_a, c_b, w_ada, b_ada.reshape(DEPTH, 1, width))


def _ffn_body(x_ref, sh_ref, sc_ref, g_ref, w1_ref, w2_ref, lg_ref, lb_ref, o_ref, acc_ref, *, fc):
    x = x_ref[...]
    h = _modulate(x, sh_ref, sc_ref).astype(BF16)
    for c in range(D_FF // fc):
        a = _dot(h, w1_ref[:, c * fc:(c + 1) * fc])
        a = jnp.square(jnp.maximum(a, 0.0)).astype(BF16)
        d = _dot(a, w2_ref[c * fc:(c + 1) * fc, :])
        if c == 0:
            acc_ref[...] = d
        else:
            acc_ref[...] += d
    o_ref[...] = _residual_ln(x, g_ref, acc_ref[...], lg_ref, lb_ref)


def _ffn_layer(x, mod, tps, layer, w1, w2, ln_g, ln_b, tm):
    T = x.shape[0]
    return pl.pallas_call(
        functools.partial(_ffn_body, fc=1024),
        out_shape=jax.ShapeDtypeStruct((T, D_MODEL), F32),
        grid=(T // tm,),
        in_specs=[_row_spec(tm, D_MODEL),
                  _mod_spec(mod, layer, 3, tps), _mod_spec(mod, layer, 4, tps), _mod_spec(mod, layer, 5, tps),
                  _full_spec(w1, layer), _full_spec(w2, layer),
                  _vec_spec(layer, D_MODEL), _vec_spec(layer, D_MODEL)],
        out_specs=_row_spec(tm, D_MODEL),
        scratch_shapes=[pltpu.VMEM((tm, D_MODEL), F32)],
        compiler_params=_cparams("arbitrary"),
        name="ffn_sublayer",
    )(x, mod, mod, mod, w1, w2, ln_g, ln_b)


def _sgu_body(x_ref, sh_ref, sc_ref, g_ref, win_ref, lng_ref, lnb_ref, ws_ref, bias_ref, wout_ref,
              lg_ref, lb_ref, o_ref, *rest, tm, pieces, emit_v):
    if emit_v:
        v_ref, gated_ref = rest
    else:
        (gated_ref,) = rest
    rows = [slice(p * tm // pieces, (p + 1) * tm // pieces) for p in range(pieces)]
    per_row = lambda ref, r: ref[...] if ref.shape[0] == 1 else ref[r, :]

    def in_products(r):
        h = (x_ref[r, :] * (1.0 + per_row(sc_ref, r)) + per_row(sh_ref, r)).astype(BF16)
        return _dot(h, win_ref[:, :D_MODEL]), _dot(h, win_ref[:, D_MODEL:])

    def gate(r, zu, zv):
        u = _gelu_tanh(zu)
        v = _layer_norm(_gelu_tanh(zv), lng_ref[...], lnb_ref[...])
        if emit_v:
            v_ref[r, :] = v
        vb = v.astype(BF16)
        nch = (r.stop - r.start) // CHUNK_A
        for g in range(A_GROUPS):
            c = slice(g * A_GDIM, (g + 1) * A_GDIM)
            rhs = jnp.concatenate([vb[n * CHUNK_A:(n + 1) * CHUNK_A, c] for n in range(nch)], axis=1)
            mixed = _dot(ws_ref[g], rhs)
            for n in range(nch):
                rn = slice(n * CHUNK_A, (n + 1) * CHUNK_A)
                gated = u[rn, c] * (mixed[:, n * A_GDIM:(n + 1) * A_GDIM] + bias_ref[:, c])
                gated_ref[r.start + n * CHUNK_A:r.start + (n + 1) * CHUNK_A, c] = gated.astype(BF16)
        return _dot(gated_ref[r, :], wout_ref[...])

    def finish(r, out):
        y = ALPHA * x_ref[r, :] + per_row(g_ref, r) * out
        o_ref[r, :] = _layer_norm(y, lg_ref[...], lb_ref[...])

    z = in_products(rows[0])
    outs = []
    for p, r in enumerate(rows):
        z_next = in_products(rows[p + 1]) if p + 1 < pieces else None
        outs.append(gate(r, *z))
        z = z_next
        if p > 0:
            finish(rows[p - 1], outs[p - 1])
    finish(rows[-1], outs[-1])


def _sgu_layer(x, mod, tps, layer, j, w_in, ln_g, ln_b, ws, bias, w_out, ln1_g, ln1_b, tm, emit_v):
    T = x.shape[0]
    out_shape = [jax.ShapeDtypeStruct((T, D_MODEL), F32)]
    out_specs = [_row_spec(tm, D_MODEL)]
    if emit_v:
        out_shape.append(jax.ShapeDtypeStruct((T, D_MODEL), F32))
        out_specs.append(_row_spec(tm, D_MODEL))
    res = pl.pallas_call(
        functools.partial(_sgu_body, tm=tm, pieces=2 if tm >= 4 * CHUNK_A else 1, emit_v=emit_v),
        out_shape=out_shape,
        grid=(T // tm,),
        in_specs=[_row_spec(tm, D_MODEL),
                  _mod_spec(mod, layer, 0, tps), _mod_spec(mod, layer, 1, tps), _mod_spec(mod, layer, 2, tps),
                  _full_spec(w_in, j), _vec_spec(j, D_MODEL), _vec_spec(j, D_MODEL),
                  _full_spec(ws, j), _full_spec(bias, j), _full_spec(w_out, j),
                  _vec_spec(layer, D_MODEL), _vec_spec(layer, D_MODEL)],
        out_specs=out_specs,
        scratch_shapes=[pltpu.VMEM((tm, D_MODEL), BF16)],
        compiler_params=_cparams("arbitrary"),
        name="sgu_sublayer",
    )(x, mod, mod, mod, w_in, ln_g, ln_b, ws, bias, w_out, ln1_g, ln1_b)
    return (res[0], res[1]) if emit_v else (res[0], None)


def _hgrn_proj_body(x_ref, sh_ref, sc_ref, win_ref, lb_ref, q_ref, k_ref, lf_ref, v_ref, gs_ref):
    d = D_MODEL
    tm = x_ref.shape[0]
    pieces = 4 if tm % (4 * LANES) == 0 and sh_ref.shape[0] == 1 else 1
    rows = [slice(p * tm // pieces, (p + 1) * tm // pieces) for p in range(pieces)]

    def products(r):
        h = _modulate(x_ref[r, :], sh_ref, sc_ref).astype(BF16)
        return [_dot(h, win_ref[:, part * d:(part + 1) * d]) for part in (1, 0, 2, 3)]

    def gate(r, fz, zq, zv, zg):
        lb = lb_ref[...]
        e = jnp.exp(-jnp.abs(fz))
        a = jnp.log(lb)
        b = jnp.log1p(-lb) + (jnp.minimum(fz, 0.0) - jnp.log1p(e))
        lf_ref[r, :] = (jnp.maximum(a, b) + jnp.log1p(jnp.exp(-jnp.abs(a - b)))) * math.log2(math.e)
        k_ref[r, :] = (1.0 - lb) * (jnp.where(fz >= 0.0, e, 1.0) / (1.0 + e))
        q_ref[r, :] = _silu(zq)
        v_ref[r, :] = zv
        gs_ref[r, :] = _silu(zg)

    z = [products(r) for r in rows]
    for r, zr in zip(rows, z):
        gate(r, *zr)


def _hgrn_proj(x, mod, tps, layer, j, w_in, lb, tm):
    T = x.shape[0]
    shp = jax.ShapeDtypeStruct((T, D_MODEL), F32)
    return pl.pallas_call(
        _hgrn_proj_body,
        out_shape=[shp] * 5,
        grid=(T // tm,),
        in_specs=[_row_spec(tm, D_MODEL), _mod_spec(mod, layer, 0, tps), _mod_spec(mod, layer, 1, tps),
                  _full_spec(w_in, j), _full_spec(lb)],
        out_specs=[_row_spec(tm, D_MODEL)] * 5,
        compiler_params=_cparams("arbitrary"),
        name="hgrn_proj",
    )(x, mod, mod, w_in, lb)


def _hgrn_rec_body(*refs, C, nchunk, hb, has_s0):
    if has_s0:
        q_ref, k_ref, g_ref, v_ref, s0_ref, o_ref, sout_ref, st_ref = refs
    else:
        q_ref, k_ref, g_ref, v_ref, o_ref, sout_ref, st_ref = refs
    t = pl.program_id(2)

    @pl.when(t == 0)
    def _():
        for hd in range(hb):
            st_ref[hd] = s0_ref[hd].T if has_s0 else jnp.zeros((B_DV, B_DK), F32)

    row = lax.broadcasted_iota(jnp.int32, (C, B_DK), 0)
    row_a = lax.broadcasted_iota(jnp.int32, (C, C), 0)
    col_a = lax.broadcasted_iota(jnp.int32, (C, C), 1)
    tri = jnp.where(row_a >= col_a, 1.0, 0.0).astype(BF16)
    sub = row & (SUBLANES - 1)
    levels = []
    for m in (1, 2, 4, 8, 16, 32, 64, 128):
        if 2 * m <= C:
            shift = int(math.log2(2 * m))
            pair = (((row_a >> shift) == (col_a >> shift)) & ((row_a & (2 * m - 1)) >= m)
                    & ((col_a & (2 * m - 1)) < m))
            levels.append((m, (row & (2 * m - 1)) >= m, pair))

    def prefix(c, hd):
        rows = slice(c * C, (c + 1) * C)
        cols = slice(hd * B_DK, (hd + 1) * B_DK)
        gc = g_ref[rows, cols]
        g_hi = gc.astype(BF16)
        r1 = gc - g_hi.astype(F32)
        g_mid = r1.astype(BF16)
        g_lo = (r1 - g_mid.astype(F32)).astype(BF16)
        b3 = _dot(tri, jnp.concatenate([g_hi, g_mid, g_lo], axis=1))
        return rows, cols, b3

    def products(pre, st):
        rows, cols, b3 = pre
        qc = q_ref[rows, cols]
        kc = k_ref[rows, cols]
        b = b3[:, 0:B_DK] + b3[:, B_DK:2 * B_DK] + b3[:, 2 * B_DK:3 * B_DK]
        o_inter = _dot_nt((qc * jnp.exp2(b)).astype(BF16), st.astype(BF16))
        b3d = b.reshape(C // SUBLANES, SUBLANES, B_DK)
        sub_row = lambda i: jnp.broadcast_to(b3d[:, i:i + 1, :], b3d.shape).reshape(C, B_DK)
        level_dots = []
        for m, upper, pair in levels:
            if m == 1:
                bref = jnp.where(upper, pltpu.roll(b3d, 1, 1).reshape(C, B_DK), b)
            elif m == 2:
                bref = jnp.where(sub < 4, sub_row(1), sub_row(5))
            elif m == 4:
                bref = sub_row(3)
            else:
                bref = jnp.concatenate(
                    [jnp.broadcast_to(b[i * 2 * m + m - 1:i * 2 * m + m, :], (2 * m, B_DK))
                     for i in range(C // (2 * m))], axis=0)
            q_up = jnp.where(upper, qc * jnp.exp2(b - bref), 0.0)
            k_lo = jnp.where(upper, 0.0, kc * jnp.exp2(bref - b))
            level_dots.append(_dot_nt(q_up.astype(BF16), k_lo.astype(BF16)))
        diag = jnp.sum(qc * kc, axis=1, keepdims=True)
        b_last = b[C - 1:C, :]
        k_dec = (kc * jnp.exp2(b_last - b)).astype(BF16)
        return rows, cols, o_inter, level_dots, diag, k_dec, jnp.exp2(b_last)

    def finish(prod, st):
        rows, cols, o_inter, level_dots, diag, k_dec, decay = prod
        vb = v_ref[rows, cols].astype(BF16)
        a_mat = jnp.where(row_a == col_a, diag, 0.0)
        for (m, upper, pair), a_m in zip(levels, level_dots):
            a_mat = jnp.where(pair, a_m, a_mat)
        o_ref[rows, cols] = o_inter + _dot(a_mat.astype(BF16), vb)
        return st * decay + _dot_tn(vb, k_dec)

    states = [st_ref[hd] for hd in range(hb)]
    pre = [prefix(0, hd) for hd in range(hb)]
    for c in range(nchunk):
        prods = [products(pre[hd], states[hd]) for hd in range(hb)]
        if c + 1 < nchunk:
            pre = [prefix(c + 1, hd) for hd in range(hb)]
        states = [finish(prods[hd], states[hd]) for hd in range(hb)]
    for hd in range(hb):
        st_ref[hd] = states[hd]

    @pl.when(t == pl.num_programs(2) - 1)
    def _():
        for hd in range(hb):
            sout_ref[hd] = states[hd].T


def _hgrn_rec(q, k, lf, v, s0, n_seq, seq_len, tm, C, hb):
    T = q.shape[0]
    nt = seq_len // tm
    blk = pl.BlockSpec((tm, hb * B_DK), lambda b, h, t: (b * nt + t, h))
    st_spec = pl.BlockSpec((None, hb, B_DK, B_DV), lambda b, h, t: (b, h, 0, 0))
    has_s0 = s0 is not None
    in_specs = [blk] * 4 + ([st_spec] if has_s0 else [])
    args = (q, k, lf, v) + ((s0,) if has_s0 else ())
    return pl.pallas_call(
        functools.partial(_hgrn_rec_body, C=C, nchunk=tm // C, hb=hb, has_s0=has_s0),
        out_shape=[jax.ShapeDtypeStruct((T, D_MODEL), F32),
                   jax.ShapeDtypeStruct((n_seq, B_HEADS, B_DK, B_DV), F32)],
        grid=(n_seq, B_HEADS // hb, nt),
        in_specs=in_specs,
        out_specs=[blk, st_spec],
        scratch_shapes=[pltpu.VMEM((hb, B_DV, B_DK), F32)],
        compiler_params=_cparams("arbitrary", "arbitrary", "arbitrary"),
        name="hgrn_recurrence",
    )(*args)


def _hgrn_out_body(x_ref, g_ref, o_ref, gs_ref, wout_ref, lg_ref, lb_ref, y_ref):
    o = o_ref[...]
    parts = [_rms(o[:, h * B_DV:(h + 1) * B_DV]) for h in range(B_HEADS)]
    y = (jnp.concatenate(parts, axis=1) * gs_ref[...]).astype(BF16)
    y_ref[...] = _residual_ln(x_ref[...], g_ref, _dot(y, wout_ref[...]), lg_ref, lb_ref)


def _hgrn_out(x, mod, tps, layer, j, o, gs, w_out, ln_g, ln_b, tm):
    T = x.shape[0]
    return pl.pallas_call(
        _hgrn_out_body,
        out_shape=jax.ShapeDtypeStruct((T, D_MODEL), F32),
        grid=(T // tm,),
        in_specs=[_row_spec(tm, D_MODEL), _mod_spec(mod, layer, 2, tps),
                  _row_spec(tm, D_MODEL), _row_spec(tm, D_MODEL),
                  _full_spec(w_out, j), _vec_spec(layer, D_MODEL), _vec_spec(layer, D_MODEL)],
        out_specs=_row_spec(tm, D_MODEL),
        compiler_params=_cparams("arbitrary"),
        name="hgrn_out",
    )(x, mod, o, gs, w_out, ln_g, ln_b)


def _rope_lanes(x, cc_ref, ss_ref, period_first_half):
    n = x.shape[1]
    half = C_ROPE // 2
    rot = jnp.where(period_first_half, pltpu.roll(x, n - half, 1), pltpu.roll(x, half, 1))
    return x * cc_ref[...] + rot * ss_ref[...]


def _mla_proj_body(x_ref, sh_ref, sc_ref, win_ref, gq_ref, gkv_ref, wn_ref, wr_ref, wuk_ref,
                   ccq_ref, ssq_ref, cck_ref, ssk_ref, q_ref, kcat_ref, klt_ref, lat_ref, kr_ref):
    tm = x_ref.shape[0]
    pieces = 2 if tm % (4 * LANES) == 0 and sh_ref.shape[0] == 1 else 1
    rows = [slice(p * tm // pieces, (p + 1) * tm // pieces) for p in range(pieces)]

    def down(r):
        h = _modulate(x_ref[r, :], sh_ref, sc_ref).astype(BF16)
        return _dot(h, win_ref[...])

    def keys_and_queries(r, a):
        cq = (_rms(a[:, :C_QLORA]) * gq_ref[...]).astype(BF16)
        ckv = _rms(a[:, C_QLORA:C_QLORA + C_KVLORA]) * gkv_ref[...]
        kr_slab = a[:, C_QLORA + C_KVLORA:]
        lane_k = lax.broadcasted_iota(jnp.int32, kr_slab.shape, 1)
        kr_slab = _rope_lanes(kr_slab, cck_ref.at[r, :], ssk_ref.at[r, :], (lane_k & (C_ROPE - 1)) < C_ROPE // 2)
        lat_ref[r, :] = ckv
        kr_ref[r, :] = kr_slab[:, :C_ROPE]
        kcat_ref[r, :] = jnp.concatenate([ckv, kr_slab], axis=1).astype(BF16)
        klt_ref[:, r] = ckv.T.astype(BF16)
        return _dot(cq, wn_ref[...]), _dot(cq, wr_ref[...])

    def absorb(r, qn, qr):
        qn = qn.astype(BF16)
        lane_q = lax.broadcasted_iota(jnp.int32, qr.shape, 1)
        qr = _rope_lanes(qr, ccq_ref.at[r, :], ssq_ref.at[r, :], (lane_q & (C_ROPE - 1)) < C_ROPE // 2).astype(BF16)
        zeros = jnp.zeros((qr.shape[0], LANES - C_ROPE), BF16)
        for hd in range(C_HEADS):
            ql = _dot(qn[:, hd * C_NOPE:(hd + 1) * C_NOPE], wuk_ref[hd]).astype(BF16)
            q_ref[hd, r, :] = jnp.concatenate([ql, qr[:, hd * C_ROPE:(hd + 1) * C_ROPE], zeros], axis=1)

    downs = [down(r) for r in rows]
    ups = [keys_and_queries(r, a) for r, a in zip(rows, downs)]
    for r, (qn, qr) in zip(rows, ups):
        absorb(r, qn, qr)


def _mla_proj(x, mod, tps, tab_tiles, layer, j, w_in, g_q, g_kv, wn, wr, wuk, ccq, ssq, cck, ssk, tm):
    T = x.shape[0]
    nt = T // tm
    tab = lambda w: pl.BlockSpec((tm, w), lambda i: (i % tab_tiles, 0))
    return pl.pallas_call(
        _mla_proj_body,
        out_shape=[jax.ShapeDtypeStruct((nt, C_HEADS, tm, C_QK), BF16),
                   jax.ShapeDtypeStruct((T, C_QK), BF16),
                   jax.ShapeDtypeStruct((C_KVLORA, T), BF16),
                   jax.ShapeDtypeStruct((T, C_KVLORA), F32),
                   jax.ShapeDtypeStruct((T, C_ROPE), F32)],
        grid=(nt,),
        in_specs=[_row_spec(tm, D_MODEL), _mod_spec(mod, layer, 0, tps), _mod_spec(mod, layer, 1, tps),
                  _full_spec(w_in, j), _vec_spec(j, C_QLORA), _vec_spec(j, C_KVLORA),
                  _full_spec(wn, j), _full_spec(wr, j), _full_spec(wuk, j),
                  tab(C_HEADS * C_ROPE), tab(C_HEADS * C_ROPE), tab(LANES), tab(LANES)],
        out_specs=[pl.BlockSpec((None, C_HEADS, tm, C_QK), lambda i: (i, 0, 0, 0)),
                   _row_spec(tm, C_QK), pl.BlockSpec((C_KVLORA, tm), lambda i: (0, i)),
                   _row_spec(tm, C_KVLORA), _row_spec(tm, C_ROPE)],
        compiler_params=_cparams("arbitrary"),
        name="mla_proj",
    )(x, mod, mod, w_in, g_q, g_kv, wn, wr, wuk, ccq, ssq, cck, ssk)


def _softmax_init(m_ref, l_ref, acc_ref):
    m_ref[...] = jnp.full_like(m_ref, -jnp.inf)
    l_ref[...] = jnp.zeros_like(l_ref)
    acc_ref[...] = jnp.zeros_like(acc_ref)


def _attn_body(qi_ref, kj_ref, last_ref, q_ref, k_ref, kt_ref, o_ref, m_ref, l_ref, acc_ref, *, tq, tk, scale2):
    p_id = pl.program_id(1)
    qi = qi_ref[p_id]
    kj = kj_ref[p_id]

    @pl.when(kj == 0)
    def _():
        _softmax_init(m_ref, l_ref, acc_ref)

    def step(masked):
        k = k_ref[...]
        kt = kt_ref[...]
        if masked:
            key = lax.broadcasted_iota(jnp.int32, (tk, tq), 0) + kj * tk
            tok = lax.broadcasted_iota(jnp.int32, (tk, tq), 1) + qi * tq
            keep = key <= tok
        def scores(hd):
            t = _dot_nt(k, q_ref[hd]) * scale2
            return jnp.where(keep, t, -jnp.inf) if masked else t

        ahead = 2
        queue = [scores(hd) for hd in range(ahead)]
        for hd in range(C_HEADS):
            t = queue.pop(0)
            if hd + ahead < C_HEADS:
                queue.append(scores(hd + ahead))
            m_prev = m_ref[hd]
            m_new = jnp.maximum(m_prev, jnp.max(t, axis=0, keepdims=True))
            alpha = jnp.exp2(m_prev - m_new)
            p = jnp.exp2(t - m_new)
            l_ref[hd] = alpha * l_ref[hd] + jnp.sum(p, axis=0, keepdims=True)
            acc_ref[hd] = alpha * acc_ref[hd] + _dot(kt, p.astype(BF16))
            m_ref[hd] = m_new

    fully_visible = (kj + 1) * tk - 1 <= qi * tq
    pl.when(fully_visible)(lambda: step(False))
    pl.when(jnp.logical_not(fully_visible))(lambda: step(True))

    @pl.when(last_ref[p_id] == 1)
    def _():
        for hd in range(C_HEADS):
            o_ref[hd] = (acc_ref[hd] / l_ref[hd]).astype(BF16)


def _attn_prompt(q, kcat, klat_t, n_seq, seq_len, tq, tk):
    assert seq_len % tq == 0 and seq_len % tk == 0
    nq, nk = seq_len // tq, seq_len // tk
    pairs =[(i, j) for i in range(nq) for j in range((i * tq + tq - 1) // tk + 1)]
    qi = jnp.asarray([p[0] for p in pairs], jnp.int32)
    kj = jnp.asarray([p[1] for p in pairs], jnp.int32)
    last = jnp.asarray([int(n + 1 == len(pairs) or pairs[n + 1][0] != p[0]) for n, p in enumerate(pairs)], jnp.int32)
    scale2 = (C_NOPE + C_ROPE) ** -0.5 * math.log2(math.e)
    grid_spec = pltpu.PrefetchScalarGridSpec(
        num_scalar_prefetch=3,
        grid=(n_seq, len(pairs)),
        in_specs=[pl.BlockSpec((None, C_HEADS, tq, C_QK), lambda b, p, qi, kj, last: (b * nq + qi[p], 0, 0, 0)),
                  pl.BlockSpec((tk, C_QK), lambda b, p, qi, kj, last: (b * nk + kj[p], 0)),
                  pl.BlockSpec((C_KVLORA, tk), lambda b, p, qi, kj, last: (0, b * nk + kj[p]))],
        out_specs=pl.BlockSpec((None, C_HEADS, C_KVLORA, tq), lambda b, p, qi, kj, last: (b * nq + qi[p], 0, 0, 0)),
        scratch_shapes=[pltpu.VMEM((C_HEADS, 1, tq), F32), pltpu.VMEM((C_HEADS, 1, tq), F32),
                        pltpu.VMEM((C_HEADS, C_KVLORA, tq), F32)])
    return pl.pallas_call(
        functools.partial(_attn_body, tq=tq, tk=tk, scale2=scale2),
        out_shape=jax.ShapeDtypeStruct((n_seq * nq, C_HEADS, C_KVLORA, tq), BF16),
        grid_spec=grid_spec,
        compiler_params=_cparams("arbitrary", "arbitrary"),
        name="mla_attention_prompt",
    )(qi, kj, last, q, kcat, klat_t)


def _attn_paged_body(pt_ref, q_ref, nlat_ref, nrope_ref, lat_hbm, rope_hbm, o_ref,
                     lat_buf, rope_buf, sem, m_ref, l_ref, acc_ref, *, nsq, pages, sub, j, seq_new, scale2):
    b = pl.program_id(0)
    g = pl.program_id(1)
    n_groups = pl.num_programs(1)
    n_steps = pl.num_programs(0) * n_groups
    step = b * n_groups + g
    slot = lax.rem(step, PAGED_SLOTS)
    lookahead = PAGED_SLOTS - 1

    def page_copy(kind, page_id, sl, i):
        src, dst = (lat_hbm, lat_buf) if kind == 0 else (rope_hbm, rope_buf)
        return pltpu.make_async_copy(src.at[j, page_id], dst.at[sl, i], sem.at[kind, sl])

    def start_group(bb, gg, sl):
        for i in range(nsq * pages):
            page_id = pt_ref[bb * nsq + i // pages, gg * pages + i % pages]
            page_copy(0, page_id, sl, i).start(priority=i % 2)
            page_copy(1, page_id, sl, i).start(priority=(i + 1) % 2)

    def start_step(s, sl):
        s = jnp.minimum(s, n_steps - 1)
        start_group(s // n_groups, lax.rem(s, n_groups), sl)

    @pl.when(step == 0)
    def _():
        for s in range(lookahead):
            start_step(s, s)

    def wait_group(sl):
        for i in range(nsq * pages):
            page_copy(0, 0, sl, i).wait()
            page_copy(1, 0, sl, i).wait()

    wait_group(slot)

    def update(carry, t, values):
        m_prev, l_prev, acc = carry
        m_new = jnp.maximum(m_prev, jnp.max(t, axis=-1, keepdims=True))
        alpha = jnp.exp2(m_prev - m_new)
        p = jnp.exp2(t - m_new)
        return (m_new, alpha * l_prev + jnp.sum(p, axis=-1, keepdims=True),
                alpha * acc + _dot(p.astype(BF16), values))

    first = g == 0
    page = lat_buf.shape[2]
    qls = [q_ref[s][:, :C_KVLORA] for s in range(nsq)]
    qrs = [q_ref[s][:, C_KVLORA:C_KVLORA + C_ROPE] for s in range(nsq)]
    work = []
    for s in range(nsq):
        for u in range(pages // sub):
            e0 = s * pages + u * sub
            lat = lat_buf[slot, e0:e0 + sub].reshape(sub * page, C_KVLORA).astype(BF16)
            rp_t = jnp.concatenate([rope_buf[slot, e0 + i] for i in range(sub)], axis=1).astype(BF16)
            work.append((s, lat, (_dot_nt(qls[s], lat) + _dot(qrs[s], rp_t)) * scale2))
    start_step(step + lookahead, lax.rem(step + lookahead, PAGED_SLOTS))
    carries = [(jnp.where(first, -jnp.inf, m_ref[s]), jnp.where(first, 0.0, l_ref[s]),
                jnp.where(first, 0.0, acc_ref[s])) for s in range(nsq)]
    for s, lat, t in work:
        carries[s] = update(carries[s], t, lat)

    @pl.when(step == n_steps - 1)
    def _():
        for ahead in range(1, PAGED_SLOTS):
            wait_group(lax.rem(step + ahead, PAGED_SLOTS))

    @pl.when(g < n_groups - 1)
    def _():
        for s in range(nsq):
            m_ref[s], l_ref[s], acc_ref[s] = carries[s]

    @pl.when(g == n_groups - 1)
    def _():
        for s in range(nsq):
            nlat = nlat_ref[s].astype(BF16)
            t2 = (_dot_nt(qls[s], nlat) + _dot_nt(qrs[s], nrope_ref[s].astype(BF16))) * scale2
            tok = lax.broadcasted_iota(jnp.int32, t2.shape, 0) & (seq_new - 1)
            key = lax.broadcasted_iota(jnp.int32, t2.shape, 1)
            _, l_fin, acc = update(carries[s], jnp.where(key <= tok, t2, -jnp.inf), nlat)
            o_ref[s] = (acc / l_fin).astype(BF16)


def _attn_paged(q, new_lat, new_rope, pool_lat, pool_rope_t, page_table, j, seq_new):
    n_seq, n_pages = page_table.shape
    page = pool_lat.shape[2]
    nsq, pages, sub = 4, 16, 4
    assert n_seq % nsq == 0 and n_pages % pages == 0 and seq_new & (seq_new - 1) == 0
    rows = q.shape[1]
    scale2 = (C_NOPE + C_ROPE) ** -0.5 * math.log2(math.e)
    grid_spec = pltpu.PrefetchScalarGridSpec(
        num_scalar_prefetch=1,
        grid=(n_seq // nsq, n_pages // pages),
        in_specs=[pl.BlockSpec((nsq, rows, C_QK), lambda b, s, pt: (b, 0, 0)),
                  pl.BlockSpec((nsq,) + new_lat.shape[1:], lambda b, s, pt: (b, 0, 0)),
                  pl.BlockSpec((nsq,) + new_rope.shape[1:], lambda b, s, pt: (b, 0, 0)),
                  pl.BlockSpec(memory_space=pl.ANY), pl.BlockSpec(memory_space=pl.ANY)],
        out_specs=pl.BlockSpec((nsq, rows, C_KVLORA), lambda b, s, pt: (b, 0, 0)),
        scratch_shapes=[pltpu.VMEM((PAGED_SLOTS, nsq * pages, page, C_KVLORA), F32),
                        pltpu.VMEM((PAGED_SLOTS, nsq * pages, C_ROPE, page), F32),
                        pltpu.SemaphoreType.DMA((2, PAGED_SLOTS)),
                        pltpu.VMEM((nsq, rows, 1), F32), pltpu.VMEM((nsq, rows, 1), F32),
                        pltpu.VMEM((nsq, rows, C_KVLORA), F32)])
    return pl.pallas_call(
        functools.partial(_attn_paged_body, nsq=nsq, pages=pages, sub=sub, j=j, seq_new=seq_new, scale2=scale2),
        out_shape=jax.ShapeDtypeStruct((n_seq, rows, C_KVLORA), BF16),
        grid_spec=grid_spec,
        compiler_params=_cparams("arbitrary", "arbitrary"),
        name="mla_attention_paged",
    )(page_table, q, new_lat, new_rope, pool_lat, pool_rope_t)


def _mla_out_body(x_ref, g_ref, o_ref, wuv_ref, wout_ref, lg_ref, lb_ref, y_ref):
    tm = x_ref.shape[0]
    pieces = 2 if tm % (4 * LANES) == 0 and g_ref.shape[0] == 1 else 1
    rows = [slice(p * tm // pieces, (p + 1) * tm // pieces) for p in range(pieces)]

    def product(r):
        parts = [_dot_tn(o_ref[hd, :, r], wuv_ref[hd]) for hd in range(C_HEADS)]
        return _dot(jnp.concatenate(parts, axis=1).astype(BF16), wout_ref[...])

    outs = [product(r) for r in rows]
    for r, out in zip(rows, outs):
        y_ref[r, :] = _residual_ln(x_ref[r, :], g_ref, out, lg_ref, lb_ref)


def _mla_out(x, mod, tps, layer, j, o_lat, wuv, w_out, ln_g, ln_b, tm):
    T = x.shape[0]
    return pl.pallas_call(
        _mla_out_body,
        out_shape=jax.ShapeDtypeStruct((T, D_MODEL), F32),
        grid=(T // tm,),
        in_specs=[_row_spec(tm, D_MODEL), _mod_spec(mod, layer, 2, tps),
                  pl.BlockSpec((None, C_HEADS, C_KVLORA, tm), lambda i: (i, 0, 0, 0)),
                  _full_spec(wuv, j), _full_spec(w_out, j),
                  _vec_spec(layer, D_MODEL), _vec_spec(layer, D_MODEL)],
        out_specs=_row_spec(tm, D_MODEL),
        compiler_params=_cparams("arbitrary"),
        name="mla_out",
    )(x, mod, o_lat, wuv, w_out, ln_g, ln_b)


def _rope_tables(pos, reps, width):
    half = C_ROPE // 2
    inv = ROPE_THETA ** (-jnp.arange(half, dtype=F32) / half)
    ang = pos.astype(F32)[:, None] * inv
    cos, sin = jnp.cos(ang), jnp.sin(ang)
    cc = jnp.tile(jnp.concatenate([cos, cos], axis=1), (1, reps))
    ss = jnp.tile(jnp.concatenate([-sin, sin], axis=1), (1, reps))
    pad = width - cc.shape[1]
    return jnp.pad(cc, ((0, 0), (0, pad))), jnp.pad(ss, ((0, 0), (0, pad)))


def _prepare_params(p):
    vec = lambda a: a.reshape(a.shape[0], 1, a.shape[1])
    w_uq = p['c_w_uq']
    n_c = w_uq.shape[0]
    c_w_in = jnp.pad(p['c_w_in'], ((0, 0), (0, 0), (0, LANES - C_ROPE)))
    lb_all = jnp.cumsum(jax.nn.softmax(p['b_lb'].astype(F32), axis=0), axis=0)
    lb_all = lb_all - lb_all[:1]
    return dict(
        ln1_g=vec(p['ln1_g']), ln1_b=vec(p['ln1_b']), ln2_g=vec(p['ln2_g']), ln2_b=vec(p['ln2_b']),
        ffn_w1=p['ffn_w1'].astype(BF16), ffn_w2=p['ffn_w2'].astype(BF16),
        a_w_in=p['a_w_in'].astype(BF16), a_ln_g=vec(p['a_ln_g']), a_ln_b=vec(p['a_ln_b']),
        a_w_out=p['a_w_out'].astype(BF16),
        b_w_in=p['b_w_in'].astype(BF16), b_w_out=p['b_w_out'].astype(BF16), lb_all=lb_all,
        c_w_in=c_w_in.astype(BF16), c_g_q=vec(p['c_g_q']), c_g_kv=vec(p['c_g_kv']),
        c_wn=w_uq[..., :C_NOPE].reshape(n_c, C_QLORA, C_HEADS * C_NOPE).astype(BF16),
        c_wr=w_uq[..., C_NOPE:].reshape(n_c, C_QLORA, C_HEADS * C_ROPE).astype(BF16),
        c_wuk=jnp.transpose(p['c_w_uk'], (0, 2, 3, 1)).astype(BF16),
        c_wuv=jnp.transpose(p['c_w_uv'], (0, 2, 1, 3)).astype(BF16),
        c_w_out=p['c_w_out'].astype(BF16),
    )


def _sgu_mixing(w_s, b_s, chunk):
    reps = CHUNK_A // chunk
    causal = jnp.tril(jnp.ones((chunk, chunk), dtype=bool))
    ws = jnp.where(causal, w_s[:, :, :chunk, :chunk], 0)
    eye = jnp.eye(reps, dtype=w_s.dtype)
    ws = jnp.einsum('ab,jgts->jgatbs', eye, ws).reshape(w_s.shape[0], A_GROUPS, CHUNK_A, CHUNK_A)
    bias = jnp.tile(jnp.transpose(b_s[:, :, :chunk], (0, 2, 1)), (1, reps, 1))
    bias = jnp.repeat(bias, A_GDIM, axis=2)
    return ws.astype(BF16), bias


def _run_trunk(x, mod, n_seq, seq_len, q_pos, hgrn_state0, mla_cache, prm, raw, tm):
    T = x.shape[0]
    assert T % tm == 0 and (tm % seq_len == 0 or seq_len % tm == 0) and tm % CHUNK_A == 0
    per_seq_mod = mod.ndim == 5
    tps = (seq_len // tm) if per_seq_mod else 1
    sgu_chunk = min(CHUNK_A, seq_len)
    ws, bias = _sgu_mixing(raw['a_w_s'], raw['a_b_s'], sgu_chunk)
    chunk_v, hgrn_states, lat_rows, rope_rows = [], [], [], []
    for i in range(DEPTH):
        kind, j = i % N_MIXERS, i // N_MIXERS
        if kind == 0:
            x, v_rows = _sgu_layer(x, mod, tps, i, j, prm['a_w_in'], prm['a_ln_g'], prm['a_ln_b'], ws, bias,
                                   prm['a_w_out'], prm['ln1_g'], prm['ln1_b'], tm, emit_v=mla_cache is not None)
            chunk_v.append(v_rows)
        elif kind == 1:
            lb = prm['lb_all'][i].reshape(1, D_MODEL)
            q, k, lf, v, gs = _hgrn_proj(x, mod, tps, i, j, prm['b_w_in'], lb, tm)
            if seq_len % 64 == 0:
                C, lpad, hb = 64, seq_len, 8
                rec_tm = min(seq_len, 512)
                rec_in = (q, k, lf, v)
            else:
                C = lpad = rec_tm = SUBLANES
                hb = B_HEADS
                padseq = lambda a: jnp.pad(a.reshape(n_seq, seq_len, D_MODEL),
                                           ((0, 0), (0, lpad - seq_len), (0, 0))).reshape(n_seq * lpad, D_MODEL)
                rec_in = tuple(padseq(a) for a in (q, k, lf, v))
            s0 = None if hgrn_state0 is None else hgrn_state0[j]
            o, S = _hgrn_rec(*rec_in, s0, n_seq, lpad, rec_tm, C, hb)
            if lpad != seq_len:
                o = o.reshape(n_seq, lpad, D_MODEL)[:, :seq_len].reshape(T, D_MODEL)
            hgrn_states.append(S)
            x = _hgrn_out(x, mod, tps, i, j, o, gs, prm['b_w_out'], prm['ln1_g'], prm['ln1_b'], tm)
        else:
            tq = min(tm, 512)
            tps_q = (seq_len // tq) if per_seq_mod else 1
            pos_rows = q_pos if tq <= seq_len else jnp.tile(q_pos, tq // seq_len)
            ccq, ssq = _rope_tables(pos_rows, C_HEADS, C_HEADS * C_ROPE)
            cck, ssk = _rope_tables(pos_rows, 1, LANES)
            qcat, kcat, klat_t, lat, kr = _mla_proj(x, mod, tps_q, pos_rows.shape[0] // tq, i, j, prm['c_w_in'],
                                            prm['c_g_q'], prm['c_g_kv'], prm['c_wn'], prm['c_wr'], prm['c_wuk'],
                                            ccq, ssq, cck, ssk, tq)
            if mla_cache is None:
                o_lat = _attn_prompt(qcat, kcat, klat_t, n_seq, seq_len, tq, min(seq_len, 512))
            else:
                pool_lat, pool_rope_t, pt = mla_cache
                qs = qcat.reshape(C_HEADS, n_seq, seq_len, C_QK).transpose(1, 0, 2, 3)
                qs = qs.reshape(n_seq, C_HEADS * seq_len, C_QK)
                padk = lambda a: jnp.pad(a.reshape(n_seq, seq_len, a.shape[1]), ((0, 0), (0, 16 - seq_len), (0, 0)))
                o_s = _attn_paged(qs, padk(lat), padk(kr), pool_lat, pool_rope_t, pt, j, seq_len)
                o_lat = o_s.reshape(n_seq, C_HEADS, seq_len, C_KVLORA).transpose(1, 3, 0, 2)
                o_lat = o_lat.reshape(1, C_HEADS, C_KVLORA, T)
            x = _mla_out(x, mod, tps_q, i, j, o_lat, prm['c_wuv'], prm['c_w_out'], prm['ln1_g'], prm['ln1_b'], tq)
            lat_rows.append(lat.reshape(n_seq, seq_len, C_KVLORA))
            rope_rows.append(kr.reshape(n_seq, seq_len, C_ROPE))
        x = _ffn_layer(x, mod, tps, i, prm['ffn_w1'], prm['ffn_w2'], prm['ln2_g'], prm['ln2_b'], tm)
    stack = lambda xs: jnp.stack(xs) if xs and xs[0] is not None else None
    return x, stack(chunk_v), jnp.stack(hgrn_states), jnp.stack(lat_rows), jnp.stack(rope_rows)


def kernel(x_prompt, x_sample, cache_kv_latent, cache_k_rope, state_hgrn, page_table, c_prompt, c_sample,
           w_ada, b_ada, ln1_g, ln1_b, ln2_g, ln2_b, ffn_w1, ffn_w2, a_w_in, a_ln_g, a_ln_b, a_w_s, a_b_s,
           a_w_out, b_w_in, b_lb, b_w_out, c_w_in, c_g_q, c_g_kv, c_w_uq, c_w_uk, c_w_uv, c_w_out):
    raw = dict(ln1_g=ln1_g, ln1_b=ln1_b, ln2_g=ln2_g, ln2_b=ln2_b, ffn_w1=ffn_w1, ffn_w2=ffn_w2,
               a_w_in=a_w_in, a_ln_g=a_ln_g, a_ln_b=a_ln_b, a_w_s=a_w_s, a_b_s=a_b_s, a_w_out=a_w_out,
               b_w_in=b_w_in, b_lb=b_lb, b_w_out=b_w_out, c_w_in=c_w_in, c_g_q=c_g_q, c_g_kv=c_g_kv,
               c_w_uq=c_w_uq, c_w_uk=c_w_uk, c_w_uv=c_w_uv, c_w_out=c_w_out)
    prm = _prepare_params(raw)
    nb, seq, d = x_prompt.shape
    ns, sseq, _ = x_sample.shape
    past_len = page_table.shape[1] * cache_kv_latent.shape[2]
    pos_prompt = jnp.arange(seq, dtype=jnp.int32)
    pos_sample = past_len + jnp.arange(sseq, dtype=jnp.int32)

    mod_p, mod_s = _modulation(c_prompt, jnp.repeat(c_sample, sseq, axis=0), w_ada, b_ada)
    mod_p = mod_p.reshape(DEPTH, nb, 6, 1, d)

    tm_p = 512
    y_p, _, hs_p, lat_p, rope_p = _run_trunk(x_prompt.reshape(nb * seq, d), mod_p, nb, seq, pos_prompt,
                                             None, None, prm, raw, tm_p)
    y_s, v_s, hs_s, lat_s, rope_s = _run_trunk(x_sample.reshape(ns * sseq, d), mod_s, ns, sseq, pos_sample,
                                               state_hgrn,
                                               (cache_kv_latent, jnp.swapaxes(cache_k_rope, 2, 3), page_table),
                                               prm, raw, ns * sseq)
    return (y_p.reshape(nb, seq, d), y_s.reshape(ns, sseq, d), hs_p, hs_s, lat_p, rope_p, lat_s, rope_s,
            v_s.reshape(v_s.shape[0], ns, sseq, d))
```

```python
import functools
import math

import jax
import jax.numpy as jnp
from jax import lax
from jax.experimental import pallas as pl
from jax.experimental.pallas import tpu as pltpu

F32 = jnp.float32
BF16 = jnp.bfloat16

D_MODEL = 1024
DEPTH = 4
N_MIXERS = 3
CHUNK_A = 128
A_GROUPS = 8
A_GDIM = D_MODEL // A_GROUPS
B_HEADS = 8
B_DK = 128
B_DV = D_MODEL // B_HEADS
C_HEADS = 8
C_NOPE = 128
C_ROPE = 64
C_V = 128
C_QLORA = 512
C_KVLORA = 256
ROPE_THETA = 10000.0
D_FF = 4 * D_MODEL
ALPHA = (2.0 * DEPTH) ** 0.25
EPS = 1e-6

LANES = 128
SUBLANES = 8
C_QK = C_KVLORA + LANES
VMEM_LIMIT = 56 * 1024 * 1024
PAGED_SLOTS = 3


def _cparams(*sem):
    return pltpu.CompilerParams(dimension_semantics=sem, vmem_limit_bytes=VMEM_LIMIT)


def _dot(a, b):
    return jnp.dot(a, b, preferred_element_type=F32)


def _dot_nt(a, b):
    return lax.dot_general(a, b, (((1,), (1,)), ((), ())), preferred_element_type=F32)


def _dot_tn(a, b):
    return lax.dot_general(a, b, (((0,), (0,)), ((), ())), preferred_element_type=F32)


def _layer_norm(y, g, b):
    mu = jnp.mean(y, axis=-1, keepdims=True)
    yc = y - mu
    var = jnp.mean(yc * yc, axis=-1, keepdims=True)
    return yc * lax.rsqrt(var + EPS) * g + b


def _rms(y):
    return y * lax.rsqrt(jnp.mean(y * y, axis=-1, keepdims=True) + EPS)


def _silu(x):
    return x * jax.nn.sigmoid(x)


def _gelu_tanh(x):
    c = math.sqrt(2.0 / math.pi)
    half = 0.5 * x
    return half + half * jnp.tanh(x * (c + (c * 0.044715) * (x * x)))


def _modulate(x, sh_ref, sc_ref):
    return x * (1.0 + sc_ref[...]) + sh_ref[...]


def _residual_ln(x, gate_ref, out, lg_ref, lb_ref):
    return _layer_norm(ALPHA * x + gate_ref[...] * out, lg_ref[...], lb_ref[...])


def _mod_spec(mod, layer, which, tiles_per_seq):
    if mod.ndim == 5:
        return pl.BlockSpec((None, None, None, 1, D_MODEL), lambda i: (layer, i // tiles_per_seq, which, 0, 0))
    return pl.BlockSpec((None, mod.shape[1], D_MODEL), lambda i: (layer, 0, which))


def _vec_spec(layer, width):
    return pl.BlockSpec((None, 1, width), lambda i: (layer, 0, 0))


def _full_spec(arr, layer=None):
    if layer is None:
        nd = arr.ndim
        return pl.BlockSpec(arr.shape, lambda i: (0,) * nd)
    nd = arr.ndim - 1
    return pl.BlockSpec((None,) + arr.shape[1:], lambda i: (layer,) + (0,) * nd)


def _row_spec(tm, width):
    return pl.BlockSpec((tm, width), lambda i: (i, 0))


def _mod_body(ca_ref, cb_ref, w_ref, b_ref, oa_ref, ob_ref):
    w = w_ref[...].astype(BF16)
    oa_ref[...] = _dot(_silu(ca_ref[...]).astype(BF16), w) + b_ref[...]
    ob_ref[...] = _dot(_silu(cb_ref[...]).astype(BF16), w) + b_ref[...]


def _modulation(c_a, c_b, w_ada, b_ada):
    tn = 1536
    width = w_ada.shape[2]
    rows = lambda c: pl.BlockSpec((c.shape[0], D_MODEL), lambda l, j: (0, 0))
    out = lambda c: pl.BlockSpec((None, c.shape[0], tn), lambda l, j: (l, 0, j))
    return pl.pallas_call(
        _mod_body,
        out_shape=[jax.ShapeDtypeStruct((DEPTH, c.shape[0], width), F32) for c in (c_a, c_b)],
        grid=(DEPTH, width // tn),
        in_specs=[rows(c_a), rows(c_b),
                  pl.BlockSpec((None, D_MODEL, tn), lambda l, j: (l, 0, j)),
                  pl.BlockSpec((None, 1, tn), lambda l, j: (l, 0, j))],
        out_specs=[out(c_a), out(c_b)],
        compiler_params=_cparams("arbitrary", "arbitrary"),
        name="adaln_modulation",
    )(c_a, c_b, w_ada, b_ada.reshape(DEPTH, 1, width))


def _ffn_body(x_ref, sh_ref, sc_ref, g_ref, w1_ref, w2_ref, lg_ref, lb_ref, o_ref, acc_ref, *, fc):
    x = x_ref[...]
    h = _modulate(x, sh_ref, sc_ref).astype(BF16)
    for c in range(D_FF // fc):
        a = _dot(h, w1_ref[:, c * fc:(c + 1) * fc])
        a = jnp.square(jnp.maximum(a, 0.0)).astype(BF16)
        d = _dot(a, w2_ref[c * fc:(c + 1) * fc, :])
        if c == 0:
            acc_ref[...] = d
        else:
            acc_ref[...] += d
    o_ref[...] = _residual_ln(x, g_ref, acc_ref[...], lg_ref, lb_ref)


def _ffn_stream_body(x_ref, sh_ref, sc_ref, g_ref, w1_ref, w2_ref, lg_ref, lb_ref, o_ref, acc_ref):
    c = pl.program_id(0)
    x = x_ref[...]
    h = _modulate(x, sh_ref, sc_ref).astype(BF16)
    a = jnp.square(jnp.maximum(_dot(h, w1_ref[...]), 0.0)).astype(BF16)
    d = _dot(a, w2_ref[...])

    @pl.when(c == 0)
    def _():
        acc_ref[...] = d

    @pl.when(c > 0)
    def _():
        acc_ref[...] += d

    @pl.when(c == pl.num_programs(0) - 1)
    def _():
        o_ref[...] = _residual_ln(x, g_ref, acc_ref[...], lg_ref, lb_ref)


def _ffn_layer(x, mod, tps, layer, w1, w2, ln_g, ln_b, tm):
    T = x.shape[0]
    if T == tm and mod.ndim == 3:
        fc = 1024
        return pl.pallas_call(
            _ffn_stream_body,
            out_shape=jax.ShapeDtypeStruct((T, D_MODEL), F32),
            grid=(D_FF // fc,),
            in_specs=[pl.BlockSpec((tm, D_MODEL), lambda c: (0, 0)),
                      _mod_spec(mod, layer, 3, tps), _mod_spec(mod, layer, 4, tps), _mod_spec(mod, layer, 5, tps),
                      pl.BlockSpec((None, D_MODEL, fc), lambda c: (layer, 0, c)),
                      pl.BlockSpec((None, fc, D_MODEL), lambda c: (layer, c, 0)),
                      _vec_spec(layer, D_MODEL), _vec_spec(layer, D_MODEL)],
            out_specs=pl.BlockSpec((tm, D_MODEL), lambda c: (0, 0)),
            scratch_shapes=[pltpu.VMEM((tm, D_MODEL), F32)],
            compiler_params=_cparams("arbitrary"),
            name="ffn_sublayer_stream",
        )(x, mod, mod, mod, w1, w2, ln_g, ln_b)
    return pl.pallas_call(
        functools.partial(_ffn_body, fc=1024),
        out_shape=jax.ShapeDtypeStruct((T, D_MODEL), F32),
        grid=(T // tm,),
        in_specs=[_row_spec(tm, D_MODEL),
                  _mod_spec(mod, layer, 3, tps), _mod_spec(mod, layer, 4, tps), _mod_spec(mod, layer, 5, tps),
                  _full_spec(w1, layer), _full_spec(w2, layer),
                  _vec_spec(layer, D_MODEL), _vec_spec(layer, D_MODEL)],
        out_specs=_row_spec(tm, D_MODEL),
        scratch_shapes=[pltpu.VMEM((tm, D_MODEL), F32)],
        compiler_params=_cparams("arbitrary"),
        name="ffn_sublayer",
    )(x, mod, mod, mod, w1, w2, ln_g, ln_b)


def _sgu_body(x_ref, sh_ref, sc_ref, g_ref, win_ref, lng_ref, lnb_ref, ws_ref, bias_ref, wout_ref,
              lg_ref, lb_ref, o_ref, *rest, tm, pieces, emit_v):
    if emit_v:
        v_ref, gated_ref = rest
    else:
        (gated_ref,) = rest
    rows = [slice(p * tm // pieces, (p + 1) * tm // pieces) for p in range(pieces)]
    per_row = lambda ref, r: ref[...] if ref.shape[0] == 1 else ref[r, :]

    def in_products(r):
        h = (x_ref[r, :] * (1.0 + per_row(sc_ref, r)) + per_row(sh_ref, r)).astype(BF16)
        return _dot(h, win_ref[:, :D_MODEL]), _dot(h, win_ref[:, D_MODEL:])

    def gate(r, zu, zv):
        u = _gelu_tanh(zu)
        v = _layer_norm(_gelu_tanh(zv), lng_ref[...], lnb_ref[...])
        if emit_v:
            v_ref[r, :] = v
        vb = v.astype(BF16)
        nch = (r.stop - r.start) // CHUNK_A
        for g in range(A_GROUPS):
            c = slice(g * A_GDIM, (g + 1) * A_GDIM)
            rhs = jnp.concatenate([vb[n * CHUNK_A:(n + 1) * CHUNK_A, c] for n in range(nch)], axis=1)
            mixed = _dot(ws_ref[g], rhs)
            for n in range(nch):
                rn = slice(n * CHUNK_A, (n + 1) * CHUNK_A)
                gated = u[rn, c] * (mixed[:, n * A_GDIM:(n + 1) * A_GDIM] + bias_ref[:, c])
                gated_ref[r.start + n * CHUNK_A:r.start + (n + 1) * CHUNK_A, c] = gated.astype(BF16)
        return _dot(gated_ref[r, :], wout_ref[...])

    def finish(r, out):
        y = ALPHA * x_ref[r, :] + per_row(g_ref, r) * out
        o_ref[r, :] = _layer_norm(y, lg_ref[...], lb_ref[...])

    z = in_products(rows[0])
    outs = []
    for p, r in enumerate(rows):
        z_next = in_products(rows[p + 1]) if p + 1 < pieces else None
        outs.append(gate(r, *z))
        z = z_next
        if p > 0:
            finish(rows[p - 1], outs[p - 1])
    finish(rows[-1], outs[-1])


def _sgu_layer(x, mod, tps, layer, j, w_in, ln_g, ln_b, ws, bias, w_out, ln1_g, ln1_b, tm, emit_v):
    T = x.shape[0]
    out_shape = [jax.ShapeDtypeStruct((T, D_MODEL), F32)]
    out_specs = [_row_spec(tm, D_MODEL)]
    if emit_v:
        out_shape.append(jax.ShapeDtypeStruct((T, D_MODEL), F32))
        out_specs.append(_row_spec(tm, D_MODEL))
    res = pl.pallas_call(
        functools.partial(_sgu_body, tm=tm, pieces=2 if tm >= 4 * CHUNK_A else 1, emit_v=emit_v),
        out_shape=out_shape,
        grid=(T // tm,),
        in_specs=[_row_spec(tm, D_MODEL),
                  _mod_spec(mod, layer, 0, tps), _mod_spec(mod, layer, 1, tps), _mod_spec(mod, layer, 2, tps),
                  _full_spec(w_in, j), _vec_spec(j, D_MODEL), _vec_spec(j, D_MODEL),
                  _full_spec(ws, j), _full_spec(bias, j), _full_spec(w_out, j),
                  _vec_spec(layer, D_MODEL), _vec_spec(layer, D_MODEL)],
        out_specs=out_specs,
        scratch_shapes=[pltpu.VMEM((tm, D_MODEL), BF16)],
        compiler_params=_cparams("arbitrary"),
        name="sgu_sublayer",
    )(x, mod, mod, mod, w_in, ln_g, ln_b, ws, bias, w_out, ln1_g, ln1_b)
    return (res[0], res[1]) if emit_v else (res[0], None)


def _hgrn_proj_body(x_ref, sh_ref, sc_ref, win_ref, lb_ref, q_ref, k_ref, lf_ref, v_ref, gs_ref):
    d = D_MODEL
    tm = x_ref.shape[0]
    pieces = 4 if tm % (4 * LANES) == 0 and sh_ref.shape[0] == 1 else 1
    rows = [slice(p * tm // pieces, (p + 1) * tm // pieces) for p in range(pieces)]

    def products(r):
        h = _modulate(x_ref[r, :], sh_ref, sc_ref).astype(BF16)
        return [_dot(h, win_ref[:, part * d:(part + 1) * d]) for part in (1, 0, 2, 3)]

    def gate(r, fz, zq, zv, zg):
        lb = lb_ref[...]
        e = jnp.exp(-jnp.abs(fz))
        a = jnp.log(lb)
        b = jnp.log1p(-lb) + (jnp.minimum(fz, 0.0) - jnp.log1p(e))
        lf_ref[r, :] = (jnp.maximum(a, b) + jnp.log1p(jnp.exp(-jnp.abs(a - b)))) * math.log2(math.e)
        k_ref[r, :] = (1.0 - lb) * (jnp.where(fz >= 0.0, e, 1.0) / (1.0 + e))
        q_ref[r, :] = _silu(zq)
        v_ref[r, :] = zv
        gs_ref[r, :] = _silu(zg)

    z = [products(r) for r in rows]
    for r, zr in zip(rows, z):
        gate(r, *zr)


def _hgrn_proj(x, mod, tps, layer, j, w_in, lb, tm):
    T = x.shape[0]
    shp = jax.ShapeDtypeStruct((T, D_MODEL), F32)
    return pl.pallas_call(
        _hgrn_proj_body,
        out_shape=[shp] * 5,
        grid=(T // tm,),
        in_specs=[_row_spec(tm, D_MODEL), _mod_spec(mod, layer, 0, tps), _mod_spec(mod, layer, 1, tps),
                  _full_spec(w_in, j), _full_spec(lb)],
        out_specs=[_row_spec(tm, D_MODEL)] * 5,
        compiler_params=_cparams("arbitrary"),
        name="hgrn_proj",
    )(x, mod, mod, w_in, lb)


def _hgrn_rec_body(*refs, C, nchunk, hb, has_s0):
    if has_s0:
        q_ref, k_ref, g_ref, v_ref, s0_ref, o_ref, sout_ref, st_ref = refs
    else:
        q_ref, k_ref, g_ref, v_ref, o_ref, sout_ref, st_ref = refs
    t = pl.program_id(2)

    @pl.when(t == 0)
    def _():
        for hd in range(hb):
            st_ref[hd] = s0_ref[hd].T if has_s0 else jnp.zeros((B_DV, B_DK), F32)

    row = lax.broadcasted_iota(jnp.int32, (C, B_DK), 0)
    row_a = lax.broadcasted_iota(jnp.int32, (C, C), 0)
    col_a = lax.broadcasted_iota(jnp.int32, (C, C), 1)
    tri = jnp.where(row_a >= col_a, 1.0, 0.0).astype(BF16)
    sub = row & (SUBLANES - 1)
    levels = []
    for m in (1, 2, 4, 8, 16, 32, 64, 128):
        if 2 * m <= C:
            shift = int(math.log2(2 * m))
            pair = (((row_a >> shift) == (col_a >> shift)) & ((row_a & (2 * m - 1)) >= m)
                    & ((col_a & (2 * m - 1)) < m))
            levels.append((m, (row & (2 * m - 1)) >= m, pair))

    def prefix(c, hd):
        rows = slice(c * C, (c + 1) * C)
        cols = slice(hd * B_DK, (hd + 1) * B_DK)
        gc = g_ref[rows, cols]
        g_hi = gc.astype(BF16)
        r1 = gc - g_hi.astype(F32)
        g_mid = r1.astype(BF16)
        g_lo = (r1 - g_mid.astype(F32)).astype(BF16)
        b3 = _dot(tri, jnp.concatenate([g_hi, g_mid, g_lo], axis=1))
        return rows, cols, b3

    def products(pre, st):
        rows, cols, b3 = pre
        qc = q_ref[rows, cols]
        kc = k_ref[rows, cols]
        b = b3[:, 0:B_DK] + b3[:, B_DK:2 * B_DK] + b3[:, 2 * B_DK:3 * B_DK]
        o_inter = _dot_nt((qc * jnp.exp2(b)).astype(BF16), st.astype(BF16))
        b3d = b.reshape(C // SUBLANES, SUBLANES, B_DK)
        sub_row = lambda i: jnp.broadcast_to(b3d[:, i:i + 1, :], b3d.shape).reshape(C, B_DK)
        level_dots = []
        for m, upper, pair in levels:
            if m == 1:
                bref = jnp.where(upper, pltpu.roll(b3d, 1, 1).reshape(C, B_DK), b)
            elif m == 2:
                bref = jnp.where(sub < 4, sub_row(1), sub_row(5))
            elif m == 4:
                bref = sub_row(3)
            else:
                bref = jnp.concatenate(
                    [jnp.broadcast_to(b[i * 2 * m + m - 1:i * 2 * m + m, :], (2 * m, B_DK))
                     for i in range(C // (2 * m))], axis=0)
            q_up = jnp.where(upper, qc * jnp.exp2(b - bref), 0.0)
            k_lo = jnp.where(upper, 0.0, kc * jnp.exp2(bref - b))
            level_dots.append(_dot_nt(q_up.astype(BF16), k_lo.astype(BF16)))
        diag = jnp.sum(qc * kc, axis=1, keepdims=True)
        b_last = b[C - 1:C, :]
        k_dec = (kc * jnp.exp2(b_last - b)).astype(BF16)
        return rows, cols, o_inter, level_dots, diag, k_dec, jnp.exp2(b_last)

    def finish(prod, st):
        rows, cols, o_inter, level_dots, diag, k_dec, decay = prod
        vb = v_ref[rows, cols].astype(BF16)
        a_mat = jnp.where(row_a == col_a, diag, 0.0)
        for (m, upper, pair), a_m in zip(levels, level_dots):
            a_mat = jnp.where(pair, a_m, a_mat)
        o_ref[rows, cols] = o_inter + _dot(a_mat.astype(BF16), vb)
        return st * decay + _dot_tn(vb, k_dec)

    states = [st_ref[hd] for hd in range(hb)]
    pre = [prefix(0, hd) for hd in range(hb)]
    for c in range(nchunk):
        prods = [products(pre[hd], states[hd]) for hd in range(hb)]
        if c + 1 < nchunk:
            pre = [prefix(c + 1, hd) for hd in range(hb)]
        states = [finish(prods[hd], states[hd]) for hd in range(hb)]
    for hd in range(hb):
        st_ref[hd] = states[hd]

    @pl.when(t == pl.num_programs(2) - 1)
    def _():
        for hd in range(hb):
            sout_ref[hd] = states[hd].T


def _hgrn_rec(q, k, lf, v, s0, n_seq, seq_len, tm, C, hb):
    T = q.shape[0]
    nt = seq_len // tm
    blk = pl.BlockSpec((tm, hb * B_DK), lambda b, h, t: (b * nt + t, h))
    st_spec = pl.BlockSpec((None, hb, B_DK, B_DV), lambda b, h, t: (b, h, 0, 0))
    has_s0 = s0 is not None
    in_specs = [blk] * 4 + ([st_spec] if has_s0 else [])
    args = (q, k, lf, v) + ((s0,) if has_s0 else ())
    return pl.pallas_call(
        functools.partial(_hgrn_rec_body, C=C, nchunk=tm // C, hb=hb, has_s0=has_s0),
        out_shape=[jax.ShapeDtypeStruct((T, D_MODEL), F32),
                   jax.ShapeDtypeStruct((n_seq, B_HEADS, B_DK, B_DV), F32)],
        grid=(n_seq, B_HEADS // hb, nt),
        in_specs=in_specs,
        out_specs=[blk, st_spec],
        scratch_shapes=[pltpu.VMEM((hb, B_DV, B_DK), F32)],
        compiler_params=_cparams("arbitrary", "arbitrary", "arbitrary"),
        name="hgrn_recurrence",
    )(*args)


def _hgrn_out_body(x_ref, g_ref, o_ref, gs_ref, wout_ref, lg_ref, lb_ref, y_ref):
    o = o_ref[...]
    parts = [_rms(o[:, h * B_DV:(h + 1) * B_DV]) for h in range(B_HEADS)]
    y = (jnp.concatenate(parts, axis=1) * gs_ref[...]).astype(BF16)
    y_ref[...] = _residual_ln(x_ref[...], g_ref, _dot(y, wout_ref[...]), lg_ref, lb_ref)


def _hgrn_out(x, mod, tps, layer, j, o, gs, w_out, ln_g, ln_b, tm):
    T = x.shape[0]
    return pl.pallas_call(
        _hgrn_out_body,
        out_shape=jax.ShapeDtypeStruct((T, D_MODEL), F32),
        grid=(T // tm,),
        in_specs=[_row_spec(tm, D_MODEL), _mod_spec(mod, layer, 2, tps),
                  _row_spec(tm, D_MODEL), _row_spec(tm, D_MODEL),
                  _full_spec(w_out, j), _vec_spec(layer, D_MODEL), _vec_spec(layer, D_MODEL)],
        out_specs=_row_spec(tm, D_MODEL),
        compiler_params=_cparams("arbitrary"),
        name="hgrn_out",
    )(x, mod, o, gs, w_out, ln_g, ln_b)


def _rope_lanes(x, cc_ref, ss_ref, period_first_half):
    n = x.shape[1]
    half = C_ROPE // 2
    rot = jnp.where(period_first_half, pltpu.roll(x, n - half, 1), pltpu.roll(x, half, 1))
    return x * cc_ref[...] + rot * ss_ref[...]


def _mla_proj_body(x_ref, sh_ref, sc_ref, win_ref, gq_ref, gkv_ref, wn_ref, wr_ref, wuk_ref,
                   ccq_ref, ssq_ref, cck_ref, ssk_ref, q_ref, kcat_ref, klt_ref, lat_ref, kr_ref):
    tm = x_ref.shape[0]
    pieces = 2 if tm % (4 * LANES) == 0 and sh_ref.shape[0] == 1 else 1
    rows = [slice(p * tm // pieces, (p + 1) * tm // pieces) for p in range(pieces)]

    def down(r):
        h = _modulate(x_ref[r, :], sh_ref, sc_ref).astype(BF16)
        return _dot(h, win_ref[...])

    def keys_and_queries(r, a):
        cq = (_rms(a[:, :C_QLORA]) * gq_ref[...]).astype(BF16)
        ckv = _rms(a[:, C_QLORA:C_QLORA + C_KVLORA]) * gkv_ref[...]
        kr_slab = a[:, C_QLORA + C_KVLORA:]
        lane_k = lax.broadcasted_iota(jnp.int32, kr_slab.shape, 1)
        kr_slab = _rope_lanes(kr_slab, cck_ref.at[r, :], ssk_ref.at[r, :], (lane_k & (C_ROPE - 1)) < C_ROPE // 2)
        lat_ref[r, :] = ckv
        kr_ref[r, :] = kr_slab[:, :C_ROPE]
        kcat_ref[r, :] = jnp.concatenate([ckv, kr_slab], axis=1).astype(BF16)
        klt_ref[:, r] = ckv.T.astype(BF16)
        return _dot(cq, wn_ref[...]), _dot(cq, wr_ref[...])

    def absorb(r, qn, qr):
        qn = qn.astype(BF16)
        lane_q = lax.broadcasted_iota(jnp.int32, qr.shape, 1)
        qr = _rope_lanes(qr, ccq_ref.at[r, :], ssq_ref.at[r, :], (lane_q & (C_ROPE - 1)) < C_ROPE // 2).astype(BF16)
        zeros = jnp.zeros((qr.shape[0], LANES - C_ROPE), BF16)
        for hd in range(C_HEADS):
            ql = _dot(qn[:, hd * C_NOPE:(hd + 1) * C_NOPE], wuk_ref[hd]).astype(BF16)
            q_ref[hd, r, :] = jnp.concatenate([ql, qr[:, hd * C_ROPE:(hd + 1) * C_ROPE], zeros], axis=1)

    downs = [down(r) for r in rows]
    ups = [keys_and_queries(r, a) for r, a in zip(rows, downs)]
    for r, (qn, qr) in zip(rows, ups):
        absorb(r, qn, qr)


def _mla_proj(x, mod, tps, tab_tiles, layer, j, w_in, g_q, g_kv, wn, wr, wuk, ccq, ssq, cck, ssk, tm):
    T = x.shape[0]
    nt = T // tm
    tab = lambda w: pl.BlockSpec((tm, w), lambda i: (i % tab_tiles, 0))
    return pl.pallas_call(
        _mla_proj_body,
        out_shape=[jax.ShapeDtypeStruct((nt, C_HEADS, tm, C_QK), BF16),
                   jax.ShapeDtypeStruct((T, C_QK), BF16),
                   jax.ShapeDtypeStruct((C_KVLORA, T), BF16),
                   jax.ShapeDtypeStruct((T, C_KVLORA), F32),
                   jax.ShapeDtypeStruct((T, C_ROPE), F32)],
        grid=(nt,),
        in_specs=[_row_spec(tm, D_MODEL), _mod_spec(mod, layer, 0, tps), _mod_spec(mod, layer, 1, tps),
                  _full_spec(w_in, j), _vec_spec(j, C_QLORA), _vec_spec(j, C_KVLORA),
                  _full_spec(wn, j), _full_spec(wr, j), _full_spec(wuk, j),
                  tab(C_HEADS * C_ROPE), tab(C_HEADS * C_ROPE), tab(LANES), tab(LANES)],
        out_specs=[pl.BlockSpec((None, C_HEADS, tm, C_QK), lambda i: (i, 0, 0, 0)),
                   _row_spec(tm, C_QK), pl.BlockSpec((C_KVLORA, tm), lambda i: (0, i)),
                   _row_spec(tm, C_KVLORA), _row_spec(tm, C_ROPE)],
        compiler_params=_cparams("arbitrary"),
        name="mla_proj",
    )(x, mod, mod, w_in, g_q, g_kv, wn, wr, wuk, ccq, ssq, cck, ssk)


def _softmax_init(m_ref, l_ref, acc_ref):
    m_ref[...] = jnp.full_like(m_ref, -jnp.inf)
    l_ref[...] = jnp.zeros_like(l_ref)
    acc_ref[...] = jnp.zeros_like(acc_ref)


def _attn_body(qi_ref, kj_ref, last_ref, q_ref, k_ref, kt_ref, o_ref, m_ref, l_ref, acc_ref, *, tq, tk, scale2):
    p_id = pl.program_id(1)
    qi = qi_ref[p_id]
    kj = kj_ref[p_id]

    @pl.when(kj == 0)
    def _():
        _softmax_init(m_ref, l_ref, acc_ref)

    def step(masked):
        k = k_ref[...]
        kt = kt_ref[...]
        if masked:
            key = lax.broadcasted_iota(jnp.int32, (tk, tq), 0) + kj * tk
            tok = lax.broadcasted_iota(jnp.int32, (tk, tq), 1) + qi * tq
            keep = key <= tok
        def scores(hd):
            t = _dot_nt(k, q_ref[hd]) * scale2
            return jnp.where(keep, t, -jnp.inf) if masked else t

        ahead = 2
        queue = [scores(hd) for hd in range(ahead)]
        for hd in range(C_HEADS):
            t = queue.pop(0)
            if hd + ahead < C_HEADS:
                queue.append(scores(hd + ahead))
            m_prev = m_ref[hd]
            m_new = jnp.maximum(m_prev, jnp.max(t, axis=0, keepdims=True))
            alpha = jnp.exp2(m_prev - m_new)
            p = jnp.exp2(t - m_new)
            l_ref[hd] = alpha * l_ref[hd] + jnp.sum(p, axis=0, keepdims=True)
            acc_ref[hd] = alpha * acc_ref[hd] + _dot(kt, p.astype(BF16))
            m_ref[hd] = m_new

    fully_visible = (kj + 1) * tk - 1 <= qi * tq
    pl.when(fully_visible)(lambda: step(False))
    pl.when(jnp.logical_not(fully_visible))(lambda: step(True))

    @pl.when(last_ref[p_id] == 1)
    def _():
        for hd in range(C_HEADS):
            o_ref[hd] = (acc_ref[hd] / l_ref[hd]).astype(BF16)


def _attn_prompt(q, kcat, klat_t, n_seq, seq_len, tq, tk):
    assert seq_len % tq == 0 and seq_len % tk == 0
    nq, nk = seq_len // tq, seq_len // tk
    pairs =[(i, j) for i in range(nq) for j in range((i * tq + tq - 1) // tk + 1)]
    qi = jnp.asarray([p[0] for p in pairs], jnp.int32)
    kj = jnp.asarray([p[1] for p in pairs], jnp.int32)
    last = jnp.asarray([int(n + 1 == len(pairs) or pairs[n + 1][0] != p[0]) for n, p in enumerate(pairs)], jnp.int32)
    scale2 = (C_NOPE + C_ROPE) ** -0.5 * math.log2(math.e)
    grid_spec = pltpu.PrefetchScalarGridSpec(
        num_scalar_prefetch=3,
        grid=(n_seq, len(pairs)),
        in_specs=[pl.BlockSpec((None, C_HEADS, tq, C_QK), lambda b, p, qi, kj, last: (b * nq + qi[p], 0, 0, 0)),
                  pl.BlockSpec((tk, C_QK), lambda b, p, qi, kj, last: (b * nk + kj[p], 0)),
                  pl.BlockSpec((C_KVLORA, tk), lambda b, p, qi, kj, last: (0, b * nk + kj[p]))],
        out_specs=pl.BlockSpec((None, C_HEADS, C_KVLORA, tq), lambda b, p, qi, kj, last: (b * nq + qi[p], 0, 0, 0)),
        scratch_shapes=[pltpu.VMEM((C_HEADS, 1, tq), F32), pltpu.VMEM((C_HEADS, 1, tq), F32),
                        pltpu.VMEM((C_HEADS, C_KVLORA, tq), F32)])
    return pl.pallas_call(
        functools.partial(_attn_body, tq=tq, tk=tk, scale2=scale2),
        out_shape=jax.ShapeDtypeStruct((n_seq * nq, C_HEADS, C_KVLORA, tq), BF16),
        grid_spec=grid_spec,
        compiler_params=_cparams("arbitrary", "arbitrary"),
        name="mla_attention_prompt",
    )(qi, kj, last, q, kcat, klat_t)


def _attn_paged_body(pt_ref, q_ref, nlat_ref, nrope_ref, lat_hbm, rope_hbm, o_ref,
                     lat_buf, rope_buf, sem, m_ref, l_ref, acc_ref, *, nsq, pages, sub, j, seq_new, scale2):
    b = pl.program_id(0)
    g = pl.program_id(1)
    n_groups = pl.num_programs(1)
    n_steps = pl.num_programs(0) * n_groups
    step = b * n_groups + g
    slot = lax.rem(step, PAGED_SLOTS)
    lookahead = PAGED_SLOTS - 1

    def page_copy(kind, page_id, sl, i):
        src, dst = (lat_hbm, lat_buf) if kind == 0 else (rope_hbm, rope_buf)
        return pltpu.make_async_copy(src.at[j, page_id], dst.at[sl, i], sem.at[kind, sl])

    def start_group(bb, gg, sl):
        for i in range(nsq * pages):
            page_id = pt_ref[bb * nsq + i // pages, gg * pages + i % pages]
            page_copy(0, page_id, sl, i).start(priority=i % 2)
            page_copy(1, page_id, sl, i).start(priority=(i + 1) % 2)

    def start_step(s, sl):
        s = jnp.minimum(s, n_steps - 1)
        start_group(s // n_groups, lax.rem(s, n_groups), sl)

    @pl.when(step == 0)
    def _():
        for s in range(lookahead):
            start_step(s, s)

    def wait_group(sl):
        for i in range(nsq * pages):
            page_copy(0, 0, sl, i).wait()
            page_copy(1, 0, sl, i).wait()

    wait_group(slot)

    def update(carry, t, values):
        m_prev, l_prev, acc = carry
        m_new = jnp.maximum(m_prev, jnp.max(t, axis=-1, keepdims=True))
        alpha = jnp.exp2(m_prev - m_new)
        p = jnp.exp2(t - m_new)
        return (m_new, alpha * l_prev + jnp.sum(p, axis=-1, keepdims=True),
                alpha * acc + _dot(p.astype(BF16), values))

    first = g == 0
    page = lat_buf.shape[2]
    qls = [q_ref[s][:, :C_KVLORA] for s in range(nsq)]
    qrs = [q_ref[s][:, C_KVLORA:C_KVLORA + C_ROPE] for s in range(nsq)]
    work = []
    for s in range(nsq):
        for u in range(pages // sub):
            e0 = s * pages + u * sub
            lat = lat_buf[slot, e0:e0 + sub].reshape(sub * page, C_KVLORA).astype(BF16)
            rp_t = jnp.concatenate([rope_buf[slot, e0 + i] for i in range(sub)], axis=1).astype(BF16)
            work.append((s, lat, (_dot_nt(qls[s], lat) + _dot(qrs[s], rp_t)) * scale2))
    start_step(step + lookahead, lax.rem(step + lookahead, PAGED_SLOTS))
    carries = [(jnp.where(first, -jnp.inf, m_ref[s]), jnp.where(first, 0.0, l_ref[s]),
                jnp.where(first, 0.0, acc_ref[s])) for s in range(nsq)]
    for s, lat, t in work:
        carries[s] = update(carries[s], t, lat)

    @pl.when(step == n_steps - 1)
    def _():
        for ahead in range(1, PAGED_SLOTS):
            wait_group(lax.rem(step + ahead, PAGED_SLOTS))

    @pl.when(g < n_groups - 1)
    def _():
        for s in range(nsq):
            m_ref[s], l_ref[s], acc_ref[s] = carries[s]

    @pl.when(g == n_groups - 1)
    def _():
        for s in range(nsq):
            nlat = nlat_ref[s].astype(BF16)
            t2 = (_dot_nt(qls[s], nlat) + _dot_nt(qrs[s], nrope_ref[s].astype(BF16))) * scale2
            tok = lax.broadcasted_iota(jnp.int32, t2.shape, 0) & (seq_new - 1)
            key = lax.broadcasted_iota(jnp.int32, t2.shape, 1)
            _, l_fin, acc = update(carries[s], jnp.where(key <= tok, t2, -jnp.inf), nlat)
            o_ref[s] = (acc / l_fin).astype(BF16)


def _attn_paged(q, new_lat, new_rope, pool_lat, pool_rope_t, page_table, j, seq_new):
    n_seq, n_pages = page_table.shape
    page = pool_lat.shape[2]
    nsq, pages, sub = 4, 16, 4
    assert n_seq % nsq == 0 and n_pages % pages == 0 and seq_new & (seq_new - 1) == 0
    rows = q.shape[1]
    scale2 = (C_NOPE + C_ROPE) ** -0.5 * math.log2(math.e)
    grid_spec = pltpu.PrefetchScalarGridSpec(
        num_scalar_prefetch=1,
        grid=(n_seq // nsq, n_pages // pages),
        in_specs=[pl.BlockSpec((nsq, rows, C_QK), lambda b, s, pt: (b, 0, 0)),
                  pl.BlockSpec((nsq,) + new_lat.shape[1:], lambda b, s, pt: (b, 0, 0)),
                  pl.BlockSpec((nsq,) + new_rope.shape[1:], lambda b, s, pt: (b, 0, 0)),
                  pl.BlockSpec(memory_space=pl.ANY), pl.BlockSpec(memory_space=pl.ANY)],
        out_specs=pl.BlockSpec((nsq, rows, C_KVLORA), lambda b, s, pt: (b, 0, 0)),
        scratch_shapes=[pltpu.VMEM((PAGED_SLOTS, nsq * pages, page, C_KVLORA), F32),
                        pltpu.VMEM((PAGED_SLOTS, nsq * pages, C_ROPE, page), F32),
                        pltpu.SemaphoreType.DMA((2, PAGED_SLOTS)),
                        pltpu.VMEM((nsq, rows, 1), F32), pltpu.VMEM((nsq, rows, 1), F32),
                        pltpu.VMEM((nsq, rows, C_KVLORA), F32)])
    return pl.pallas_call(
        functools.partial(_attn_paged_body, nsq=nsq, pages=pages, sub=sub, j=j, seq_new=seq_new, scale2=scale2),
        out_shape=jax.ShapeDtypeStruct((n_seq, rows, C_KVLORA), BF16),
        grid_spec=grid_spec,
        compiler_params=_cparams("arbitrary", "arbitrary"),
        name="mla_attention_paged",
    )(page_table, q, new_lat, new_rope, pool_lat, pool_rope_t)


def _mla_out_body(x_ref, g_ref, o_ref, wuv_ref, wout_ref, lg_ref, lb_ref, y_ref):
    tm = x_ref.shape[0]
    pieces = 2 if tm % (4 * LANES) == 0 and g_ref.shape[0] == 1 else 1
    rows = [slice(p * tm // pieces, (p + 1) * tm // pieces) for p in range(pieces)]

    def product(r):
        parts = [_dot_tn(o_ref[hd, :, r], wuv_ref[hd]) for hd in range(C_HEADS)]
        return _dot(jnp.concatenate(parts, axis=1).astype(BF16), wout_ref[...])

    outs = [product(r) for r in rows]
    for r, out in zip(rows, outs):
        y_ref[r, :] = _residual_ln(x_ref[r, :], g_ref, out, lg_ref, lb_ref)


def _mla_out(x, mod, tps, layer, j, o_lat, wuv, w_out, ln_g, ln_b, tm):
    T = x.shape[0]
    return pl.pallas_call(
        _mla_out_body,
        out_shape=jax.ShapeDtypeStruct((T, D_MODEL), F32),
        grid=(T // tm,),
        in_specs=[_row_spec(tm, D_MODEL), _mod_spec(mod, layer, 2, tps),
                  pl.BlockSpec((None, C_HEADS, C_KVLORA, tm), lambda i: (i, 0, 0, 0)),
                  _full_spec(wuv, j), _full_spec(w_out, j),
                  _vec_spec(layer, D_MODEL), _vec_spec(layer, D_MODEL)],
        out_specs=_row_spec(tm, D_MODEL),
        compiler_params=_cparams("arbitrary"),
        name="mla_out",
    )(x, mod, o_lat, wuv, w_out, ln_g, ln_b)


def _rope_tables(pos, reps, width):
    half = C_ROPE // 2
    inv = ROPE_THETA ** (-jnp.arange(half, dtype=F32) / half)
    ang = pos.astype(F32)[:, None] * inv
    cos, sin = jnp.cos(ang), jnp.sin(ang)
    cc = jnp.tile(jnp.concatenate([cos, cos], axis=1), (1, reps))
    ss = jnp.tile(jnp.concatenate([-sin, sin], axis=1), (1, reps))
    pad = width - cc.shape[1]
    return jnp.pad(cc, ((0, 0), (0, pad))), jnp.pad(ss, ((0, 0), (0, pad)))


def _prepare_params(p):
    vec = lambda a: a.reshape(a.shape[0], 1, a.shape[1])
    w_uq = p['c_w_uq']
    n_c = w_uq.shape[0]
    c_w_in = jnp.pad(p['c_w_in'], ((0, 0), (0, 0), (0, LANES - C_ROPE)))
    lb_all = jnp.cumsum(jax.nn.softmax(p['b_lb'].astype(F32), axis=0), axis=0)
    lb_all = lb_all - lb_all[:1]
    return dict(
        ln1_g=vec(p['ln1_g']), ln1_b=vec(p['ln1_b']), ln2_g=vec(p['ln2_g']), ln2_b=vec(p['ln2_b']),
        ffn_w1=p['ffn_w1'].astype(BF16), ffn_w2=p['ffn_w2'].astype(BF16),
        a_w_in=p['a_w_in'].astype(BF16), a_ln_g=vec(p['a_ln_g']), a_ln_b=vec(p['a_ln_b']),
        a_w_out=p['a_w_out'].astype(BF16),
        b_w_in=p['b_w_in'].astype(BF16), b_w_out=p['b_w_out'].astype(BF16), lb_all=lb_all,
        c_w_in=c_w_in.astype(BF16), c_g_q=vec(p['c_g_q']), c_g_kv=vec(p['c_g_kv']),
        c_wn=w_uq[..., :C_NOPE].reshape(n_c, C_QLORA, C_HEADS * C_NOPE).astype(BF16),
        c_wr=w_uq[..., C_NOPE:].reshape(n_c, C_QLORA, C_HEADS * C_ROPE).astype(BF16),
        c_wuk=jnp.transpose(p['c_w_uk'], (0, 2, 3, 1)).astype(BF16),
        c_wuv=jnp.transpose(p['c_w_uv'], (0, 2, 1, 3)).astype(BF16),
        c_w_out=p['c_w_out'].astype(BF16),
    )


def _sgu_mixing(w_s, b_s, chunk):
    reps = CHUNK_A // chunk
    causal = jnp.tril(jnp.ones((chunk, chunk), dtype=bool))
    ws = jnp.where(causal, w_s[:, :, :chunk, :chunk], 0)
    eye = jnp.eye(reps, dtype=w_s.dtype)
    ws = jnp.einsum('ab,jgts->jgatbs', eye, ws).reshape(w_s.shape[0], A_GROUPS, CHUNK_A, CHUNK_A)
    bias = jnp.tile(jnp.transpose(b_s[:, :, :chunk], (0, 2, 1)), (1, reps, 1))
    bias = jnp.repeat(bias, A_GDIM, axis=2)
    return ws.astype(BF16), bias


def _run_trunk(x, mod, n_seq, seq_len, q_pos, hgrn_state0, mla_cache, prm, raw, tm):
    T = x.shape[0]
    assert T % tm == 0 and (tm % seq_len == 0 or seq_len % tm == 0) and tm % CHUNK_A == 0
    per_seq_mod = mod.ndim == 5
    tps = (seq_len // tm) if per_seq_mod else 1
    sgu_chunk = min(CHUNK_A, seq_len)
    ws, bias = _sgu_mixing(raw['a_w_s'], raw['a_b_s'], sgu_chunk)
    chunk_v, hgrn_states, lat_rows, rope_rows = [], [], [], []
    for i in range(DEPTH):
        kind, j = i % N_MIXERS, i // N_MIXERS
        if kind == 0:
            x, v_rows = _sgu_layer(x, mod, tps, i, j, prm['a_w_in'], prm['a_ln_g'], prm['a_ln_b'], ws, bias,
                                   prm['a_w_out'], prm['ln1_g'], prm['ln1_b'], tm, emit_v=mla_cache is not None)
            chunk_v.append(v_rows)
        elif kind == 1:
            lb = prm['lb_all'][i].reshape(1, D_MODEL)
            q, k, lf, v, gs = _hgrn_proj(x, mod, tps, i, j, prm['b_w_in'], lb, tm)
            if seq_len % 64 == 0:
                C, lpad, hb = 64, seq_len, 8
                rec_tm = min(seq_len, 512)
                rec_in = (q, k, lf, v)
            else:
                C = lpad = rec_tm = SUBLANES
                hb = B_HEADS
                padseq = lambda a: jnp.pad(a.reshape(n_seq, seq_len, D_MODEL),
                                           ((0, 0), (0, lpad - seq_len), (0, 0))).reshape(n_seq * lpad, D_MODEL)
                rec_in = tuple(padseq(a) for a in (q, k, lf, v))
            s0 = None if hgrn_state0 is None else hgrn_state0[j]
            o, S = _hgrn_rec(*rec_in, s0, n_seq, lpad, rec_tm, C, hb)
            if lpad != seq_len:
                o = o.reshape(n_seq, lpad, D_MODEL)[:, :seq_len].reshape(T, D_MODEL)
            hgrn_states.append(S)
            x = _hgrn_out(x, mod, tps, i, j, o, gs, prm['b_w_out'], prm['ln1_g'], prm['ln1_b'], tm)
        else:
            tq = min(tm, 512)
            tps_q = (seq_len // tq) if per_seq_mod else 1
            pos_rows = q_pos if tq <= seq_len else jnp.tile(q_pos, tq // seq_len)
            ccq, ssq = _rope_tables(pos_rows, C_HEADS, C_HEADS * C_ROPE)
            cck, ssk = _rope_tables(pos_rows, 1, LANES)
            qcat, kcat, klat_t, lat, kr = _mla_proj(x, mod, tps_q, pos_rows.shape[0] // tq, i, j, prm['c_w_in'],
                                            prm['c_g_q'], prm['c_g_kv'], prm['c_wn'], prm['c_wr'], prm['c_wuk'],
                                            ccq, ssq, cck, ssk, tq)
            if mla_cache is None:
                o_lat = _attn_prompt(qcat, kcat, klat_t, n_seq, seq_len, tq, min(seq_len, 512))
            else:
                pool_lat, pool_rope_t, pt = mla_cache
                qs = qcat.reshape(C_HEADS, n_seq, seq_len, C_QK).transpose(1, 0, 2, 3)
                qs = qs.reshape(n_seq, C_HEADS * seq_len, C_QK)
                padk = lambda a: jnp.pad(a.reshape(n_seq, seq_len, a.shape[1]), ((0, 0), (0, 16 - seq_len), (0, 0)))
                o_s = _attn_paged(qs, padk(lat), padk(kr), pool_lat, pool_rope_t, pt, j, seq_len)
                o_lat = o_s.reshape(n_seq, C_HEADS, seq_len, C_KVLORA).transpose(1, 3, 0, 2)
                o_lat = o_lat.reshape(1, C_HEADS, C_KVLORA, T)
            x = _mla_out(x, mod, tps_q, i, j, o_lat, prm['c_wuv'], prm['c_w_out'], prm['ln1_g'], prm['ln1_b'], tq)
            lat_rows.append(lat.reshape(n_seq, seq_len, C_KVLORA))
            rope_rows.append(kr.reshape(n_seq, seq_len, C_ROPE))
        x = _ffn_layer(x, mod, tps, i, prm['ffn_w1'], prm['ffn_w2'], prm['ln2_g'], prm['ln2_b'], tm)
    stack = lambda xs: jnp.stack(xs) if xs and xs[0] is not None else None
    return x, stack(chunk_v), jnp.stack(hgrn_states), jnp.stack(lat_rows), jnp.stack(rope_rows)


def kernel(x_prompt, x_sample, cache_kv_latent, cache_k_rope, state_hgrn, page_table, c_prompt, c_sample,
           w_ada, b_ada, ln1_g, ln1_b, ln2_g, ln2_b, ffn_w1, ffn_w2, a_w_in, a_ln_g, a_ln_b, a_w_s, a_b_s,
           a_w_out, b_w_in, b_lb, b_w_out, c_w_in, c_g_q, c_g_kv, c_w_uq, c_w_uk, c_w_uv, c_w_out):
    raw = dict(ln1_g=ln1_g, ln1_b=ln1_b, ln2_g=ln2_g, ln2_b=ln2_b, ffn_w1=ffn_w1, ffn_w2=ffn_w2,
               a_w_in=a_w_in, a_ln_g=a_ln_g, a_ln_b=a_ln_b, a_w_s=a_w_s, a_b_s=a_b_s, a_w_out=a_w_out,
               b_w_in=b_w_in, b_lb=b_lb, b_w_out=b_w_out, c_w_in=c_w_in, c_g_q=c_g_q, c_g_kv=c_g_kv,
               c_w_uq=c_w_uq, c_w_uk=c_w_uk, c_w_uv=c_w_uv, c_w_out=c_w_out)
    prm = _prepare_params(raw)
    nb, seq, d = x_prompt.shape
    ns, sseq, _ = x_sample.shape
    past_len = page_table.shape[1] * cache_kv_latent.shape[2]
    pos_prompt = jnp.arange(seq, dtype=jnp.int32)
    pos_sample = past_len + jnp.arange(sseq, dtype=jnp.int32)

    mod_p, mod_s = _modulation(c_prompt, jnp.repeat(c_sample, sseq, axis=0), w_ada, b_ada)
    mod_p = mod_p.reshape(DEPTH, nb, 6, 1, d)

    tm_p = 512
    y_p, _, hs_p, lat_p, rope_p = _run_trunk(x_prompt.reshape(nb * seq, d), mod_p, nb, seq, pos_prompt,
                                             None, None, prm, raw, tm_p)
    y_s, v_s, hs_s, lat_s, rope_s = _run_trunk(x_sample.reshape(ns * sseq, d), mod_s, ns, sseq, pos_sample,
                                               state_hgrn,
                                               (cache_kv_latent, jnp.swapaxes(cache_k_rope, 2, 3), page_table),
                                               prm, raw, ns * sseq)
    return (y_p.reshape(nb, seq, d), y_s.reshape(ns, sseq, d), hs_p, hs_s, lat_p, rope_p, lat_s, rope_s,
            v_s.reshape(v_s.shape[0], ns, sseq, d))
```
